```python
import jax, jax.numpy as jnp
from jax import lax
import numpy as np

D_MODEL = 1024
BATCH = 2
SEQ = 8192
DEPTH = 1

CHUNK = 64
Q_BLOCK = 2 * CHUNK
CONV_CH = D_MODEL // 2
CONV_WIDTH = 31
FOX_HEADS = 8
FOX_HEAD_DIM = 64
FOX_WIDTH = FOX_HEADS * FOX_HEAD_DIM
N_EXPERTS = 256
TOP_K = 8
N_GROUPS = 8
TOPK_GROUPS = 4
EXPERT_HIDDEN = 256
SHARED_HIDDEN = 256
ROUTED_SCALE = 2.5
EXPERT_BLOCK = 128
LN_EPS = 1e-5
DEEPNORM_ALPHA = (2.0 * DEPTH) ** 0.25
DEEPNORM_BETA = (8.0 * DEPTH) ** -0.25
IN_COLS = 2 * CONV_CH + 3 * FOX_WIDTH + FOX_HEADS + 2 * D_MODEL

kernel_name = 'hybrid_conformer_fox_moe_deepnorm_adaln'


def layer_norm(x, g=None, b=None):
    xf = x.astype(jnp.float32)
    mu = jnp.mean(xf, axis=-1, keepdims=True)
    xc = xf - mu
    var = jnp.mean(xc * xc, axis=-1, keepdims=True)
    y = xc * lax.rsqrt(var + LN_EPS)
    if g is not None:
        y = y * g.astype(jnp.float32) + b.astype(jnp.float32)
    return y.astype(x.dtype)


def causal_depthwise_conv(u, w, b):
    k = w.shape[0]
    out = lax.conv_general_dilated(
        u, w[:, None, :].astype(u.dtype), window_strides=(1,), padding=[(k - 1, 0)],
        dimension_numbers=('NWC', 'WIO', 'NWC'), feature_group_count=u.shape[-1])
    return out + b


def forgetting_attention(q, k, v, log_f):
    bsz, seq, nh, dh = q.shape
    nb = seq // Q_BLOCK
    scale = dh ** -0.5
    cum = jnp.cumsum(log_f, axis=1)
    qb = q.reshape(bsz, nb, Q_BLOCK, nh, dh).transpose(1, 0, 3, 2, 4)
    cq = cum.reshape(bsz, nb, Q_BLOCK, nh).transpose(1, 0, 3, 2)
    kh = k.transpose(0, 2, 1, 3)
    vh = v.transpose(0, 2, 1, 3)
    ck = cum.transpose(0, 2, 1)
    key_pos = jnp.arange(seq)

    def block(args):
        q_blk, cq_blk, i = args
        s = jnp.einsum('bhqd,bhkd->bhqk', q_blk, kh,
                       preferred_element_type=jnp.float32) * scale
        s = s + (cq_blk[..., :, None] - ck[:, :, None, :])
        q_pos = i * Q_BLOCK + jnp.arange(Q_BLOCK)
        s = jnp.where(key_pos[None, :] <= q_pos[:, None], s, -1e30)
        p = jax.nn.softmax(s, axis=-1)
        return jnp.einsum('bhqk,bhkd->bhqd', p.astype(vh.dtype), vh)

    o = lax.map(block, (qb, cq, jnp.arange(nb)))
    return o.transpose(1, 0, 3, 2, 4).reshape(bsz, seq, nh * dh)


def token_mixers(h, w_in, b_forget, conv_w, conv_b, conv_ln_g, conv_ln_b,
                 w_conv_out, w_fox_out, w_mix_out):
    bsz, seq, _ = h.shape
    proj = h @ w_in
    o1 = 2 * CONV_CH
    o2 = o1 + FOX_WIDTH
    o3 = o2 + FOX_WIDTH
    o4 = o3 + FOX_WIDTH
    o5 = o4 + FOX_HEADS
    o6 = o5 + D_MODEL
    conv_in, q, k, v, f_logit, gate_a, gate_b = jnp.split(
        proj, [o1, o2, o3, o4, o5, o6], axis=-1)
    ua, ub = jnp.split(conv_in, 2, axis=-1)
    u = ua * jax.nn.sigmoid(ub)
    u = causal_depthwise_conv(u, conv_w, conv_b)
    u = jax.nn.silu(layer_norm(u, conv_ln_g, conv_ln_b))
    y_a = u @ w_conv_out
    log_f = jax.nn.log_sigmoid(f_logit.astype(jnp.float32) + b_forget.astype(jnp.float32))
    heads = lambda t: t.reshape(bsz, seq, FOX_HEADS, FOX_HEAD_DIM)
    y_b = forgetting_attention(heads(q), heads(k), heads(v), log_f) @ w_fox_out
    merged = jax.nn.sigmoid(gate_a) * y_a + jax.nn.sigmoid(gate_b) * y_b
    return merged @ w_mix_out


def moe_ffn(h, w_router, router_bias, w_exp_gate, w_exp_up, w_exp_down,
            w_sh_gate, w_sh_up, w_sh_down):
    t = h.shape[0]
    scores = jax.nn.sigmoid(h.astype(jnp.float32) @ w_router.astype(jnp.float32))
    sel = scores + router_bias.astype(jnp.float32)
    gsel = sel.reshape(t, N_GROUPS, N_EXPERTS // N_GROUPS)
    group_score = lax.top_k(gsel, 2)[0].sum(-1)
    _, top_groups = lax.top_k(group_score, TOPK_GROUPS)
    group_mask = jax.nn.one_hot(top_groups, N_GROUPS).sum(1) > 0
    masked = jnp.where(group_mask[:, :, None], gsel, -jnp.inf).reshape(t, N_EXPERTS)
    _, top_idx = lax.top_k(masked, TOP_K)
    wts = jnp.take_along_axis(scores, top_idx, axis=-1)
    wts = wts / jnp.sum(wts, axis=-1, keepdims=True) * ROUTED_SCALE

    n_assign = t * TOP_K
    n_pad = -(-(n_assign + N_EXPERTS * (EXPERT_BLOCK - 1)) // EXPERT_BLOCK) * EXPERT_BLOCK
    n_blocks = n_pad // EXPERT_BLOCK
    flat_e = top_idx.reshape(n_assign)
    flat_tok = jnp.arange(n_assign, dtype=jnp.int32) // TOP_K
    flat_w = wts.reshape(n_assign)
    order = jnp.argsort(flat_e)
    e_sorted = flat_e[order]
    tok_sorted = flat_tok[order]
    w_sorted = flat_w[order]
    counts = jnp.bincount(flat_e, length=N_EXPERTS)
    start = jnp.cumsum(counts) - counts
    padded = (counts + EXPERT_BLOCK - 1) // EXPERT_BLOCK * EXPERT_BLOCK
    pend = jnp.cumsum(padded)
    pstart = pend - padded
    dest = pstart[e_sorted] + (jnp.arange(n_assign) - start[e_sorted])
    row_tok = jnp.zeros((n_pad,), jnp.int32).at[dest].set(tok_sorted)
    row_w = jnp.zeros((n_pad,), jnp.float32).at[dest].set(w_sorted)
    block_expert = jnp.clip(
        jnp.searchsorted(pend, jnp.arange(n_blocks) * EXPERT_BLOCK, side='right'),
        0, N_EXPERTS - 1)

    def expert_block(args):
        rows, rw, e = args
        xb = h[rows]
        y = (jax.nn.silu(xb @ w_exp_gate[e]) * (xb @ w_exp_up[e])) @ w_exp_down[e]
        return y * rw[:, None].astype(y.dtype)

    y_pad = lax.map(expert_block, (row_tok.reshape(n_blocks, EXPERT_BLOCK),
                                   row_w.reshape(n_blocks, EXPERT_BLOCK), block_expert))
    routed = jax.ops.segment_sum(y_pad.reshape(n_pad, -1), row_tok, num_segments=t)
    shared = (jax.nn.silu(h @ w_sh_gate) * (h @ w_sh_up)) @ w_sh_down
    return routed.astype(h.dtype) + shared


def setup_inputs(seed: int = 0) -> dict:
    key = jax.random.key(seed)
    ks = jax.random.split(key, 25)
    nrm = lambda k, shape, s: jax.random.normal(k, shape, jnp.float32) * s
    L, D, E, F = DEPTH, D_MODEL, N_EXPERTS, EXPERT_HIDDEN
    beta = DEEPNORM_BETA
    return {
        'x': nrm(ks[0], (BATCH, SEQ, D), 1.0),
        'c': nrm(ks[1], (BATCH, D), 1.0),
        'w_ada': nrm(ks[2], (L, D, 6 * D), 0.1 * D ** -0.5),
        'b_ada': nrm(ks[3], (L, 6 * D), 0.01),
        'w_in': nrm(ks[4], (L, D, IN_COLS), D ** -0.5),
        'b_forget': jnp.linspace(1.0, 6.0, FOX_HEADS)[None, :] + nrm(ks[5], (L, FOX_HEADS), 0.1),
        'conv_w': nrm(ks[6], (L, CONV_WIDTH, CONV_CH), CONV_WIDTH ** -0.5),
        'conv_b': nrm(ks[7], (L, CONV_CH), 0.01),
        'conv_ln_g': 1.0 + nrm(ks[8], (L, CONV_CH), 0.01),
        'conv_ln_b': nrm(ks[9], (L, CONV_CH), 0.01),
        'w_conv_out': nrm(ks[10], (L, CONV_CH, D), CONV_CH ** -0.5 * beta),
        'w_fox_out': nrm(ks[11], (L, FOX_WIDTH, D), FOX_WIDTH ** -0.5 * beta),
        'w_mix_out': nrm(ks[12], (L, D, D), D ** -0.5 * beta),
        'ln1_g': 1.0 + nrm(ks[13], (L, D), 0.01),
        'ln1_b': nrm(ks[14], (L, D), 0.01),
        'w_router': nrm(ks[15], (L, D, E), D ** -0.5),
        'router_bias': nrm(ks[16], (L, E), 0.01),
        'w_exp_gate': nrm(ks[17], (L, E, D, F), D ** -0.5),
        'w_exp_up': nrm(ks[18], (L, E, D, F), D ** -0.5),
        'w_exp_down': nrm(ks[19], (L, E, F, D), F ** -0.5 * beta),
        'w_sh_gate': nrm(ks[20], (L, D, SHARED_HIDDEN), D ** -0.5),
        'w_sh_up': nrm(ks[21], (L, D, SHARED_HIDDEN), D ** -0.5),
        'w_sh_down': nrm(ks[22], (L, SHARED_HIDDEN, D), SHARED_HIDDEN ** -0.5 * beta),
        'ln2_g': 1.0 + nrm(ks[23], (L, D), 0.01),
        'ln2_b': nrm(ks[24], (L, D), 0.01),
    }


def reference(x, c, w_ada, b_ada, w_in, b_forget, conv_w, conv_b, conv_ln_g, conv_ln_b,
              w_conv_out, w_fox_out, w_mix_out, ln1_g, ln1_b, w_router, router_bias,
              w_exp_gate, w_exp_up, w_exp_down, w_sh_gate, w_sh_up, w_sh_down,
              ln2_g, ln2_b):
    bsz, seq, d = x.shape
    cond = jax.nn.silu(c)
    for l in range(DEPTH):
        ada = (cond @ w_ada[l] + b_ada[l])[:, None, :]
        shift1, scale1, gate1, shift2, scale2, gate2 = jnp.split(ada, 6, axis=-1)
        h = layer_norm(x) * (1 + scale1) + shift1
        y = token_mixers(h, w_in[l], b_forget[l], conv_w[l], conv_b[l], conv_ln_g[l],
                         conv_ln_b[l], w_conv_out[l], w_fox_out[l], w_mix_out[l])
        x = layer_norm(DEEPNORM_ALPHA * x + (1 + gate1) * y, ln1_g[l], ln1_b[l])
        h = layer_norm(x) * (1 + scale2) + shift2
        y = moe_ffn(h.reshape(bsz * seq, d), w_router[l], router_bias[l], w_exp_gate[l],
                    w_exp_up[l], w_exp_down[l], w_sh_gate[l], w_sh_up[l],
                    w_sh_down[l]).reshape(bsz, seq, d)
        x = layer_norm(DEEPNORM_ALPHA * x + (1 + gate2) * y, ln2_g[l], ln2_b[l])
    return x
```

```python
import functools

import jax
import jax.numpy as jnp
import numpy as np
from jax import lax
from jax.experimental import pallas as pl
from jax.experimental.pallas import tpu as pltpu

F32 = jnp.float32
BF16 = jnp.bfloat16

LN_EPS = 1e-5
CONV_WIDTH = 31
FOX_HEADS = 8
FOX_HEAD_DIM = 64
N_GROUPS = 8
TOPK_GROUPS = 4
TOP_K = 8
ROUTED_SCALE = 2.5
EXPERT_BLOCK = 128

V7X_LANES = 128
HEAD_PAD = 128
CONV_HALO = 32
VMEM_LIMIT = 56 * 1024 * 1024
NEG_BIG = -1e30


def _cparams(sem):
    return pltpu.CompilerParams(dimension_semantics=sem, vmem_limit_bytes=VMEM_LIMIT)


def _ln(v):
    mu = jnp.mean(v, axis=-1, keepdims=True)
    vc = v - mu
    var = jnp.mean(vc * vc, axis=-1, keepdims=True)
    return vc * lax.rsqrt(var + LN_EPS)


def _split3(v):
    hi = v.astype(BF16)
    r1 = v - hi.astype(F32)
    mid = r1.astype(BF16)
    lo = (r1 - mid.astype(F32)).astype(BF16)
    return hi, mid, lo


def _dot(a, b):
    return jnp.dot(a, b, preferred_element_type=F32)


def _dot_nt(a, b):
    return lax.dot_general(a, b, (((1,), (1,)), ((), ())), preferred_element_type=F32)


def _dot_tn(a, b):
    return lax.dot_general(a, b, (((0,), (0,)), ((), ())), preferred_element_type=F32)


def _const_spec(shape):
    nd = len(shape)
    return pl.BlockSpec(shape, lambda *_: (0,) * nd)


def _ada_kernel(c_ref, w_ref, b_ref, o_ref):
    c = c_ref[...]
    cond = c * jax.nn.sigmoid(c)
    ch, cm, _ = _split3(cond)
    w = w_ref[...]
    wh, wm, _ = _split3(w)
    o_ref[...] = _dot(ch, wh) + _dot(ch, wm) + _dot(cm, wh) + b_ref[...]


def _ada(c_pad, w, b):
    rows, d = c_pad.shape
    n = w.shape[1]
    tn = 1024
    return pl.pallas_call(
        _ada_kernel,
        grid=(n // tn,),
        in_specs=[_const_spec((rows, d)),
                  pl.BlockSpec((d, tn), lambda j: (0, j)),
                  pl.BlockSpec((1, tn), lambda j: (0, j))],
        out_specs=pl.BlockSpec((rows, tn), lambda j: (0, j)),
        out_shape=jax.ShapeDtypeStruct((rows, n), F32),
        compiler_params=_cparams(("arbitrary",)),
        name="ada",
    )(c_pad, w, b)


def _inproj_kernel(x_ref, sc_ref, sh_ref, wglu_ref, wq_ref, wk_ref, wvt_ref, wf_ref, wga_ref, wgb_ref,
                   bf_ref, pq_ref, pk_ref,
                   u_ref, q_ref, k_ref, vt_ref, sga_ref, sgb_ref, carry_ref, *, tiles_per_seq, conv_ch):
    i = pl.program_id(0)
    tm = x_ref.shape[0]
    h = _ln(x_ref[...]) * (1.0 + sc_ref[0]) + sh_ref[0]
    hb = h.astype(BF16)

    glu = _dot(hb, wglu_ref[...])
    u_ref[...] = glu[:, :conv_ch] * jax.nn.sigmoid(glu[:, conv_ch:])

    f = _dot(hb, wf_ref[...]) + bf_ref[...]
    logf = jnp.minimum(f, 0.0) - jnp.log(1.0 + jnp.exp(-jnp.abs(f)))
    lh, lm, ll = _split3(logf)
    row = lax.broadcasted_iota(jnp.int32, (tm, tm), 0)
    col = lax.broadcasted_iota(jnp.int32, (tm, tm), 1)
    tri = jnp.where(row >= col, 1.0, 0.0).astype(BF16)
    cs = _dot(tri, lh) + _dot(tri, lm) + _dot(tri, ll)

    @pl.when(i % tiles_per_seq == 0)
    def _():
        carry_ref[...] = jnp.zeros_like(carry_ref)

    cum = cs + carry_ref[...]
    carry_ref[...] = cum[tm - 1:tm, :]

    ch, cm, cl = _split3(cum)
    lane = lax.broadcasted_iota(jnp.int32, cum.shape, 1)
    tail = jnp.where(lane == 24, 1.0, 0.0)
    pieces = jnp.where(lane < 8, ch.astype(F32), jnp.where(lane < 16, cm.astype(F32),
                       jnp.where(lane < 24, cl.astype(F32), tail))).astype(BF16)
    scale = FOX_HEAD_DIM ** -0.5
    q_ref[...] = (_dot(hb, wq_ref[...]) * scale + _dot(pieces, pq_ref[...])).astype(BF16)
    k_ref[...] = (_dot(hb, wk_ref[...]) + _dot(pieces, pk_ref[...])).astype(BF16)
    vt_ref[0, 0] = _dot_nt(wvt_ref[...], hb).astype(BF16)
    sga_ref[...] = jax.nn.sigmoid(_dot(hb, wga_ref[...])).astype(BF16)
    sgb_ref[...] = jax.nn.sigmoid(_dot(hb, wgb_ref[...])).astype(BF16)


def _inproj(x2, scale1, shift1, wts, tm, seq):
    t, d = x2.shape
    tps = seq // tm
    bsz = t // seq
    wglu, wq, wk, wvt, wf, wga, wgb, bf, pq, pk = wts
    conv_ch = wglu.shape[1] // 2
    fw = wvt.shape[0]
    qw = wq.shape[1]
    mod_spec = pl.BlockSpec((1, 1, d), lambda i: (i // tps, 0, 0))
    row_spec = lambda n: pl.BlockSpec((tm, n), lambda i: (i, 0))
    return pl.pallas_call(
        functools.partial(_inproj_kernel, tiles_per_seq=tps, conv_ch=conv_ch),
        grid=(t // tm,),
        in_specs=[row_spec(d), mod_spec, mod_spec] + [_const_spec(w.shape) for w in wts],
        out_specs=[row_spec(conv_ch), row_spec(qw), row_spec(qw),
                   pl.BlockSpec((1, 1, fw, tm), lambda i: (i // tps, i % tps, 0, 0)),
                   row_spec(d), row_spec(d)],
        out_shape=[jax.ShapeDtypeStruct((t, conv_ch), F32),
                   jax.ShapeDtypeStruct((t, qw), BF16),
                   jax.ShapeDtypeStruct((t, qw), BF16),
                   jax.ShapeDtypeStruct((bsz, tps, fw, tm), BF16),
                   jax.ShapeDtypeStruct((t, d), BF16),
                   jax.ShapeDtypeStruct((t, d), BF16)],
        scratch_shapes=[pltpu.VMEM((1, V7X_LANES), F32)],
        compiler_params=_cparams(("arbitrary",)),
        name="inproj",
    )(x2, scale1, shift1, *wts)


def _conv_kernel(cur_ref, prev_ref, w_ref, b_ref, g_ref, be_ref, wout_ref, sga_ref, o_ref, ext_ref,
                 *, tiles_per_seq, chunk):
    i = pl.program_id(0)
    tm = cur_ref.shape[0]
    first = (i % tiles_per_seq) == 0
    ext_ref[0:CONV_HALO, :] = jnp.where(first, 0.0, prev_ref[...])
    ext_ref[CONV_HALO:, :] = cur_ref[...]
    w = w_ref[...]
    off = CONV_HALO - (CONV_WIDTH - 1)
    outs = []
    for c0 in range(0, tm, chunk):
        acc = jnp.zeros((chunk, cur_ref.shape[1]), F32)
        for j in range(CONV_WIDTH):
            acc = acc + w[j:j + 1, :] * ext_ref[c0 + off + j:c0 + off + j + chunk, :]
        outs.append(acc)
    v = jnp.concatenate(outs, axis=0) + b_ref[...]
    v = _ln(v) * g_ref[...] + be_ref[...]
    v = v * jax.nn.sigmoid(v)
    ya = _dot(v.astype(BF16), wout_ref[...])
    o_ref[...] = (sga_ref[...].astype(F32) * ya).astype(BF16)


def _conv(u, conv_w, conv_b, g, be, wout, sga, tm, seq):
    t, ch = u.shape
    d = wout.shape[1]
    tps = seq // tm
    halo_per_tile = tm // CONV_HALO
    return pl.pallas_call(
        functools.partial(_conv_kernel, tiles_per_seq=tps, chunk=64),
        grid=(t // tm,),
        in_specs=[pl.BlockSpec((tm, ch), lambda i: (i, 0)),
                  pl.BlockSpec((CONV_HALO, ch), lambda i: (jnp.maximum(i * halo_per_tile - 1, 0), 0)),
                  _const_spec(conv_w.shape), _const_spec(conv_b.shape), _const_spec(g.shape),
                  _const_spec(be.shape), _const_spec(wout.shape),
                  pl.BlockSpec((tm, d), lambda i: (i, 0))],
        out_specs=pl.BlockSpec((tm, d), lambda i: (i, 0)),
        out_shape=jax.ShapeDtypeStruct((t, d), BF16),
        scratch_shapes=[pltpu.VMEM((tm + CONV_HALO, ch), F32)],
        compiler_params=_cparams(("arbitrary",)),
        name="conv",
    )(u, u, conv_w, conv_b, g, be, wout, sga)


def _attn_kernel(q_ref, k_ref, vt_ref, o_ref, *, blk):
    qi = pl.program_id(2)
    q = q_ref[...]

    def step(kj, carry, masked):
        m, l, acc = carry
        k = k_ref[pl.ds(pl.multiple_of(kj * blk, blk), blk), :]
        s = _dot_nt(k, q)
        if masked:
            kpos = lax.broadcasted_iota(jnp.int32, s.shape, 0)
            qpos = lax.broadcasted_iota(jnp.int32, s.shape, 1)
            s = jnp.where(kpos <= qpos, s, NEG_BIG)
        m_new = jnp.maximum(m, jnp.max(s, axis=0, keepdims=True))
        alpha = jnp.exp(m - m_new)
        p = jnp.exp(s - m_new)
        l = alpha * l + jnp.sum(p, axis=0, keepdims=True)
        acc = alpha * acc + _dot(vt_ref[kj], p.astype(BF16))
        return m_new, l, acc

    init = (jnp.full((1, blk), NEG_BIG, F32), jnp.zeros((1, blk), F32),
            jnp.zeros((FOX_HEAD_DIM, blk), F32))
    carry = lax.fori_loop(0, qi, lambda kj, c: step(kj, c, False), init)
    m, l, acc = step(qi, carry, True)
    o_ref[...] = (acc / l).astype(BF16)


def _attn(q, k, vt, blk):
    bsz, seq, _ = q.shape
    nkb = seq // blk
    return pl.pallas_call(
        functools.partial(_attn_kernel, blk=blk),
        grid=(bsz, FOX_HEADS, seq // blk),
        in_specs=[pl.BlockSpec((None, blk, HEAD_PAD), lambda b, h, i: (b, i, h)),
                  pl.BlockSpec((None, seq, HEAD_PAD), lambda b, h, i: (b, 0, h)),
                  pl.BlockSpec((None, nkb, FOX_HEAD_DIM, blk), lambda b, h, i: (b, 0, h, 0))],
        out_specs=pl.BlockSpec((None, FOX_HEAD_DIM, blk), lambda b, h, i: (b, h, i)),
        out_shape=jax.ShapeDtypeStruct((bsz, FOX_HEADS * FOX_HEAD_DIM, seq), BF16),
        compiler_params=_cparams(("arbitrary", "arbitrary", "arbitrary")),
        name="attn",
    )(q, k, vt)


def _mix_kernel(ot_ref, gya_ref, sgb_ref, x_ref, g1_ref, sc2_ref, sh2_ref, lg_ref, lb_ref,
                wfox_ref, wmix_ref, wrh_ref, wrl_ref, wsg_ref, wsu_ref, wsd_ref,
                x1_ref, hp_ref, st_ref, shr_ref, *, alpha):
    yb = _dot_tn(ot_ref[...], wfox_ref[...])
    merged = gya_ref[...].astype(F32) + sgb_ref[...].astype(F32) * yb
    y = _dot(merged.astype(BF16), wmix_ref[...])
    x1 = _ln(alpha * x_ref[...] + (1.0 + g1_ref[0]) * y) * lg_ref[...] + lb_ref[...]
    x1_ref[...] = x1
    h2 = _ln(x1) * (1.0 + sc2_ref[0]) + sh2_ref[0]
    hb = h2.astype(BF16)
    hl = (h2 - hb.astype(F32)).astype(BF16)

    half = hb.shape[1] // 2
    lo_bits = lax.bitcast_convert_type(hb[:, :half].astype(F32), jnp.uint32)
    hi_bits = lax.bitcast_convert_type(hb[:, half:].astype(F32), jnp.uint32)
    hp_ref[...] = (lo_bits >> 16) | (hi_bits & jnp.uint32(0xFFFF0000))

    logits_t = _dot_nt(wrh_ref[...], hb) + _dot_nt(wrl_ref[...], hb) + _dot_nt(wrh_ref[...], hl)
    st_ref[...] = jax.nn.sigmoid(logits_t)

    g = _dot(hb, wsg_ref[...])
    u = _dot(hb, wsu_ref[...])
    a = (g * jax.nn.sigmoid(g) * u).astype(BF16)
    shr_ref[...] = _dot(a, wsd_ref[...])


def _mix(ot, gya, sgb, x2, gate1, scale2, shift2, lg, lb, wts, tm, seq, alpha):
    t, d = x2.shape
    tps = seq // tm
    fw = ot.shape[1]
    n_exp = wts[2].shape[0]
    mod_spec = pl.BlockSpec((1, 1, d), lambda i: (i // tps, 0, 0))
    row_spec = lambda n: pl.BlockSpec((tm, n), lambda i: (i, 0))
    return pl.pallas_call(
        functools.partial(_mix_kernel, alpha=alpha),
        grid=(t // tm,),
        in_specs=[pl.BlockSpec((None, fw, tm), lambda i: (i // tps, 0, i % tps)),
                  row_spec(d), row_spec(d), row_spec(d), mod_spec, mod_spec, mod_spec,
                  _const_spec(lg.shape), _const_spec(lb.shape)] + [_const_spec(w.shape) for w in wts],
        out_specs=[row_spec(d), row_spec(d // 2), pl.BlockSpec((n_exp, tm), lambda i: (0, i)), row_spec(d)],
        out_shape=[jax.ShapeDtypeStruct((t, d), F32),
                   jax.ShapeDtypeStruct((t, d // 2), jnp.uint32),
                   jax.ShapeDtypeStruct((n_exp, t), F32),
                   jax.ShapeDtypeStruct((t, d), F32)],
        compiler_params=_cparams(("arbitrary",)),
        name="mix",
    )(ot, gya, sgb, x2, gate1, scale2, shift2, lg, lb, *wts)


def _route_kernel(st_ref, bias_ref, idx_ref, wts_ref, rank_ref, cnt_ref, carry_ref):
    i = pl.program_id(0)
    n_exp, tr = st_ref.shape
    gsz = n_exp // N_GROUPS
    neg_inf = -jnp.inf

    @pl.when(i == 0)
    def _():
        carry_ref[...] = jnp.zeros_like(carry_ref)

    shape3 = (N_GROUPS, gsz, tr)

    def max01(v):
        return jnp.max(jnp.max(v, axis=1, keepdims=True), axis=0, keepdims=True)

    def min01(v):
        return jnp.min(jnp.min(v, axis=1, keepdims=True), axis=0, keepdims=True)

    def sum01(v):
        return jnp.sum(jnp.sum(v, axis=1, keepdims=True), axis=0, keepdims=True)

    sc = st_ref[...].reshape(shape3)
    gsel = (st_ref[...] + bias_ref[...]).reshape(shape3)
    pos = lax.broadcasted_iota(jnp.int32, shape3, 1)
    m1 = jnp.max(gsel, axis=1, keepdims=True)
    i1 = jnp.min(jnp.where(gsel == m1, pos, gsz), axis=1, keepdims=True)
    m2 = jnp.max(jnp.where(pos == i1, neg_inf, gsel), axis=1, keepdims=True)
    gs = m1 + m2

    gid = lax.broadcasted_iota(jnp.int32, gs.shape, 0)
    gkeep = jnp.zeros(gs.shape, F32)
    for _ in range(TOPK_GROUPS):
        mx = jnp.max(gs, axis=0, keepdims=True)
        gi = jnp.min(jnp.where(gs == mx, gid, N_GROUPS), axis=0, keepdims=True)
        hit = gid == gi
        gkeep = gkeep + jnp.where(hit, 1.0, 0.0)
        gs = jnp.where(hit, neg_inf, gs)

    cur = jnp.where(jnp.broadcast_to(gkeep, shape3) > 0.5, gsel, neg_inf)
    eid = lax.broadcasted_iota(jnp.int32, shape3, 0) * gsz + pos
    onehot = jnp.zeros(shape3, F32)
    idxs, ws = [], []
    wsum = jnp.zeros((1, 1, tr), F32)
    for _ in range(TOP_K):
        mx = max01(cur)
        ik = min01(jnp.where(cur == mx, eid, n_exp))
        hit = eid == ik
        wk = sum01(jnp.where(hit, sc, 0.0))
        idxs.append(ik)
        ws.append(wk)
        wsum = wsum + wk
        onehot = onehot + jnp.where(hit, 1.0, 0.0)
        cur = jnp.where(hit, neg_inf, cur)

    ra = lax.broadcasted_iota(jnp.int32, (tr, tr), 0)
    rb = lax.broadcasted_iota(jnp.int32, (tr, tr), 1)
    upper = jnp.where(ra < rb, 1.0, 0.0).astype(BF16)
    onehot2 = onehot.reshape(n_exp, tr)
    prior = (_dot(onehot2.astype(BF16), upper) + carry_ref[...]).reshape(shape3)
    for slot in range(TOP_K):
        idx_ref[slot:slot + 1, :] = idxs[slot].reshape(1, tr)
        wts_ref[slot:slot + 1, :] = (ws[slot] / wsum * ROUTED_SCALE).reshape(1, tr)
        rk = sum01(jnp.where(eid == idxs[slot], prior, 0.0))
        rank_ref[slot:slot + 1, :] = rk.reshape(1, tr).astype(jnp.int32)
    total = carry_ref[...] + jnp.sum(onehot2, axis=1, keepdims=True)
    carry_ref[...] = total
    cnt_ref[...] = jnp.broadcast_to(total, cnt_ref.shape).astype(jnp.int32)


def _route(scores_t, bias_col, tr):
    n_exp, t = scores_t.shape
    slot_spec = pl.BlockSpec((TOP_K, tr), lambda i: (0, i))
    return pl.pallas_call(
        _route_kernel,
        grid=(t // tr,),
        in_specs=[pl.BlockSpec((n_exp, tr), lambda i: (0, i)), _const_spec(bias_col.shape)],
        out_specs=[slot_spec, slot_spec, slot_spec, _const_spec((n_exp, V7X_LANES))],
        out_shape=[jax.ShapeDtypeStruct((TOP_K, t), jnp.int32),
                   jax.ShapeDtypeStruct((TOP_K, t), F32),
                   jax.ShapeDtypeStruct((TOP_K, t), jnp.int32),
                   jax.ShapeDtypeStruct((n_exp, V7X_LANES), jnp.int32)],
        scratch_shapes=[pltpu.VMEM((n_exp, 1), F32)],
        compiler_params=_cparams(("arbitrary",)),
        name="route",
    )(scores_t, bias_col)


def _dest_kernel(pstart_ref, idx_ref, rank_ref, o_ref):
    idx = idx_ref[...]
    n_exp = pstart_ref.shape[0]

    def body(e, acc):
        return acc + jnp.where(idx == e, pstart_ref[e], 0)

    o_ref[...] = lax.fori_loop(0, n_exp, body, rank_ref[...])


def _dest(pstart, idx, rank, tr):
    k, t = idx.shape
    spec = pl.BlockSpec((k, tr), lambda i, ps: (0, i))
    return pl.pallas_call(
        _dest_kernel,
        grid_spec=pltpu.PrefetchScalarGridSpec(
            num_scalar_prefetch=1, grid=(t // tr,), in_specs=[spec, spec], out_specs=spec),
        out_shape=jax.ShapeDtypeStruct((k, t), jnp.int32),
        compiler_params=_cparams(("arbitrary",)),
        name="dest",
    )(pstart, idx, rank)


def _dispatch_kernel(dest_ref, hp_ref, zeros_ref, xs_ref, sem):
    del zeros_ref
    td = hp_ref.shape[0]

    def row_copy(r, slot):
        return pltpu.make_async_copy(hp_ref.at[pl.ds(r, 1)], xs_ref.at[pl.ds(dest_ref[r * TOP_K + slot], 1)], sem)

    def start(r, c):
        for slot in range(TOP_K):
            row_copy(r, slot).start()
        return c

    def wait(r, c):
        for slot in range(TOP_K):
            row_copy(r, slot).wait()
        return c

    lax.fori_loop(0, td, start, 0)
    lax.fori_loop(0, td, wait, 0)


def _dispatch(dest_flat, hp, n_pad, td):
    t, w = hp.shape
    zeros = jnp.zeros((n_pad, w), hp.dtype)
    return pl.pallas_call(
        _dispatch_kernel,
        grid=(t // td,),
        in_specs=[pl.BlockSpec((td * TOP_K,), lambda i: (i,), memory_space=pltpu.SMEM),
                  pl.BlockSpec((td, w), lambda i: (i, 0)),
                  pl.BlockSpec(memory_space=pl.ANY)],
        out_specs=pl.BlockSpec(memory_space=pl.ANY),
        out_shape=jax.ShapeDtypeStruct((n_pad, w), hp.dtype),
        scratch_shapes=[pltpu.SemaphoreType.DMA(())],
        input_output_aliases={2: 0},
        compiler_params=_cparams(("arbitrary",)),
        name="dispatch",
    )(dest_flat, hp, zeros)


def _expert_kernel(be_ref, nused_ref, xs_ref, wg_ref, wu_ref, wd_ref, ys_ref, wg_s, wu_s, wd_s):
    p = pl.program_id(0)
    prev = be_ref[jnp.maximum(p - 1, 0)]
    fresh = jnp.logical_or(p == 0, be_ref[p] != prev)
    live = p < nused_ref[0]

    @pl.when(jnp.logical_and(live, fresh))
    def _():
        wg_s[...] = wg_ref[...].astype(BF16)
        wu_s[...] = wu_ref[...].astype(BF16)
        wd_s[...] = wd_ref[...].astype(BF16)

    @pl.when(live)
    def _():
        words = xs_ref[...]
        lo = lax.bitcast_convert_type(words << 16, F32)
        hi = lax.bitcast_convert_type(words & jnp.uint32(0xFFFF0000), F32)
        x = jnp.concatenate([lo, hi], axis=1).astype(BF16)
        g = _dot(x, wg_s[...])
        u = _dot(x, wu_s[...])
        a = (g * jax.nn.sigmoid(g) * u).astype(BF16)
        ys_ref[...] = _dot(a, wd_s[...])

    @pl.when(jnp.logical_not(live))
    def _():
        ys_ref[...] = jnp.zeros_like(ys_ref)


def _experts(block_expert, n_used, xs, wg, wu, wd, layer):
    n_pad, w = xs.shape
    _, n_exp, d, f = wg.shape
    n_blocks = n_pad // EXPERT_BLOCK

    def xmap(p, be, nu):
        return (jnp.minimum(p, nu[0] - 1), 0)

    def wmap(p, be, nu):
        return (layer, be[jnp.minimum(p, nu[0] - 1)], 0, 0)

    return pl.pallas_call(
        _expert_kernel,
        grid_spec=pltpu.PrefetchScalarGridSpec(
            num_scalar_prefetch=2, grid=(n_blocks,),
            in_specs=[pl.BlockSpec((EXPERT_BLOCK, w), xmap),
                      pl.BlockSpec((None, None, d, f), wmap),
                      pl.BlockSpec((None, None, d, f), wmap),
                      pl.BlockSpec((None, None, f, d), wmap)],
            out_specs=pl.BlockSpec((EXPERT_BLOCK, d), lambda p, be, nu: (p, 0)),
            scratch_shapes=[pltpu.VMEM((d, f), BF16), pltpu.VMEM((d, f), BF16), pltpu.VMEM((f, d), BF16)]),
        out_shape=jax.ShapeDtypeStruct((n_pad, d), F32),
        compiler_params=_cparams(("arbitrary",)),
        name="experts",
    )(block_expert, n_used, xs, wg, wu, wd)


def _combine_kernel(dest_ref, ys_ref, w_ref, shr_ref, x1_ref, g2_ref, lg_ref, lb_ref, o_ref, buf, sem, *, alpha):
    tc = x1_ref.shape[0]

    def row_copy(r, slot):
        return pltpu.make_async_copy(ys_ref.at[pl.ds(dest_ref[r * TOP_K + slot], 1)],
                                     buf.at[slot, pl.ds(r, 1)], sem)

    def start(r, c):
        for slot in range(TOP_K):
            row_copy(r, slot).start()
        return c

    def wait(r, c):
        for slot in range(TOP_K):
            row_copy(r, slot).wait()
        return c

    lax.fori_loop(0, tc, start, 0)
    lax.fori_loop(0, tc, wait, 0)

    w = w_ref[...]
    y = shr_ref[...]
    for slot in range(TOP_K):
        y = y + buf[slot] * w[:, slot:slot + 1]
    z = alpha * x1_ref[...] + (1.0 + g2_ref[0]) * y
    o_ref[...] = _ln(z) * lg_ref[...] + lb_ref[...]


def _combine(dest_flat, ys, wts_tk, shared, x1, gate2, lg, lb, tc, seq, alpha):
    t, d = x1.shape
    tps = seq // tc
    row_spec = lambda n: pl.BlockSpec((tc, n), lambda i: (i, 0))
    return pl.pallas_call(
        functools.partial(_combine_kernel, alpha=alpha),
        grid=(t // tc,),
        in_specs=[pl.BlockSpec((tc * TOP_K,), lambda i: (i,), memory_space=pltpu.SMEM),
                  pl.BlockSpec(memory_space=pl.ANY),
                  row_spec(TOP_K), row_spec(d), row_spec(d),
                  pl.BlockSpec((1, 1, d), lambda i: (i // tps, 0, 0)),
                  _const_spec(lg.shape), _const_spec(lb.shape)],
        out_specs=row_spec(d),
        out_shape=jax.ShapeDtypeStruct((t, d), F32),
        scratch_shapes=[pltpu.VMEM((TOP_K, tc, d), F32), pltpu.SemaphoreType.DMA(())],
        compiler_params=_cparams(("arbitrary",)),
        name="combine",
    )(dest_flat, ys, wts_tk, shared, x1, gate2, lg, lb)


def _placement():
    pq = np.zeros((V7X_LANES, FOX_HEADS * HEAD_PAD), np.float32)
    pk = np.zeros((V7X_LANES, FOX_HEADS * HEAD_PAD), np.float32)
    for h in range(FOX_HEADS):
        base = h * HEAD_PAD + FOX_HEAD_DIM
        for piece in range(3):
            pq[piece * 8 + h, base + piece] = 1.0
            pk[24, base + piece] = 1.0
            pq[24, base + 3 + piece] = 1.0
            pk[piece * 8 + h, base + 3 + piece] = -1.0
    return jnp.asarray(pq, BF16), jnp.asarray(pk, BF16)


def _inproj_weights(w_in, b_forget, d):
    conv2 = d
    fw = FOX_HEADS * FOX_HEAD_DIM
    o1, o2, o3, o4 = conv2, conv2 + fw, conv2 + 2 * fw, conv2 + 3 * fw
    o5 = o4 + FOX_HEADS
    o6 = o5 + d

    def pad_heads(w):
        w = w.reshape(d, FOX_HEADS, FOX_HEAD_DIM)
        w = jnp.pad(w, ((0, 0), (0, 0), (0, HEAD_PAD - FOX_HEAD_DIM)))
        return w.reshape(d, FOX_HEADS * HEAD_PAD).astype(BF16)

    wglu = w_in[:, :o1].astype(BF16)
    wq = pad_heads(w_in[:, o1:o2])
    wk = pad_heads(w_in[:, o2:o3])
    wvt = w_in[:, o3:o4].T.astype(BF16)
    wf8 = w_in[:, o4:o5]
    wf = jnp.pad(jnp.concatenate([wf8, wf8, wf8], axis=1), ((0, 0), (0, V7X_LANES - 3 * FOX_HEADS))).astype(BF16)
    bf = jnp.pad(jnp.concatenate([b_forget, b_forget, b_forget]), (0, V7X_LANES - 3 * FOX_HEADS))[None, :].astype(F32)
    wga = w_in[:, o5:o6].astype(BF16)
    wgb = w_in[:, o6:].astype(BF16)
    pq, pk = _placement()
    return (wglu, wq, wk, wvt, wf, wga, wgb, bf, pq, pk)


def _layer(x2, ada, bsz, seq, w_in, b_forget, conv_w, conv_b, conv_ln_g, conv_ln_b, w_conv_out, w_fox_out,
           w_mix_out, ln1_g, ln1_b, w_router, router_bias, w_exp_gate, w_exp_up, w_exp_down,
           w_sh_gate, w_sh_up, w_sh_down, ln2_g, ln2_b, depth, layer):
    t, d = x2.shape
    n_exp = w_router.shape[1]
    alpha = (2.0 * depth) ** 0.25
    mods = [ada[:bsz, j * d:(j + 1) * d][:, None, :] for j in range(6)]
    shift1, scale1, gate1, shift2, scale2, gate2 = mods

    tm = min(512, seq)
    u, q, k, vt, sga, sgb = _inproj(x2, scale1, shift1, _inproj_weights(w_in, b_forget, d), tm, seq)

    tcv = min(256, seq)
    conv_w_pad = jnp.pad(conv_w, ((0, CONV_HALO - CONV_WIDTH), (0, 0)))
    gya = _conv(u, conv_w_pad, conv_b[None, :], conv_ln_g[None, :], conv_ln_b[None, :],
                w_conv_out.astype(BF16), sga, tcv, seq)

    ot = _attn(q.reshape(bsz, seq, -1), k.reshape(bsz, seq, -1), vt, tm)

    tmx = min(256, seq)
    wr_t = w_router.T
    wr_h = wr_t.astype(BF16)
    wr_l = (wr_t - wr_h.astype(F32)).astype(BF16)
    mix_w = (w_fox_out.astype(BF16), w_mix_out.astype(BF16), wr_h, wr_l,
             w_sh_gate.astype(BF16), w_sh_up.astype(BF16), w_sh_down.astype(BF16))
    x1, hp, scores_t, shared = _mix(ot, gya, sgb, x2, gate1, scale2, shift2, ln1_g[None, :], ln1_b[None, :],
                                    mix_w, tmx, seq, alpha)

    tr = min(512, t)
    idx, wts, rank, cnt = _route(scores_t, router_bias[:, None], tr)

    counts = cnt[:, 0]
    padded = (counts + EXPERT_BLOCK - 1) // EXPERT_BLOCK * EXPERT_BLOCK
    pend = jnp.cumsum(padded)
    pstart = (pend - padded).astype(jnp.int32)
    n_assign = t * TOP_K
    n_pad = -(-(n_assign + n_exp * (EXPERT_BLOCK - 1)) // EXPERT_BLOCK) * EXPERT_BLOCK
    n_blocks = n_pad // EXPERT_BLOCK
    block_expert = jnp.clip(jnp.searchsorted(pend, jnp.arange(n_blocks) * EXPERT_BLOCK, side='right'),
                            0, n_exp - 1).astype(jnp.int32)
    n_used = (pend[-1:] // EXPERT_BLOCK).astype(jnp.int32)

    dest = _dest(pstart, idx, rank, tr)
    dest_flat = dest.T.reshape(-1)
    xs = _dispatch(dest_flat, hp, n_pad, min(256, t))
    ys = _experts(block_expert, n_used, xs, w_exp_gate, w_exp_up, w_exp_down, layer)
    return _combine(dest_flat, ys, wts.T, shared, x1, gate2, ln2_g[None, :], ln2_b[None, :],
                    min(128, seq), seq, alpha)


def kernel(x, c, w_ada, b_ada, w_in, b_forget, conv_w, conv_b, conv_ln_g, conv_ln_b, w_conv_out, w_fox_out,
           w_mix_out, ln1_g, ln1_b, w_router, router_bias, w_exp_gate, w_exp_up, w_exp_down, w_sh_gate,
           w_sh_up, w_sh_down, ln2_g, ln2_b):
    bsz, seq, d = x.shape
    depth = w_ada.shape[0]
    c_pad = jnp.pad(c, ((0, 8 - bsz), (0, 0)))
    x2 = x.reshape(bsz * seq, d)
    for l in range(depth):
        ada = _ada(c_pad, w_ada[l], b_ada[l][None, :])
        x2 = _layer(x2, ada, bsz, seq, w_in[l], b_forget[l], conv_w[l], conv_b[l], conv_ln_g[l], conv_ln_b[l],
                    w_conv_out[l], w_fox_out[l], w_mix_out[l], ln1_g[l], ln1_b[l], w_router[l], router_bias[l],
                    w_exp_gate, w_exp_up, w_exp_down, w_sh_gate[l], w_sh_up[l], w_sh_down[l],
                    ln2_g[l], ln2_b[l], depth, l)
    return x2.reshape(bsz, seq, d)
```

```python
import functools

import jax
import jax.numpy as jnp
import numpy as np
from jax import lax
from jax.experimental import pallas as pl
from jax.experimental.pallas import tpu as pltpu

F32 = jnp.float32
BF16 = jnp.bfloat16

LN_EPS = 1e-5
CONV_WIDTH = 31
FOX_HEADS = 8
FOX_HEAD_DIM = 64
N_GROUPS = 8
TOPK_GROUPS = 4
TOP_K = 8
ROUTED_SCALE = 2.5
EXPERT_BLOCK = 128

V7X_LANES = 128
HEAD_PAD = 128
CONV_HALO = 32
VMEM_LIMIT = 56 * 1024 * 1024
NEG_BIG = -1e30
LOG2E = 1.4426950408889634
ATTN_EXTRA_ROWS = 16
ATTN_HEADS_PER_STEP = 4
ATTN_Q_COLS = 256


def _cparams(sem):
    return pltpu.CompilerParams(dimension_semantics=sem, vmem_limit_bytes=VMEM_LIMIT)


def _ln(v):
    mu = jnp.mean(v, axis=-1, keepdims=True)
    vc = v - mu
    var = jnp.mean(vc * vc, axis=-1, keepdims=True)
    return vc * lax.rsqrt(var + LN_EPS)


def _split3(v):
    hi = v.astype(BF16)
    r1 = v - hi.astype(F32)
    mid = r1.astype(BF16)
    lo = (r1 - mid.astype(F32)).astype(BF16)
    return hi, mid, lo


def _dot(a, b):
    return jnp.dot(a, b, preferred_element_type=F32)


def _dot_nt(a, b):
    return lax.dot_general(a, b, (((1,), (1,)), ((), ())), preferred_element_type=F32)


def _dot_tn(a, b):
    return lax.dot_general(a, b, (((0,), (0,)), ((), ())), preferred_element_type=F32)


def _const_spec(shape):
    nd = len(shape)
    return pl.BlockSpec(shape, lambda *_: (0,) * nd)


def _ada_kernel(c_ref, w_ref, b_ref, o_ref):
    c = c_ref[...]
    cond = c * jax.nn.sigmoid(c)
    ch, cm, _ = _split3(cond)
    w = w_ref[...]
    wh, wm, _ = _split3(w)
    o_ref[...] = _dot(ch, wh) + _dot(ch, wm) + _dot(cm, wh) + b_ref[...]


def _ada(c_pad, w, b):
    rows, d = c_pad.shape
    n = w.shape[1]
    tn = 1024
    return pl.pallas_call(
        _ada_kernel,
        grid=(n // tn,),
        in_specs=[_const_spec((rows, d)),
                  pl.BlockSpec((d, tn), lambda j: (0, j)),
                  pl.BlockSpec((1, tn), lambda j: (0, j))],
        out_specs=pl.BlockSpec((rows, tn), lambda j: (0, j)),
        out_shape=jax.ShapeDtypeStruct((rows, n), F32),
        compiler_params=_cparams(("arbitrary",)),
        name="ada",
    )(c_pad, w, b)


def _inproj_kernel(x_ref, sc_ref, sh_ref, wglu_ref, wq_ref, wk_ref, wvt_ref, wf_ref, wga_ref, wgb_ref,
                   bf_ref, pq_ref, pk_ref,
                   u_ref, q_ref, k_ref, vt_ref, sga_ref, sgb_ref, carry_ref, *, tiles_per_seq, conv_ch):
    i = pl.program_id(0)
    tm = x_ref.shape[0]
    h = _ln(x_ref[...]) * (1.0 + sc_ref[0]) + sh_ref[0]
    hb = h.astype(BF16)

    glu = _dot(hb, wglu_ref[...])
    u_ref[...] = glu[:, :conv_ch] * jax.nn.sigmoid(glu[:, conv_ch:])

    f = _dot(hb, wf_ref[...]) + bf_ref[...]
    logf = jnp.minimum(f, 0.0) - jnp.log(1.0 + jnp.exp(-jnp.abs(f)))
    lh, lm, ll = _split3(logf)
    row = lax.broadcasted_iota(jnp.int32, (tm, tm), 0)
    col = lax.broadcasted_iota(jnp.int32, (tm, tm), 1)
    tri = jnp.where(row >= col, 1.0, 0.0).astype(BF16)
    cs = _dot(tri, lh) + _dot(tri, lm) + _dot(tri, ll)

    @pl.when(i % tiles_per_seq == 0)
    def _():
        carry_ref[...] = jnp.zeros_like(carry_ref)

    cum = cs + carry_ref[...]
    carry_ref[...] = cum[tm - 1:tm, :]

    ch, cm, cl = _split3(cum * LOG2E)
    lane = lax.broadcasted_iota(jnp.int32, cum.shape, 1)
    tail = jnp.where(lane == 24, 1.0, 0.0)
    pieces = jnp.where(lane < 8, ch.astype(F32), jnp.where(lane < 16, cm.astype(F32),
                       jnp.where(lane < 24, cl.astype(F32), tail))).astype(BF16)
    scale = FOX_HEAD_DIM ** -0.5 * LOG2E
    q_ref[...] = (_dot(hb, wq_ref[...]) * scale + _dot(pieces, pq_ref[...])).astype(BF16)
    k_ref[...] = (_dot(hb, wk_ref[...]) + _dot(pieces, pk_ref[...])).astype(BF16)
    vt_ref[0, 0] = _dot_nt(wvt_ref[...], hb).astype(BF16)
    sga_ref[...] = jax.nn.sigmoid(_dot(hb, wga_ref[...])).astype(BF16)
    sgb_ref[...] = jax.nn.sigmoid(_dot(hb, wgb_ref[...])).astype(BF16)


def _inproj(x2, scale1, shift1, wts, tm, seq):
    t, d = x2.shape
    tps = seq // tm
    bsz = t // seq
    wglu, wq, wk, wvt, wf, wga, wgb, bf, pq, pk = wts
    conv_ch = wglu.shape[1] // 2
    fw = wvt.shape[0]
    qw = wq.shape[1]
    mod_spec = pl.BlockSpec((1, 1, d), lambda i: (i // tps, 0, 0))
    row_spec = lambda n: pl.BlockSpec((tm, n), lambda i: (i, 0))
    return pl.pallas_call(
        functools.partial(_inproj_kernel, tiles_per_seq=tps, conv_ch=conv_ch),
        grid=(t // tm,),
        in_specs=[row_spec(d), mod_spec, mod_spec] + [_const_spec(w.shape) for w in wts],
        out_specs=[row_spec(conv_ch), row_spec(qw), row_spec(qw),
                   pl.BlockSpec((1, 1, fw, tm), lambda i: (i // tps, i % tps, 0, 0)),
                   row_spec(d), row_spec(d)],
        out_shape=[jax.ShapeDtypeStruct((t, conv_ch), F32),
                   jax.ShapeDtypeStruct((t, qw), BF16),
                   jax.ShapeDtypeStruct((t, qw), BF16),
                   jax.ShapeDtypeStruct((bsz, tps, fw, tm), BF16),
                   jax.ShapeDtypeStruct((t, d), BF16),
                   jax.ShapeDtypeStruct((t, d), BF16)],
        scratch_shapes=[pltpu.VMEM((1, V7X_LANES), F32)],
        compiler_params=_cparams(("arbitrary",)),
        name="inproj",
    )(x2, scale1, shift1, *wts)


def _conv_kernel(cur_ref, prev_ref, w_ref, b_ref, g_ref, be_ref, wout_ref, sga_ref, o_ref, ext_ref,
                 *, tiles_per_seq, chunk):
    i = pl.program_id(0)
    tm = cur_ref.shape[0]
    first = (i % tiles_per_seq) == 0
    ext_ref[0:CONV_HALO, :] = jnp.where(first, 0.0, prev_ref[...])
    ext_ref[CONV_HALO:, :] = cur_ref[...]
    w = w_ref[...]
    off = CONV_HALO - (CONV_WIDTH - 1)
    outs = []
    for c0 in range(0, tm, chunk):
        acc = jnp.zeros((chunk, cur_ref.shape[1]), F32)
        for j in range(CONV_WIDTH):
            acc = acc + w[j:j + 1, :] * ext_ref[c0 + off + j:c0 + off + j + chunk, :]
        outs.append(acc)
    v = jnp.concatenate(outs, axis=0) + b_ref[...]
    v = _ln(v) * g_ref[...] + be_ref[...]
    v = v * jax.nn.sigmoid(v)
    ya = _dot(v.astype(BF16), wout_ref[...])
    o_ref[...] = (sga_ref[...].astype(F32) * ya).astype(BF16)


def _conv(u, conv_w, conv_b, g, be, wout, sga, tm, seq):
    t, ch = u.shape
    d = wout.shape[1]
    tps = seq // tm
    halo_per_tile = tm // CONV_HALO
    return pl.pallas_call(
        functools.partial(_conv_kernel, tiles_per_seq=tps, chunk=64),
        grid=(t // tm,),
        in_specs=[pl.BlockSpec((tm, ch), lambda i: (i, 0)),
                  pl.BlockSpec((CONV_HALO, ch), lambda i: (jnp.maximum(i * halo_per_tile - 1, 0), 0)),
                  _const_spec(conv_w.shape), _const_spec(conv_b.shape), _const_spec(g.shape),
                  _const_spec(be.shape), _const_spec(wout.shape),
                  pl.BlockSpec((tm, d), lambda i: (i, 0))],
        out_specs=pl.BlockSpec((tm, d), lambda i: (i, 0)),
        out_shape=jax.ShapeDtypeStruct((t, d), BF16),
        scratch_shapes=[pltpu.VMEM((tm + CONV_HALO, ch), F32)],
        compiler_params=_cparams(("arbitrary",)),
        name="conv",
    )(u, u, conv_w, conv_b, g, be, wout, sga)


def _attn_kernel(q_ref, k_ref, vt_ref, o_ref, *, blk, heads):
    qi = pl.program_id(2)
    row = lax.broadcasted_iota(jnp.int32, (ATTN_EXTRA_ROWS, blk), 0)
    ones_rows = jnp.where(row == 0, 1.0, 0.0).astype(BF16)

    ncol = blk // ATTN_Q_COLS
    chains = [(j, c) for j in range(heads) for c in range(ncol)]

    def scores(kj, chain, masked):
        j, c = chain
        k = k_ref[pl.ds(pl.multiple_of(kj * blk, blk), blk), j * HEAD_PAD:(j + 1) * HEAD_PAD]
        q = q_ref[c * ATTN_Q_COLS:(c + 1) * ATTN_Q_COLS, j * HEAD_PAD:(j + 1) * HEAD_PAD]
        s = _dot_nt(k, q)
        if masked:
            kpos = lax.broadcasted_iota(jnp.int32, s.shape, 0)
            qpos = lax.broadcasted_iota(jnp.int32, s.shape, 1) + c * ATTN_Q_COLS
            s = jnp.where(kpos <= qpos, s, NEG_BIG)
        return s

    def probs(s, m):
        m_new = jnp.maximum(m, jnp.max(s, axis=0, keepdims=True))
        return jnp.exp2(s - m_new).astype(BF16), m_new

    def update(kj, chain, p, m, m_new, acc):
        j, _ = chain
        vt = vt_ref[kj, j * FOX_HEAD_DIM:(j + 1) * FOX_HEAD_DIM, :]
        lhs = jnp.concatenate([vt, ones_rows], axis=0)
        return jnp.exp2(m - m_new) * acc + _dot(lhs, p)

    def step(kj, carry, masked):
        n = len(chains)
        s, pm, out = {}, {}, [None] * n
        for i in range(n + 2):
            if i < n:
                s[i] = scores(kj, chains[i], masked)
            if 1 <= i <= n:
                pm[i - 1] = probs(s.pop(i - 1), carry[i - 1][0])
            if i >= 2:
                p, m_new = pm.pop(i - 2)
                m, acc = carry[i - 2]
                out[i - 2] = (m_new, update(kj, chains[i - 2], p, m, m_new, acc))
        return tuple(out)

    init = tuple((jnp.full((1, ATTN_Q_COLS), NEG_BIG, F32),
                  jnp.zeros((FOX_HEAD_DIM + ATTN_EXTRA_ROWS, ATTN_Q_COLS), F32)) for _ in chains)
    carry = lax.fori_loop(0, qi, lambda kj, cr: step(kj, cr, False), init)
    carry = step(qi, carry, True)
    for (j, c), (_, acc) in zip(chains, carry):
        o_ref[j * FOX_HEAD_DIM:(j + 1) * FOX_HEAD_DIM, c * ATTN_Q_COLS:(c + 1) * ATTN_Q_COLS] = (
            acc[:FOX_HEAD_DIM] / acc[FOX_HEAD_DIM:FOX_HEAD_DIM + 1]).astype(BF16)


def _attn(q, k, vt, blk, heads):
    bsz, seq, _ = q.shape
    nkb = seq // blk
    return pl.pallas_call(
        functools.partial(_attn_kernel, blk=blk, heads=heads),
        grid=(bsz, FOX_HEADS // heads, seq // blk),
        in_specs=[pl.BlockSpec((None, blk, heads * HEAD_PAD), lambda b, h, i: (b, i, h)),
                  pl.BlockSpec((None, seq, heads * HEAD_PAD), lambda b, h, i: (b, 0, h)),
                  pl.BlockSpec((None, nkb, heads * FOX_HEAD_DIM, blk), lambda b, h, i: (b, 0, h, 0))],
        out_specs=pl.BlockSpec((None, heads * FOX_HEAD_DIM, blk), lambda b, h, i: (b, h, i)),
        out_shape=jax.ShapeDtypeStruct((bsz, FOX_HEADS * FOX_HEAD_DIM, seq), BF16),
        compiler_params=_cparams(("arbitrary", "arbitrary", "arbitrary")),
        name="attn",
    )(q, k, vt)


def _mix_kernel(ot_ref, gya_ref, sgb_ref, x_ref, g1_ref, sc2_ref, sh2_ref, lg_ref, lb_ref,
                wfox_ref, wmix_ref, wrh_ref, wrl_ref, wsg_ref, wsu_ref, wsd_ref,
                x1_ref, hp_ref, st_ref, shr_ref, *, alpha):
    yb = _dot_tn(ot_ref[...], wfox_ref[...])
    merged = gya_ref[...].astype(F32) + sgb_ref[...].astype(F32) * yb
    y = _dot(merged.astype(BF16), wmix_ref[...])
    x1 = _ln(alpha * x_ref[...] + (1.0 + g1_ref[0]) * y) * lg_ref[...] + lb_ref[...]
    x1_ref[...] = x1
    h2 = _ln(x1) * (1.0 + sc2_ref[0]) + sh2_ref[0]
    hb = h2.astype(BF16)
    hl = (h2 - hb.astype(F32)).astype(BF16)

    half = hb.shape[1] // 2
    lo_bits = lax.bitcast_convert_type(hb[:, :half].astype(F32), jnp.uint32)
    hi_bits = lax.bitcast_convert_type(hb[:, half:].astype(F32), jnp.uint32)
    hp_ref[...] = (lo_bits >> 16) | (hi_bits & jnp.uint32(0xFFFF0000))

    logits_t = _dot_nt(wrh_ref[...], hb) + _dot_nt(wrl_ref[...], hb) + _dot_nt(wrh_ref[...], hl)
    st_ref[...] = jax.nn.sigmoid(logits_t)

    g = _dot(hb, wsg_ref[...])
    u = _dot(hb, wsu_ref[...])
    a = (g * jax.nn.sigmoid(g) * u).astype(BF16)
    shr_ref[...] = _dot(a, wsd_ref[...])


def _mix(ot, gya, sgb, x2, gate1, scale2, shift2, lg, lb, wts, tm, seq, alpha):
    t, d = x2.shape
    tps = seq // tm
    fw = ot.shape[1]
    n_exp = wts[2].shape[0]
    mod_spec = pl.BlockSpec((1, 1, d), lambda i: (i // tps, 0, 0))
    row_spec = lambda n: pl.BlockSpec((tm, n), lambda i: (i, 0))
    return pl.pallas_call(
        functools.partial(_mix_kernel, alpha=alpha),
        grid=(t // tm,),
        in_specs=[pl.BlockSpec((None, fw, tm), lambda i: (i // tps, 0, i % tps)),
                  row_spec(d), row_spec(d), row_spec(d), mod_spec, mod_spec, mod_spec,
                  _const_spec(lg.shape), _const_spec(lb.shape)] + [_const_spec(w.shape) for w in wts],
        out_specs=[row_spec(d), row_spec(d // 2), pl.BlockSpec((n_exp, tm), lambda i: (0, i)), row_spec(d)],
        out_shape=[jax.ShapeDtypeStruct((t, d), F32),
                   jax.ShapeDtypeStruct((t, d // 2), jnp.uint32),
                   jax.ShapeDtypeStruct((n_exp, t), F32),
                   jax.ShapeDtypeStruct((t, d), F32)],
        compiler_params=_cparams(("arbitrary",)),
        name="mix",
    )(ot, gya, sgb, x2, gate1, scale2, shift2, lg, lb, *wts)


def _route_kernel(st_ref, bias_ref, idx_ref, wts_ref, rank_ref, cnt_ref, carry_ref):
    i = pl.program_id(0)
    n_exp, tr = st_ref.shape
    gsz = n_exp // N_GROUPS
    neg_inf = -jnp.inf

    @pl.when(i == 0)
    def _():
        carry_ref[...] = jnp.zeros_like(carry_ref)

    shape3 = (N_GROUPS, gsz, tr)

    def max01(v):
        return jnp.max(jnp.max(v, axis=1, keepdims=True), axis=0, keepdims=True)

    def min01(v):
        return jnp.min(jnp.min(v, axis=1, keepdims=True), axis=0, keepdims=True)

    def sum01(v):
        return jnp.sum(jnp.sum(v, axis=1, keepdims=True), axis=0, keepdims=True)

    sc = st_ref[...].reshape(shape3)
    gsel = (st_ref[...] + bias_ref[...]).reshape(shape3)
    pos = lax.broadcasted_iota(jnp.int32, shape3, 1)
    m1 = jnp.max(gsel, axis=1, keepdims=True)
    i1 = jnp.min(jnp.where(gsel == m1, pos, gsz), axis=1, keepdims=True)
    m2 = jnp.max(jnp.where(pos == i1, neg_inf, gsel), axis=1, keepdims=True)
    gs = m1 + m2

    gid = lax.broadcasted_iota(jnp.int32, gs.shape, 0)
    gkeep = jnp.zeros(gs.shape, F32)
    for _ in range(TOPK_GROUPS):
        mx = jnp.max(gs, axis=0, keepdims=True)
        gi = jnp.min(jnp.where(gs == mx, gid, N_GROUPS), axis=0, keepdims=True)
        hit = gid == gi
        gkeep = gkeep + jnp.where(hit, 1.0, 0.0)
        gs = jnp.where(hit, neg_inf, gs)

    cur = jnp.where(jnp.broadcast_to(gkeep, shape3) > 0.5, gsel, neg_inf)
    eid = lax.broadcasted_iota(jnp.int32, shape3, 0) * gsz + pos
    onehot = jnp.zeros(shape3, F32)
    idxs, ws = [], []
    wsum = jnp.zeros((1, 1, tr), F32)
    for _ in range(TOP_K):
        mx = max01(cur)
        ik = min01(jnp.where(cur == mx, eid, n_exp))
        hit = eid == ik
        wk = sum01(jnp.where(hit, sc, 0.0))
        idxs.append(ik)
        ws.append(wk)
        wsum = wsum + wk
        onehot = onehot + jnp.where(hit, 1.0, 0.0)
        cur = jnp.where(hit, neg_inf, cur)

    ra = lax.broadcasted_iota(jnp.int32, (tr, tr), 0)
    rb = lax.broadcasted_iota(jnp.int32, (tr, tr), 1)
    upper = jnp.where(ra < rb, 1.0, 0.0).astype(BF16)
    onehot2 = onehot.reshape(n_exp, tr)
    prior = (_dot(onehot2.astype(BF16), upper) + carry_ref[...]).reshape(shape3)
    for slot in range(TOP_K):
        idx_ref[slot:slot + 1, :] = idxs[slot].reshape(1, tr)
        wts_ref[slot:slot + 1, :] = (ws[slot] / wsum * ROUTED_SCALE).reshape(1, tr)
        rk = sum01(jnp.where(eid == idxs[slot], prior, 0.0))
        rank_ref[slot:slot + 1, :] = rk.reshape(1, tr).astype(jnp.int32)
    total = carry_ref[...] + jnp.sum(onehot2, axis=1, keepdims=True)
    carry_ref[...] = total
    cnt_ref[...] = jnp.broadcast_to(total, cnt_ref.shape).astype(jnp.int32)


def _route(scores_t, bias_col, tr):
    n_exp, t = scores_t.shape
    slot_spec = pl.BlockSpec((TOP_K, tr), lambda i: (0, i))
    return pl.pallas_call(
        _route_kernel,
        grid=(t // tr,),
        in_specs=[pl.BlockSpec((n_exp, tr), lambda i: (0, i)), _const_spec(bias_col.shape)],
        out_specs=[slot_spec, slot_spec, slot_spec, _const_spec((n_exp, V7X_LANES))],
        out_shape=[jax.ShapeDtypeStruct((TOP_K, t), jnp.int32),
                   jax.ShapeDtypeStruct((TOP_K, t), F32),
                   jax.ShapeDtypeStruct((TOP_K, t), jnp.int32),
                   jax.ShapeDtypeStruct((n_exp, V7X_LANES), jnp.int32)],
        scratch_shapes=[pltpu.VMEM((n_exp, 1), F32)],
        compiler_params=_cparams(("arbitrary",)),
        name="route",
    )(scores_t, bias_col)


def _dest_kernel(pstart_ref, idx_ref, rank_ref, o_ref):
    idx = idx_ref[...]
    n_exp = pstart_ref.shape[0]

    def body(e, acc):
        return acc + jnp.where(idx == e, pstart_ref[e], 0)

    o_ref[...] = lax.fori_loop(0, n_exp, body, rank_ref[...])


def _dest(pstart, idx, rank, tr):
    k, t = idx.shape
    spec = pl.BlockSpec((k, tr), lambda i, ps: (0, i))
    return pl.pallas_call(
        _dest_kernel,
        grid_spec=pltpu.PrefetchScalarGridSpec(
            num_scalar_prefetch=1, grid=(t // tr,), in_specs=[spec, spec], out_specs=spec),
        out_shape=jax.ShapeDtypeStruct((k, t), jnp.int32),
        compiler_params=_cparams(("arbitrary",)),
        name="dest",
    )(pstart, idx, rank)


def _dispatch_kernel(dest_ref, hp_ref, xs_ref, sem):
    td = hp_ref.shape[0]

    def row_copy(r, slot):
        return pltpu.make_async_copy(hp_ref.at[pl.ds(r, 1)], xs_ref.at[pl.ds(dest_ref[r * TOP_K + slot], 1)], sem)

    def start(r, c):
        for slot in range(TOP_K):
            row_copy(r, slot).start()
        return c

    def wait(r, c):
        for slot in range(TOP_K):
            row_copy(r, slot).wait()
        return c

    lax.fori_loop(0, td, start, 0)
    lax.fori_loop(0, td, wait, 0)


def _dispatch(dest_flat, hp, n_pad, td):
    t, w = hp.shape
    return pl.pallas_call(
        _dispatch_kernel,
        grid=(t // td,),
        in_specs=[pl.BlockSpec((td * TOP_K,), lambda i: (i,), memory_space=pltpu.SMEM),
                  pl.BlockSpec((td, w), lambda i: (i, 0))],
        out_specs=pl.BlockSpec(memory_space=pl.ANY),
        out_shape=jax.ShapeDtypeStruct((n_pad, w), hp.dtype),
        scratch_shapes=[pltpu.SemaphoreType.DMA(())],
        compiler_params=_cparams(("arbitrary",)),
        name="dispatch",
    )(dest_flat, hp)


def _expert_kernel(be_ref, rows_ref, nused_ref, xs_ref, wg_ref, wu_ref, wd_ref, ys_ref, wg_s, wu_s, wd_s):
    p = pl.program_id(0)
    prev = be_ref[jnp.maximum(p - 1, 0)]
    fresh = jnp.logical_or(p == 0, be_ref[p] != prev)
    live = p < nused_ref[0]

    @pl.when(jnp.logical_and(live, fresh))
    def _():
        wg_s[...] = wg_ref[...].astype(BF16)
        wu_s[...] = wu_ref[...].astype(BF16)
        wd_s[...] = wd_ref[...].astype(BF16)

    @pl.when(live)
    def _():
        words = xs_ref[...]
        valid = lax.broadcasted_iota(jnp.int32, words.shape, 0) < rows_ref[p]
        words = jnp.where(valid, words, jnp.uint32(0))
        lo = lax.bitcast_convert_type(words << 16, F32)
        hi = lax.bitcast_convert_type(words & jnp.uint32(0xFFFF0000), F32)
        x = jnp.concatenate([lo, hi], axis=1).astype(BF16)
        g = _dot(x, wg_s[...])
        u = _dot(x, wu_s[...])
        a = (g * jax.nn.sigmoid(g) * u).astype(BF16)
        ys_ref[...] = _dot(a, wd_s[...])


def _experts(block_expert, block_rows, n_used, xs, wg, wu, wd, layer):
    n_pad, w = xs.shape
    _, n_exp, d, f = wg.shape
    n_blocks = n_pad // EXPERT_BLOCK

    def xmap(p, be, rows, nu):
        return (jnp.minimum(p, nu[0] - 1), 0)

    def wmap(p, be, rows, nu):
        return (layer, be[jnp.minimum(p, nu[0] - 1)], 0, 0)

    return pl.pallas_call(
        _expert_kernel,
        grid_spec=pltpu.PrefetchScalarGridSpec(
            num_scalar_prefetch=3, grid=(n_blocks,),
            in_specs=[pl.BlockSpec((EXPERT_BLOCK, w), xmap),
                      pl.BlockSpec((None, None, d, f), wmap),
                      pl.BlockSpec((None, None, d, f), wmap),
                      pl.BlockSpec((None, None, f, d), wmap)],
            out_specs=pl.BlockSpec((EXPERT_BLOCK, d), xmap),
            scratch_shapes=[pltpu.VMEM((d, f), BF16), pltpu.VMEM((d, f), BF16), pltpu.VMEM((f, d), BF16)]),
        out_shape=jax.ShapeDtypeStruct((n_pad, d), F32),
        compiler_params=_cparams(("arbitrary",)),
        name="experts",
    )(block_expert, block_rows, n_used, xs, wg, wu, wd)


def _combine_kernel(dest_ref, ys_ref, w_ref, shr_ref, x1_ref, g2_ref, lg_ref, lb_ref, o_ref, buf, sem, *, alpha):
    tc = x1_ref.shape[0]

    def row_copy(r, slot):
        return pltpu.make_async_copy(ys_ref.at[pl.ds(dest_ref[r * TOP_K + slot], 1)],
                                     buf.at[slot, pl.ds(r, 1)], sem)

    def start(r, c):
        for slot in range(TOP_K):
            row_copy(r, slot).start()
        return c

    def wait(r, c):
        for slot in range(TOP_K):
            row_copy(r, slot).wait()
        return c

    lax.fori_loop(0, tc, start, 0)
    lax.fori_loop(0, tc, wait, 0)

    w = w_ref[...]
    y = shr_ref[...]
    for slot in range(TOP_K):
        y = y + buf[slot] * w[:, slot:slot + 1]
    z = alpha * x1_ref[...] + (1.0 + g2_ref[0]) * y
    o_ref[...] = _ln(z) * lg_ref[...] + lb_ref[...]


def _combine(dest_flat, ys, wts_tk, shared, x1, gate2, lg, lb, tc, seq, alpha):
    t, d = x1.shape
    tps = seq // tc
    row_spec = lambda n: pl.BlockSpec((tc, n), lambda i: (i, 0))
    return pl.pallas_call(
        functools.partial(_combine_kernel, alpha=alpha),
        grid=(t // tc,),
        in_specs=[pl.BlockSpec((tc * TOP_K,), lambda i: (i,), memory_space=pltpu.SMEM),
                  pl.BlockSpec(memory_space=pl.ANY),
                  row_spec(TOP_K), row_spec(d), row_spec(d),
                  pl.BlockSpec((1, 1, d), lambda i: (i // tps, 0, 0)),
                  _const_spec(lg.shape), _const_spec(lb.shape)],
        out_specs=row_spec(d),
        out_shape=jax.ShapeDtypeStruct((t, d), F32),
        scratch_shapes=[pltpu.VMEM((TOP_K, tc, d), F32), pltpu.SemaphoreType.DMA(())],
        compiler_params=_cparams(("arbitrary",)),
        name="combine",
    )(dest_flat, ys, wts_tk, shared, x1, gate2, lg, lb)


def _placement():
    pq = np.zeros((V7X_LANES, FOX_HEADS * HEAD_PAD), np.float32)
    pk = np.zeros((V7X_LANES, FOX_HEADS * HEAD_PAD), np.float32)
    for h in range(FOX_HEADS):
        base = h * HEAD_PAD + FOX_HEAD_DIM
        for piece in range(3):
            pq[piece * 8 + h, base + piece] = 1.0
            pk[24, base + piece] = 1.0
            pq[24, base + 3 + piece] = 1.0
            pk[piece * 8 + h, base + 3 + piece] = -1.0
    return jnp.asarray(pq, BF16), jnp.asarray(pk, BF16)


def _inproj_weights(w_in, b_forget, d):
    conv2 = d
    fw = FOX_HEADS * FOX_HEAD_DIM
    o1, o2, o3, o4 = conv2, conv2 + fw, conv2 + 2 * fw, conv2 + 3 * fw
    o5 = o4 + FOX_HEADS
    o6 = o5 + d

    def pad_heads(w):
        w = w.reshape(d, FOX_HEADS, FOX_HEAD_DIM)
        w = jnp.pad(w, ((0, 0), (0, 0), (0, HEAD_PAD - FOX_HEAD_DIM)))
        return w.reshape(d, FOX_HEADS * HEAD_PAD).astype(BF16)

    wglu = w_in[:, :o1].astype(BF16)
    wq = pad_heads(w_in[:, o1:o2])
    wk = pad_heads(w_in[:, o2:o3])
    wvt = w_in[:, o3:o4].T.astype(BF16)
    wf8 = w_in[:, o4:o5]
    wf = jnp.pad(jnp.concatenate([wf8, wf8, wf8], axis=1), ((0, 0), (0, V7X_LANES - 3 * FOX_HEADS))).astype(BF16)
    bf = jnp.pad(jnp.concatenate([b_forget, b_forget, b_forget]), (0, V7X_LANES - 3 * FOX_HEADS))[None, :].astype(F32)
    wga = w_in[:, o5:o6].astype(BF16)
    wgb = w_in[:, o6:].astype(BF16)
    pq, pk = _placement()
    return (wglu, wq, wk, wvt, wf, wga, wgb, bf, pq, pk)


def _layer(x2, ada, bsz, seq, w_in, b_forget, conv_w, conv_b, conv_ln_g, conv_ln_b, w_conv_out, w_fox_out,
           w_mix_out, ln1_g, ln1_b, w_router, router_bias, w_exp_gate, w_exp_up, w_exp_down,
           w_sh_gate, w_sh_up, w_sh_down, ln2_g, ln2_b, depth, layer):
    t, d = x2.shape
    n_exp = w_router.shape[1]
    alpha = (2.0 * depth) ** 0.25
    mods = [ada[:bsz, j * d:(j + 1) * d][:, None, :] for j in range(6)]
    shift1, scale1, gate1, shift2, scale2, gate2 = mods

    tm = min(512, seq)
    u, q, k, vt, sga, sgb = _inproj(x2, scale1, shift1, _inproj_weights(w_in, b_forget, d), tm, seq)

    tcv = min(256, seq)
    conv_w_pad = jnp.pad(conv_w, ((0, CONV_HALO - CONV_WIDTH), (0, 0)))
    gya = _conv(u, conv_w_pad, conv_b[None, :], conv_ln_g[None, :], conv_ln_b[None, :],
                w_conv_out.astype(BF16), sga, tcv, seq)

    ot = _attn(q.reshape(bsz, seq, -1), k.reshape(bsz, seq, -1), vt, tm, ATTN_HEADS_PER_STEP)

    tmx = min(256, seq)
    wr_t = w_router.T
    wr_h = wr_t.astype(BF16)
    wr_l = (wr_t - wr_h.astype(F32)).astype(BF16)
    mix_w = (w_fox_out.astype(BF16), w_mix_out.astype(BF16), wr_h, wr_l,
             w_sh_gate.astype(BF16), w_sh_up.astype(BF16), w_sh_down.astype(BF16))
    x1, hp, scores_t, shared = _mix(ot, gya, sgb, x2, gate1, scale2, shift2, ln1_g[None, :], ln1_b[None, :],
                                    mix_w, tmx, seq, alpha)

    tr = min(512, t)
    idx, wts, rank, cnt = _route(scores_t, router_bias[:, None], tr)

    counts = cnt[:, 0]
    padded = (counts + EXPERT_BLOCK - 1) // EXPERT_BLOCK * EXPERT_BLOCK
    pend = jnp.cumsum(padded)
    pstart = (pend - padded).astype(jnp.int32)
    n_assign = t * TOP_K
    n_pad = -(-(n_assign + n_exp * (EXPERT_BLOCK - 1)) // EXPERT_BLOCK) * EXPERT_BLOCK
    n_blocks = n_pad // EXPERT_BLOCK
    block_start = jnp.arange(n_blocks, dtype=jnp.int32) * EXPERT_BLOCK
    block_expert = jnp.minimum(jnp.sum(pend[None, :] <= block_start[:, None], axis=1), n_exp - 1).astype(jnp.int32)
    block_rows = jnp.clip((pstart + counts)[block_expert] - block_start, 0, EXPERT_BLOCK).astype(jnp.int32)
    n_used = (pend[-1:] // EXPERT_BLOCK).astype(jnp.int32)

    dest = _dest(pstart, idx, rank, tr)
    dest_flat = dest.T.reshape(-1)
    xs = _dispatch(dest_flat, hp, n_pad, min(256, t))
    ys = _experts(block_expert, block_rows, n_used, xs, w_exp_gate, w_exp_up, w_exp_down, layer)
    return _combine(dest_flat, ys, wts.T, shared, x1, gate2, ln2_g[None, :], ln2_b[None, :],
                    min(128, seq), seq, alpha)


def kernel(x, c, w_ada, b_ada, w_in, b_forget, conv_w, conv_b, conv_ln_g, conv_ln_b, w_conv_out, w_fox_out,
           w_mix_out, ln1_g, ln1_b, w_router, router_bias, w_exp_gate, w_exp_up, w_exp_down, w_sh_gate,
           w_sh_up, w_sh_down, ln2_g, ln2_b):
    bsz, seq, d = x.shape
    depth = w_ada.shape[0]
    c_pad = jnp.pad(c, ((0, 8 - bsz), (0, 0)))
    x2 = x.reshape(bsz * seq, d)
    for l in range(depth):
        ada = _ada(c_pad, w_ada[l], b_ada[l][None, :])
        x2 = _layer(x2, ada, bsz, seq, w_in[l], b_forget[l], conv_w[l], conv_b[l], conv_ln_g[l], conv_ln_b[l],
                    w_conv_out[l], w_fox_out[l], w_mix_out[l], ln1_g[l], ln1_b[l], w_router[l], router_bias[l],
                    w_exp_gate, w_exp_up, w_exp_down, w_sh_gate[l], w_sh_up[l], w_sh_down[l],
                    ln2_g[l], ln2_b[l], depth, l)
    return x2.reshape(bsz, seq, d)
```

```python
import functools

import jax
import jax.numpy as jnp
import numpy as np
from jax import lax
from jax.experimental import pallas as pl
from jax.experimental.pallas import tpu as pltpu

F32 = jnp.float32
BF16 = jnp.bfloat16

LN_EPS = 1e-5
CONV_WIDTH = 31
FOX_HEADS = 8
FOX_HEAD_DIM = 64
N_GROUPS = 8
TOPK_GROUPS = 4
TOP_K = 8
ROUTED_SCALE = 2.5
EXPERT_BLOCK = 256
ROW_SLABS = 4

V7X_LANES = 128
HEAD_PAD = 128
CONV_HALO = 32
VMEM_LIMIT = 56 * 1024 * 1024
NEG_BIG = -1e30
LOG2E = 1.4426950408889634
ATTN_EXTRA_ROWS = 16
ATTN_HEADS_PER_STEP = 4
ATTN_Q_COLS = 256


def _cparams(sem):
    return pltpu.CompilerParams(dimension_semantics=sem, vmem_limit_bytes=VMEM_LIMIT)


def _ln(v):
    mu = jnp.mean(v, axis=-1, keepdims=True)
    vc = v - mu
    var = jnp.mean(vc * vc, axis=-1, keepdims=True)
    return vc * lax.rsqrt(var + LN_EPS)


def _split3(v):
    hi = v.astype(BF16)
    r1 = v - hi.astype(F32)
    mid = r1.astype(BF16)
    lo = (r1 - mid.astype(F32)).astype(BF16)
    return hi, mid, lo


def _dot(a, b):
    return jnp.dot(a, b, preferred_element_type=F32)


def _dot_nt(a, b):
    return lax.dot_general(a, b, (((1,), (1,)), ((), ())), preferred_element_type=F32)


def _dot_tn(a, b):
    return lax.dot_general(a, b, (((0,), (0,)), ((), ())), preferred_element_type=F32)


def _store_packed_rows(ref, v):
    n, d = v.shape
    half = d // 2
    vb = v.astype(BF16).astype(F32)
    lo_bits = lax.bitcast_convert_type(vb[:, :half], jnp.uint32)
    hi_bits = lax.bitcast_convert_type(vb[:, half:], jnp.uint32)
    words = (lo_bits >> 16) | (hi_bits & jnp.uint32(0xFFFF0000))
    for c in range(ROW_SLABS):
        ref[pl.ds(c, n, stride=ROW_SLABS), :] = words[:, c * V7X_LANES:(c + 1) * V7X_LANES]


def _load_packed_rows(ref, n, lead=()):
    lo, hi = [], []
    for c in range(ROW_SLABS):
        w = ref[lead + (pl.ds(c, n, stride=ROW_SLABS), slice(None))]
        lo.append(lax.bitcast_convert_type(w << 16, F32))
        hi.append(lax.bitcast_convert_type(w & jnp.uint32(0xFFFF0000), F32))
    return lo + hi


def _const_spec(shape):
    nd = len(shape)
    return pl.BlockSpec(shape, lambda *_: (0,) * nd)


def _ada_kernel(c_ref, w_ref, b_ref, o_ref):
    c = c_ref[...]
    cond = c * jax.nn.sigmoid(c)
    ch, cm, _ = _split3(cond)
    w = w_ref[...]
    wh, wm, _ = _split3(w)
    o_ref[...] = _dot(ch, wh) + _dot(ch, wm) + _dot(cm, wh) + b_ref[...]


def _ada(c_pad, w, b):
    rows, d = c_pad.shape
    n = w.shape[1]
    tn = 1024
    return pl.pallas_call(
        _ada_kernel,
        grid=(n // tn,),
        in_specs=[_const_spec((rows, d)),
                  pl.BlockSpec((d, tn), lambda j: (0, j)),
                  pl.BlockSpec((1, tn), lambda j: (0, j))],
        out_specs=pl.BlockSpec((rows, tn), lambda j: (0, j)),
        out_shape=jax.ShapeDtypeStruct((rows, n), F32),
        compiler_params=_cparams(("arbitrary",)),
        name="ada",
    )(c_pad, w, b)


def _inproj_kernel(x_ref, sc_ref, sh_ref, wglu_ref, wq_ref, wk_ref, wvt_ref, wf_ref, wga_ref, wgb_ref,
                   bf_ref, pq_ref, pk_ref,
                   u_ref, q_ref, k_ref, vt_ref, sga_ref, sgb_ref, carry_ref, *, tiles_per_seq, conv_ch):
    i = pl.program_id(0)
    tm = x_ref.shape[0]
    h = _ln(x_ref[...]) * (1.0 + sc_ref[0]) + sh_ref[0]
    hb = h.astype(BF16)

    glu = _dot(hb, wglu_ref[...])
    u_ref[...] = glu[:, :conv_ch] * jax.nn.sigmoid(glu[:, conv_ch:])

    f = _dot(hb, wf_ref[...]) + bf_ref[...]
    logf = jnp.minimum(f, 0.0) - jnp.log(1.0 + jnp.exp(-jnp.abs(f)))
    lh, lm, ll = _split3(logf)
    row = lax.broadcasted_iota(jnp.int32, (tm, tm), 0)
    col = lax.broadcasted_iota(jnp.int32, (tm, tm), 1)
    tri = jnp.where(row >= col, 1.0, 0.0).astype(BF16)
    cs = _dot(tri, lh) + _dot(tri, lm) + _dot(tri, ll)

    @pl.when(i % tiles_per_seq == 0)
    def _():
        carry_ref[...] = jnp.zeros_like(carry_ref)

    cum = cs + carry_ref[...]
    carry_ref[...] = cum[tm - 1:tm, :]

    ch, cm, cl = _split3(cum * LOG2E)
    lane = lax.broadcasted_iota(jnp.int32, cum.shape, 1)
    tail = jnp.where(lane == 24, 1.0, 0.0)
    pieces = jnp.where(lane < 8, ch.astype(F32), jnp.where(lane < 16, cm.astype(F32),
                       jnp.where(lane < 24, cl.astype(F32), tail))).astype(BF16)
    scale = FOX_HEAD_DIM ** -0.5 * LOG2E
    q_ref[...] = (_dot(hb, wq_ref[...]) * scale + _dot(pieces, pq_ref[...])).astype(BF16)
    k_ref[...] = (_dot(hb, wk_ref[...]) + _dot(pieces, pk_ref[...])).astype(BF16)
    vt_ref[0, 0] = _dot_nt(wvt_ref[...], hb).astype(BF16)
    sga_ref[...] = jax.nn.sigmoid(_dot(hb, wga_ref[...])).astype(BF16)
    sgb_ref[...] = jax.nn.sigmoid(_dot(hb, wgb_ref[...])).astype(BF16)


def _inproj(x2, scale1, shift1, wts, tm, seq):
    t, d = x2.shape
    tps = seq // tm
    bsz = t // seq
    wglu, wq, wk, wvt, wf, wga, wgb, bf, pq, pk = wts
    conv_ch = wglu.shape[1] // 2
    fw = wvt.shape[0]
    qw = wq.shape[1]
    mod_spec = pl.BlockSpec((1, 1, d), lambda i: (i // tps, 0, 0))
    row_spec = lambda n: pl.BlockSpec((tm, n), lambda i: (i, 0))
    return pl.pallas_call(
        functools.partial(_inproj_kernel, tiles_per_seq=tps, conv_ch=conv_ch),
        grid=(t // tm,),
        in_specs=[row_spec(d), mod_spec, mod_spec] + [_const_spec(w.shape) for w in wts],
        out_specs=[row_spec(conv_ch), row_spec(qw), row_spec(qw),
                   pl.BlockSpec((1, 1, fw, tm), lambda i: (i // tps, i % tps, 0, 0)),
                   row_spec(d), row_spec(d)],
        out_shape=[jax.ShapeDtypeStruct((t, conv_ch), F32),
                   jax.ShapeDtypeStruct((t, qw), BF16),
                   jax.ShapeDtypeStruct((t, qw), BF16),
                   jax.ShapeDtypeStruct((bsz, tps, fw, tm), BF16),
                   jax.ShapeDtypeStruct((t, d), BF16),
                   jax.ShapeDtypeStruct((t, d), BF16)],
        scratch_shapes=[pltpu.VMEM((1, V7X_LANES), F32)],
        compiler_params=_cparams(("arbitrary",)),
        name="inproj",
    )(x2, scale1, shift1, *wts)


def _conv_kernel(cur_ref, prev_ref, w_ref, b_ref, g_ref, be_ref, wout_ref, sga_ref, o_ref, ext_ref,
                 *, tiles_per_seq, chunk):
    i = pl.program_id(0)
    tm = cur_ref.shape[0]
    first = (i % tiles_per_seq) == 0
    ext_ref[0:CONV_HALO, :] = jnp.where(first, 0.0, prev_ref[...])
    ext_ref[CONV_HALO:, :] = cur_ref[...]
    w = w_ref[...]
    off = CONV_HALO - (CONV_WIDTH - 1)
    outs = []
    for c0 in range(0, tm, chunk):
        acc = jnp.zeros((chunk, cur_ref.shape[1]), F32)
        for j in range(CONV_WIDTH):
            acc = acc + w[j:j + 1, :] * ext_ref[c0 + off + j:c0 + off + j + chunk, :]
        outs.append(acc)
    v = jnp.concatenate(outs, axis=0) + b_ref[...]
    v = _ln(v) * g_ref[...] + be_ref[...]
    v = v * jax.nn.sigmoid(v)
    ya = _dot(v.astype(BF16), wout_ref[...])
    o_ref[...] = (sga_ref[...].astype(F32) * ya).astype(BF16)


def _conv(u, conv_w, conv_b, g, be, wout, sga, tm, seq):
    t, ch = u.shape
    d = wout.shape[1]
    tps = seq // tm
    halo_per_tile = tm // CONV_HALO
    return pl.pallas_call(
        functools.partial(_conv_kernel, tiles_per_seq=tps, chunk=64),
        grid=(t // tm,),
        in_specs=[pl.BlockSpec((tm, ch), lambda i: (i, 0)),
                  pl.BlockSpec((CONV_HALO, ch), lambda i: (jnp.maximum(i * halo_per_tile - 1, 0), 0)),
                  _const_spec(conv_w.shape), _const_spec(conv_b.shape), _const_spec(g.shape),
                  _const_spec(be.shape), _const_spec(wout.shape),
                  pl.BlockSpec((tm, d), lambda i: (i, 0))],
        out_specs=pl.BlockSpec((tm, d), lambda i: (i, 0)),
        out_shape=jax.ShapeDtypeStruct((t, d), BF16),
        scratch_shapes=[pltpu.VMEM((tm + CONV_HALO, ch), F32)],
        compiler_params=_cparams(("arbitrary",)),
        name="conv",
    )(u, u, conv_w, conv_b, g, be, wout, sga)


def _attn_kernel(q_ref, k_ref, vt_ref, o_ref, *, blk, heads):
    qi = pl.program_id(2)
    row = lax.broadcasted_iota(jnp.int32, (ATTN_EXTRA_ROWS, blk), 0)
    ones_rows = jnp.where(row == 0, 1.0, 0.0).astype(BF16)

    ncol = blk // ATTN_Q_COLS
    chains = [(j, c) for j in range(heads) for c in range(ncol)]

    def scores(kj, chain, masked):
        j, c = chain
        k = k_ref[pl.ds(pl.multiple_of(kj * blk, blk), blk), j * HEAD_PAD:(j + 1) * HEAD_PAD]
        q = q_ref[c * ATTN_Q_COLS:(c + 1) * ATTN_Q_COLS, j * HEAD_PAD:(j + 1) * HEAD_PAD]
        s = _dot_nt(k, q)
        if masked:
            kpos = lax.broadcasted_iota(jnp.int32, s.shape, 0)
            qpos = lax.broadcasted_iota(jnp.int32, s.shape, 1) + c * ATTN_Q_COLS
            s = jnp.where(kpos <= qpos, s, NEG_BIG)
        return s

    def probs(s, m):
        m_new = jnp.maximum(m, jnp.max(s, axis=0, keepdims=True))
        return jnp.exp2(s - m_new).astype(BF16), m_new

    def update(kj, chain, p, m, m_new, acc):
        j, _ = chain
        vt = vt_ref[kj, j * FOX_HEAD_DIM:(j + 1) * FOX_HEAD_DIM, :]
        lhs = jnp.concatenate([vt, ones_rows], axis=0)
        return jnp.exp2(m - m_new) * acc + _dot(lhs, p)

    def step(kj, carry, masked):
        n = len(chains)
        s, pm, out = {}, {}, [None] * n
        for i in range(n + 2):
            if i < n:
                s[i] = scores(kj, chains[i], masked)
            if 1 <= i <= n:
                pm[i - 1] = probs(s.pop(i - 1), carry[i - 1][0])
            if i >= 2:
                p, m_new = pm.pop(i - 2)
                m, acc = carry[i - 2]
                out[i - 2] = (m_new, update(kj, chains[i - 2], p, m, m_new, acc))
        return tuple(out)

    init = tuple((jnp.full((1, ATTN_Q_COLS), NEG_BIG, F32),
                  jnp.zeros((FOX_HEAD_DIM + ATTN_EXTRA_ROWS, ATTN_Q_COLS), F32)) for _ in chains)
    carry = lax.fori_loop(0, qi, lambda kj, cr: step(kj, cr, False), init)
    carry = step(qi, carry, True)
    for (j, c), (_, acc) in zip(chains, carry):
        o_ref[j * FOX_HEAD_DIM:(j + 1) * FOX_HEAD_DIM, c * ATTN_Q_COLS:(c + 1) * ATTN_Q_COLS] = (
            acc[:FOX_HEAD_DIM] / acc[FOX_HEAD_DIM:FOX_HEAD_DIM + 1]).astype(BF16)


def _attn(q, k, vt, blk, heads):
    bsz, seq, _ = q.shape
    nkb = seq // blk
    return pl.pallas_call(
        functools.partial(_attn_kernel, blk=blk, heads=heads),
        grid=(bsz, FOX_HEADS // heads, seq // blk),
        in_specs=[pl.BlockSpec((None, blk, heads * HEAD_PAD), lambda b, h, i: (b, i, h)),
                  pl.BlockSpec((None, seq, heads * HEAD_PAD), lambda b, h, i: (b, 0, h)),
                  pl.BlockSpec((None, nkb, heads * FOX_HEAD_DIM, blk), lambda b, h, i: (b, 0, h, 0))],
        out_specs=pl.BlockSpec((None, heads * FOX_HEAD_DIM, blk), lambda b, h, i: (b, h, i)),
        out_shape=jax.ShapeDtypeStruct((bsz, FOX_HEADS * FOX_HEAD_DIM, seq), BF16),
        compiler_params=_cparams(("arbitrary", "arbitrary", "arbitrary")),
        name="attn",
    )(q, k, vt)


def _mix_kernel(ot_ref, gya_ref, sgb_ref, x_ref, g1_ref, sc2_ref, sh2_ref, lg_ref, lb_ref,
                wfox_ref, wmix_ref, wrh_ref, wrl_ref, wsg_ref, wsu_ref, wsd_ref,
                x1_ref, hp_ref, st_ref, shr_ref, *, alpha):
    yb = _dot_tn(ot_ref[...], wfox_ref[...])
    merged = gya_ref[...].astype(F32) + sgb_ref[...].astype(F32) * yb
    y = _dot(merged.astype(BF16), wmix_ref[...])
    x1 = _ln(alpha * x_ref[...] + (1.0 + g1_ref[0]) * y) * lg_ref[...] + lb_ref[...]
    x1_ref[...] = x1
    h2 = _ln(x1) * (1.0 + sc2_ref[0]) + sh2_ref[0]
    hb = h2.astype(BF16)
    hl = (h2 - hb.astype(F32)).astype(BF16)

    _store_packed_rows(hp_ref, h2)

    logits_t = _dot_nt(wrh_ref[...], hb) + _dot_nt(wrl_ref[...], hb) + _dot_nt(wrh_ref[...], hl)
    st_ref[...] = jax.nn.sigmoid(logits_t)

    g = _dot(hb, wsg_ref[...])
    u = _dot(hb, wsu_ref[...])
    a = (g * jax.nn.sigmoid(g) * u).astype(BF16)
    shr_ref[...] = _dot(a, wsd_ref[...])


def _mix(ot, gya, sgb, x2, gate1, scale2, shift2, lg, lb, wts, tm, seq, alpha):
    t, d = x2.shape
    tps = seq // tm
    fw = ot.shape[1]
    n_exp = wts[2].shape[0]
    mod_spec = pl.BlockSpec((1, 1, d), lambda i: (i // tps, 0, 0))
    row_spec = lambda n: pl.BlockSpec((tm, n), lambda i: (i, 0))
    return pl.pallas_call(
        functools.partial(_mix_kernel, alpha=alpha),
        grid=(t // tm,),
        in_specs=[pl.BlockSpec((None, fw, tm), lambda i: (i // tps, 0, i % tps)),
                  row_spec(d), row_spec(d), row_spec(d), mod_spec, mod_spec, mod_spec,
                  _const_spec(lg.shape), _const_spec(lb.shape)] + [_const_spec(w.shape) for w in wts],
        out_specs=[row_spec(d), pl.BlockSpec((tm * ROW_SLABS, V7X_LANES), lambda i: (i, 0)),
                   pl.BlockSpec((n_exp, tm), lambda i: (0, i)), row_spec(d)],
        out_shape=[jax.ShapeDtypeStruct((t, d), F32),
                   jax.ShapeDtypeStruct((t * ROW_SLABS, V7X_LANES), jnp.uint32),
                   jax.ShapeDtypeStruct((n_exp, t), F32),
                   jax.ShapeDtypeStruct((t, d), F32)],
        compiler_params=_cparams(("arbitrary",)),
        name="mix",
    )(ot, gya, sgb, x2, gate1, scale2, shift2, lg, lb, *wts)


def _route_kernel(st_ref, bias_ref, idx_ref, wts_ref, rank_ref, cnt_ref, carry_ref):
    i = pl.program_id(0)
    n_exp, tr = st_ref.shape
    gsz = n_exp // N_GROUPS
    neg_inf = -jnp.inf

    @pl.when(i == 0)
    def _():
        carry_ref[...] = jnp.zeros_like(carry_ref)

    shape3 = (N_GROUPS, gsz, tr)

    def max01(v):
        return jnp.max(jnp.max(v, axis=1, keepdims=True), axis=0, keepdims=True)

    def min01(v):
        return jnp.min(jnp.min(v, axis=1, keepdims=True), axis=0, keepdims=True)

    def sum01(v):
        return jnp.sum(jnp.sum(v, axis=1, keepdims=True), axis=0, keepdims=True)

    sc = st_ref[...].reshape(shape3)
    gsel = (st_ref[...] + bias_ref[...]).reshape(shape3)
    pos = lax.broadcasted_iota(jnp.int32, shape3, 1)
    m1 = jnp.max(gsel, axis=1, keepdims=True)
    i1 = jnp.min(jnp.where(gsel == m1, pos, gsz), axis=1, keepdims=True)
    m2 = jnp.max(jnp.where(pos == i1, neg_inf, gsel), axis=1, keepdims=True)
    gs = m1 + m2

    gid = lax.broadcasted_iota(jnp.int32, gs.shape, 0)
    gkeep = jnp.zeros(gs.shape, F32)
    for _ in range(TOPK_GROUPS):
        mx = jnp.max(gs, axis=0, keepdims=True)
        gi = jnp.min(jnp.where(gs == mx, gid, N_GROUPS), axis=0, keepdims=True)
        hit = gid == gi
        gkeep = gkeep + jnp.where(hit, 1.0, 0.0)
        gs = jnp.where(hit, neg_inf, gs)

    cur = jnp.where(jnp.broadcast_to(gkeep, shape3) > 0.5, gsel, neg_inf)
    eid = lax.broadcasted_iota(jnp.int32, shape3, 0) * gsz + pos
    onehot = jnp.zeros(shape3, F32)
    idxs, ws = [], []
    wsum = jnp.zeros((1, 1, tr), F32)
    for _ in range(TOP_K):
        mx = max01(cur)
        ik = min01(jnp.where(cur == mx, eid, n_exp))
        hit = eid == ik
        wk = sum01(jnp.where(hit, sc, 0.0))
        idxs.append(ik)
        ws.append(wk)
        wsum = wsum + wk
        onehot = onehot + jnp.where(hit, 1.0, 0.0)
        cur = jnp.where(hit, neg_inf, cur)

    ra = lax.broadcasted_iota(jnp.int32, (tr, tr), 0)
    rb = lax.broadcasted_iota(jnp.int32, (tr, tr), 1)
    upper = jnp.where(ra < rb, 1.0, 0.0).astype(BF16)
    onehot2 = onehot.reshape(n_exp, tr)
    prior = (_dot(onehot2.astype(BF16), upper) + carry_ref[...]).reshape(shape3)
    for slot in range(TOP_K):
        idx_ref[slot:slot + 1, :] = idxs[slot].reshape(1, tr)
        wts_ref[slot:slot + 1, :] = (ws[slot] / wsum * ROUTED_SCALE).reshape(1, tr)
        rk = sum01(jnp.where(eid == idxs[slot], prior, 0.0))
        rank_ref[slot:slot + 1, :] = rk.reshape(1, tr).astype(jnp.int32)
    total = carry_ref[...] + jnp.sum(onehot2, axis=1, keepdims=True)
    carry_ref[...] = total
    cnt_ref[...] = jnp.broadcast_to(total, cnt_ref.shape).astype(jnp.int32)


def _route(scores_t, bias_col, tr):
    n_exp, t = scores_t.shape
    slot_spec = pl.BlockSpec((TOP_K, tr), lambda i: (0, i))
    return pl.pallas_call(
        _route_kernel,
        grid=(t // tr,),
        in_specs=[pl.BlockSpec((n_exp, tr), lambda i: (0, i)), _const_spec(bias_col.shape)],
        out_specs=[slot_spec, slot_spec, slot_spec, _const_spec((n_exp, V7X_LANES))],
        out_shape=[jax.ShapeDtypeStruct((TOP_K, t), jnp.int32),
                   jax.ShapeDtypeStruct((TOP_K, t), F32),
                   jax.ShapeDtypeStruct((TOP_K, t), jnp.int32),
                   jax.ShapeDtypeStruct((n_exp, V7X_LANES), jnp.int32)],
        scratch_shapes=[pltpu.VMEM((n_exp, 1), F32)],
        compiler_params=_cparams(("arbitrary",)),
        name="route",
    )(scores_t, bias_col)


def _dest_kernel(pstart_ref, idx_ref, rank_ref, o_ref):
    idx = idx_ref[...]
    n_exp = pstart_ref.shape[0]

    def body(e, acc):
        return acc + jnp.where(idx == e, pstart_ref[e], 0)

    o_ref[...] = lax.fori_loop(0, n_exp, body, rank_ref[...]) * ROW_SLABS


def _dest(pstart, idx, rank, tr):
    k, t = idx.shape
    spec = pl.BlockSpec((k, tr), lambda i, ps: (0, i))
    return pl.pallas_call(
        _dest_kernel,
        grid_spec=pltpu.PrefetchScalarGridSpec(
            num_scalar_prefetch=1, grid=(t // tr,), in_specs=[spec, spec], out_specs=spec),
        out_shape=jax.ShapeDtypeStruct((k, t), jnp.int32),
        compiler_params=_cparams(("arbitrary",)),
        name="dest",
    )(pstart, idx, rank)


def _dispatch_kernel(dest_ref, hp_ref, xs_ref, sem):
    td = hp_ref.shape[0] // ROW_SLABS

    def row_copy(r, slot):
        dst = pl.multiple_of(dest_ref[r * TOP_K + slot], ROW_SLABS)
        src = pl.multiple_of(r * ROW_SLABS, ROW_SLABS)
        return pltpu.make_async_copy(hp_ref.at[pl.ds(src, ROW_SLABS)], xs_ref.at[pl.ds(dst, ROW_SLABS)], sem)

    def start(r, c):
        for slot in range(TOP_K):
            row_copy(r, slot).start(priority=slot % 2)
        return c

    def wait(r, c):
        for slot in range(TOP_K):
            row_copy(r, slot).wait()
        return c

    lax.fori_loop(0, td, start, 0)
    lax.fori_loop(0, td, wait, 0)


def _dispatch(dest_flat, hp, n_pad, td):
    t = hp.shape[0] // ROW_SLABS
    return pl.pallas_call(
        _dispatch_kernel,
        grid=(t // td,),
        in_specs=[pl.BlockSpec((td * TOP_K,), lambda i: (i,), memory_space=pltpu.SMEM),
                  pl.BlockSpec((td * ROW_SLABS, V7X_LANES), lambda i: (i, 0))],
        out_specs=pl.BlockSpec(memory_space=pl.ANY),
        out_shape=jax.ShapeDtypeStruct((n_pad * ROW_SLABS, V7X_LANES), hp.dtype),
        scratch_shapes=[pltpu.SemaphoreType.DMA(())],
        compiler_params=_cparams(("arbitrary",)),
        name="dispatch",
    )(dest_flat, hp)


def _expert_kernel(be_ref, rows_ref, nused_ref, xs_ref, wg_ref, wu_ref, wd_ref, ys_ref, wg_s, wu_s, wd_s):
    p = pl.program_id(0)
    prev = be_ref[jnp.maximum(p - 1, 0)]
    fresh = jnp.logical_or(p == 0, be_ref[p] != prev)
    live = p < nused_ref[0]

    @pl.when(jnp.logical_and(live, fresh))
    def _():
        wg_s[...] = wg_ref[...].astype(BF16)
        wu_s[...] = wu_ref[...].astype(BF16)
        wd_s[...] = wd_ref[...].astype(BF16)

    @pl.when(live)
    def _():
        x = jnp.concatenate(_load_packed_rows(xs_ref, EXPERT_BLOCK), axis=1)
        valid = lax.broadcasted_iota(jnp.int32, x.shape, 0) < rows_ref[p]
        x = jnp.where(valid, x, 0.0).astype(BF16)
        g = _dot(x, wg_s[...])
        u = _dot(x, wu_s[...])
        a = (g * jax.nn.sigmoid(g) * u).astype(BF16)
        _store_packed_rows(ys_ref, _dot(a, wd_s[...]))


def _experts(block_expert, block_rows, n_used, xs, wg, wu, wd, layer):
    n_pad = xs.shape[0] // ROW_SLABS
    _, n_exp, d, f = wg.shape
    n_blocks = n_pad // EXPERT_BLOCK

    def xmap(p, be, rows, nu):
        return (jnp.minimum(p, nu[0] - 1), 0)

    def wmap(p, be, rows, nu):
        return (layer, be[jnp.minimum(p, nu[0] - 1)], 0, 0)

    slab_spec = pl.BlockSpec((EXPERT_BLOCK * ROW_SLABS, V7X_LANES), xmap)
    return pl.pallas_call(
        _expert_kernel,
        grid_spec=pltpu.PrefetchScalarGridSpec(
            num_scalar_prefetch=3, grid=(n_blocks,),
            in_specs=[slab_spec,
                      pl.BlockSpec((None, None, d, f), wmap),
                      pl.BlockSpec((None, None, d, f), wmap),
                      pl.BlockSpec((None, None, f, d), wmap)],
            out_specs=slab_spec,
            scratch_shapes=[pltpu.VMEM((d, f), BF16), pltpu.VMEM((d, f), BF16), pltpu.VMEM((f, d), BF16)]),
        out_shape=jax.ShapeDtypeStruct(xs.shape, xs.dtype),
        compiler_params=_cparams(("arbitrary",)),
        name="experts",
    )(block_expert, block_rows, n_used, xs, wg, wu, wd)


def _combine_kernel(dest_ref, ys_ref, w_ref, shr_ref, x1_ref, g2_ref, lg_ref, lb_ref, o_ref, buf, sem, *, alpha):
    tc = x1_ref.shape[0]

    def row_copy(r, slot):
        src = pl.multiple_of(dest_ref[r * TOP_K + slot], ROW_SLABS)
        dst = pl.multiple_of(r * ROW_SLABS, ROW_SLABS)
        return pltpu.make_async_copy(ys_ref.at[pl.ds(src, ROW_SLABS)], buf.at[slot, pl.ds(dst, ROW_SLABS)], sem)

    def start(r, c):
        for slot in range(TOP_K):
            row_copy(r, slot).start(priority=slot % 2)
        return c

    def wait(r, c):
        for slot in range(TOP_K):
            row_copy(r, slot).wait()
        return c

    lax.fori_loop(0, tc, start, 0)
    lax.fori_loop(0, tc, wait, 0)

    w = w_ref[...]
    chunks = None
    for slot in range(TOP_K):
        wk = w[:, slot:slot + 1]
        part = [c * wk for c in _load_packed_rows(buf, tc, lead=(slot,))]
        chunks = part if chunks is None else [a + b for a, b in zip(chunks, part)]
    y = shr_ref[...] + jnp.concatenate(chunks, axis=1)
    z = alpha * x1_ref[...] + (1.0 + g2_ref[0]) * y
    o_ref[...] = _ln(z) * lg_ref[...] + lb_ref[...]


def _combine(dest_flat, ys, wts_tk, shared, x1, gate2, lg, lb, tc, seq, alpha):
    t, d = x1.shape
    tps = seq // tc
    row_spec = lambda n: pl.BlockSpec((tc, n), lambda i: (i, 0))
    return pl.pallas_call(
        functools.partial(_combine_kernel, alpha=alpha),
        grid=(t // tc,),
        in_specs=[pl.BlockSpec((tc * TOP_K,), lambda i: (i,), memory_space=pltpu.SMEM),
                  pl.BlockSpec(memory_space=pl.ANY),
                  row_spec(TOP_K), row_spec(d), row_spec(d),
                  pl.BlockSpec((1, 1, d), lambda i: (i // tps, 0, 0)),
                  _const_spec(lg.shape), _const_spec(lb.shape)],
        out_specs=row_spec(d),
        out_shape=jax.ShapeDtypeStruct((t, d), F32),
        scratch_shapes=[pltpu.VMEM((TOP_K, tc * ROW_SLABS, V7X_LANES), jnp.uint32), pltpu.SemaphoreType.DMA(())],
        compiler_params=_cparams(("arbitrary",)),
        name="combine",
    )(dest_flat, ys, wts_tk, shared, x1, gate2, lg, lb)


def _placement():
    pq = np.zeros((V7X_LANES, FOX_HEADS * HEAD_PAD), np.float32)
    pk = np.zeros((V7X_LANES, FOX_HEADS * HEAD_PAD), np.float32)
    for h in range(FOX_HEADS):
        base = h * HEAD_PAD + FOX_HEAD_DIM
        for piece in range(3):
            pq[piece * 8 + h, base + piece] = 1.0
            pk[24, base + piece] = 1.0
            pq[24, base + 3 + piece] = 1.0
            pk[piece * 8 + h, base + 3 + piece] = -1.0
    return jnp.asarray(pq, BF16), jnp.asarray(pk, BF16)


def _inproj_weights(w_in, b_forget, d):
    conv2 = d
    fw = FOX_HEADS * FOX_HEAD_DIM
    o1, o2, o3, o4 = conv2, conv2 + fw, conv2 + 2 * fw, conv2 + 3 * fw
    o5 = o4 + FOX_HEADS
    o6 = o5 + d

    def pad_heads(w):
        w = w.reshape(d, FOX_HEADS, FOX_HEAD_DIM)
        w = jnp.pad(w, ((0, 0), (0, 0), (0, HEAD_PAD - FOX_HEAD_DIM)))
        return w.reshape(d, FOX_HEADS * HEAD_PAD).astype(BF16)

    wglu = w_in[:, :o1].astype(BF16)
    wq = pad_heads(w_in[:, o1:o2])
    wk = pad_heads(w_in[:, o2:o3])
    wvt = w_in[:, o3:o4].T.astype(BF16)
    wf8 = w_in[:, o4:o5]
    wf = jnp.pad(jnp.concatenate([wf8, wf8, wf8], axis=1), ((0, 0), (0, V7X_LANES - 3 * FOX_HEADS))).astype(BF16)
    bf = jnp.pad(jnp.concatenate([b_forget, b_forget, b_forget]), (0, V7X_LANES - 3 * FOX_HEADS))[None, :].astype(F32)
    wga = w_in[:, o5:o6].astype(BF16)
    wgb = w_in[:, o6:].astype(BF16)
    pq, pk = _placement()
    return (wglu, wq, wk, wvt, wf, wga, wgb, bf, pq, pk)


def _layer(x2, ada, bsz, seq, w_in, b_forget, conv_w, conv_b, conv_ln_g, conv_ln_b, w_conv_out, w_fox_out,
           w_mix_out, ln1_g, ln1_b, w_router, router_bias, w_exp_gate, w_exp_up, w_exp_down,
           w_sh_gate, w_sh_up, w_sh_down, ln2_g, ln2_b, depth, layer):
    t, d = x2.shape
    n_exp = w_router.shape[1]
    alpha = (2.0 * depth) ** 0.25
    mods = [ada[:bsz, j * d:(j + 1) * d][:, None, :] for j in range(6)]
    shift1, scale1, gate1, shift2, scale2, gate2 = mods

    tm = min(512, seq)
    u, q, k, vt, sga, sgb = _inproj(x2, scale1, shift1, _inproj_weights(w_in, b_forget, d), tm, seq)

    tcv = min(256, seq)
    conv_w_pad = jnp.pad(conv_w, ((0, CONV_HALO - CONV_WIDTH), (0, 0)))
    gya = _conv(u, conv_w_pad, conv_b[None, :], conv_ln_g[None, :], conv_ln_b[None, :],
                w_conv_out.astype(BF16), sga, tcv, seq)

    ot = _attn(q.reshape(bsz, seq, -1), k.reshape(bsz, seq, -1), vt, tm, ATTN_HEADS_PER_STEP)

    tmx = min(256, seq)
    wr_t = w_router.T
    wr_h = wr_t.astype(BF16)
    wr_l = (wr_t - wr_h.astype(F32)).astype(BF16)
    mix_w = (w_fox_out.astype(BF16), w_mix_out.astype(BF16), wr_h, wr_l,
             w_sh_gate.astype(BF16), w_sh_up.astype(BF16), w_sh_down.astype(BF16))
    x1, hp, scores_t, shared = _mix(ot, gya, sgb, x2, gate1, scale2, shift2, ln1_g[None, :], ln1_b[None, :],
                                    mix_w, tmx, seq, alpha)

    tr = min(512, t)
    idx, wts, rank, cnt = _route(scores_t, router_bias[:, None], tr)

    counts = cnt[:, 0]
    padded = (counts + EXPERT_BLOCK - 1) // EXPERT_BLOCK * EXPERT_BLOCK
    pend = jnp.cumsum(padded)
    pstart = (pend - padded).astype(jnp.int32)
    n_assign = t * TOP_K
    n_pad = -(-(n_assign + n_exp * (EXPERT_BLOCK - 1)) // EXPERT_BLOCK) * EXPERT_BLOCK
    n_blocks = n_pad // EXPERT_BLOCK
    block_start = jnp.arange(n_blocks, dtype=jnp.int32) * EXPERT_BLOCK
    block_expert = jnp.minimum(jnp.sum(pend[None, :] <= block_start[:, None], axis=1), n_exp - 1).astype(jnp.int32)
    block_rows = jnp.clip((pstart + counts)[block_expert] - block_start, 0, EXPERT_BLOCK).astype(jnp.int32)
    n_used = (pend[-1:] // EXPERT_BLOCK).astype(jnp.int32)

    dest = _dest(pstart, idx, rank, tr)
    dest_flat = dest.T.reshape(-1)
    xs = _dispatch(dest_flat, hp, n_pad, min(256, t))
    ys = _experts(block_expert, block_rows, n_used, xs, w_exp_gate, w_exp_up, w_exp_down, layer)
    return _combine(dest_flat, ys, wts.T, shared, x1, gate2, ln2_g[None, :], ln2_b[None, :],
                    min(128, seq), seq, alpha)


def kernel(x, c, w_ada, b_ada, w_in, b_forget, conv_w, conv_b, conv_ln_g, conv_ln_b, w_conv_out, w_fox_out,
           w_mix_out, ln1_g, ln1_b, w_router, router_bias, w_exp_gate, w_exp_up, w_exp_down, w_sh_gate,
           w_sh_up, w_sh_down, ln2_g, ln2_b):
    bsz, seq, d = x.shape
    depth = w_ada.shape[0]
    c_pad = jnp.pad(c, ((0, 8 - bsz), (0, 0)))
    x2 = x.reshape(bsz * seq, d)
    for l in range(depth):
        ada = _ada(c_pad, w_ada[l], b_ada[l][None, :])
        x2 = _layer(x2, ada, bsz, seq, w_in[l], b_forget[l], conv_w[l], conv_b[l], conv_ln_g[l], conv_ln_b[l],
                    w_conv_out[l], w_fox_out[l], w_mix_out[l], ln1_g[l], ln1_b[l], w_router[l], router_bias[l],
                    w_exp_gate, w_exp_up, w_exp_down, w_sh_gate[l], w_sh_up[l], w_sh_down[l],
                    ln2_g[l], ln2_b[l], depth, l)
    return x2.reshape(bsz, seq, d)
```

```python
import functools

import jax
import jax.numpy as jnp
import numpy as np
from jax import lax
from jax.experimental import pallas as pl
from jax.experimental.pallas import tpu as pltpu

F32 = jnp.float32
BF16 = jnp.bfloat16

LN_EPS = 1e-5
CONV_WIDTH = 31
FOX_HEADS = 8
FOX_HEAD_DIM = 64
N_GROUPS = 8
TOPK_GROUPS = 4
TOP_K = 8
ROUTED_SCALE = 2.5
EXPERT_BLOCK = 256
ROW_SLABS = 4

V7X_LANES = 128
HEAD_PAD = 128
CONV_HALO = 32
VMEM_LIMIT = 56 * 1024 * 1024
NEG_BIG = -1e30
LOG2E = 1.4426950408889634
ATTN_EXTRA_ROWS = 16
ATTN_HEADS_PER_STEP = 4
ATTN_Q_COLS = 256


def _cparams(sem):
    return pltpu.CompilerParams(dimension_semantics=sem, vmem_limit_bytes=VMEM_LIMIT)


def _ln(v):
    mu = jnp.mean(v, axis=-1, keepdims=True)
    vc = v - mu
    var = jnp.mean(vc * vc, axis=-1, keepdims=True)
    return vc * lax.rsqrt(var + LN_EPS)


def _split3(v):
    hi = v.astype(BF16)
    r1 = v - hi.astype(F32)
    mid = r1.astype(BF16)
    lo = (r1 - mid.astype(F32)).astype(BF16)
    return hi, mid, lo


def _dot(a, b):
    return jnp.dot(a, b, preferred_element_type=F32)


def _dot_nt(a, b):
    return lax.dot_general(a, b, (((1,), (1,)), ((), ())), preferred_element_type=F32)


def _dot_tn(a, b):
    return lax.dot_general(a, b, (((0,), (0,)), ((), ())), preferred_element_type=F32)


def _store_packed_rows(ref, v):
    n, d = v.shape
    half = d // 2
    vb = v.astype(BF16).astype(F32)
    lo_bits = lax.bitcast_convert_type(vb[:, :half], jnp.uint32)
    hi_bits = lax.bitcast_convert_type(vb[:, half:], jnp.uint32)
    words = (lo_bits >> 16) | (hi_bits & jnp.uint32(0xFFFF0000))
    for c in range(ROW_SLABS):
        ref[pl.ds(c, n, stride=ROW_SLABS), :] = words[:, c * V7X_LANES:(c + 1) * V7X_LANES]


def _load_packed_rows(ref, n, lead=()):
    lo, hi = [], []
    for c in range(ROW_SLABS):
        w = ref[lead + (pl.ds(c, n, stride=ROW_SLABS), slice(None))]
        lo.append(lax.bitcast_convert_type(w << 16, F32))
        hi.append(lax.bitcast_convert_type(w & jnp.uint32(0xFFFF0000), F32))
    return lo + hi


def _const_spec(shape):
    nd = len(shape)
    return pl.BlockSpec(shape, lambda *_: (0,) * nd)


def _ada_kernel(c_ref, w_ref, b_ref, o_ref):
    c = c_ref[...]
    cond = c * jax.nn.sigmoid(c)
    ch, cm, _ = _split3(cond)
    w = w_ref[...]
    wh, wm, _ = _split3(w)
    o_ref[...] = _dot(ch, wh) + _dot(ch, wm) + _dot(cm, wh) + b_ref[...]


def _ada(c_pad, w, b):
    rows, d = c_pad.shape
    n = w.shape[1]
    tn = 1024
    return pl.pallas_call(
        _ada_kernel,
        grid=(n // tn,),
        in_specs=[_const_spec((rows, d)),
                  pl.BlockSpec((d, tn), lambda j: (0, j)),
                  pl.BlockSpec((1, tn), lambda j: (0, j))],
        out_specs=pl.BlockSpec((rows, tn), lambda j: (0, j)),
        out_shape=jax.ShapeDtypeStruct((rows, n), F32),
        compiler_params=_cparams(("arbitrary",)),
        name="ada",
    )(c_pad, w, b)


def _inproj_kernel(x_ref, sc_ref, sh_ref, wglu_ref, wq_ref, wk_ref, wvt_ref, wf_ref, wga_ref, wgb_ref,
                   bf_ref, pq_ref, pk_ref,
                   u_ref, q_ref, k_ref, vt_ref, sga_ref, sgb_ref, carry_ref, *, tiles_per_seq, conv_ch):
    i = pl.program_id(0)
    tm = x_ref.shape[0]
    h = _ln(x_ref[...]) * (1.0 + sc_ref[0]) + sh_ref[0]
    hb = h.astype(BF16)

    glu = _dot(hb, wglu_ref[...])
    u_ref[...] = glu[:, :conv_ch] * jax.nn.sigmoid(glu[:, conv_ch:])

    f = _dot(hb, wf_ref[...]) + bf_ref[...]
    logf = jnp.minimum(f, 0.0) - jnp.log(1.0 + jnp.exp(-jnp.abs(f)))
    lh, lm, ll = _split3(logf)
    row = lax.broadcasted_iota(jnp.int32, (tm, tm), 0)
    col = lax.broadcasted_iota(jnp.int32, (tm, tm), 1)
    tri = jnp.where(row >= col, 1.0, 0.0).astype(BF16)
    cs = _dot(tri, lh) + _dot(tri, lm) + _dot(tri, ll)

    @pl.when(i % tiles_per_seq == 0)
    def _():
        carry_ref[...] = jnp.zeros_like(carry_ref)

    cum = cs + carry_ref[...]
    carry_ref[...] = cum[tm - 1:tm, :]

    ch, cm, cl = _split3(cum * LOG2E)
    lane = lax.broadcasted_iota(jnp.int32, cum.shape, 1)
    tail = jnp.where(lane == 24, 1.0, 0.0)
    pieces = jnp.where(lane < 8, ch.astype(F32), jnp.where(lane < 16, cm.astype(F32),
                       jnp.where(lane < 24, cl.astype(F32), tail))).astype(BF16)
    scale = FOX_HEAD_DIM ** -0.5 * LOG2E
    q_ref[...] = (_dot(hb, wq_ref[...]) * scale + _dot(pieces, pq_ref[...])).astype(BF16)
    k_ref[...] = (_dot(hb, wk_ref[...]) + _dot(pieces, pk_ref[...])).astype(BF16)
    vt_ref[0, 0] = _dot_nt(wvt_ref[...], hb).astype(BF16)
    sga_ref[...] = jax.nn.sigmoid(_dot(hb, wga_ref[...])).astype(BF16)
    sgb_ref[...] = jax.nn.sigmoid(_dot(hb, wgb_ref[...])).astype(BF16)


def _inproj(x2, scale1, shift1, wts, tm, seq):
    t, d = x2.shape
    tps = seq // tm
    bsz = t // seq
    wglu, wq, wk, wvt, wf, wga, wgb, bf, pq, pk = wts
    conv_ch = wglu.shape[1] // 2
    fw = wvt.shape[0]
    qw = wq.shape[1]
    mod_spec = pl.BlockSpec((1, 1, d), lambda i: (i // tps, 0, 0))
    row_spec = lambda n: pl.BlockSpec((tm, n), lambda i: (i, 0))
    return pl.pallas_call(
        functools.partial(_inproj_kernel, tiles_per_seq=tps, conv_ch=conv_ch),
        grid=(t // tm,),
        in_specs=[row_spec(d), mod_spec, mod_spec] + [_const_spec(w.shape) for w in wts],
        out_specs=[row_spec(conv_ch), row_spec(qw), row_spec(qw),
                   pl.BlockSpec((1, 1, fw, tm), lambda i: (i // tps, i % tps, 0, 0)),
                   row_spec(d), row_spec(d)],
        out_shape=[jax.ShapeDtypeStruct((t, conv_ch), F32),
                   jax.ShapeDtypeStruct((t, qw), BF16),
                   jax.ShapeDtypeStruct((t, qw), BF16),
                   jax.ShapeDtypeStruct((bsz, tps, fw, tm), BF16),
                   jax.ShapeDtypeStruct((t, d), BF16),
                   jax.ShapeDtypeStruct((t, d), BF16)],
        scratch_shapes=[pltpu.VMEM((1, V7X_LANES), F32)],
        compiler_params=_cparams(("arbitrary",)),
        name="inproj",
    )(x2, scale1, shift1, *wts)


def _conv_kernel(cur_ref, prev_ref, w_ref, b_ref, g_ref, be_ref, wout_ref, sga_ref, o_ref, ext_ref,
                 *, tiles_per_seq, chunk):
    i = pl.program_id(0)
    tm = cur_ref.shape[0]
    first = (i % tiles_per_seq) == 0
    ext_ref[0:CONV_HALO, :] = jnp.where(first, 0.0, prev_ref[...])
    ext_ref[CONV_HALO:, :] = cur_ref[...]
    w = w_ref[...]
    off = CONV_HALO - (CONV_WIDTH - 1)
    outs = []
    for c0 in range(0, tm, chunk):
        acc = jnp.zeros((chunk, cur_ref.shape[1]), F32)
        for j in range(CONV_WIDTH):
            acc = acc + w[j:j + 1, :] * ext_ref[c0 + off + j:c0 + off + j + chunk, :]
        outs.append(acc)
    v = jnp.concatenate(outs, axis=0) + b_ref[...]
    v = _ln(v) * g_ref[...] + be_ref[...]
    v = v * jax.nn.sigmoid(v)
    ya = _dot(v.astype(BF16), wout_ref[...])
    o_ref[...] = (sga_ref[...].astype(F32) * ya).astype(BF16)


def _conv(u, conv_w, conv_b, g, be, wout, sga, tm, seq):
    t, ch = u.shape
    d = wout.shape[1]
    tps = seq // tm
    halo_per_tile = tm // CONV_HALO
    return pl.pallas_call(
        functools.partial(_conv_kernel, tiles_per_seq=tps, chunk=64),
        grid=(t // tm,),
        in_specs=[pl.BlockSpec((tm, ch), lambda i: (i, 0)),
                  pl.BlockSpec((CONV_HALO, ch), lambda i: (jnp.maximum(i * halo_per_tile - 1, 0), 0)),
                  _const_spec(conv_w.shape), _const_spec(conv_b.shape), _const_spec(g.shape),
                  _const_spec(be.shape), _const_spec(wout.shape),
                  pl.BlockSpec((tm, d), lambda i: (i, 0))],
        out_specs=pl.BlockSpec((tm, d), lambda i: (i, 0)),
        out_shape=jax.ShapeDtypeStruct((t, d), BF16),
        scratch_shapes=[pltpu.VMEM((tm + CONV_HALO, ch), F32)],
        compiler_params=_cparams(("arbitrary",)),
        name="conv",
    )(u, u, conv_w, conv_b, g, be, wout, sga)


def _attn_kernel(q_ref, k_ref, vt_ref, o_ref, *, blk, heads):
    qi = pl.program_id(2)
    row = lax.broadcasted_iota(jnp.int32, (ATTN_EXTRA_ROWS, blk), 0)
    ones_rows = jnp.where(row == 0, 1.0, 0.0).astype(BF16)

    ncol = blk // ATTN_Q_COLS
    chains = [(j, c) for j in range(heads) for c in range(ncol)]

    def scores(kj, chain, masked):
        j, c = chain
        k = k_ref[pl.ds(pl.multiple_of(kj * blk, blk), blk), j * HEAD_PAD:(j + 1) * HEAD_PAD]
        q = q_ref[c * ATTN_Q_COLS:(c + 1) * ATTN_Q_COLS, j * HEAD_PAD:(j + 1) * HEAD_PAD]
        s = _dot_nt(k, q)
        if masked:
            kpos = lax.broadcasted_iota(jnp.int32, s.shape, 0)
            qpos = lax.broadcasted_iota(jnp.int32, s.shape, 1) + c * ATTN_Q_COLS
            s = jnp.where(kpos <= qpos, s, NEG_BIG)
        return s

    def probs(s, m):
        m_new = jnp.maximum(m, jnp.max(s, axis=0, keepdims=True))
        return jnp.exp2(s - m_new).astype(BF16), m_new

    def update(kj, chain, p, m, m_new, acc):
        j, _ = chain
        vt = vt_ref[kj, j * FOX_HEAD_DIM:(j + 1) * FOX_HEAD_DIM, :]
        lhs = jnp.concatenate([vt, ones_rows], axis=0)
        return jnp.exp2(m - m_new) * acc + _dot(lhs, p)

    def step(kj, carry, masked):
        n = len(chains)
        s, pm, out = {}, {}, [None] * n
        for i in range(n + 2):
            if i < n:
                s[i] = scores(kj, chains[i], masked)
            if 1 <= i <= n:
                pm[i - 1] = probs(s.pop(i - 1), carry[i - 1][0])
            if i >= 2:
                p, m_new = pm.pop(i - 2)
                m, acc = carry[i - 2]
                out[i - 2] = (m_new, update(kj, chains[i - 2], p, m, m_new, acc))
        return tuple(out)

    init = tuple((jnp.full((1, ATTN_Q_COLS), NEG_BIG, F32),
                  jnp.zeros((FOX_HEAD_DIM + ATTN_EXTRA_ROWS, ATTN_Q_COLS), F32)) for _ in chains)
    carry = lax.fori_loop(0, qi, lambda kj, cr: step(kj, cr, False), init)
    carry = step(qi, carry, True)
    for (j, c), (_, acc) in zip(chains, carry):
        o_ref[j * FOX_HEAD_DIM:(j + 1) * FOX_HEAD_DIM, c * ATTN_Q_COLS:(c + 1) * ATTN_Q_COLS] = (
            acc[:FOX_HEAD_DIM] / acc[FOX_HEAD_DIM:FOX_HEAD_DIM + 1]).astype(BF16)


def _attn(q, k, vt, blk, heads):
    bsz, seq, _ = q.shape
    nkb = seq // blk
    return pl.pallas_call(
        functools.partial(_attn_kernel, blk=blk, heads=heads),
        grid=(bsz, FOX_HEADS // heads, seq // blk),
        in_specs=[pl.BlockSpec((None, blk, heads * HEAD_PAD), lambda b, h, i: (b, i, h)),
                  pl.BlockSpec((None, seq, heads * HEAD_PAD), lambda b, h, i: (b, 0, h)),
                  pl.BlockSpec((None, nkb, heads * FOX_HEAD_DIM, blk), lambda b, h, i: (b, 0, h, 0))],
        out_specs=pl.BlockSpec((None, heads * FOX_HEAD_DIM, blk), lambda b, h, i: (b, h, i)),
        out_shape=jax.ShapeDtypeStruct((bsz, FOX_HEADS * FOX_HEAD_DIM, seq), BF16),
        compiler_params=_cparams(("arbitrary", "arbitrary", "arbitrary")),
        name="attn",
    )(q, k, vt)


def _mix_kernel(ot_ref, gya_ref, sgb_ref, x_ref, g1_ref, sc2_ref, sh2_ref, lg_ref, lb_ref,
                wfox_ref, wmix_ref, wrh_ref, wrl_ref, wsg_ref, wsu_ref, wsd_ref,
                x1_ref, hp_ref, st_ref, shr_ref, *, alpha):
    yb = _dot_tn(ot_ref[...], wfox_ref[...])
    merged = gya_ref[...].astype(F32) + sgb_ref[...].astype(F32) * yb
    y = _dot(merged.astype(BF16), wmix_ref[...])
    x1 = _ln(alpha * x_ref[...] + (1.0 + g1_ref[0]) * y) * lg_ref[...] + lb_ref[...]
    x1_ref[...] = x1
    h2 = _ln(x1) * (1.0 + sc2_ref[0]) + sh2_ref[0]
    hb = h2.astype(BF16)
    hl = (h2 - hb.astype(F32)).astype(BF16)

    _store_packed_rows(hp_ref, h2)

    logits_t = _dot_nt(wrh_ref[...], hb) + _dot_nt(wrl_ref[...], hb) + _dot_nt(wrh_ref[...], hl)
    st_ref[...] = jax.nn.sigmoid(logits_t)

    g = _dot(hb, wsg_ref[...])
    u = _dot(hb, wsu_ref[...])
    a = (g * jax.nn.sigmoid(g) * u).astype(BF16)
    shr_ref[...] = _dot(a, wsd_ref[...])


def _mix(ot, gya, sgb, x2, gate1, scale2, shift2, lg, lb, wts, tm, seq, alpha):
    t, d = x2.shape
    tps = seq // tm
    fw = ot.shape[1]
    n_exp = wts[2].shape[0]
    mod_spec = pl.BlockSpec((1, 1, d), lambda i: (i // tps, 0, 0))
    row_spec = lambda n: pl.BlockSpec((tm, n), lambda i: (i, 0))
    return pl.pallas_call(
        functools.partial(_mix_kernel, alpha=alpha),
        grid=(t // tm,),
        in_specs=[pl.BlockSpec((None, fw, tm), lambda i: (i // tps, 0, i % tps)),
                  row_spec(d), row_spec(d), row_spec(d), mod_spec, mod_spec, mod_spec,
                  _const_spec(lg.shape), _const_spec(lb.shape)] + [_const_spec(w.shape) for w in wts],
        out_specs=[row_spec(d), pl.BlockSpec((tm * ROW_SLABS, V7X_LANES), lambda i: (i, 0)),
                   pl.BlockSpec((n_exp, tm), lambda i: (0, i)), row_spec(d)],
        out_shape=[jax.ShapeDtypeStruct((t, d), F32),
                   jax.ShapeDtypeStruct((t * ROW_SLABS, V7X_LANES), jnp.uint32),
                   jax.ShapeDtypeStruct((n_exp, t), F32),
                   jax.ShapeDtypeStruct((t, d), F32)],
        compiler_params=_cparams(("arbitrary",)),
        name="mix",
    )(ot, gya, sgb, x2, gate1, scale2, shift2, lg, lb, *wts)


def _route_kernel(st_ref, bias_ref, idx_ref, wts_ref, rank_ref, cnt_ref, carry_ref):
    i = pl.program_id(0)
    n_exp, tr = st_ref.shape
    gsz = n_exp // N_GROUPS
    neg_inf = -jnp.inf

    @pl.when(i == 0)
    def _():
        carry_ref[...] = jnp.zeros_like(carry_ref)

    shape3 = (N_GROUPS, gsz, tr)

    def max01(v):
        return jnp.max(jnp.max(v, axis=1, keepdims=True), axis=0, keepdims=True)

    def min01(v):
        return jnp.min(jnp.min(v, axis=1, keepdims=True), axis=0, keepdims=True)

    def sum01(v):
        return jnp.sum(jnp.sum(v, axis=1, keepdims=True), axis=0, keepdims=True)

    sc = st_ref[...].reshape(shape3)
    gsel = (st_ref[...] + bias_ref[...]).reshape(shape3)
    pos = lax.broadcasted_iota(jnp.int32, shape3, 1)
    m1 = jnp.max(gsel, axis=1, keepdims=True)
    i1 = jnp.min(jnp.where(gsel == m1, pos, gsz), axis=1, keepdims=True)
    m2 = jnp.max(jnp.where(pos == i1, neg_inf, gsel), axis=1, keepdims=True)
    gs = m1 + m2

    gid = lax.broadcasted_iota(jnp.int32, gs.shape, 0)
    gkeep = jnp.zeros(gs.shape, F32)
    for _ in range(TOPK_GROUPS):
        mx = jnp.max(gs, axis=0, keepdims=True)
        gi = jnp.min(jnp.where(gs == mx, gid, N_GROUPS), axis=0, keepdims=True)
        hit = gid == gi
        gkeep = gkeep + jnp.where(hit, 1.0, 0.0)
        gs = jnp.where(hit, neg_inf, gs)

    cur = jnp.where(jnp.broadcast_to(gkeep, shape3) > 0.5, gsel, neg_inf)
    eid = lax.broadcasted_iota(jnp.int32, shape3, 0) * gsz + pos
    onehot = jnp.zeros(shape3, F32)
    idxs, ws = [], []
    wsum = jnp.zeros((1, 1, tr), F32)
    for _ in range(TOP_K):
        mx = max01(cur)
        ik = min01(jnp.where(cur == mx, eid, n_exp))
        hit = eid == ik
        wk = sum01(jnp.where(hit, sc, 0.0))
        idxs.append(ik)
        ws.append(wk)
        wsum = wsum + wk
        onehot = onehot + jnp.where(hit, 1.0, 0.0)
        cur = jnp.where(hit, neg_inf, cur)

    ra = lax.broadcasted_iota(jnp.int32, (tr, tr), 0)
    rb = lax.broadcasted_iota(jnp.int32, (tr, tr), 1)
    upper = jnp.where(ra < rb, 1.0, 0.0).astype(BF16)
    onehot2 = onehot.reshape(n_exp, tr)
    prior = (_dot(onehot2.astype(BF16), upper) + carry_ref[...]).reshape(shape3)
    for slot in range(TOP_K):
        idx_ref[slot:slot + 1, :] = idxs[slot].reshape(1, tr)
        wts_ref[slot:slot + 1, :] = (ws[slot] / wsum * ROUTED_SCALE).reshape(1, tr)
        rk = sum01(jnp.where(eid == idxs[slot], prior, 0.0))
        rank_ref[slot:slot + 1, :] = rk.reshape(1, tr).astype(jnp.int32)
    total = carry_ref[...] + jnp.sum(onehot2, axis=1, keepdims=True)
    carry_ref[...] = total
    cnt_ref[...] = jnp.broadcast_to(total, cnt_ref.shape).astype(jnp.int32)


def _route(scores_t, bias_col, tr):
    n_exp, t = scores_t.shape
    slot_spec = pl.BlockSpec((TOP_K, tr), lambda i: (0, i))
    return pl.pallas_call(
        _route_kernel,
        grid=(t // tr,),
        in_specs=[pl.BlockSpec((n_exp, tr), lambda i: (0, i)), _const_spec(bias_col.shape)],
        out_specs=[slot_spec, slot_spec, slot_spec, _const_spec((n_exp, V7X_LANES))],
        out_shape=[jax.ShapeDtypeStruct((TOP_K, t), jnp.int32),
                   jax.ShapeDtypeStruct((TOP_K, t), F32),
                   jax.ShapeDtypeStruct((TOP_K, t), jnp.int32),
                   jax.ShapeDtypeStruct((n_exp, V7X_LANES), jnp.int32)],
        scratch_shapes=[pltpu.VMEM((n_exp, 1), F32)],
        compiler_params=_cparams(("arbitrary",)),
        name="route",
    )(scores_t, bias_col)


def _dest_kernel(pstart_ref, idx_ref, rank_ref, o_ref):
    idx = idx_ref[...]
    n_exp = pstart_ref.shape[0]

    def body(e, acc):
        return acc + jnp.where(idx == e, pstart_ref[e], 0)

    o_ref[...] = lax.fori_loop(0, n_exp, body, rank_ref[...]) * ROW_SLABS


def _dest(pstart, idx, rank, tr):
    k, t = idx.shape
    spec = pl.BlockSpec((k, tr), lambda i, ps: (0, i))
    return pl.pallas_call(
        _dest_kernel,
        grid_spec=pltpu.PrefetchScalarGridSpec(
            num_scalar_prefetch=1, grid=(t // tr,), in_specs=[spec, spec], out_specs=spec),
        out_shape=jax.ShapeDtypeStruct((k, t), jnp.int32),
        compiler_params=_cparams(("arbitrary",)),
        name="dest",
    )(pstart, idx, rank)


def _dispatch_kernel(dest_ref, hp_ref, xs_ref, sem):
    td = hp_ref.shape[0] // ROW_SLABS

    def row_copy(r, slot):
        dst = pl.multiple_of(dest_ref[r * TOP_K + slot], ROW_SLABS)
        src = pl.multiple_of(r * ROW_SLABS, ROW_SLABS)
        return pltpu.make_async_copy(hp_ref.at[pl.ds(src, ROW_SLABS)], xs_ref.at[pl.ds(dst, ROW_SLABS)], sem)

    def start(r, c):
        for slot in range(TOP_K):
            row_copy(r, slot).start(priority=slot % 2)
        return c

    def wait(r, c):
        for slot in range(TOP_K):
            row_copy(r, slot).wait()
        return c

    lax.fori_loop(0, td, start, 0, unroll=4)
    lax.fori_loop(0, td, wait, 0, unroll=8)


def _dispatch(dest_flat, hp, n_pad, td):
    t = hp.shape[0] // ROW_SLABS
    return pl.pallas_call(
        _dispatch_kernel,
        grid=(t // td,),
        in_specs=[pl.BlockSpec((td * TOP_K,), lambda i: (i,), memory_space=pltpu.SMEM),
                  pl.BlockSpec((td * ROW_SLABS, V7X_LANES), lambda i: (i, 0))],
        out_specs=pl.BlockSpec(memory_space=pl.ANY),
        out_shape=jax.ShapeDtypeStruct((n_pad * ROW_SLABS, V7X_LANES), hp.dtype),
        scratch_shapes=[pltpu.SemaphoreType.DMA(())],
        compiler_params=_cparams(("arbitrary",)),
        name="dispatch",
    )(dest_flat, hp)


def _expert_kernel(be_ref, rows_ref, nxt_ref, nused_ref, xs_hbm, wg_hbm, wu_hbm, wd_hbm, ys_hbm,
                   xbuf, ybuf, wgf, wuf, wdf, wg_s, wu_s, wd_s, sem_x, sem_y, sem_w, *, layer):
    n = nused_ref[0]
    blk_rows = EXPERT_BLOCK * ROW_SLABS

    def block_rows(q):
        return pl.ds(pl.multiple_of(q * blk_rows, blk_rows), blk_rows)

    def x_copy(q, slot):
        return pltpu.make_async_copy(xs_hbm.at[block_rows(q)], xbuf.at[slot], sem_x.at[slot])

    def y_copy(q, slot):
        return pltpu.make_async_copy(ybuf.at[slot], ys_hbm.at[block_rows(q)], sem_y.at[slot])

    def w_copies(e, ws):
        return (pltpu.make_async_copy(wg_hbm.at[layer, e], wgf.at[ws], sem_w.at[ws]),
                pltpu.make_async_copy(wu_hbm.at[layer, e], wuf.at[ws], sem_w.at[ws]),
                pltpu.make_async_copy(wd_hbm.at[layer, e], wdf.at[ws], sem_w.at[ws]))

    x_copy(0, 0).start(priority=1)
    for cp in w_copies(be_ref[0], 0):
        cp.start()

    def body(q, wslot):
        slot = q % 2
        fresh = jnp.logical_or(q == 0, be_ref[q] != be_ref[jnp.maximum(q - 1, 0)])
        wslot = jnp.where(jnp.logical_and(fresh, q > 0), 1 - wslot, wslot)

        @pl.when(fresh)
        def _():
            for cp in w_copies(be_ref[q], wslot):
                cp.wait()
            wg_s[...] = wgf[wslot].astype(BF16)
            wu_s[...] = wuf[wslot].astype(BF16)
            wd_s[...] = wdf[wslot].astype(BF16)
            nxt = nxt_ref[q]

            @pl.when(nxt >= 0)
            def _():
                for cp in w_copies(nxt, 1 - wslot):
                    cp.start()

        @pl.when(q + 1 < n)
        def _():
            x_copy(q + 1, 1 - slot).start(priority=1)

        x_copy(q, slot).wait()

        @pl.when(q >= 2)
        def _():
            y_copy(q - 2, slot).wait()

        x = jnp.concatenate(_load_packed_rows(xbuf, EXPERT_BLOCK, lead=(slot,)), axis=1)
        valid = lax.broadcasted_iota(jnp.int32, x.shape, 0) < rows_ref[q]
        x = jnp.where(valid, x, 0.0).astype(BF16)
        g = _dot(x, wg_s[...])
        u = _dot(x, wu_s[...])
        a = (g * jax.nn.sigmoid(g) * u).astype(BF16)
        _store_packed_rows(ybuf.at[slot], _dot(a, wd_s[...]))
        y_copy(q, slot).start()
        return wslot

    lax.fori_loop(0, n, body, jnp.int32(0))

    @pl.when(n >= 2)
    def _():
        y_copy(n - 2, n % 2).wait()

    y_copy(n - 1, (n - 1) % 2).wait()


def _experts(block_expert, block_rows, next_expert, n_used, xs, wg, wu, wd, layer):
    _, n_exp, d, f = wg.shape
    blk = (EXPERT_BLOCK * ROW_SLABS, V7X_LANES)
    any_spec = pl.BlockSpec(memory_space=pl.ANY)
    return pl.pallas_call(
        functools.partial(_expert_kernel, layer=layer),
        grid_spec=pltpu.PrefetchScalarGridSpec(
            num_scalar_prefetch=4, grid=(1,),
            in_specs=[any_spec, any_spec, any_spec, any_spec],
            out_specs=any_spec,
            scratch_shapes=[pltpu.VMEM((2,) + blk, xs.dtype), pltpu.VMEM((2,) + blk, xs.dtype),
                            pltpu.VMEM((2, d, f), F32), pltpu.VMEM((2, d, f), F32), pltpu.VMEM((2, f, d), F32),
                            pltpu.VMEM((d, f), BF16), pltpu.VMEM((d, f), BF16), pltpu.VMEM((f, d), BF16),
                            pltpu.SemaphoreType.DMA((2,)), pltpu.SemaphoreType.DMA((2,)),
                            pltpu.SemaphoreType.DMA((2,))]),
        out_shape=jax.ShapeDtypeStruct(xs.shape, xs.dtype),
        compiler_params=_cparams(("arbitrary",)),
        name="experts",
    )(block_expert, block_rows, next_expert, n_used, xs, wg, wu, wd)


def _combine_kernel(dest_ref, ys_ref, w_ref, shr_ref, x1_ref, g2_ref, lg_ref, lb_ref, o_ref, buf, sem, *, alpha):
    tc = x1_ref.shape[0]

    def row_copy(r, slot):
        src = pl.multiple_of(dest_ref[r * TOP_K + slot], ROW_SLABS)
        dst = pl.multiple_of(r * ROW_SLABS, ROW_SLABS)
        return pltpu.make_async_copy(ys_ref.at[pl.ds(src, ROW_SLABS)], buf.at[slot, pl.ds(dst, ROW_SLABS)], sem)

    def start(r, c):
        for slot in range(TOP_K):
            row_copy(r, slot).start(priority=slot % 2)
        return c

    def wait(r, c):
        for slot in range(TOP_K):
            row_copy(r, slot).wait()
        return c

    lax.fori_loop(0, tc, start, 0, unroll=4)
    lax.fori_loop(0, tc, wait, 0, unroll=8)

    w = w_ref[...]
    chunks = None
    for slot in range(TOP_K):
        wk = w[:, slot:slot + 1]
        part = [c * wk for c in _load_packed_rows(buf, tc, lead=(slot,))]
        chunks = part if chunks is None else [a + b for a, b in zip(chunks, part)]
    y = shr_ref[...] + jnp.concatenate(chunks, axis=1)
    z = alpha * x1_ref[...] + (1.0 + g2_ref[0]) * y
    o_ref[...] = _ln(z) * lg_ref[...] + lb_ref[...]


def _combine(dest_flat, ys, wts_tk, shared, x1, gate2, lg, lb, tc, seq, alpha):
    t, d = x1.shape
    tps = seq // tc
    row_spec = lambda n: pl.BlockSpec((tc, n), lambda i: (i, 0))
    return pl.pallas_call(
        functools.partial(_combine_kernel, alpha=alpha),
        grid=(t // tc,),
        in_specs=[pl.BlockSpec((tc * TOP_K,), lambda i: (i,), memory_space=pltpu.SMEM),
                  pl.BlockSpec(memory_space=pl.ANY),
                  row_spec(TOP_K), row_spec(d), row_spec(d),
                  pl.BlockSpec((1, 1, d), lambda i: (i // tps, 0, 0)),
                  _const_spec(lg.shape), _const_spec(lb.shape)],
        out_specs=row_spec(d),
        out_shape=jax.ShapeDtypeStruct((t, d), F32),
        scratch_shapes=[pltpu.VMEM((TOP_K, tc * ROW_SLABS, V7X_LANES), jnp.uint32), pltpu.SemaphoreType.DMA(())],
        compiler_params=_cparams(("arbitrary",)),
        name="combine",
    )(dest_flat, ys, wts_tk, shared, x1, gate2, lg, lb)


def _placement():
    pq = np.zeros((V7X_LANES, FOX_HEADS * HEAD_PAD), np.float32)
    pk = np.zeros((V7X_LANES, FOX_HEADS * HEAD_PAD), np.float32)
    for h in range(FOX_HEADS):
        base = h * HEAD_PAD + FOX_HEAD_DIM
        for piece in range(3):
            pq[piece * 8 + h, base + piece] = 1.0
            pk[24, base + piece] = 1.0
            pq[24, base + 3 + piece] = 1.0
            pk[piece * 8 + h, base + 3 + piece] = -1.0
    return jnp.asarray(pq, BF16), jnp.asarray(pk, BF16)


def _inproj_weights(w_in, b_forget, d):
    conv2 = d
    fw = FOX_HEADS * FOX_HEAD_DIM
    o1, o2, o3, o4 = conv2, conv2 + fw, conv2 + 2 * fw, conv2 + 3 * fw
    o5 = o4 + FOX_HEADS
    o6 = o5 + d

    def pad_heads(w):
        w = w.reshape(d, FOX_HEADS, FOX_HEAD_DIM)
        w = jnp.pad(w, ((0, 0), (0, 0), (0, HEAD_PAD - FOX_HEAD_DIM)))
        return w.reshape(d, FOX_HEADS * HEAD_PAD).astype(BF16)

    wglu = w_in[:, :o1].astype(BF16)
    wq = pad_heads(w_in[:, o1:o2])
    wk = pad_heads(w_in[:, o2:o3])
    wvt = w_in[:, o3:o4].T.astype(BF16)
    wf8 = w_in[:, o4:o5]
    wf = jnp.pad(jnp.concatenate([wf8, wf8, wf8], axis=1), ((0, 0), (0, V7X_LANES - 3 * FOX_HEADS))).astype(BF16)
    bf = jnp.pad(jnp.concatenate([b_forget, b_forget, b_forget]), (0, V7X_LANES - 3 * FOX_HEADS))[None, :].astype(F32)
    wga = w_in[:, o5:o6].astype(BF16)
    wgb = w_in[:, o6:].astype(BF16)
    pq, pk = _placement()
    return (wglu, wq, wk, wvt, wf, wga, wgb, bf, pq, pk)


def _layer(x2, ada, bsz, seq, w_in, b_forget, conv_w, conv_b, conv_ln_g, conv_ln_b, w_conv_out, w_fox_out,
           w_mix_out, ln1_g, ln1_b, w_router, router_bias, w_exp_gate, w_exp_up, w_exp_down,
           w_sh_gate, w_sh_up, w_sh_down, ln2_g, ln2_b, depth, layer):
    t, d = x2.shape
    n_exp = w_router.shape[1]
    alpha = (2.0 * depth) ** 0.25
    mods = [ada[:bsz, j * d:(j + 1) * d][:, None, :] for j in range(6)]
    shift1, scale1, gate1, shift2, scale2, gate2 = mods

    tm = min(512, seq)
    u, q, k, vt, sga, sgb = _inproj(x2, scale1, shift1, _inproj_weights(w_in, b_forget, d), tm, seq)

    tcv = min(256, seq)
    conv_w_pad = jnp.pad(conv_w, ((0, CONV_HALO - CONV_WIDTH), (0, 0)))
    gya = _conv(u, conv_w_pad, conv_b[None, :], conv_ln_g[None, :], conv_ln_b[None, :],
                w_conv_out.astype(BF16), sga, tcv, seq)

    ot = _attn(q.reshape(bsz, seq, -1), k.reshape(bsz, seq, -1), vt, tm, ATTN_HEADS_PER_STEP)

    tmx = min(256, seq)
    wr_t = w_router.T
    wr_h = wr_t.astype(BF16)
    wr_l = (wr_t - wr_h.astype(F32)).astype(BF16)
    mix_w = (w_fox_out.astype(BF16), w_mix_out.astype(BF16), wr_h, wr_l,
             w_sh_gate.astype(BF16), w_sh_up.astype(BF16), w_sh_down.astype(BF16))
    x1, hp, scores_t, shared = _mix(ot, gya, sgb, x2, gate1, scale2, shift2, ln1_g[None, :], ln1_b[None, :],
                                    mix_w, tmx, seq, alpha)

    tr = min(512, t)
    idx, wts, rank, cnt = _route(scores_t, router_bias[:, None], tr)

    counts = cnt[:, 0]
    padded = (counts + EXPERT_BLOCK - 1) // EXPERT_BLOCK * EXPERT_BLOCK
    pend = jnp.cumsum(padded)
    pstart = (pend - padded).astype(jnp.int32)
    n_assign = t * TOP_K
    n_pad = -(-(n_assign + n_exp * (EXPERT_BLOCK - 1)) // EXPERT_BLOCK) * EXPERT_BLOCK
    n_blocks = n_pad // EXPERT_BLOCK
    block_start = jnp.arange(n_blocks, dtype=jnp.int32) * EXPERT_BLOCK
    block_expert = jnp.minimum(jnp.sum(pend[None, :] <= block_start[:, None], axis=1), n_exp - 1).astype(jnp.int32)
    block_rows = jnp.clip((pstart + counts)[block_expert] - block_start, 0, EXPERT_BLOCK).astype(jnp.int32)
    n_used = (pend[-1:] // EXPERT_BLOCK).astype(jnp.int32)
    blk_id = jnp.arange(n_blocks, dtype=jnp.int32)
    changes = jnp.logical_and(blk_id < n_used[0], jnp.concatenate(
        [jnp.ones((1,), jnp.bool_), block_expert[1:] != block_expert[:-1]]))
    next_change = lax.cummin(jnp.where(changes, blk_id, n_blocks)[::-1])[::-1]
    next_change = jnp.concatenate([next_change[1:], jnp.full((1,), n_blocks, jnp.int32)])
    next_expert = jnp.where(next_change < n_blocks, block_expert[jnp.minimum(next_change, n_blocks - 1)], -1)

    dest = _dest(pstart, idx, rank, tr)
    dest_flat = dest.T.reshape(-1)
    xs = _dispatch(dest_flat, hp, n_pad, min(256, t))
    ys = _experts(block_expert, block_rows, next_expert.astype(jnp.int32), n_used, xs,
                  w_exp_gate, w_exp_up, w_exp_down, layer)
    return _combine(dest_flat, ys, wts.T, shared, x1, gate2, ln2_g[None, :], ln2_b[None, :],
                    min(128, seq), seq, alpha)


def kernel(x, c, w_ada, b_ada, w_in, b_forget, conv_w, conv_b, conv_ln_g, conv_ln_b, w_conv_out, w_fox_out,
           w_mix_out, ln1_g, ln1_b, w_router, router_bias, w_exp_gate, w_exp_up, w_exp_down, w_sh_gate,
           w_sh_up, w_sh_down, ln2_g, ln2_b):
    bsz, seq, d = x.shape
    depth = w_ada.shape[0]
    c_pad = jnp.pad(c, ((0, 8 - bsz), (0, 0)))
    x2 = x.reshape(bsz * seq, d)
    for l in range(depth):
        ada = _ada(c_pad, w_ada[l], b_ada[l][None, :])
        x2 = _layer(x2, ada, bsz, seq, w_in[l], b_forget[l], conv_w[l], conv_b[l], conv_ln_g[l], conv_ln_b[l],
                    w_conv_out[l], w_fox_out[l], w_mix_out[l], ln1_g[l], ln1_b[l], w_router[l], router_bias[l],
                    w_exp_gate, w_exp_up, w_exp_down, w_sh_gate[l], w_sh_up[l], w_sh_down[l],
                    ln2_g[l], ln2_b[l], depth, l)
    return x2.reshape(bsz, seq, d)
```

```python
import functools

import jax
import jax.numpy as jnp
import numpy as np
from jax import lax
from jax.experimental import pallas as pl
from jax.experimental.pallas import tpu as pltpu

F32 = jnp.float32
BF16 = jnp.bfloat16

LN_EPS = 1e-5
CONV_WIDTH = 31
FOX_HEADS = 8
FOX_HEAD_DIM = 64
N_GROUPS = 8
TOPK_GROUPS = 4
TOP_K = 8
ROUTED_SCALE = 2.5
EXPERT_BLOCK = 256
EXPERT_X_AHEAD = 2
EXPERT_W_AHEAD = 2
ROW_SLABS = 4

V7X_LANES = 128
HEAD_PAD = 128
CONV_HALO = 32
VMEM_LIMIT = 56 * 1024 * 1024
NEG_BIG = -1e30
LOG2E = 1.4426950408889634
ATTN_EXTRA_ROWS = 16
ATTN_HEADS_PER_STEP = 8
ATTN_Q_COLS = 256


def _cparams(sem):
    return pltpu.CompilerParams(dimension_semantics=sem, vmem_limit_bytes=VMEM_LIMIT)


def _ln(v):
    mu = jnp.mean(v, axis=-1, keepdims=True)
    vc = v - mu
    var = jnp.mean(vc * vc, axis=-1, keepdims=True)
    return vc * lax.rsqrt(var + LN_EPS)


def _split3(v):
    hi = v.astype(BF16)
    r1 = v - hi.astype(F32)
    mid = r1.astype(BF16)
    lo = (r1 - mid.astype(F32)).astype(BF16)
    return hi, mid, lo


def _dot(a, b):
    return jnp.dot(a, b, preferred_element_type=F32)


def _dot_nt(a, b):
    return lax.dot_general(a, b, (((1,), (1,)), ((), ())), preferred_element_type=F32)


def _dot_tn(a, b):
    return lax.dot_general(a, b, (((0,), (0,)), ((), ())), preferred_element_type=F32)


def _store_packed_rows(ref, v):
    n, d = v.shape
    half = d // 2
    vb = v.astype(BF16).astype(F32)
    lo_bits = lax.bitcast_convert_type(vb[:, :half], jnp.uint32)
    hi_bits = lax.bitcast_convert_type(vb[:, half:], jnp.uint32)
    words = (lo_bits >> 16) | (hi_bits & jnp.uint32(0xFFFF0000))
    for c in range(ROW_SLABS):
        ref[pl.ds(c, n, stride=ROW_SLABS), :] = words[:, c * V7X_LANES:(c + 1) * V7X_LANES]


def _load_packed_rows(ref, n, lead=()):
    lo, hi = [], []
    for c in range(ROW_SLABS):
        w = ref[lead + (pl.ds(c, n, stride=ROW_SLABS), slice(None))]
        lo.append(lax.bitcast_convert_type(w << 16, F32))
        hi.append(lax.bitcast_convert_type(w & jnp.uint32(0xFFFF0000), F32))
    return lo + hi


def _const_spec(shape):
    nd = len(shape)
    return pl.BlockSpec(shape, lambda *_: (0,) * nd)


def _ada_kernel(c_ref, w_ref, b_ref, o_ref):
    c = c_ref[...]
    cond = c * jax.nn.sigmoid(c)
    ch, cm, _ = _split3(cond)
    w = w_ref[...]
    wh, wm, _ = _split3(w)
    o_ref[...] = _dot(ch, wh) + _dot(ch, wm) + _dot(cm, wh) + b_ref[...]


def _ada(c_pad, w, b):
    rows, d = c_pad.shape
    n = w.shape[1]
    tn = 1024
    return pl.pallas_call(
        _ada_kernel,
        grid=(n // tn,),
        in_specs=[_const_spec((rows, d)),
                  pl.BlockSpec((d, tn), lambda j: (0, j)),
                  pl.BlockSpec((1, tn), lambda j: (0, j))],
        out_specs=pl.BlockSpec((rows, tn), lambda j: (0, j)),
        out_shape=jax.ShapeDtypeStruct((rows, n), F32),
        compiler_params=_cparams(("arbitrary",)),
        name="ada",
    )(c_pad, w, b)


def _inproj_kernel(x_ref, sc_ref, sh_ref, wglu_ref, wq_ref, wk_ref, wvt_ref, wf_ref, wga_ref, wgb_ref,
                   bf_ref, pq_ref, pk_ref,
                   u_ref, q_ref, k_ref, vt_ref, sga_ref, sgb_ref, carry_ref, *, tiles_per_seq, conv_ch):
    i = pl.program_id(0)
    tm = x_ref.shape[0]
    h = _ln(x_ref[...]) * (1.0 + sc_ref[0]) + sh_ref[0]
    hb = h.astype(BF16)

    glu = _dot(hb, wglu_ref[...])
    u_ref[...] = glu[:, :conv_ch] * jax.nn.sigmoid(glu[:, conv_ch:])

    f = _dot(hb, wf_ref[...]) + bf_ref[...]
    logf = jnp.minimum(f, 0.0) - jnp.log(1.0 + jnp.exp(-jnp.abs(f)))
    lh, lm, ll = _split3(logf)
    row = lax.broadcasted_iota(jnp.int32, (tm, tm), 0)
    col = lax.broadcasted_iota(jnp.int32, (tm, tm), 1)
    tri = jnp.where(row >= col, 1.0, 0.0).astype(BF16)
    cs = _dot(tri, lh) + _dot(tri, lm) + _dot(tri, ll)

    @pl.when(i % tiles_per_seq == 0)
    def _():
        carry_ref[...] = jnp.zeros_like(carry_ref)

    cum = cs + carry_ref[...]
    carry_ref[...] = cum[tm - 1:tm, :]

    ch, cm, cl = _split3(cum * LOG2E)
    lane = lax.broadcasted_iota(jnp.int32, cum.shape, 1)
    tail = jnp.where(lane == 24, 1.0, 0.0)
    pieces = jnp.where(lane < 8, ch.astype(F32), jnp.where(lane < 16, cm.astype(F32),
                       jnp.where(lane < 24, cl.astype(F32), tail))).astype(BF16)
    scale = FOX_HEAD_DIM ** -0.5 * LOG2E
    q_ref[...] = (_dot(hb, wq_ref[...]) * scale + _dot(pieces, pq_ref[...])).astype(BF16)
    k_ref[...] = (_dot(hb, wk_ref[...]) + _dot(pieces, pk_ref[...])).astype(BF16)
    vt_ref[0, 0] = _dot_nt(wvt_ref[...], hb).astype(BF16)
    sga_ref[...] = jax.nn.sigmoid(_dot(hb, wga_ref[...])).astype(BF16)
    sgb_ref[...] = jax.nn.sigmoid(_dot(hb, wgb_ref[...])).astype(BF16)


def _inproj(x2, scale1, shift1, wts, tm, seq):
    t, d = x2.shape
    tps = seq // tm
    bsz = t // seq
    wglu, wq, wk, wvt, wf, wga, wgb, bf, pq, pk = wts
    conv_ch = wglu.shape[1] // 2
    fw = wvt.shape[0]
    qw = wq.shape[1]
    mod_spec = pl.BlockSpec((1, 1, d), lambda i: (i // tps, 0, 0))
    row_spec = lambda n: pl.BlockSpec((tm, n), lambda i: (i, 0))
    return pl.pallas_call(
        functools.partial(_inproj_kernel, tiles_per_seq=tps, conv_ch=conv_ch),
        grid=(t // tm,),
        in_specs=[row_spec(d), mod_spec, mod_spec] + [_const_spec(w.shape) for w in wts],
        out_specs=[row_spec(conv_ch), row_spec(qw), row_spec(qw),
                   pl.BlockSpec((1, 1, fw, tm), lambda i: (i // tps, i % tps, 0, 0)),
                   row_spec(d), row_spec(d)],
        out_shape=[jax.ShapeDtypeStruct((t, conv_ch), F32),
                   jax.ShapeDtypeStruct((t, qw), BF16),
                   jax.ShapeDtypeStruct((t, qw), BF16),
                   jax.ShapeDtypeStruct((bsz, tps, fw, tm), BF16),
                   jax.ShapeDtypeStruct((t, d), BF16),
                   jax.ShapeDtypeStruct((t, d), BF16)],
        scratch_shapes=[pltpu.VMEM((1, V7X_LANES), F32)],
        compiler_params=_cparams(("arbitrary",)),
        name="inproj",
    )(x2, scale1, shift1, *wts)


def _conv_kernel(cur_ref, prev_ref, w_ref, b_ref, g_ref, be_ref, wout_ref, sga_ref, o_ref, ext_ref, shift_ref,
                 *, tiles_per_seq, chunk):
    i = pl.program_id(0)
    tm = cur_ref.shape[0]
    first = (i % tiles_per_seq) == 0
    ext_ref[0:CONV_HALO, :] = jnp.where(first, 0.0, prev_ref[...])
    ext_ref[CONV_HALO:, :] = cur_ref[...]
    w = w_ref[...]
    off = CONV_HALO - (CONV_WIDTH - 1)
    outs = []
    span = tm + CONV_HALO - 8
    for res in range(1, 8):
        shift_ref[res - 1] = ext_ref[res:res + span, :]
    for c0 in range(0, tm, chunk):
        acc = jnp.zeros((chunk, cur_ref.shape[1]), F32)
        for j in range(CONV_WIDTH):
            res, lo = (off + j) % 8, c0 + (off + j) // 8 * 8
            rows = ext_ref[lo:lo + chunk, :] if res == 0 else shift_ref[res - 1, lo:lo + chunk, :]
            acc = acc + w[j:j + 1, :] * rows
        outs.append(acc)
    v = jnp.concatenate(outs, axis=0) + b_ref[...]
    v = _ln(v) * g_ref[...] + be_ref[...]
    v = v * jax.nn.sigmoid(v)
    ya = _dot(v.astype(BF16), wout_ref[...])
    o_ref[...] = (sga_ref[...].astype(F32) * ya).astype(BF16)


def _conv(u, conv_w, conv_b, g, be, wout, sga, tm, seq):
    t, ch = u.shape
    d = wout.shape[1]
    tps = seq // tm
    halo_per_tile = tm // CONV_HALO
    return pl.pallas_call(
        functools.partial(_conv_kernel, tiles_per_seq=tps, chunk=64),
        grid=(t // tm,),
        in_specs=[pl.BlockSpec((tm, ch), lambda i: (i, 0)),
                  pl.BlockSpec((CONV_HALO, ch), lambda i: (jnp.maximum(i * halo_per_tile - 1, 0), 0)),
                  _const_spec(conv_w.shape), _const_spec(conv_b.shape), _const_spec(g.shape),
                  _const_spec(be.shape), _const_spec(wout.shape),
                  pl.BlockSpec((tm, d), lambda i: (i, 0))],
        out_specs=pl.BlockSpec((tm, d), lambda i: (i, 0)),
        out_shape=jax.ShapeDtypeStruct((t, d), BF16),
        scratch_shapes=[pltpu.VMEM((tm + CONV_HALO, ch), F32), pltpu.VMEM((7, tm + CONV_HALO - 8, ch), F32)],
        compiler_params=_cparams(("arbitrary",)),
        name="conv",
    )(u, u, conv_w, conv_b, g, be, wout, sga)


def _attn_kernel(q_ref, k_ref, vt_ref, o_ref, *, blk, heads):
    qi = pl.program_id(2)
    row = lax.broadcasted_iota(jnp.int32, (ATTN_EXTRA_ROWS, blk), 0)
    ones_rows = jnp.where(row == 0, 1.0, 0.0).astype(BF16)

    ncol = blk // ATTN_Q_COLS
    chains = [(j, c) for j in range(heads) for c in range(ncol)]

    def scores(kj, chain, masked):
        j, c = chain
        k = k_ref[pl.ds(pl.multiple_of(kj * blk, blk), blk), j * HEAD_PAD:(j + 1) * HEAD_PAD]
        q = q_ref[c * ATTN_Q_COLS:(c + 1) * ATTN_Q_COLS, j * HEAD_PAD:(j + 1) * HEAD_PAD]
        s = _dot_nt(k, q)
        if masked:
            kpos = lax.broadcasted_iota(jnp.int32, s.shape, 0)
            qpos = lax.broadcasted_iota(jnp.int32, s.shape, 1) + c * ATTN_Q_COLS
            s = jnp.where(kpos <= qpos, s, NEG_BIG)
        return s

    def probs(s, m):
        m_new = jnp.maximum(m, jnp.max(s, axis=0, keepdims=True))
        return jnp.exp2(s - m_new).astype(BF16), m_new

    def update(kj, chain, p, m, m_new, acc):
        j, _ = chain
        vt = vt_ref[kj, j * FOX_HEAD_DIM:(j + 1) * FOX_HEAD_DIM, :]
        lhs = jnp.concatenate([vt, ones_rows], axis=0)
        return jnp.exp2(m - m_new) * acc + _dot(lhs, p)

    def step(kj, carry, masked):
        n = len(chains)
        s, pm, out = {}, {}, [None] * n
        for i in range(n + 2):
            if i < n:
                s[i] = scores(kj, chains[i], masked)
            if 1 <= i <= n:
                pm[i - 1] = probs(s.pop(i - 1), carry[i - 1][0])
            if i >= 2:
                p, m_new = pm.pop(i - 2)
                m, acc = carry[i - 2]
                out[i - 2] = (m_new, update(kj, chains[i - 2], p, m, m_new, acc))
        return tuple(out)

    init = tuple((jnp.full((1, ATTN_Q_COLS), NEG_BIG, F32),
                  jnp.zeros((FOX_HEAD_DIM + ATTN_EXTRA_ROWS, ATTN_Q_COLS), F32)) for _ in chains)
    carry = lax.fori_loop(0, qi, lambda kj, cr: step(kj, cr, False), init)
    carry = step(qi, carry, True)
    for (j, c), (_, acc) in zip(chains, carry):
        o_ref[j * FOX_HEAD_DIM:(j + 1) * FOX_HEAD_DIM, c * ATTN_Q_COLS:(c + 1) * ATTN_Q_COLS] = (
            acc[:FOX_HEAD_DIM] / acc[FOX_HEAD_DIM:FOX_HEAD_DIM + 1]).astype(BF16)


def _attn(q, k, vt, blk, heads):
    bsz, seq, _ = q.shape
    nkb = seq // blk
    return pl.pallas_call(
        functools.partial(_attn_kernel, blk=blk, heads=heads),
        grid=(bsz, FOX_HEADS // heads, seq // blk),
        in_specs=[pl.BlockSpec((None, blk, heads * HEAD_PAD), lambda b, h, i: (b, i, h)),
                  pl.BlockSpec((None, seq, heads * HEAD_PAD), lambda b, h, i: (b, 0, h),
                               pipeline_mode=pl.Buffered(1)),
                  pl.BlockSpec((None, nkb, heads * FOX_HEAD_DIM, blk), lambda b, h, i: (b, 0, h, 0),
                               pipeline_mode=pl.Buffered(1))],
        out_specs=pl.BlockSpec((None, heads * FOX_HEAD_DIM, blk), lambda b, h, i: (b, h, i)),
        out_shape=jax.ShapeDtypeStruct((bsz, FOX_HEADS * FOX_HEAD_DIM, seq), BF16),
        compiler_params=_cparams(("arbitrary", "arbitrary", "arbitrary")),
        name="attn",
    )(q, k, vt)


def _mix_kernel(ot_ref, gya_ref, sgb_ref, x_ref, g1_ref, sc2_ref, sh2_ref, lg_ref, lb_ref,
                wfox_ref, wmix_ref, wrh_ref, wrl_ref, wsg_ref, wsu_ref, wsd_ref,
                x1_ref, hp_ref, st_ref, shr_ref, *, alpha):
    yb = _dot_tn(ot_ref[...], wfox_ref[...])
    merged = gya_ref[...].astype(F32) + sgb_ref[...].astype(F32) * yb
    y = _dot(merged.astype(BF16), wmix_ref[...])
    x1 = _ln(alpha * x_ref[...] + (1.0 + g1_ref[0]) * y) * lg_ref[...] + lb_ref[...]
    x1_ref[...] = x1
    h2 = _ln(x1) * (1.0 + sc2_ref[0]) + sh2_ref[0]
    hb = h2.astype(BF16)
    hl = (h2 - hb.astype(F32)).astype(BF16)

    _store_packed_rows(hp_ref, h2)

    logits_t = _dot_nt(wrh_ref[...], hb) + _dot_nt(wrl_ref[...], hb) + _dot_nt(wrh_ref[...], hl)
    st_ref[...] = jax.nn.sigmoid(logits_t)

    g = _dot(hb, wsg_ref[...])
    u = _dot(hb, wsu_ref[...])
    a = (g * jax.nn.sigmoid(g) * u).astype(BF16)
    shr_ref[...] = _dot(a, wsd_ref[...])


def _mix(ot, gya, sgb, x2, gate1, scale2, shift2, lg, lb, wts, tm, seq, alpha):
    t, d = x2.shape
    tps = seq // tm
    fw = ot.shape[1]
    n_exp = wts[2].shape[0]
    mod_spec = pl.BlockSpec((1, 1, d), lambda i: (i // tps, 0, 0))
    row_spec = lambda n: pl.BlockSpec((tm, n), lambda i: (i, 0))
    return pl.pallas_call(
        functools.partial(_mix_kernel, alpha=alpha),
        grid=(t // tm,),
        in_specs=[pl.BlockSpec((None, fw, tm), lambda i: (i // tps, 0, i % tps)),
                  row_spec(d), row_spec(d), row_spec(d), mod_spec, mod_spec, mod_spec,
                  _const_spec(lg.shape), _const_spec(lb.shape)] + [_const_spec(w.shape) for w in wts],
        out_specs=[row_spec(d), pl.BlockSpec((tm * ROW_SLABS, V7X_LANES), lambda i: (i, 0)),
                   pl.BlockSpec((n_exp, tm), lambda i: (0, i)), row_spec(d)],
        out_shape=[jax.ShapeDtypeStruct((t, d), F32),
                   jax.ShapeDtypeStruct((t * ROW_SLABS, V7X_LANES), jnp.uint32),
                   jax.ShapeDtypeStruct((n_exp, t), F32),
                   jax.ShapeDtypeStruct((t, d), F32)],
        compiler_params=_cparams(("arbitrary",)),
        name="mix",
    )(ot, gya, sgb, x2, gate1, scale2, shift2, lg, lb, *wts)


def _route_kernel(st_ref, bias_ref, idx_ref, wts_ref, rank_ref, cnt_ref, carry_ref):
    i = pl.program_id(0)
    n_exp, tr = st_ref.shape
    gsz = n_exp // N_GROUPS
    neg_inf = -jnp.inf

    @pl.when(i == 0)
    def _():
        carry_ref[...] = jnp.zeros_like(carry_ref)

    shape3 = (N_GROUPS, gsz, tr)

    def max01(v):
        return jnp.max(jnp.max(v, axis=1, keepdims=True), axis=0, keepdims=True)

    def min01(v):
        return jnp.min(jnp.min(v, axis=1, keepdims=True), axis=0, keepdims=True)

    def sum01(v):
        return jnp.sum(jnp.sum(v, axis=1, keepdims=True), axis=0, keepdims=True)

    sc = st_ref[...].reshape(shape3)
    gsel = (st_ref[...] + bias_ref[...]).reshape(shape3)
    pos = lax.broadcasted_iota(jnp.int32, shape3, 1)
    m1 = jnp.max(gsel, axis=1, keepdims=True)
    i1 = jnp.min(jnp.where(gsel == m1, pos, gsz), axis=1, keepdims=True)
    m2 = jnp.max(jnp.where(pos == i1, neg_inf, gsel), axis=1, keepdims=True)
    gs = m1 + m2

    gid = lax.broadcasted_iota(jnp.int32, gs.shape, 0)
    gkeep = jnp.zeros(gs.shape, F32)
    for _ in range(TOPK_GROUPS):
        mx = jnp.max(gs, axis=0, keepdims=True)
        gi = jnp.min(jnp.where(gs == mx, gid, N_GROUPS), axis=0, keepdims=True)
        hit = gid == gi
        gkeep = gkeep + jnp.where(hit, 1.0, 0.0)
        gs = jnp.where(hit, neg_inf, gs)

    cur = jnp.where(jnp.broadcast_to(gkeep, shape3) > 0.5, gsel, neg_inf)
    eid = lax.broadcasted_iota(jnp.int32, shape3, 0) * gsz + pos
    onehot = jnp.zeros(shape3, F32)
    idxs, ws = [], []
    wsum = jnp.zeros((1, 1, tr), F32)
    for _ in range(TOP_K):
        mx = max01(cur)
        ik = min01(jnp.where(cur == mx, eid, n_exp))
        hit = eid == ik
        wk = sum01(jnp.where(hit, sc, 0.0))
        idxs.append(ik)
        ws.append(wk)
        wsum = wsum + wk
        onehot = onehot + jnp.where(hit, 1.0, 0.0)
        cur = jnp.where(hit, neg_inf, cur)

    ra = lax.broadcasted_iota(jnp.int32, (tr, tr), 0)
    rb = lax.broadcasted_iota(jnp.int32, (tr, tr), 1)
    upper = jnp.where(ra < rb, 1.0, 0.0).astype(BF16)
    onehot2 = onehot.reshape(n_exp, tr)
    prior = (_dot(onehot2.astype(BF16), upper) + carry_ref[...]).reshape(shape3)
    for slot in range(TOP_K):
        idx_ref[slot:slot + 1, :] = idxs[slot].reshape(1, tr)
        wts_ref[slot:slot + 1, :] = (ws[slot] / wsum * ROUTED_SCALE).reshape(1, tr)
        rk = sum01(jnp.where(eid == idxs[slot], prior, 0.0))
        rank_ref[slot:slot + 1, :] = rk.reshape(1, tr).astype(jnp.int32)
    total = carry_ref[...] + jnp.sum(onehot2, axis=1, keepdims=True)
    carry_ref[...] = total
    cnt_ref[...] = jnp.broadcast_to(total, cnt_ref.shape).astype(jnp.int32)


def _route(scores_t, bias_col, tr):
    n_exp, t = scores_t.shape
    slot_spec = pl.BlockSpec((TOP_K, tr), lambda i: (0, i))
    return pl.pallas_call(
        _route_kernel,
        grid=(t // tr,),
        in_specs=[pl.BlockSpec((n_exp, tr), lambda i: (0, i)), _const_spec(bias_col.shape)],
        out_specs=[slot_spec, slot_spec, slot_spec, _const_spec((n_exp, V7X_LANES))],
        out_shape=[jax.ShapeDtypeStruct((TOP_K, t), jnp.int32),
                   jax.ShapeDtypeStruct((TOP_K, t), F32),
                   jax.ShapeDtypeStruct((TOP_K, t), jnp.int32),
                   jax.ShapeDtypeStruct((n_exp, V7X_LANES), jnp.int32)],
        scratch_shapes=[pltpu.VMEM((n_exp, 1), F32)],
        compiler_params=_cparams(("arbitrary",)),
        name="route",
    )(scores_t, bias_col)


def _dest_kernel(pstart_ref, idx_ref, rank_ref, o_ref):
    idx = idx_ref[...]
    n_exp = pstart_ref.shape[0]

    def body(e, acc):
        return acc + jnp.where(idx == e, pstart_ref[e], 0)

    o_ref[...] = lax.fori_loop(0, n_exp, body, rank_ref[...]) * ROW_SLABS


def _dest(pstart, idx, rank, tr):
    k, t = idx.shape
    spec = pl.BlockSpec((k, tr), lambda i, ps: (0, i))
    return pl.pallas_call(
        _dest_kernel,
        grid_spec=pltpu.PrefetchScalarGridSpec(
            num_scalar_prefetch=1, grid=(t // tr,), in_specs=[spec, spec], out_specs=spec),
        out_shape=jax.ShapeDtypeStruct((k, t), jnp.int32),
        compiler_params=_cparams(("arbitrary",)),
        name="dest",
    )(pstart, idx, rank)


def _dispatch_kernel(dest_ref, hp_ref, xs_ref, sem):
    td = hp_ref.shape[0] // ROW_SLABS

    def row_copy(r, slot):
        dst = pl.multiple_of(dest_ref[r * TOP_K + slot], ROW_SLABS)
        src = pl.multiple_of(r * ROW_SLABS, ROW_SLABS)
        return pltpu.make_async_copy(hp_ref.at[pl.ds(src, ROW_SLABS)], xs_ref.at[pl.ds(dst, ROW_SLABS)], sem)

    def start(r, c):
        for slot in range(TOP_K):
            row_copy(r, slot).start(priority=slot % 2)
        return c

    def wait(r, c):
        for slot in range(TOP_K):
            row_copy(r, slot).wait()
        return c

    lax.fori_loop(0, td, start, 0, unroll=4)
    lax.fori_loop(0, td, wait, 0, unroll=8)


def _dispatch(dest_flat, hp, n_pad, td):
    t = hp.shape[0] // ROW_SLABS
    return pl.pallas_call(
        _dispatch_kernel,
        grid=(t // td,),
        in_specs=[pl.BlockSpec((td * TOP_K,), lambda i: (i,), memory_space=pltpu.SMEM),
                  pl.BlockSpec((td * ROW_SLABS, V7X_LANES), lambda i: (i, 0))],
        out_specs=pl.BlockSpec(memory_space=pl.ANY),
        out_shape=jax.ShapeDtypeStruct((n_pad * ROW_SLABS, V7X_LANES), hp.dtype),
        scratch_shapes=[pltpu.SemaphoreType.DMA(())],
        compiler_params=_cparams(("arbitrary",)),
        name="dispatch",
    )(dest_flat, hp)


def _expert_kernel(be_ref, rows_ref, seq_ref, cnt_ref, xs_hbm, wg_hbm, wu_hbm, wd_hbm, ys_hbm,
                   xbuf, ybuf, wgf, wuf, wdf, wg_s, wu_s, wd_s, sem_x, sem_y, sem_w, *, layer):
    n = cnt_ref[0]
    n_seq = cnt_ref[1]
    blk_rows = EXPERT_BLOCK * ROW_SLABS
    nx, nw = EXPERT_X_AHEAD + 1, EXPERT_W_AHEAD + 1

    def block_rows(q):
        start = q * blk_rows
        return pl.ds(start if isinstance(q, int) else pl.multiple_of(start, blk_rows), blk_rows)

    def x_copy(q, slot):
        return pltpu.make_async_copy(xs_hbm.at[block_rows(q)], xbuf.at[slot], sem_x.at[slot])

    def y_copy(q, slot):
        return pltpu.make_async_copy(ybuf.at[slot], ys_hbm.at[block_rows(q)], sem_y.at[slot])

    def w_copies(k, ws):
        e = seq_ref[k]
        out = []
        for hbm, buf in ((wg_hbm, wgf), (wu_hbm, wuf), (wd_hbm, wdf)):
            half = buf.shape[1] // 2
            for part in range(2):
                rows = pl.ds(part * half, half)
                out.append((pltpu.make_async_copy(hbm.at[layer, e, rows], buf.at[ws, rows], sem_w.at[ws]), part))
        return out

    for j in range(EXPERT_X_AHEAD):
        @pl.when(j < n)
        def _():
            x_copy(j, j).start(priority=1)

    for j in range(EXPERT_W_AHEAD):
        @pl.when(j < n_seq)
        def _():
            for cp, prio in w_copies(j, j):
                cp.start(priority=prio)

    def body(q, k):
        slot = q % 2
        fresh = jnp.logical_or(q == 0, be_ref[q] != be_ref[jnp.maximum(q - 1, 0)])

        @pl.when(fresh)
        def _():
            ws = k % nw
            for cp, _ in w_copies(k, ws):
                cp.wait()
            wg_s[...] = wgf[ws].astype(BF16)
            wu_s[...] = wuf[ws].astype(BF16)
            wd_s[...] = wdf[ws].astype(BF16)

            @pl.when(k + EXPERT_W_AHEAD < n_seq)
            def _():
                for cp, prio in w_copies(k + EXPERT_W_AHEAD, (k + EXPERT_W_AHEAD) % nw):
                    cp.start(priority=prio)

        @pl.when(q + EXPERT_X_AHEAD < n)
        def _():
            x_copy(q + EXPERT_X_AHEAD, (q + EXPERT_X_AHEAD) % nx).start(priority=1)

        x_copy(q, q % nx).wait()

        @pl.when(q >= 2)
        def _():
            y_copy(q - 2, slot).wait()

        x = jnp.concatenate(_load_packed_rows(xbuf, EXPERT_BLOCK, lead=(q % nx,)), axis=1)
        valid = lax.broadcasted_iota(jnp.int32, x.shape, 0) < rows_ref[q]
        x = jnp.where(valid, x, 0.0).astype(BF16)
        g = _dot(x, wg_s[...])
        u = _dot(x, wu_s[...])
        a = (g * jax.nn.sigmoid(g) * u).astype(BF16)
        _store_packed_rows(ybuf.at[slot], _dot(a, wd_s[...]))
        y_copy(q, slot).start()
        return k + fresh.astype(jnp.int32)

    lax.fori_loop(0, n, body, jnp.int32(0))

    @pl.when(n >= 2)
    def _():
        y_copy(n - 2, n % 2).wait()

    y_copy(n - 1, (n - 1) % 2).wait()


def _experts(block_expert, block_rows, expert_seq, counts2, xs, wg, wu, wd, layer):
    _, n_exp, d, f = wg.shape
    blk = (EXPERT_BLOCK * ROW_SLABS, V7X_LANES)
    nx, nw = EXPERT_X_AHEAD + 1, EXPERT_W_AHEAD + 1
    any_spec = pl.BlockSpec(memory_space=pl.ANY)
    return pl.pallas_call(
        functools.partial(_expert_kernel, layer=layer),
        grid_spec=pltpu.PrefetchScalarGridSpec(
            num_scalar_prefetch=4, grid=(1,),
            in_specs=[any_spec, any_spec, any_spec, any_spec],
            out_specs=any_spec,
            scratch_shapes=[pltpu.VMEM((nx,) + blk, xs.dtype), pltpu.VMEM((2,) + blk, xs.dtype),
                            pltpu.VMEM((nw, d, f), F32), pltpu.VMEM((nw, d, f), F32), pltpu.VMEM((nw, f, d), F32),
                            pltpu.VMEM((d, f), BF16), pltpu.VMEM((d, f), BF16), pltpu.VMEM((f, d), BF16),
                            pltpu.SemaphoreType.DMA((nx,)), pltpu.SemaphoreType.DMA((2,)),
                            pltpu.SemaphoreType.DMA((nw,))]),
        out_shape=jax.ShapeDtypeStruct(xs.shape, xs.dtype),
        compiler_params=_cparams(("arbitrary",)),
        name="experts",
    )(block_expert, block_rows, expert_seq, counts2, xs, wg, wu, wd)


def _combine_kernel(dest_ref, ys_ref, w_ref, shr_ref, x1_ref, g2_ref, lg_ref, lb_ref, o_ref, buf, sem, *, alpha):
    tc = x1_ref.shape[0]

    def row_copy(r, slot):
        src = pl.multiple_of(dest_ref[r * TOP_K + slot], ROW_SLABS)
        dst = pl.multiple_of(r * ROW_SLABS, ROW_SLABS)
        return pltpu.make_async_copy(ys_ref.at[pl.ds(src, ROW_SLABS)], buf.at[slot, pl.ds(dst, ROW_SLABS)], sem)

    def start(r, c):
        for slot in range(TOP_K):
            row_copy(r, slot).start(priority=slot % 2)
        return c

    def wait(r, c):
        for slot in range(TOP_K):
            row_copy(r, slot).wait()
        return c

    lax.fori_loop(0, tc, start, 0, unroll=4)
    lax.fori_loop(0, tc, wait, 0, unroll=8)

    w = w_ref[...]
    chunks = None
    for slot in range(TOP_K):
        wk = w[:, slot:slot + 1]
        part = [c * wk for c in _load_packed_rows(buf, tc, lead=(slot,))]
        chunks = part if chunks is None else [a + b for a, b in zip(chunks, part)]
    y = shr_ref[...] + jnp.concatenate(chunks, axis=1)
    z = alpha * x1_ref[...] + (1.0 + g2_ref[0]) * y
    o_ref[...] = _ln(z) * lg_ref[...] + lb_ref[...]


def _combine(dest_flat, ys, wts_tk, shared, x1, gate2, lg, lb, tc, seq, alpha):
    t, d = x1.shape
    tps = seq // tc
    row_spec = lambda n: pl.BlockSpec((tc, n), lambda i: (i, 0))
    return pl.pallas_call(
        functools.partial(_combine_kernel, alpha=alpha),
        grid=(t // tc,),
        in_specs=[pl.BlockSpec((tc * TOP_K,), lambda i: (i,), memory_space=pltpu.SMEM),
                  pl.BlockSpec(memory_space=pl.ANY),
                  row_spec(TOP_K), row_spec(d), row_spec(d),
                  pl.BlockSpec((1, 1, d), lambda i: (i // tps, 0, 0)),
                  _const_spec(lg.shape), _const_spec(lb.shape)],
        out_specs=row_spec(d),
        out_shape=jax.ShapeDtypeStruct((t, d), F32),
        scratch_shapes=[pltpu.VMEM((TOP_K, tc * ROW_SLABS, V7X_LANES), jnp.uint32), pltpu.SemaphoreType.DMA(())],
        compiler_params=_cparams(("arbitrary",)),
        name="combine",
    )(dest_flat, ys, wts_tk, shared, x1, gate2, lg, lb)


def _placement():
    pq = np.zeros((V7X_LANES, FOX_HEADS * HEAD_PAD), np.float32)
    pk = np.zeros((V7X_LANES, FOX_HEADS * HEAD_PAD), np.float32)
    for h in range(FOX_HEADS):
        base = h * HEAD_PAD + FOX_HEAD_DIM
        for piece in range(3):
            pq[piece * 8 + h, base + piece] = 1.0
            pk[24, base + piece] = 1.0
            pq[24, base + 3 + piece] = 1.0
            pk[piece * 8 + h, base + 3 + piece] = -1.0
    return jnp.asarray(pq, BF16), jnp.asarray(pk, BF16)


def _inproj_weights(w_in, b_forget, d):
    conv2 = d
    fw = FOX_HEADS * FOX_HEAD_DIM
    o1, o2, o3, o4 = conv2, conv2 + fw, conv2 + 2 * fw, conv2 + 3 * fw
    o5 = o4 + FOX_HEADS
    o6 = o5 + d

    def pad_heads(w):
        w = w.reshape(d, FOX_HEADS, FOX_HEAD_DIM)
        w = jnp.pad(w, ((0, 0), (0, 0), (0, HEAD_PAD - FOX_HEAD_DIM)))
        return w.reshape(d, FOX_HEADS * HEAD_PAD).astype(BF16)

    wglu = w_in[:, :o1].astype(BF16)
    wq = pad_heads(w_in[:, o1:o2])
    wk = pad_heads(w_in[:, o2:o3])
    wvt = w_in[:, o3:o4].T.astype(BF16)
    wf8 = w_in[:, o4:o5]
    wf = jnp.pad(jnp.concatenate([wf8, wf8, wf8], axis=1), ((0, 0), (0, V7X_LANES - 3 * FOX_HEADS))).astype(BF16)
    bf = jnp.pad(jnp.concatenate([b_forget, b_forget, b_forget]), (0, V7X_LANES - 3 * FOX_HEADS))[None, :].astype(F32)
    wga = w_in[:, o5:o6].astype(BF16)
    wgb = w_in[:, o6:].astype(BF16)
    pq, pk = _placement()
    return (wglu, wq, wk, wvt, wf, wga, wgb, bf, pq, pk)


def _layer(x2, ada, bsz, seq, w_in, b_forget, conv_w, conv_b, conv_ln_g, conv_ln_b, w_conv_out, w_fox_out,
           w_mix_out, ln1_g, ln1_b, w_router, router_bias, w_exp_gate, w_exp_up, w_exp_down,
           w_sh_gate, w_sh_up, w_sh_down, ln2_g, ln2_b, depth, layer):
    t, d = x2.shape
    n_exp = w_router.shape[1]
    alpha = (2.0 * depth) ** 0.25
    mods = [ada[:bsz, j * d:(j + 1) * d][:, None, :] for j in range(6)]
    shift1, scale1, gate1, shift2, scale2, gate2 = mods

    tm = min(512, seq)
    u, q, k, vt, sga, sgb = _inproj(x2, scale1, shift1, _inproj_weights(w_in, b_forget, d), tm, seq)

    tcv = min(256, seq)
    conv_w_pad = jnp.pad(conv_w, ((0, CONV_HALO - CONV_WIDTH), (0, 0)))
    gya = _conv(u, conv_w_pad, conv_b[None, :], conv_ln_g[None, :], conv_ln_b[None, :],
                w_conv_out.astype(BF16), sga, tcv, seq)

    ot = _attn(q.reshape(bsz, seq, -1), k.reshape(bsz, seq, -1), vt, tm, ATTN_HEADS_PER_STEP)

    tmx = min(256, seq)
    wr_t = w_router.T
    wr_h = wr_t.astype(BF16)
    wr_l = (wr_t - wr_h.astype(F32)).astype(BF16)
    mix_w = (w_fox_out.astype(BF16), w_mix_out.astype(BF16), wr_h, wr_l,
             w_sh_gate.astype(BF16), w_sh_up.astype(BF16), w_sh_down.astype(BF16))
    x1, hp, scores_t, shared = _mix(ot, gya, sgb, x2, gate1, scale2, shift2, ln1_g[None, :], ln1_b[None, :],
                                    mix_w, tmx, seq, alpha)

    tr = min(512, t)
    idx, wts, rank, cnt = _route(scores_t, router_bias[:, None], tr)

    counts = cnt[:, 0]
    padded = (counts + EXPERT_BLOCK - 1) // EXPERT_BLOCK * EXPERT_BLOCK
    pend = jnp.cumsum(padded)
    pstart = (pend - padded).astype(jnp.int32)
    n_assign = t * TOP_K
    n_pad = -(-(n_assign + n_exp * (EXPERT_BLOCK - 1)) // EXPERT_BLOCK) * EXPERT_BLOCK
    n_blocks = n_pad // EXPERT_BLOCK
    block_start = jnp.arange(n_blocks, dtype=jnp.int32) * EXPERT_BLOCK
    block_expert = jnp.minimum(jnp.sum(pend[None, :] <= block_start[:, None], axis=1), n_exp - 1).astype(jnp.int32)
    block_rows = jnp.clip((pstart + counts)[block_expert] - block_start, 0, EXPERT_BLOCK).astype(jnp.int32)
    n_used = (pend[-1:] // EXPERT_BLOCK).astype(jnp.int32)
    owns = counts > 0
    expert_seq = jnp.nonzero(owns, size=n_exp, fill_value=0)[0].astype(jnp.int32)
    counts2 = jnp.concatenate([n_used, jnp.sum(owns, dtype=jnp.int32)[None]])

    dest = _dest(pstart, idx, rank, tr)
    dest_flat = dest.T.reshape(-1)
    xs = _dispatch(dest_flat, hp, n_pad, min(256, t))
    ys = _experts(block_expert, block_rows, expert_seq, counts2, xs, w_exp_gate, w_exp_up, w_exp_down, layer)
    return _combine(dest_flat, ys, wts.T, shared, x1, gate2, ln2_g[None, :], ln2_b[None, :],
                    min(128, seq), seq, alpha)


def kernel(x, c, w_ada, b_ada, w_in, b_forget, conv_w, conv_b, conv_ln_g, conv_ln_b, w_conv_out, w_fox_out,
           w_mix_out, ln1_g, ln1_b, w_router, router_bias, w_exp_gate, w_exp_up, w_exp_down, w_sh_gate,
           w_sh_up, w_sh_down, ln2_g, ln2_b):
    bsz, seq, d = x.shape
    depth = w_ada.shape[0]
    c_pad = jnp.pad(c, ((0, 8 - bsz), (0, 0)))
    x2 = x.reshape(bsz * seq, d)
    for l in range(depth):
        ada = _ada(c_pad, w_ada[l], b_ada[l][None, :])
        x2 = _layer(x2, ada, bsz, seq, w_in[l], b_forget[l], conv_w[l], conv_b[l], conv_ln_g[l], conv_ln_b[l],
                    w_conv_out[l], w_fox_out[l], w_mix_out[l], ln1_g[l], ln1_b[l], w_router[l], router_bias[l],
                    w_exp_gate, w_exp_up, w_exp_down, w_sh_gate[l], w_sh_up[l], w_sh_down[l],
                    ln2_g[l], ln2_b[l], depth, l)
    return x2.reshape(bsz, seq, d)
```

```python
import functools

import jax
import jax.numpy as jnp
import numpy as np
from jax import lax
from jax.experimental import pallas as pl
from jax.experimental.pallas import tpu as pltpu

F32 = jnp.float32
BF16 = jnp.bfloat16

LN_EPS = 1e-5
CONV_WIDTH = 31
FOX_HEADS = 8
FOX_HEAD_DIM = 64
N_GROUPS = 8
TOPK_GROUPS = 4
TOP_K = 8
ROUTED_SCALE = 2.5
EXPERT_BLOCK = 256
EXPERT_X_AHEAD = 2
EXPERT_W_AHEAD = 2
ROW_SLABS = 4

V7X_LANES = 128
HEAD_PAD = 128
CONV_HALO = 32
VMEM_LIMIT = 56 * 1024 * 1024
NEG_BIG = -1e30
LOG2E = 1.4426950408889634
ATTN_EXTRA_ROWS = 16
ATTN_HEADS_PER_STEP = 8
ATTN_Q_COLS = 256


def _cparams(sem):
    return pltpu.CompilerParams(dimension_semantics=sem, vmem_limit_bytes=VMEM_LIMIT)


def _ln(v):
    mu = jnp.mean(v, axis=-1, keepdims=True)
    vc = v - mu
    var = jnp.mean(vc * vc, axis=-1, keepdims=True)
    return vc * lax.rsqrt(var + LN_EPS)


def _split3(v):
    hi = v.astype(BF16)
    r1 = v - hi.astype(F32)
    mid = r1.astype(BF16)
    lo = (r1 - mid.astype(F32)).astype(BF16)
    return hi, mid, lo


def _dot(a, b):
    return jnp.dot(a, b, preferred_element_type=F32)


def _dot_nt(a, b):
    return lax.dot_general(a, b, (((1,), (1,)), ((), ())), preferred_element_type=F32)


def _dot_tn(a, b):
    return lax.dot_general(a, b, (((0,), (0,)), ((), ())), preferred_element_type=F32)


def _store_packed_rows(ref, v):
    n, d = v.shape
    half = d // 2
    vb = v.astype(BF16).astype(F32)
    lo_bits = lax.bitcast_convert_type(vb[:, :half], jnp.uint32)
    hi_bits = lax.bitcast_convert_type(vb[:, half:], jnp.uint32)
    words = (lo_bits >> 16) | (hi_bits & jnp.uint32(0xFFFF0000))
    for c in range(ROW_SLABS):
        ref[pl.ds(c, n, stride=ROW_SLABS), :] = words[:, c * V7X_LANES:(c + 1) * V7X_LANES]


def _load_packed_rows(ref, n, lead=()):
    lo, hi = [], []
    for c in range(ROW_SLABS):
        w = ref[lead + (pl.ds(c, n, stride=ROW_SLABS), slice(None))]
        lo.append(lax.bitcast_convert_type(w << 16, F32))
        hi.append(lax.bitcast_convert_type(w & jnp.uint32(0xFFFF0000), F32))
    return lo + hi


def _const_spec(shape):
    nd = len(shape)
    return pl.BlockSpec(shape, lambda *_: (0,) * nd)


def _ada_kernel(c_ref, w_ref, b_ref, o_ref):
    c = c_ref[...]
    cond = c * jax.nn.sigmoid(c)
    ch, cm, _ = _split3(cond)
    w = w_ref[...]
    wh, wm, _ = _split3(w)
    o_ref[...] = _dot(ch, wh) + _dot(ch, wm) + _dot(cm, wh) + b_ref[...]


def _ada(c_pad, w, b):
    rows, d = c_pad.shape
    n = w.shape[1]
    tn = 1024
    return pl.pallas_call(
        _ada_kernel,
        grid=(n // tn,),
        in_specs=[_const_spec((rows, d)),
                  pl.BlockSpec((d, tn), lambda j: (0, j)),
                  pl.BlockSpec((1, tn), lambda j: (0, j))],
        out_specs=pl.BlockSpec((rows, tn), lambda j: (0, j)),
        out_shape=jax.ShapeDtypeStruct((rows, n), F32),
        compiler_params=_cparams(("arbitrary",)),
        name="ada",
    )(c_pad, w, b)


def _inproj_kernel(x_ref, sc_ref, sh_ref, wglu_ref, wq_ref, wk_ref, wvt_ref, wf_ref, wga_ref, wgb_ref,
                   bf_ref, pq_ref, pk_ref,
                   u_ref, q_ref, k_ref, vt_ref, sga_ref, sgb_ref, carry_ref, *, tiles_per_seq, conv_ch):
    i = pl.program_id(0)
    tm = x_ref.shape[0]
    h = _ln(x_ref[...]) * (1.0 + sc_ref[0]) + sh_ref[0]
    hb = h.astype(BF16)

    glu = _dot(hb, wglu_ref[...])
    u_ref[...] = glu[:, :conv_ch] * jax.nn.sigmoid(glu[:, conv_ch:])

    f = _dot(hb, wf_ref[...]) + bf_ref[...]
    logf = jnp.minimum(f, 0.0) - jnp.log(1.0 + jnp.exp(-jnp.abs(f)))
    lh, lm, ll = _split3(logf)
    row = lax.broadcasted_iota(jnp.int32, (tm, tm), 0)
    col = lax.broadcasted_iota(jnp.int32, (tm, tm), 1)
    tri = jnp.where(row >= col, 1.0, 0.0).astype(BF16)
    cs = _dot(tri, lh) + _dot(tri, lm) + _dot(tri, ll)

    @pl.when(i % tiles_per_seq == 0)
    def _():
        carry_ref[...] = jnp.zeros_like(carry_ref)

    cum = cs + carry_ref[...]
    carry_ref[...] = cum[tm - 1:tm, :]

    ch, cm, cl = _split3(cum * LOG2E)
    lane = lax.broadcasted_iota(jnp.int32, cum.shape, 1)
    tail = jnp.where(lane == 24, 1.0, 0.0)
    pieces = jnp.where(lane < 8, ch.astype(F32), jnp.where(lane < 16, cm.astype(F32),
                       jnp.where(lane < 24, cl.astype(F32), tail))).astype(BF16)
    scale = FOX_HEAD_DIM ** -0.5 * LOG2E
    q_ref[...] = (_dot(hb, wq_ref[...]) * scale + _dot(pieces, pq_ref[...])).astype(BF16)
    k_ref[...] = (_dot(hb, wk_ref[...]) + _dot(pieces, pk_ref[...])).astype(BF16)
    vt_ref[0, 0] = _dot_nt(wvt_ref[...], hb).astype(BF16)
    sga_ref[...] = jax.nn.sigmoid(_dot(hb, wga_ref[...])).astype(BF16)
    sgb_ref[...] = jax.nn.sigmoid(_dot(hb, wgb_ref[...])).astype(BF16)


def _inproj(x2, scale1, shift1, wts, tm, seq):
    t, d = x2.shape
    tps = seq // tm
    bsz = t // seq
    wglu, wq, wk, wvt, wf, wga, wgb, bf, pq, pk = wts
    conv_ch = wglu.shape[1] // 2
    fw = wvt.shape[0]
    qw = wq.shape[1]
    mod_spec = pl.BlockSpec((1, 1, d), lambda i: (i // tps, 0, 0))
    row_spec = lambda n: pl.BlockSpec((tm, n), lambda i: (i, 0))
    return pl.pallas_call(
        functools.partial(_inproj_kernel, tiles_per_seq=tps, conv_ch=conv_ch),
        grid=(t // tm,),
        in_specs=[row_spec(d), mod_spec, mod_spec] + [_const_spec(w.shape) for w in wts],
        out_specs=[row_spec(conv_ch), row_spec(qw), row_spec(qw),
                   pl.BlockSpec((1, 1, fw, tm), lambda i: (i // tps, i % tps, 0, 0)),
                   row_spec(d), row_spec(d)],
        out_shape=[jax.ShapeDtypeStruct((t, conv_ch), F32),
                   jax.ShapeDtypeStruct((t, qw), BF16),
                   jax.ShapeDtypeStruct((t, qw), BF16),
                   jax.ShapeDtypeStruct((bsz, tps, fw, tm), BF16),
                   jax.ShapeDtypeStruct((t, d), BF16),
                   jax.ShapeDtypeStruct((t, d), BF16)],
        scratch_shapes=[pltpu.VMEM((1, V7X_LANES), F32)],
        compiler_params=_cparams(("arbitrary",)),
        name="inproj",
    )(x2, scale1, shift1, *wts)


def _conv_kernel(cur_ref, prev_ref, w_ref, b_ref, g_ref, be_ref, wout_ref, sga_ref, o_ref, ext_ref, shift_ref,
                 *, tiles_per_seq, chunk):
    i = pl.program_id(0)
    tm = cur_ref.shape[0]
    first = (i % tiles_per_seq) == 0
    ext_ref[0:CONV_HALO, :] = jnp.where(first, 0.0, prev_ref[...])
    ext_ref[CONV_HALO:, :] = cur_ref[...]
    w = w_ref[...]
    off = CONV_HALO - (CONV_WIDTH - 1)
    outs = []
    span = tm + CONV_HALO - 8
    for res in range(1, 8):
        shift_ref[res - 1] = ext_ref[res:res + span, :]
    for c0 in range(0, tm, chunk):
        acc = jnp.zeros((chunk, cur_ref.shape[1]), F32)
        for j in range(CONV_WIDTH):
            res, lo = (off + j) % 8, c0 + (off + j) // 8 * 8
            rows = ext_ref[lo:lo + chunk, :] if res == 0 else shift_ref[res - 1, lo:lo + chunk, :]
            acc = acc + w[j:j + 1, :] * rows
        outs.append(acc)
    v = jnp.concatenate(outs, axis=0) + b_ref[...]
    v = _ln(v) * g_ref[...] + be_ref[...]
    v = v * jax.nn.sigmoid(v)
    ya = _dot(v.astype(BF16), wout_ref[...])
    o_ref[...] = (sga_ref[...].astype(F32) * ya).astype(BF16)


def _conv(u, conv_w, conv_b, g, be, wout, sga, tm, seq):
    t, ch = u.shape
    d = wout.shape[1]
    tps = seq // tm
    halo_per_tile = tm // CONV_HALO
    return pl.pallas_call(
        functools.partial(_conv_kernel, tiles_per_seq=tps, chunk=64),
        grid=(t // tm,),
        in_specs=[pl.BlockSpec((tm, ch), lambda i: (i, 0)),
                  pl.BlockSpec((CONV_HALO, ch), lambda i: (jnp.maximum(i * halo_per_tile - 1, 0), 0)),
                  _const_spec(conv_w.shape), _const_spec(conv_b.shape), _const_spec(g.shape),
                  _const_spec(be.shape), _const_spec(wout.shape),
                  pl.BlockSpec((tm, d), lambda i: (i, 0))],
        out_specs=pl.BlockSpec((tm, d), lambda i: (i, 0)),
        out_shape=jax.ShapeDtypeStruct((t, d), BF16),
        scratch_shapes=[pltpu.VMEM((tm + CONV_HALO, ch), F32), pltpu.VMEM((7, tm + CONV_HALO - 8, ch), F32)],
        compiler_params=_cparams(("arbitrary",)),
        name="conv",
    )(u, u, conv_w, conv_b, g, be, wout, sga)


def _attn_kernel(q_ref, k_ref, vt_ref, o_ref, *, blk, heads):
    qi = pl.program_id(2)
    row = lax.broadcasted_iota(jnp.int32, (ATTN_EXTRA_ROWS, blk), 0)
    ones_rows = jnp.where(row == 0, 1.0, 0.0).astype(BF16)

    ncol = blk // ATTN_Q_COLS
    chains = [(j, c) for j in range(heads) for c in range(ncol)]

    def scores(kj, chain, masked):
        j, c = chain
        k = k_ref[pl.ds(pl.multiple_of(kj * blk, blk), blk), j * HEAD_PAD:(j + 1) * HEAD_PAD]
        q = q_ref[c * ATTN_Q_COLS:(c + 1) * ATTN_Q_COLS, j * HEAD_PAD:(j + 1) * HEAD_PAD]
        s = _dot_nt(k, q)
        if masked:
            kpos = lax.broadcasted_iota(jnp.int32, s.shape, 0)
            qpos = lax.broadcasted_iota(jnp.int32, s.shape, 1) + c * ATTN_Q_COLS
            s = jnp.where(kpos <= qpos, s, NEG_BIG)
        return s

    def probs(s, m):
        m_new = jnp.maximum(m, jnp.max(s, axis=0, keepdims=True))
        return jnp.exp2(s - m_new).astype(BF16), m_new

    def update(kj, chain, p, m, m_new, acc):
        j, _ = chain
        vt = vt_ref[kj, j * FOX_HEAD_DIM:(j + 1) * FOX_HEAD_DIM, :]
        lhs = jnp.concatenate([vt, ones_rows], axis=0)
        return jnp.exp2(m - m_new) * acc + _dot(lhs, p)

    def step(kj, carry, masked):
        n = len(chains)
        s, pm, out = {}, {}, [None] * n
        for i in range(n + 2):
            if i < n:
                s[i] = scores(kj, chains[i], masked)
            if 1 <= i <= n:
                pm[i - 1] = probs(s.pop(i - 1), carry[i - 1][0])
            if i >= 2:
                p, m_new = pm.pop(i - 2)
                m, acc = carry[i - 2]
                out[i - 2] = (m_new, update(kj, chains[i - 2], p, m, m_new, acc))
        return tuple(out)

    init = tuple((jnp.full((1, ATTN_Q_COLS), NEG_BIG, F32),
                  jnp.zeros((FOX_HEAD_DIM + ATTN_EXTRA_ROWS, ATTN_Q_COLS), F32)) for _ in chains)
    carry = lax.fori_loop(0, qi, lambda kj, cr: step(kj, cr, False), init)
    carry = step(qi, carry, True)
    for (j, c), (_, acc) in zip(chains, carry):
        o_ref[j * FOX_HEAD_DIM:(j + 1) * FOX_HEAD_DIM, c * ATTN_Q_COLS:(c + 1) * ATTN_Q_COLS] = (
            acc[:FOX_HEAD_DIM] / acc[FOX_HEAD_DIM:FOX_HEAD_DIM + 1]).astype(BF16)


def _attn(q, k, vt, blk, heads):
    bsz, seq, _ = q.shape
    nkb = seq // blk
    return pl.pallas_call(
        functools.partial(_attn_kernel, blk=blk, heads=heads),
        grid=(bsz, FOX_HEADS // heads, seq // blk),
        in_specs=[pl.BlockSpec((None, blk, heads * HEAD_PAD), lambda b, h, i: (b, i, h)),
                  pl.BlockSpec((None, seq, heads * HEAD_PAD), lambda b, h, i: (b, 0, h),
                               pipeline_mode=pl.Buffered(1)),
                  pl.BlockSpec((None, nkb, heads * FOX_HEAD_DIM, blk), lambda b, h, i: (b, 0, h, 0),
                               pipeline_mode=pl.Buffered(1))],
        out_specs=pl.BlockSpec((None, heads * FOX_HEAD_DIM, blk), lambda b, h, i: (b, h, i)),
        out_shape=jax.ShapeDtypeStruct((bsz, FOX_HEADS * FOX_HEAD_DIM, seq), BF16),
        compiler_params=_cparams(("arbitrary", "arbitrary", "arbitrary")),
        name="attn",
    )(q, k, vt)


def _mix_kernel(ot_ref, gya_ref, sgb_ref, x_ref, g1_ref, sc2_ref, sh2_ref, lg_ref, lb_ref,
                wfox_ref, wmix_ref, wrh_ref, wrl_ref, wsg_ref, wsu_ref, wsd_ref,
                x1_ref, hp_ref, st_ref, shr_ref, *, alpha):
    yb = _dot_tn(ot_ref[...], wfox_ref[...])
    merged = gya_ref[...].astype(F32) + sgb_ref[...].astype(F32) * yb
    y = _dot(merged.astype(BF16), wmix_ref[...])
    x1 = _ln(alpha * x_ref[...] + (1.0 + g1_ref[0]) * y) * lg_ref[...] + lb_ref[...]
    x1_ref[...] = x1
    h2 = _ln(x1) * (1.0 + sc2_ref[0]) + sh2_ref[0]
    hb = h2.astype(BF16)
    hl = (h2 - hb.astype(F32)).astype(BF16)

    _store_packed_rows(hp_ref, h2)

    logits_t = _dot_nt(wrh_ref[...], hb) + _dot_nt(wrl_ref[...], hb) + _dot_nt(wrh_ref[...], hl)
    st_ref[...] = jax.nn.sigmoid(logits_t)

    g = _dot(hb, wsg_ref[...])
    u = _dot(hb, wsu_ref[...])
    a = (g * jax.nn.sigmoid(g) * u).astype(BF16)
    shr_ref[...] = _dot(a, wsd_ref[...])


def _mix(ot, gya, sgb, x2, gate1, scale2, shift2, lg, lb, wts, tm, seq, alpha):
    t, d = x2.shape
    tps = seq // tm
    fw = ot.shape[1]
    n_exp = wts[2].shape[0]
    mod_spec = pl.BlockSpec((1, 1, d), lambda i: (i // tps, 0, 0))
    row_spec = lambda n: pl.BlockSpec((tm, n), lambda i: (i, 0))
    return pl.pallas_call(
        functools.partial(_mix_kernel, alpha=alpha),
        grid=(t // tm,),
        in_specs=[pl.BlockSpec((None, fw, tm), lambda i: (i // tps, 0, i % tps)),
                  row_spec(d), row_spec(d), row_spec(d), mod_spec, mod_spec, mod_spec,
                  _const_spec(lg.shape), _const_spec(lb.shape)] + [_const_spec(w.shape) for w in wts],
        out_specs=[row_spec(d), pl.BlockSpec((tm * ROW_SLABS, V7X_LANES), lambda i: (i, 0)),
                   pl.BlockSpec((n_exp, tm), lambda i: (0, i)), row_spec(d)],
        out_shape=[jax.ShapeDtypeStruct((t, d), F32),
                   jax.ShapeDtypeStruct((t * ROW_SLABS, V7X_LANES), jnp.uint32),
                   jax.ShapeDtypeStruct((n_exp, t), F32),
                   jax.ShapeDtypeStruct((t, d), F32)],
        compiler_params=_cparams(("arbitrary",)),
        name="mix",
    )(ot, gya, sgb, x2, gate1, scale2, shift2, lg, lb, *wts)


def _route_kernel(st_ref, bias_ref, idx_ref, wts_ref, rank_ref, cnt_ref, carry_ref):
    i = pl.program_id(0)
    n_exp, tr = st_ref.shape
    gsz = n_exp // N_GROUPS
    neg_inf = -jnp.inf

    @pl.when(i == 0)
    def _():
        carry_ref[...] = jnp.zeros_like(carry_ref)

    shape3 = (N_GROUPS, gsz, tr)

    def max01(v):
        return jnp.max(jnp.max(v, axis=1, keepdims=True), axis=0, keepdims=True)

    def min01(v):
        return jnp.min(jnp.min(v, axis=1, keepdims=True), axis=0, keepdims=True)

    def sum01(v):
        return jnp.sum(jnp.sum(v, axis=1, keepdims=True), axis=0, keepdims=True)

    sc = st_ref[...].reshape(shape3)
    gsel = (st_ref[...] + bias_ref[...]).reshape(shape3)
    pos = lax.broadcasted_iota(jnp.int32, shape3, 1)
    m1 = jnp.max(gsel, axis=1, keepdims=True)
    i1 = jnp.min(jnp.where(gsel == m1, pos, gsz), axis=1, keepdims=True)
    m2 = jnp.max(jnp.where(pos == i1, neg_inf, gsel), axis=1, keepdims=True)
    gs = m1 + m2

    gid = lax.broadcasted_iota(jnp.int32, gs.shape, 0)
    gkeep = jnp.zeros(gs.shape, F32)
    for _ in range(TOPK_GROUPS):
        mx = jnp.max(gs, axis=0, keepdims=True)
        gi = jnp.min(jnp.where(gs == mx, gid, N_GROUPS), axis=0, keepdims=True)
        hit = gid == gi
        gkeep = gkeep + jnp.where(hit, 1.0, 0.0)
        gs = jnp.where(hit, neg_inf, gs)

    cur = jnp.where(jnp.broadcast_to(gkeep, shape3) > 0.5, gsel, neg_inf)
    eid = lax.broadcasted_iota(jnp.int32, shape3, 0) * gsz + pos
    onehot = jnp.zeros(shape3, F32)
    idxs, ws = [], []
    wsum = jnp.zeros((1, 1, tr), F32)
    for _ in range(TOP_K):
        mx = max01(cur)
        ik = min01(jnp.where(cur == mx, eid, n_exp))
        hit = eid == ik
        wk = sum01(jnp.where(hit, sc, 0.0))
        idxs.append(ik)
        ws.append(wk)
        wsum = wsum + wk
        onehot = onehot + jnp.where(hit, 1.0, 0.0)
        cur = jnp.where(hit, neg_inf, cur)

    ra = lax.broadcasted_iota(jnp.int32, (tr, tr), 0)
    rb = lax.broadcasted_iota(jnp.int32, (tr, tr), 1)
    upper = jnp.where(ra < rb, 1.0, 0.0).astype(BF16)
    onehot2 = onehot.reshape(n_exp, tr)
    prior = (_dot(onehot2.astype(BF16), upper) + carry_ref[...]).reshape(shape3)
    for slot in range(TOP_K):
        idx_ref[slot:slot + 1, :] = idxs[slot].reshape(1, tr)
        wts_ref[slot:slot + 1, :] = (ws[slot] / wsum * ROUTED_SCALE).reshape(1, tr)
        rk = sum01(jnp.where(eid == idxs[slot], prior, 0.0))
        rank_ref[slot:slot + 1, :] = rk.reshape(1, tr).astype(jnp.int32)
    total = carry_ref[...] + jnp.sum(onehot2, axis=1, keepdims=True)
    carry_ref[...] = total
    cnt_ref[...] = jnp.broadcast_to(total, cnt_ref.shape).astype(jnp.int32)


def _route(scores_t, bias_col, tr):
    n_exp, t = scores_t.shape
    slot_spec = pl.BlockSpec((TOP_K, tr), lambda i: (0, i))
    return pl.pallas_call(
        _route_kernel,
        grid=(t // tr,),
        in_specs=[pl.BlockSpec((n_exp, tr), lambda i: (0, i)), _const_spec(bias_col.shape)],
        out_specs=[slot_spec, slot_spec, slot_spec, _const_spec((n_exp, V7X_LANES))],
        out_shape=[jax.ShapeDtypeStruct((TOP_K, t), jnp.int32),
                   jax.ShapeDtypeStruct((TOP_K, t), F32),
                   jax.ShapeDtypeStruct((TOP_K, t), jnp.int32),
                   jax.ShapeDtypeStruct((n_exp, V7X_LANES), jnp.int32)],
        scratch_shapes=[pltpu.VMEM((n_exp, 1), F32)],
        compiler_params=_cparams(("arbitrary",)),
        name="route",
    )(scores_t, bias_col)


def _dest_kernel(pstart_ref, idx_ref, rank_ref, o_ref):
    idx = idx_ref[...]
    n_exp = pstart_ref.shape[0]

    def body(e, acc):
        return acc + jnp.where(idx == e, pstart_ref[e], 0)

    o_ref[...] = lax.fori_loop(0, n_exp, body, rank_ref[...]) * ROW_SLABS


def _dest(pstart, idx, rank, tr):
    k, t = idx.shape
    spec = pl.BlockSpec((k, tr), lambda i, ps: (0, i))
    return pl.pallas_call(
        _dest_kernel,
        grid_spec=pltpu.PrefetchScalarGridSpec(
            num_scalar_prefetch=1, grid=(t // tr,), in_specs=[spec, spec], out_specs=spec),
        out_shape=jax.ShapeDtypeStruct((k, t), jnp.int32),
        compiler_params=_cparams(("arbitrary",)),
        name="dest",
    )(pstart, idx, rank)


def _dispatch_kernel(dest_ref, hp_ref, xs_ref, sem):
    td = hp_ref.shape[0] // ROW_SLABS

    def row_copy(r, slot):
        dst = pl.multiple_of(dest_ref[r * TOP_K + slot], ROW_SLABS)
        src = pl.multiple_of(r * ROW_SLABS, ROW_SLABS)
        return pltpu.make_async_copy(hp_ref.at[pl.ds(src, ROW_SLABS)], xs_ref.at[pl.ds(dst, ROW_SLABS)], sem)

    def start(r, c):
        for slot in range(TOP_K):
            row_copy(r, slot).start(priority=slot % 2)
        return c

    def wait(r, c):
        for slot in range(TOP_K):
            row_copy(r, slot).wait()
        return c

    lax.fori_loop(0, td, start, 0, unroll=4)
    lax.fori_loop(0, td, wait, 0, unroll=8)


def _dispatch(dest_flat, hp, n_pad, td):
    t = hp.shape[0] // ROW_SLABS
    return pl.pallas_call(
        _dispatch_kernel,
        grid=(t // td,),
        in_specs=[pl.BlockSpec((td * TOP_K,), lambda i: (i,), memory_space=pltpu.SMEM),
                  pl.BlockSpec((td * ROW_SLABS, V7X_LANES), lambda i: (i, 0))],
        out_specs=pl.BlockSpec(memory_space=pl.ANY),
        out_shape=jax.ShapeDtypeStruct((n_pad * ROW_SLABS, V7X_LANES), hp.dtype),
        scratch_shapes=[pltpu.SemaphoreType.DMA(())],
        compiler_params=_cparams(("arbitrary",)),
        name="dispatch",
    )(dest_flat, hp)


def _expert_kernel(be_ref, rows_ref, seq_ref, cnt_ref, xs_hbm, wg_hbm, wu_hbm, wd_hbm, ys_hbm,
                   xbuf, ybuf, wgf, wuf, wdf, wg_s, wu_s, wd_s, sem_x, sem_y, sem_w, *, layer):
    n = cnt_ref[0]
    n_seq = cnt_ref[1]
    blk_rows = EXPERT_BLOCK * ROW_SLABS
    nx, nw = EXPERT_X_AHEAD + 1, EXPERT_W_AHEAD + 1

    def block_rows(q):
        start = q * blk_rows
        return pl.ds(start if isinstance(q, int) else pl.multiple_of(start, blk_rows), blk_rows)

    def x_copy(q, slot):
        return pltpu.make_async_copy(xs_hbm.at[block_rows(q)], xbuf.at[slot], sem_x.at[slot])

    def y_copy(q, slot):
        return pltpu.make_async_copy(ybuf.at[slot], ys_hbm.at[block_rows(q)], sem_y.at[slot])

    def w_copies(k, ws):
        e = seq_ref[k]
        out = []
        for hbm, buf in ((wg_hbm, wgf), (wu_hbm, wuf), (wd_hbm, wdf)):
            half = buf.shape[1] // 2
            for part in range(2):
                rows = pl.ds(part * half, half)
                out.append((pltpu.make_async_copy(hbm.at[layer, e, rows], buf.at[ws, rows], sem_w.at[ws]), part))
        return out

    for j in range(EXPERT_X_AHEAD):
        @pl.when(j < n)
        def _():
            x_copy(j, j).start(priority=1)

    for j in range(EXPERT_W_AHEAD):
        @pl.when(j < n_seq)
        def _():
            for cp, prio in w_copies(j, j):
                cp.start(priority=prio)

    def body(q, k):
        slot = q % 2
        fresh = jnp.logical_or(q == 0, be_ref[q] != be_ref[jnp.maximum(q - 1, 0)])

        @pl.when(fresh)
        def _():
            ws = k % nw
            for cp, _ in w_copies(k, ws):
                cp.wait()
            wg_s[...] = wgf[ws].astype(BF16)
            wu_s[...] = wuf[ws].astype(BF16)
            wd_s[...] = wdf[ws].astype(BF16)

            @pl.when(k + EXPERT_W_AHEAD < n_seq)
            def _():
                for cp, prio in w_copies(k + EXPERT_W_AHEAD, (k + EXPERT_W_AHEAD) % nw):
                    cp.start(priority=prio)

        @pl.when(q + EXPERT_X_AHEAD < n)
        def _():
            x_copy(q + EXPERT_X_AHEAD, (q + EXPERT_X_AHEAD) % nx).start(priority=1)

        x_copy(q, q % nx).wait()

        @pl.when(q >= 2)
        def _():
            y_copy(q - 2, slot).wait()

        x = jnp.concatenate(_load_packed_rows(xbuf, EXPERT_BLOCK, lead=(q % nx,)), axis=1)
        valid = lax.broadcasted_iota(jnp.int32, x.shape, 0) < rows_ref[q]
        x = jnp.where(valid, x, 0.0).astype(BF16)
        g = _dot(x, wg_s[...])
        u = _dot(x, wu_s[...])
        a = (g * jax.nn.sigmoid(g) * u).astype(BF16)
        _store_packed_rows(ybuf.at[slot], _dot(a, wd_s[...]))
        y_copy(q, slot).start()
        return k + fresh.astype(jnp.int32)

    lax.fori_loop(0, n, body, jnp.int32(0))

    @pl.when(n >= 2)
    def _():
        y_copy(n - 2, n % 2).wait()

    y_copy(n - 1, (n - 1) % 2).wait()


def _experts(block_expert, block_rows, expert_seq, counts2, xs, wg, wu, wd, layer):
    _, n_exp, d, f = wg.shape
    blk = (EXPERT_BLOCK * ROW_SLABS, V7X_LANES)
    nx, nw = EXPERT_X_AHEAD + 1, EXPERT_W_AHEAD + 1
    any_spec = pl.BlockSpec(memory_space=pl.ANY)
    return pl.pallas_call(
        functools.partial(_expert_kernel, layer=layer),
        grid_spec=pltpu.PrefetchScalarGridSpec(
            num_scalar_prefetch=4, grid=(1,),
            in_specs=[any_spec, any_spec, any_spec, any_spec],
            out_specs=any_spec,
            scratch_shapes=[pltpu.VMEM((nx,) + blk, xs.dtype), pltpu.VMEM((2,) + blk, xs.dtype),
                            pltpu.VMEM((nw, d, f), F32), pltpu.VMEM((nw, d, f), F32), pltpu.VMEM((nw, f, d), F32),
                            pltpu.VMEM((d, f), BF16), pltpu.VMEM((d, f), BF16), pltpu.VMEM((f, d), BF16),
                            pltpu.SemaphoreType.DMA((nx,)), pltpu.SemaphoreType.DMA((2,)),
                            pltpu.SemaphoreType.DMA((nw,))]),
        out_shape=jax.ShapeDtypeStruct(xs.shape, xs.dtype),
        compiler_params=_cparams(("arbitrary",)),
        name="experts",
    )(block_expert, block_rows, expert_seq, counts2, xs, wg, wu, wd)


def _combine_kernel(d0_ref, d1_ref, d2_ref, ys_ref, w_ref, shr_ref, x1_ref, g2_ref, lg_ref, lb_ref, o_ref,
                    buf_a, buf_b, sem, *, alpha, tc):
    i = pl.program_id(0)
    last = pl.num_programs(0) - 1

    def row_copy(dref, r, slot, buf, sem_idx):
        src = pl.multiple_of(dref[r * TOP_K + slot], ROW_SLABS)
        dst = r * ROW_SLABS if isinstance(r, int) else pl.multiple_of(r * ROW_SLABS, ROW_SLABS)
        return pltpu.make_async_copy(ys_ref.at[pl.ds(src, ROW_SLABS)], buf.at[slot, pl.ds(dst, ROW_SLABS)],
                                     sem.at[sem_idx])

    def issue_unrolled(dref, buf, sem_idx):
        for r in range(tc):
            for slot in range(TOP_K):
                row_copy(dref, r, slot, buf, sem_idx).start(priority=slot % 2)

    def issue_loop(dref, buf, sem_idx):
        def start(r, c):
            for slot in range(TOP_K):
                row_copy(dref, r, slot, buf, sem_idx).start(priority=slot % 2)
            return c
        lax.fori_loop(0, tc, start, 0, unroll=4)

    def wait_all(dref, buf, sem_idx):
        def wait(r, c):
            for slot in range(TOP_K):
                row_copy(dref, r, slot, buf, sem_idx).wait()
            return c
        lax.fori_loop(0, tc, wait, 0, unroll=8)

    def reduce_tile(buf, rows):
        w = w_ref[rows, :]
        chunks = None
        for slot in range(TOP_K):
            wk = w[:, slot:slot + 1]
            part = [c * wk for c in _load_packed_rows(buf, tc, lead=(slot,))]
            chunks = part if chunks is None else [a + b for a, b in zip(chunks, part)]
        y = shr_ref[rows, :] + jnp.concatenate(chunks, axis=1)
        z = alpha * x1_ref[rows, :] + (1.0 + g2_ref[0]) * y
        o_ref[rows, :] = _ln(z) * lg_ref[...] + lb_ref[...]

    @pl.when(i == 0)
    def _():
        issue_loop(d0_ref, buf_a, 0)

    wait_all(d0_ref, buf_a, 0)
    issue_unrolled(d1_ref, buf_b, 1)
    reduce_tile(buf_a, pl.ds(0, tc))

    wait_all(d1_ref, buf_b, 1)
    issue_unrolled(d2_ref, buf_a, 0)
    reduce_tile(buf_b, pl.ds(tc, tc))

    @pl.when(i == last)
    def _():
        wait_all(d2_ref, buf_a, 0)


def _combine(dest_flat, ys, wts_tk, shared, x1, gate2, lg, lb, tc, seq, alpha):
    t, d = x1.shape
    tps = seq // (2 * tc)
    n_tiles = t // tc
    row_spec = lambda n: pl.BlockSpec((2 * tc, n), lambda i: (i, 0))
    tile_dest = lambda f: pl.BlockSpec((tc * TOP_K,), f, memory_space=pltpu.SMEM)
    buf = pltpu.VMEM((TOP_K, tc * ROW_SLABS, V7X_LANES), jnp.uint32)
    return pl.pallas_call(
        functools.partial(_combine_kernel, alpha=alpha, tc=tc),
        grid=(n_tiles // 2,),
        in_specs=[tile_dest(lambda i: (2 * i,)), tile_dest(lambda i: (2 * i + 1,)),
                  tile_dest(lambda i: (jnp.minimum(2 * i + 2, n_tiles - 1),)),
                  pl.BlockSpec(memory_space=pl.ANY),
                  row_spec(TOP_K), row_spec(d), row_spec(d),
                  pl.BlockSpec((1, 1, d), lambda i: (i // tps, 0, 0)),
                  _const_spec(lg.shape), _const_spec(lb.shape)],
        out_specs=row_spec(d),
        out_shape=jax.ShapeDtypeStruct((t, d), F32),
        scratch_shapes=[buf, buf, pltpu.SemaphoreType.DMA((2,))],
        compiler_params=_cparams(("arbitrary",)),
        name="combine",
    )(dest_flat, dest_flat, dest_flat, ys, wts_tk, shared, x1, gate2, lg, lb)


def _placement():
    pq = np.zeros((V7X_LANES, FOX_HEADS * HEAD_PAD), np.float32)
    pk = np.zeros((V7X_LANES, FOX_HEADS * HEAD_PAD), np.float32)
    for h in range(FOX_HEADS):
        base = h * HEAD_PAD + FOX_HEAD_DIM
        for piece in range(3):
            pq[piece * 8 + h, base + piece] = 1.0
            pk[24, base + piece] = 1.0
            pq[24, base + 3 + piece] = 1.0
            pk[piece * 8 + h, base + 3 + piece] = -1.0
    return jnp.asarray(pq, BF16), jnp.asarray(pk, BF16)


def _inproj_weights(w_in, b_forget, d):
    conv2 = d
    fw = FOX_HEADS * FOX_HEAD_DIM
    o1, o2, o3, o4 = conv2, conv2 + fw, conv2 + 2 * fw, conv2 + 3 * fw
    o5 = o4 + FOX_HEADS
    o6 = o5 + d

    def pad_heads(w):
        w = w.reshape(d, FOX_HEADS, FOX_HEAD_DIM)
        w = jnp.pad(w, ((0, 0), (0, 0), (0, HEAD_PAD - FOX_HEAD_DIM)))
        return w.reshape(d, FOX_HEADS * HEAD_PAD).astype(BF16)

    wglu = w_in[:, :o1].astype(BF16)
    wq = pad_heads(w_in[:, o1:o2])
    wk = pad_heads(w_in[:, o2:o3])
    wvt = w_in[:, o3:o4].T.astype(BF16)
    wf8 = w_in[:, o4:o5]
    wf = jnp.pad(jnp.concatenate([wf8, wf8, wf8], axis=1), ((0, 0), (0, V7X_LANES - 3 * FOX_HEADS))).astype(BF16)
    bf = jnp.pad(jnp.concatenate([b_forget, b_forget, b_forget]), (0, V7X_LANES - 3 * FOX_HEADS))[None, :].astype(F32)
    wga = w_in[:, o5:o6].astype(BF16)
    wgb = w_in[:, o6:].astype(BF16)
    pq, pk = _placement()
    return (wglu, wq, wk, wvt, wf, wga, wgb, bf, pq, pk)


def _layer(x2, ada, bsz, seq, w_in, b_forget, conv_w, conv_b, conv_ln_g, conv_ln_b, w_conv_out, w_fox_out,
           w_mix_out, ln1_g, ln1_b, w_router, router_bias, w_exp_gate, w_exp_up, w_exp_down,
           w_sh_gate, w_sh_up, w_sh_down, ln2_g, ln2_b, depth, layer):
    t, d = x2.shape
    n_exp = w_router.shape[1]
    alpha = (2.0 * depth) ** 0.25
    mods = [ada[:bsz, j * d:(j + 1) * d][:, None, :] for j in range(6)]
    shift1, scale1, gate1, shift2, scale2, gate2 = mods

    tm = min(512, seq)
    u, q, k, vt, sga, sgb = _inproj(x2, scale1, shift1, _inproj_weights(w_in, b_forget, d), tm, seq)

    tcv = min(256, seq)
    conv_w_pad = jnp.pad(conv_w, ((0, CONV_HALO - CONV_WIDTH), (0, 0)))
    gya = _conv(u, conv_w_pad, conv_b[None, :], conv_ln_g[None, :], conv_ln_b[None, :],
                w_conv_out.astype(BF16), sga, tcv, seq)

    ot = _attn(q.reshape(bsz, seq, -1), k.reshape(bsz, seq, -1), vt, tm, ATTN_HEADS_PER_STEP)

    tmx = min(256, seq)
    wr_t = w_router.T
    wr_h = wr_t.astype(BF16)
    wr_l = (wr_t - wr_h.astype(F32)).astype(BF16)
    mix_w = (w_fox_out.astype(BF16), w_mix_out.astype(BF16), wr_h, wr_l,
             w_sh_gate.astype(BF16), w_sh_up.astype(BF16), w_sh_down.astype(BF16))
    x1, hp, scores_t, shared = _mix(ot, gya, sgb, x2, gate1, scale2, shift2, ln1_g[None, :], ln1_b[None, :],
                                    mix_w, tmx, seq, alpha)

    tr = min(512, t)
    idx, wts, rank, cnt = _route(scores_t, router_bias[:, None], tr)

    counts = cnt[:, 0]
    padded = (counts + EXPERT_BLOCK - 1) // EXPERT_BLOCK * EXPERT_BLOCK
    pend = jnp.cumsum(padded)
    pstart = (pend - padded).astype(jnp.int32)
    n_assign = t * TOP_K
    n_pad = -(-(n_assign + n_exp * (EXPERT_BLOCK - 1)) // EXPERT_BLOCK) * EXPERT_BLOCK
    n_blocks = n_pad // EXPERT_BLOCK
    block_start = jnp.arange(n_blocks, dtype=jnp.int32) * EXPERT_BLOCK
    block_expert = jnp.minimum(jnp.sum(pend[None, :] <= block_start[:, None], axis=1), n_exp - 1).astype(jnp.int32)
    block_rows = jnp.clip((pstart + counts)[block_expert] - block_start, 0, EXPERT_BLOCK).astype(jnp.int32)
    n_used = (pend[-1:] // EXPERT_BLOCK).astype(jnp.int32)
    owns = counts > 0
    expert_seq = jnp.nonzero(owns, size=n_exp, fill_value=0)[0].astype(jnp.int32)
    counts2 = jnp.concatenate([n_used, jnp.sum(owns, dtype=jnp.int32)[None]])

    dest = _dest(pstart, idx, rank, tr)
    dest_flat = dest.T.reshape(-1)
    xs = _dispatch(dest_flat, hp, n_pad, min(256, t))
    ys = _experts(block_expert, block_rows, expert_seq, counts2, xs, w_exp_gate, w_exp_up, w_exp_down, layer)
    return _combine(dest_flat, ys, wts.T, shared, x1, gate2, ln2_g[None, :], ln2_b[None, :],
                    min(128, seq), seq, alpha)


def kernel(x, c, w_ada, b_ada, w_in, b_forget, conv_w, conv_b, conv_ln_g, conv_ln_b, w_conv_out, w_fox_out,
           w_mix_out, ln1_g, ln1_b, w_router, router_bias, w_exp_gate, w_exp_up, w_exp_down, w_sh_gate,
           w_sh_up, w_sh_down, ln2_g, ln2_b):
    bsz, seq, d = x.shape
    depth = w_ada.shape[0]
    c_pad = jnp.pad(c, ((0, 8 - bsz), (0, 0)))
    x2 = x.reshape(bsz * seq, d)
    for l in range(depth):
        ada = _ada(c_pad, w_ada[l], b_ada[l][None, :])
        x2 = _layer(x2, ada, bsz, seq, w_in[l], b_forget[l], conv_w[l], conv_b[l], conv_ln_g[l], conv_ln_b[l],
                    w_conv_out[l], w_fox_out[l], w_mix_out[l], ln1_g[l], ln1_b[l], w_router[l], router_bias[l],
                    w_exp_gate, w_exp_up, w_exp_down, w_sh_gate[l], w_sh_up[l], w_sh_down[l],
                    ln2_g[l], ln2_b[l], depth, l)
    return x2.reshape(bsz, seq, d)
```

```python
import functools

import jax
import jax.numpy as jnp
import numpy as np
from jax import lax
from jax.experimental import pallas as pl
from jax.experimental.pallas import tpu as pltpu

F32 = jnp.float32
BF16 = jnp.bfloat16

LN_EPS = 1e-5
CONV_WIDTH = 31
FOX_HEADS = 8
FOX_HEAD_DIM = 64
N_GROUPS = 8
TOPK_GROUPS = 4
TOP_K = 8
ROUTED_SCALE = 2.5
EXPERT_BLOCK = 256
EXPERT_X_AHEAD = 3
EXPERT_W_AHEAD = 3
ROW_SLABS = 4

V7X_LANES = 128
HEAD_PAD = 128
CONV_HALO = 32
VMEM_LIMIT = 56 * 1024 * 1024
NEG_BIG = -1e30
LOG2E = 1.4426950408889634
ATTN_EXTRA_ROWS = 16
ATTN_HEADS_PER_STEP = 8
ATTN_Q_COLS = 256


def _cparams(sem):
    return pltpu.CompilerParams(dimension_semantics=sem, vmem_limit_bytes=VMEM_LIMIT)


def _ln(v):
    mu = jnp.mean(v, axis=-1, keepdims=True)
    vc = v - mu
    var = jnp.mean(vc * vc, axis=-1, keepdims=True)
    return vc * lax.rsqrt(var + LN_EPS)


def _split3(v):
    hi = v.astype(BF16)
    r1 = v - hi.astype(F32)
    mid = r1.astype(BF16)
    lo = (r1 - mid.astype(F32)).astype(BF16)
    return hi, mid, lo


def _dot(a, b):
    return jnp.dot(a, b, preferred_element_type=F32)


def _dot_nt(a, b):
    return lax.dot_general(a, b, (((1,), (1,)), ((), ())), preferred_element_type=F32)


def _dot_tn(a, b):
    return lax.dot_general(a, b, (((0,), (0,)), ((), ())), preferred_element_type=F32)


def _store_packed_rows(ref, v):
    n, d = v.shape
    half = d // 2
    vb = v.astype(BF16).astype(F32)
    lo_bits = lax.bitcast_convert_type(vb[:, :half], jnp.uint32)
    hi_bits = lax.bitcast_convert_type(vb[:, half:], jnp.uint32)
    words = (lo_bits >> 16) | (hi_bits & jnp.uint32(0xFFFF0000))
    for c in range(ROW_SLABS):
        ref[pl.ds(c, n, stride=ROW_SLABS), :] = words[:, c * V7X_LANES:(c + 1) * V7X_LANES]


def _load_packed_rows(ref, n, lead=()):
    lo, hi = [], []
    for c in range(ROW_SLABS):
        w = ref[lead + (pl.ds(c, n, stride=ROW_SLABS), slice(None))]
        lo.append(lax.bitcast_convert_type(w << 16, F32))
        hi.append(lax.bitcast_convert_type(w & jnp.uint32(0xFFFF0000), F32))
    return lo + hi


def _const_spec(shape):
    nd = len(shape)
    return pl.BlockSpec(shape, lambda *_: (0,) * nd)


def _ada_kernel(c_ref, w_ref, b_ref, o_ref):
    c = c_ref[...]
    cond = c * jax.nn.sigmoid(c)
    ch, cm, _ = _split3(cond)
    w = w_ref[...]
    wh, wm, _ = _split3(w)
    o_ref[...] = _dot(ch, wh) + _dot(ch, wm) + _dot(cm, wh) + b_ref[...]


def _ada(c_pad, w, b):
    rows, d = c_pad.shape
    n = w.shape[1]
    tn = 1024
    return pl.pallas_call(
        _ada_kernel,
        grid=(n // tn,),
        in_specs=[_const_spec((rows, d)),
                  pl.BlockSpec((d, tn), lambda j: (0, j)),
                  pl.BlockSpec((1, tn), lambda j: (0, j))],
        out_specs=pl.BlockSpec((rows, tn), lambda j: (0, j)),
        out_shape=jax.ShapeDtypeStruct((rows, n), F32),
        compiler_params=_cparams(("arbitrary",)),
        name="ada",
    )(c_pad, w, b)


def _inproj_kernel(x_ref, sc_ref, sh_ref, wglu_ref, wq_ref, wk_ref, wvt_ref, wf_ref, wga_ref, wgb_ref,
                   bf_ref, pq_ref, pk_ref,
                   u_ref, q_ref, k_ref, vt_ref, sga_ref, sgb_ref, carry_ref, *, tiles_per_seq, conv_ch):
    i = pl.program_id(0)
    tm = x_ref.shape[0]
    h = _ln(x_ref[...]) * (1.0 + sc_ref[0]) + sh_ref[0]
    hb = h.astype(BF16)

    glu = _dot(hb, wglu_ref[...])
    u_ref[...] = glu[:, :conv_ch] * jax.nn.sigmoid(glu[:, conv_ch:])

    f = _dot(hb, wf_ref[...]) + bf_ref[...]
    logf = jnp.minimum(f, 0.0) - jnp.log(1.0 + jnp.exp(-jnp.abs(f)))
    lh, lm, ll = _split3(logf)
    row = lax.broadcasted_iota(jnp.int32, (tm, tm), 0)
    col = lax.broadcasted_iota(jnp.int32, (tm, tm), 1)
    tri = jnp.where(row >= col, 1.0, 0.0).astype(BF16)
    cs = _dot(tri, lh) + _dot(tri, lm) + _dot(tri, ll)

    @pl.when(i % tiles_per_seq == 0)
    def _():
        carry_ref[...] = jnp.zeros_like(carry_ref)

    cum = cs + carry_ref[...]
    carry_ref[...] = cum[tm - 1:tm, :]

    ch, cm, cl = _split3(cum * LOG2E)
    lane = lax.broadcasted_iota(jnp.int32, cum.shape, 1)
    tail = jnp.where(lane == 24, 1.0, 0.0)
    pieces = jnp.where(lane < 8, ch.astype(F32), jnp.where(lane < 16, cm.astype(F32),
                       jnp.where(lane < 24, cl.astype(F32), tail))).astype(BF16)
    scale = FOX_HEAD_DIM ** -0.5 * LOG2E
    q_ref[...] = (_dot(hb, wq_ref[...]) * scale + _dot(pieces, pq_ref[...])).astype(BF16)
    k_ref[...] = (_dot(hb, wk_ref[...]) + _dot(pieces, pk_ref[...])).astype(BF16)
    vt_ref[0, 0] = _dot_nt(wvt_ref[...], hb).astype(BF16)
    sga_ref[...] = jax.nn.sigmoid(_dot(hb, wga_ref[...])).astype(BF16)
    sgb_ref[...] = jax.nn.sigmoid(_dot(hb, wgb_ref[...])).astype(BF16)


def _inproj(x2, scale1, shift1, wts, tm, seq):
    t, d = x2.shape
    tps = seq // tm
    bsz = t // seq
    wglu, wq, wk, wvt, wf, wga, wgb, bf, pq, pk = wts
    conv_ch = wglu.shape[1] // 2
    fw = wvt.shape[0]
    qw = wq.shape[1]
    mod_spec = pl.BlockSpec((1, 1, d), lambda i: (i // tps, 0, 0))
    row_spec = lambda n: pl.BlockSpec((tm, n), lambda i: (i, 0))
    return pl.pallas_call(
        functools.partial(_inproj_kernel, tiles_per_seq=tps, conv_ch=conv_ch),
        grid=(t // tm,),
        in_specs=[row_spec(d), mod_spec, mod_spec] + [_const_spec(w.shape) for w in wts],
        out_specs=[row_spec(conv_ch), row_spec(qw), row_spec(qw),
                   pl.BlockSpec((1, 1, fw, tm), lambda i: (i // tps, i % tps, 0, 0)),
                   row_spec(d), row_spec(d)],
        out_shape=[jax.ShapeDtypeStruct((t, conv_ch), F32),
                   jax.ShapeDtypeStruct((t, qw), BF16),
                   jax.ShapeDtypeStruct((t, qw), BF16),
                   jax.ShapeDtypeStruct((bsz, tps, fw, tm), BF16),
                   jax.ShapeDtypeStruct((t, d), BF16),
                   jax.ShapeDtypeStruct((t, d), BF16)],
        scratch_shapes=[pltpu.VMEM((1, V7X_LANES), F32)],
        compiler_params=_cparams(("arbitrary",)),
        name="inproj",
    )(x2, scale1, shift1, *wts)


def _conv_kernel(cur_ref, prev_ref, w_ref, b_ref, g_ref, be_ref, wout_ref, sga_ref, o_ref, ext_ref, shift_ref,
                 *, tiles_per_seq, chunk):
    i = pl.program_id(0)
    tm = cur_ref.shape[0]
    first = (i % tiles_per_seq) == 0
    ext_ref[0:CONV_HALO, :] = jnp.where(first, 0.0, prev_ref[...])
    ext_ref[CONV_HALO:, :] = cur_ref[...]
    w = w_ref[...]
    off = CONV_HALO - (CONV_WIDTH - 1)
    outs = []
    span = tm + CONV_HALO - 8
    for res in range(1, 8):
        shift_ref[res - 1] = ext_ref[res:res + span, :]
    for c0 in range(0, tm, chunk):
        acc = jnp.zeros((chunk, cur_ref.shape[1]), F32)
        for j in range(CONV_WIDTH):
            res, lo = (off + j) % 8, c0 + (off + j) // 8 * 8
            rows = ext_ref[lo:lo + chunk, :] if res == 0 else shift_ref[res - 1, lo:lo + chunk, :]
            acc = acc + w[j:j + 1, :] * rows
        outs.append(acc)
    v = jnp.concatenate(outs, axis=0) + b_ref[...]
    v = _ln(v) * g_ref[...] + be_ref[...]
    v = v * jax.nn.sigmoid(v)
    ya = _dot(v.astype(BF16), wout_ref[...])
    o_ref[...] = (sga_ref[...].astype(F32) * ya).astype(BF16)


def _conv(u, conv_w, conv_b, g, be, wout, sga, tm, seq):
    t, ch = u.shape
    d = wout.shape[1]
    tps = seq // tm
    halo_per_tile = tm // CONV_HALO
    return pl.pallas_call(
        functools.partial(_conv_kernel, tiles_per_seq=tps, chunk=64),
        grid=(t // tm,),
        in_specs=[pl.BlockSpec((tm, ch), lambda i: (i, 0)),
                  pl.BlockSpec((CONV_HALO, ch), lambda i: (jnp.maximum(i * halo_per_tile - 1, 0), 0)),
                  _const_spec(conv_w.shape), _const_spec(conv_b.shape), _const_spec(g.shape),
                  _const_spec(be.shape), _const_spec(wout.shape),
                  pl.BlockSpec((tm, d), lambda i: (i, 0))],
        out_specs=pl.BlockSpec((tm, d), lambda i: (i, 0)),
        out_shape=jax.ShapeDtypeStruct((t, d), BF16),
        scratch_shapes=[pltpu.VMEM((tm + CONV_HALO, ch), F32), pltpu.VMEM((7, tm + CONV_HALO - 8, ch), F32)],
        compiler_params=_cparams(("arbitrary",)),
        name="conv",
    )(u, u, conv_w, conv_b, g, be, wout, sga)


def _attn_kernel(q_ref, k_ref, vt_ref, o_ref, *, blk, heads):
    qi = pl.program_id(2)
    row = lax.broadcasted_iota(jnp.int32, (ATTN_EXTRA_ROWS, blk), 0)
    ones_rows = jnp.where(row == 0, 1.0, 0.0).astype(BF16)

    ncol = blk // ATTN_Q_COLS
    chains = [(j, c) for j in range(heads) for c in range(ncol)]

    def scores(kj, chain, masked):
        j, c = chain
        k = k_ref[pl.ds(pl.multiple_of(kj * blk, blk), blk), j * HEAD_PAD:(j + 1) * HEAD_PAD]
        q = q_ref[c * ATTN_Q_COLS:(c + 1) * ATTN_Q_COLS, j * HEAD_PAD:(j + 1) * HEAD_PAD]
        s = _dot_nt(k, q)
        if masked:
            kpos = lax.broadcasted_iota(jnp.int32, s.shape, 0)
            qpos = lax.broadcasted_iota(jnp.int32, s.shape, 1) + c * ATTN_Q_COLS
            s = jnp.where(kpos <= qpos, s, NEG_BIG)
        return s

    def probs(s, m):
        m_new = jnp.maximum(m, jnp.max(s, axis=0, keepdims=True))
        return jnp.exp2(s - m_new).astype(BF16), m_new

    def update(kj, chain, p, m, m_new, acc):
        j, _ = chain
        vt = vt_ref[kj, j * FOX_HEAD_DIM:(j + 1) * FOX_HEAD_DIM, :]
        lhs = jnp.concatenate([vt, ones_rows], axis=0)
        return jnp.exp2(m - m_new) * acc + _dot(lhs, p)

    def step(kj, carry, masked):
        n = len(chains)
        s, pm, out = {}, {}, [None] * n
        for i in range(n + 2):
            if i < n:
                s[i] = scores(kj, chains[i], masked)
            if 1 <= i <= n:
                pm[i - 1] = probs(s.pop(i - 1), carry[i - 1][0])
            if i >= 2:
                p, m_new = pm.pop(i - 2)
                m, acc = carry[i - 2]
                out[i - 2] = (m_new, update(kj, chains[i - 2], p, m, m_new, acc))
        return tuple(out)

    init = tuple((jnp.full((1, ATTN_Q_COLS), NEG_BIG, F32),
                  jnp.zeros((FOX_HEAD_DIM + ATTN_EXTRA_ROWS, ATTN_Q_COLS), F32)) for _ in chains)
    carry = lax.fori_loop(0, qi, lambda kj, cr: step(kj, cr, False), init)
    carry = step(qi, carry, True)
    for (j, c), (_, acc) in zip(chains, carry):
        o_ref[j * FOX_HEAD_DIM:(j + 1) * FOX_HEAD_DIM, c * ATTN_Q_COLS:(c + 1) * ATTN_Q_COLS] = (
            acc[:FOX_HEAD_DIM] / acc[FOX_HEAD_DIM:FOX_HEAD_DIM + 1]).astype(BF16)


def _attn(q, k, vt, blk, heads):
    bsz, seq, _ = q.shape
    nkb = seq // blk
    return pl.pallas_call(
        functools.partial(_attn_kernel, blk=blk, heads=heads),
        grid=(bsz, FOX_HEADS // heads, seq // blk),
        in_specs=[pl.BlockSpec((None, blk, heads * HEAD_PAD), lambda b, h, i: (b, i, h)),
                  pl.BlockSpec((None, seq, heads * HEAD_PAD), lambda b, h, i: (b, 0, h),
                               pipeline_mode=pl.Buffered(1)),
                  pl.BlockSpec((None, nkb, heads * FOX_HEAD_DIM, blk), lambda b, h, i: (b, 0, h, 0),
                               pipeline_mode=pl.Buffered(1))],
        out_specs=pl.BlockSpec((None, heads * FOX_HEAD_DIM, blk), lambda b, h, i: (b, h, i)),
        out_shape=jax.ShapeDtypeStruct((bsz, FOX_HEADS * FOX_HEAD_DIM, seq), BF16),
        compiler_params=_cparams(("arbitrary", "arbitrary", "arbitrary")),
        name="attn",
    )(q, k, vt)


def _mix_kernel(ot_ref, gya_ref, sgb_ref, x_ref, g1_ref, sc2_ref, sh2_ref, lg_ref, lb_ref,
                wfox_ref, wmix_ref, wrh_ref, wrl_ref,
                x1_ref, hp_ref, st_ref, *, alpha):
    yb = _dot_tn(ot_ref[...], wfox_ref[...])
    merged = gya_ref[...].astype(F32) + sgb_ref[...].astype(F32) * yb
    y = _dot(merged.astype(BF16), wmix_ref[...])
    x1 = _ln(alpha * x_ref[...] + (1.0 + g1_ref[0]) * y) * lg_ref[...] + lb_ref[...]
    x1_ref[...] = x1
    h2 = _ln(x1) * (1.0 + sc2_ref[0]) + sh2_ref[0]
    hb = h2.astype(BF16)
    hl = (h2 - hb.astype(F32)).astype(BF16)

    _store_packed_rows(hp_ref, h2)

    logits_t = _dot_nt(wrh_ref[...], hb) + _dot_nt(wrl_ref[...], hb) + _dot_nt(wrh_ref[...], hl)
    st_ref[...] = jax.nn.sigmoid(logits_t)


def _mix(ot, gya, sgb, x2, gate1, scale2, shift2, lg, lb, wts, tm, seq, alpha):
    t, d = x2.shape
    tps = seq // tm
    fw = ot.shape[1]
    n_exp = wts[2].shape[0]
    mod_spec = pl.BlockSpec((1, 1, d), lambda i: (i // tps, 0, 0))
    row_spec = lambda n: pl.BlockSpec((tm, n), lambda i: (i, 0))
    return pl.pallas_call(
        functools.partial(_mix_kernel, alpha=alpha),
        grid=(t // tm,),
        in_specs=[pl.BlockSpec((None, fw, tm), lambda i: (i // tps, 0, i % tps)),
                  row_spec(d), row_spec(d), row_spec(d), mod_spec, mod_spec, mod_spec,
                  _const_spec(lg.shape), _const_spec(lb.shape)] + [_const_spec(w.shape) for w in wts],
        out_specs=[row_spec(d), pl.BlockSpec((tm * ROW_SLABS, V7X_LANES), lambda i: (i, 0)),
                   pl.BlockSpec((n_exp, tm), lambda i: (0, i))],
        out_shape=[jax.ShapeDtypeStruct((t, d), F32),
                   jax.ShapeDtypeStruct((t * ROW_SLABS, V7X_LANES), jnp.uint32),
                   jax.ShapeDtypeStruct((n_exp, t), F32)],
        compiler_params=_cparams(("arbitrary",)),
        name="mix",
    )(ot, gya, sgb, x2, gate1, scale2, shift2, lg, lb, *wts)


def _route_kernel(st_ref, bias_ref, idx_ref, wts_ref, rank_ref, cnt_ref, carry_ref):
    i = pl.program_id(0)
    n_exp, tr = st_ref.shape
    gsz = n_exp // N_GROUPS
    neg_inf = -jnp.inf

    @pl.when(i == 0)
    def _():
        carry_ref[...] = jnp.zeros_like(carry_ref)

    shape3 = (N_GROUPS, gsz, tr)

    def max01(v):
        return jnp.max(jnp.max(v, axis=1, keepdims=True), axis=0, keepdims=True)

    def min01(v):
        return jnp.min(jnp.min(v, axis=1, keepdims=True), axis=0, keepdims=True)

    def sum01(v):
        return jnp.sum(jnp.sum(v, axis=1, keepdims=True), axis=0, keepdims=True)

    sc = st_ref[...].reshape(shape3)
    gsel = (st_ref[...] + bias_ref[...]).reshape(shape3)
    pos = lax.broadcasted_iota(jnp.int32, shape3, 1)
    m1 = jnp.max(gsel, axis=1, keepdims=True)
    i1 = jnp.min(jnp.where(gsel == m1, pos, gsz), axis=1, keepdims=True)
    m2 = jnp.max(jnp.where(pos == i1, neg_inf, gsel), axis=1, keepdims=True)
    gs = m1 + m2

    gid = lax.broadcasted_iota(jnp.int32, gs.shape, 0)
    gkeep = jnp.zeros(gs.shape, F32)
    for _ in range(TOPK_GROUPS):
        mx = jnp.max(gs, axis=0, keepdims=True)
        gi = jnp.min(jnp.where(gs == mx, gid, N_GROUPS), axis=0, keepdims=True)
        hit = gid == gi
        gkeep = gkeep + jnp.where(hit, 1.0, 0.0)
        gs = jnp.where(hit, neg_inf, gs)

    cur = jnp.where(jnp.broadcast_to(gkeep, shape3) > 0.5, gsel, neg_inf)
    eid = lax.broadcasted_iota(jnp.int32, shape3, 0) * gsz + pos
    onehot = jnp.zeros(shape3, F32)
    idxs, ws = [], []
    wsum = jnp.zeros((1, 1, tr), F32)
    for _ in range(TOP_K):
        mx = max01(cur)
        ik = min01(jnp.where(cur == mx, eid, n_exp))
        hit = eid == ik
        wk = sum01(jnp.where(hit, sc, 0.0))
        idxs.append(ik)
        ws.append(wk)
        wsum = wsum + wk
        onehot = onehot + jnp.where(hit, 1.0, 0.0)
        cur = jnp.where(hit, neg_inf, cur)

    ra = lax.broadcasted_iota(jnp.int32, (tr, tr), 0)
    rb = lax.broadcasted_iota(jnp.int32, (tr, tr), 1)
    upper = jnp.where(ra < rb, 1.0, 0.0).astype(BF16)
    onehot2 = onehot.reshape(n_exp, tr)
    prior = (_dot(onehot2.astype(BF16), upper) + carry_ref[...]).reshape(shape3)
    for slot in range(TOP_K):
        idx_ref[slot:slot + 1, :] = idxs[slot].reshape(1, tr)
        wts_ref[slot:slot + 1, :] = (ws[slot] / wsum * ROUTED_SCALE).reshape(1, tr)
        rk = sum01(jnp.where(eid == idxs[slot], prior, 0.0))
        rank_ref[slot:slot + 1, :] = rk.reshape(1, tr).astype(jnp.int32)
    total = carry_ref[...] + jnp.sum(onehot2, axis=1, keepdims=True)
    carry_ref[...] = total
    cnt_ref[...] = jnp.broadcast_to(total, cnt_ref.shape).astype(jnp.int32)


def _route(scores_t, bias_col, tr):
    n_exp, t = scores_t.shape
    slot_spec = pl.BlockSpec((TOP_K, tr), lambda i: (0, i))
    return pl.pallas_call(
        _route_kernel,
        grid=(t // tr,),
        in_specs=[pl.BlockSpec((n_exp, tr), lambda i: (0, i)), _const_spec(bias_col.shape)],
        out_specs=[slot_spec, slot_spec, slot_spec, _const_spec((n_exp, V7X_LANES))],
        out_shape=[jax.ShapeDtypeStruct((TOP_K, t), jnp.int32),
                   jax.ShapeDtypeStruct((TOP_K, t), F32),
                   jax.ShapeDtypeStruct((TOP_K, t), jnp.int32),
                   jax.ShapeDtypeStruct((n_exp, V7X_LANES), jnp.int32)],
        scratch_shapes=[pltpu.VMEM((n_exp, 1), F32)],
        compiler_params=_cparams(("arbitrary",)),
        name="route",
    )(scores_t, bias_col)


def _dest_kernel(pstart_ref, idx_ref, rank_ref, o_ref):
    n_exp = pstart_ref.shape[0]
    tr = idx_ref.shape[1]
    eid = lax.broadcasted_iota(jnp.int32, (n_exp, tr), 0)
    pstart = pstart_ref[...]
    for slot in range(TOP_K):
        hit = eid == idx_ref[slot:slot + 1, :]
        start = jnp.sum(jnp.where(hit, pstart, 0.0), axis=0, keepdims=True).astype(jnp.int32)
        o_ref[slot:slot + 1, :] = (start + rank_ref[slot:slot + 1, :]) * ROW_SLABS


def _dest(pstart, idx, rank, tr):
    k, t = idx.shape
    spec = pl.BlockSpec((k, tr), lambda i: (0, i))
    pstart_col = pstart.astype(F32)[:, None]
    return pl.pallas_call(
        _dest_kernel,
        grid=(t // tr,),
        in_specs=[_const_spec(pstart_col.shape), spec, spec],
        out_specs=spec,
        out_shape=jax.ShapeDtypeStruct((k, t), jnp.int32),
        compiler_params=_cparams(("arbitrary",)),
        name="dest",
    )(pstart_col, idx, rank)


def _dispatch_kernel(dest_ref, hp_ref, wsg_ref, wsu_ref, wsd_ref, xs_ref, shr_ref, sem):
    td = hp_ref.shape[0] // ROW_SLABS

    def row_copy(r, slot):
        dst = pl.multiple_of(dest_ref[r * TOP_K + slot], ROW_SLABS)
        src = r * ROW_SLABS if isinstance(r, int) else pl.multiple_of(r * ROW_SLABS, ROW_SLABS)
        return pltpu.make_async_copy(hp_ref.at[pl.ds(src, ROW_SLABS)], xs_ref.at[pl.ds(dst, ROW_SLABS)], sem)

    for r in range(td):
        for slot in range(TOP_K):
            row_copy(r, slot).start(priority=slot % 2)

    hb = jnp.concatenate(_load_packed_rows(hp_ref, td), axis=1).astype(BF16)
    g = _dot(hb, wsg_ref[...])
    u = _dot(hb, wsu_ref[...])
    a = (g * jax.nn.sigmoid(g) * u).astype(BF16)
    shr_ref[...] = _dot(a, wsd_ref[...])

    def wait(r, c):
        for slot in range(TOP_K):
            row_copy(r, slot).wait()
        return c

    lax.fori_loop(0, td, wait, 0, unroll=8)


def _dispatch(dest_flat, hp, shared_w, n_pad, td):
    t = hp.shape[0] // ROW_SLABS
    d = shared_w[2].shape[1]
    return pl.pallas_call(
        _dispatch_kernel,
        grid=(t // td,),
        in_specs=[pl.BlockSpec((td * TOP_K,), lambda i: (i,), memory_space=pltpu.SMEM),
                  pl.BlockSpec((td * ROW_SLABS, V7X_LANES), lambda i: (i, 0))]
                 + [_const_spec(w.shape) for w in shared_w],
        out_specs=[pl.BlockSpec(memory_space=pl.ANY), pl.BlockSpec((td, d), lambda i: (i, 0))],
        out_shape=[jax.ShapeDtypeStruct((n_pad * ROW_SLABS, V7X_LANES), hp.dtype),
                   jax.ShapeDtypeStruct((t, d), F32)],
        scratch_shapes=[pltpu.SemaphoreType.DMA(())],
        compiler_params=_cparams(("arbitrary",)),
        name="dispatch",
    )(dest_flat, hp, *shared_w)


def _expert_kernel(be_ref, rows_ref, seq_ref, cnt_ref, xs_hbm, wg_hbm, wu_hbm, wd_hbm, ys_hbm,
                   xbuf, ybuf, wgf, wuf, wdf, wg_s, wu_s, wd_s, sem_x, sem_y, sem_w, *, layer):
    n = cnt_ref[0]
    n_seq = cnt_ref[1]
    blk_rows = EXPERT_BLOCK * ROW_SLABS
    nx, nw = EXPERT_X_AHEAD + 1, EXPERT_W_AHEAD + 1

    def block_rows(q):
        start = q * blk_rows
        return pl.ds(start if isinstance(q, int) else pl.multiple_of(start, blk_rows), blk_rows)

    def x_copy(q, slot):
        return pltpu.make_async_copy(xs_hbm.at[block_rows(q)], xbuf.at[slot], sem_x.at[slot])

    def y_copy(q, slot):
        return pltpu.make_async_copy(ybuf.at[slot], ys_hbm.at[block_rows(q)], sem_y.at[slot])

    def w_copies(k, ws):
        e = seq_ref[k]
        out = []
        for hbm, buf in ((wg_hbm, wgf), (wu_hbm, wuf), (wd_hbm, wdf)):
            half = buf.shape[1] // 2
            for part in range(2):
                rows = pl.ds(part * half, half)
                out.append((pltpu.make_async_copy(hbm.at[layer, e, rows], buf.at[ws, rows], sem_w.at[ws]), part))
        return out

    for j in range(EXPERT_X_AHEAD):
        @pl.when(j < n)
        def _():
            x_copy(j, j).start(priority=1)

    for j in range(EXPERT_W_AHEAD):
        @pl.when(j < n_seq)
        def _():
            for cp, prio in w_copies(j, j):
                cp.start(priority=prio)

    def body(q, k):
        slot = q % 2
        fresh = jnp.logical_or(q == 0, be_ref[q] != be_ref[jnp.maximum(q - 1, 0)])

        @pl.when(fresh)
        def _():
            ws = k % nw
            for cp, _ in w_copies(k, ws):
                cp.wait()
            wg_s[...] = wgf[ws].astype(BF16)
            wu_s[...] = wuf[ws].astype(BF16)
            wd_s[...] = wdf[ws].astype(BF16)

            @pl.when(k + EXPERT_W_AHEAD < n_seq)
            def _():
                for cp, prio in w_copies(k + EXPERT_W_AHEAD, (k + EXPERT_W_AHEAD) % nw):
                    cp.start(priority=prio)

        @pl.when(q + EXPERT_X_AHEAD < n)
        def _():
            x_copy(q + EXPERT_X_AHEAD, (q + EXPERT_X_AHEAD) % nx).start(priority=1)

        x_copy(q, q % nx).wait()

        @pl.when(q >= 2)
        def _():
            y_copy(q - 2, slot).wait()

        x = jnp.concatenate(_load_packed_rows(xbuf, EXPERT_BLOCK, lead=(q % nx,)), axis=1)
        valid = lax.broadcasted_iota(jnp.int32, x.shape, 0) < rows_ref[q]
        x = jnp.where(valid, x, 0.0).astype(BF16)
        g = _dot(x, wg_s[...])
        u = _dot(x, wu_s[...])
        a = (g * jax.nn.sigmoid(g) * u).astype(BF16)
        _store_packed_rows(ybuf.at[slot], _dot(a, wd_s[...]))
        y_copy(q, slot).start()
        return k + fresh.astype(jnp.int32)

    lax.fori_loop(0, n, body, jnp.int32(0))

    @pl.when(n >= 2)
    def _():
        y_copy(n - 2, n % 2).wait()

    y_copy(n - 1, (n - 1) % 2).wait()


def _experts(block_expert, block_rows, expert_seq, counts2, xs, wg, wu, wd, layer):
    _, n_exp, d, f = wg.shape
    blk = (EXPERT_BLOCK * ROW_SLABS, V7X_LANES)
    nx, nw = EXPERT_X_AHEAD + 1, EXPERT_W_AHEAD + 1
    any_spec = pl.BlockSpec(memory_space=pl.ANY)
    return pl.pallas_call(
        functools.partial(_expert_kernel, layer=layer),
        grid_spec=pltpu.PrefetchScalarGridSpec(
            num_scalar_prefetch=4, grid=(1,),
            in_specs=[any_spec, any_spec, any_spec, any_spec],
            out_specs=any_spec,
            scratch_shapes=[pltpu.VMEM((nx,) + blk, xs.dtype), pltpu.VMEM((2,) + blk, xs.dtype),
                            pltpu.VMEM((nw, d, f), F32), pltpu.VMEM((nw, d, f), F32), pltpu.VMEM((nw, f, d), F32),
                            pltpu.VMEM((d, f), BF16), pltpu.VMEM((d, f), BF16), pltpu.VMEM((f, d), BF16),
                            pltpu.SemaphoreType.DMA((nx,)), pltpu.SemaphoreType.DMA((2,)),
                            pltpu.SemaphoreType.DMA((nw,))]),
        out_shape=jax.ShapeDtypeStruct(xs.shape, xs.dtype),
        compiler_params=_cparams(("arbitrary",)),
        name="experts",
    )(block_expert, block_rows, expert_seq, counts2, xs, wg, wu, wd)


def _combine_kernel(d0_ref, d1_ref, d2_ref, ys_ref, w_ref, shr_ref, x1_ref, g2_ref, lg_ref, lb_ref, o_ref,
                    buf_a, buf_b, sem, *, alpha, tc):
    i = pl.program_id(0)
    last = pl.num_programs(0) - 1

    def row_copy(dref, r, slot, buf, sem_idx):
        src = pl.multiple_of(dref[r * TOP_K + slot], ROW_SLABS)
        dst = r * ROW_SLABS if isinstance(r, int) else pl.multiple_of(r * ROW_SLABS, ROW_SLABS)
        return pltpu.make_async_copy(ys_ref.at[pl.ds(src, ROW_SLABS)], buf.at[slot, pl.ds(dst, ROW_SLABS)],
                                     sem.at[sem_idx])

    def issue_unrolled(dref, buf, sem_idx):
        for r in range(tc):
            for slot in range(TOP_K):
                row_copy(dref, r, slot, buf, sem_idx).start(priority=slot % 2)

    def issue_loop(dref, buf, sem_idx):
        def start(r, c):
            for slot in range(TOP_K):
                row_copy(dref, r, slot, buf, sem_idx).start(priority=slot % 2)
            return c
        lax.fori_loop(0, tc, start, 0, unroll=4)

    def wait_all(dref, buf, sem_idx):
        def wait(r, c):
            for slot in range(TOP_K):
                row_copy(dref, r, slot, buf, sem_idx).wait()
            return c
        lax.fori_loop(0, tc, wait, 0, unroll=8)

    def reduce_tile(buf, rows):
        w = w_ref[rows, :]
        chunks = None
        for slot in range(TOP_K):
            wk = w[:, slot:slot + 1]
            part = [c * wk for c in _load_packed_rows(buf, tc, lead=(slot,))]
            chunks = part if chunks is None else [a + b for a, b in zip(chunks, part)]
        y = shr_ref[rows, :] + jnp.concatenate(chunks, axis=1)
        z = alpha * x1_ref[rows, :] + (1.0 + g2_ref[0]) * y
        o_ref[rows, :] = _ln(z) * lg_ref[...] + lb_ref[...]

    @pl.when(i == 0)
    def _():
        issue_loop(d0_ref, buf_a, 0)

    wait_all(d0_ref, buf_a, 0)
    issue_unrolled(d1_ref, buf_b, 1)
    reduce_tile(buf_a, pl.ds(0, tc))

    wait_all(d1_ref, buf_b, 1)
    issue_unrolled(d2_ref, buf_a, 0)
    reduce_tile(buf_b, pl.ds(tc, tc))

    @pl.when(i == last)
    def _():
        wait_all(d2_ref, buf_a, 0)


def _combine(dest_flat, ys, wts_tk, shared, x1, gate2, lg, lb, tc, seq, alpha):
    t, d = x1.shape
    tps = seq // (2 * tc)
    n_tiles = t // tc
    row_spec = lambda n: pl.BlockSpec((2 * tc, n), lambda i: (i, 0))
    tile_dest = lambda f: pl.BlockSpec((tc * TOP_K,), f, memory_space=pltpu.SMEM)
    buf = pltpu.VMEM((TOP_K, tc * ROW_SLABS, V7X_LANES), jnp.uint32)
    return pl.pallas_call(
        functools.partial(_combine_kernel, alpha=alpha, tc=tc),
        grid=(n_tiles // 2,),
        in_specs=[tile_dest(lambda i: (2 * i,)), tile_dest(lambda i: (2 * i + 1,)),
                  tile_dest(lambda i: (jnp.minimum(2 * i + 2, n_tiles - 1),)),
                  pl.BlockSpec(memory_space=pl.ANY),
                  row_spec(TOP_K), row_spec(d), row_spec(d),
                  pl.BlockSpec((1, 1, d), lambda i: (i // tps, 0, 0)),
                  _const_spec(lg.shape), _const_spec(lb.shape)],
        out_specs=row_spec(d),
        out_shape=jax.ShapeDtypeStruct((t, d), F32),
        scratch_shapes=[buf, buf, pltpu.SemaphoreType.DMA((2,))],
        compiler_params=_cparams(("arbitrary",)),
        name="combine",
    )(dest_flat, dest_flat, dest_flat, ys, wts_tk, shared, x1, gate2, lg, lb)


def _placement():
    pq = np.zeros((V7X_LANES, FOX_HEADS * HEAD_PAD), np.float32)
    pk = np.zeros((V7X_LANES, FOX_HEADS * HEAD_PAD), np.float32)
    for h in range(FOX_HEADS):
        base = h * HEAD_PAD + FOX_HEAD_DIM
        for piece in range(3):
            pq[piece * 8 + h, base + piece] = 1.0
            pk[24, base + piece] = 1.0
            pq[24, base + 3 + piece] = 1.0
            pk[piece * 8 + h, base + 3 + piece] = -1.0
    return jnp.asarray(pq, BF16), jnp.asarray(pk, BF16)


def _inproj_weights(w_in, b_forget, d):
    conv2 = d
    fw = FOX_HEADS * FOX_HEAD_DIM
    o1, o2, o3, o4 = conv2, conv2 + fw, conv2 + 2 * fw, conv2 + 3 * fw
    o5 = o4 + FOX_HEADS
    o6 = o5 + d

    def pad_heads(w):
        w = w.reshape(d, FOX_HEADS, FOX_HEAD_DIM)
        w = jnp.pad(w, ((0, 0), (0, 0), (0, HEAD_PAD - FOX_HEAD_DIM)))
        return w.reshape(d, FOX_HEADS * HEAD_PAD).astype(BF16)

    wglu = w_in[:, :o1].astype(BF16)
    wq = pad_heads(w_in[:, o1:o2])
    wk = pad_heads(w_in[:, o2:o3])
    wvt = w_in[:, o3:o4].T.astype(BF16)
    wf8 = w_in[:, o4:o5]
    wf = jnp.pad(jnp.concatenate([wf8, wf8, wf8], axis=1), ((0, 0), (0, V7X_LANES - 3 * FOX_HEADS))).astype(BF16)
    bf = jnp.pad(jnp.concatenate([b_forget, b_forget, b_forget]), (0, V7X_LANES - 3 * FOX_HEADS))[None, :].astype(F32)
    wga = w_in[:, o5:o6].astype(BF16)
    wgb = w_in[:, o6:].astype(BF16)
    pq, pk = _placement()
    return (wglu, wq, wk, wvt, wf, wga, wgb, bf, pq, pk)


def _layer(x2, ada, bsz, seq, w_in, b_forget, conv_w, conv_b, conv_ln_g, conv_ln_b, w_conv_out, w_fox_out,
           w_mix_out, ln1_g, ln1_b, w_router, router_bias, w_exp_gate, w_exp_up, w_exp_down,
           w_sh_gate, w_sh_up, w_sh_down, ln2_g, ln2_b, depth, layer):
    t, d = x2.shape
    n_exp = w_router.shape[1]
    alpha = (2.0 * depth) ** 0.25
    mods = [ada[:bsz, j * d:(j + 1) * d][:, None, :] for j in range(6)]
    shift1, scale1, gate1, shift2, scale2, gate2 = mods

    tm = min(512, seq)
    u, q, k, vt, sga, sgb = _inproj(x2, scale1, shift1, _inproj_weights(w_in, b_forget, d), tm, seq)

    tcv = min(256, seq)
    conv_w_pad = jnp.pad(conv_w, ((0, CONV_HALO - CONV_WIDTH), (0, 0)))
    gya = _conv(u, conv_w_pad, conv_b[None, :], conv_ln_g[None, :], conv_ln_b[None, :],
                w_conv_out.astype(BF16), sga, tcv, seq)

    ot = _attn(q.reshape(bsz, seq, -1), k.reshape(bsz, seq, -1), vt, tm, ATTN_HEADS_PER_STEP)

    tmx = min(256, seq)
    wr_t = w_router.T
    wr_h = wr_t.astype(BF16)
    wr_l = (wr_t - wr_h.astype(F32)).astype(BF16)
    mix_w = (w_fox_out.astype(BF16), w_mix_out.astype(BF16), wr_h, wr_l)
    shared_w = (w_sh_gate.astype(BF16), w_sh_up.astype(BF16), w_sh_down.astype(BF16))
    x1, hp, scores_t = _mix(ot, gya, sgb, x2, gate1, scale2, shift2, ln1_g[None, :], ln1_b[None, :],
                                    mix_w, tmx, seq, alpha)

    tr = min(512, t)
    idx, wts, rank, cnt = _route(scores_t, router_bias[:, None], tr)

    counts = cnt[:, 0]
    padded = (counts + EXPERT_BLOCK - 1) // EXPERT_BLOCK * EXPERT_BLOCK
    pend = jnp.cumsum(padded)
    pstart = (pend - padded).astype(jnp.int32)
    n_assign = t * TOP_K
    n_pad = -(-(n_assign + n_exp * (EXPERT_BLOCK - 1)) // EXPERT_BLOCK) * EXPERT_BLOCK
    n_blocks = n_pad // EXPERT_BLOCK
    block_start = jnp.arange(n_blocks, dtype=jnp.int32) * EXPERT_BLOCK
    block_expert = jnp.minimum(jnp.sum(pend[None, :] <= block_start[:, None], axis=1), n_exp - 1).astype(jnp.int32)
    block_rows = jnp.clip((pstart + counts)[block_expert] - block_start, 0, EXPERT_BLOCK).astype(jnp.int32)
    n_used = (pend[-1:] // EXPERT_BLOCK).astype(jnp.int32)
    owns = counts > 0
    expert_seq = jnp.nonzero(owns, size=n_exp, fill_value=0)[0].astype(jnp.int32)
    counts2 = jnp.concatenate([n_used, jnp.sum(owns, dtype=jnp.int32)[None]])

    dest = _dest(pstart, idx, rank, tr)
    dest_flat = dest.T.reshape(-1)
    xs, shared = _dispatch(dest_flat, hp, shared_w, n_pad, min(256, t))
    ys = _experts(block_expert, block_rows, expert_seq, counts2, xs, w_exp_gate, w_exp_up, w_exp_down, layer)
    return _combine(dest_flat, ys, wts.T, shared, x1, gate2, ln2_g[None, :], ln2_b[None, :],
                    min(128, seq), seq, alpha)


def kernel(x, c, w_ada, b_ada, w_in, b_forget, conv_w, conv_b, conv_ln_g, conv_ln_b, w_conv_out, w_fox_out,
           w_mix_out, ln1_g, ln1_b, w_router, router_bias, w_exp_gate, w_exp_up, w_exp_down, w_sh_gate,
           w_sh_up, w_sh_down, ln2_g, ln2_b):
    bsz, seq, d = x.shape
    depth = w_ada.shape[0]
    c_pad = jnp.pad(c, ((0, 8 - bsz), (0, 0)))
    x2 = x.reshape(bsz * seq, d)
    for l in range(depth):
        ada = _ada(c_pad, w_ada[l], b_ada[l][None, :])
        x2 = _layer(x2, ada, bsz, seq, w_in[l], b_forget[l], conv_w[l], conv_b[l], conv_ln_g[l], conv_ln_b[l],
                    w_conv_out[l], w_fox_out[l], w_mix_out[l], ln1_g[l], ln1_b[l], w_router[l], router_bias[l],
                    w_exp_gate, w_exp_up, w_exp_down, w_sh_gate[l], w_sh_up[l], w_sh_down[l],
                    ln2_g[l], ln2_b[l], depth, l)
    return x2.reshape(bsz, seq, d)
```

```python
import functools

import jax
import jax.numpy as jnp
import numpy as np
from jax import lax
from jax.experimental import pallas as pl
from jax.experimental.pallas import tpu as pltpu

F32 = jnp.float32
BF16 = jnp.bfloat16

LN_EPS = 1e-5
CONV_WIDTH = 31
FOX_HEADS = 8
FOX_HEAD_DIM = 64
N_GROUPS = 8
TOPK_GROUPS = 4
TOP_K = 8
ROUTED_SCALE = 2.5
EXPERT_BLOCK = 256
EXPERT_X_AHEAD = 4
EXPERT_W_AHEAD = 4
ROW_SLABS = 4

V7X_LANES = 128
HEAD_PAD = 128
CONV_HALO = 32
VMEM_LIMIT = 56 * 1024 * 1024
NEG_BIG = -1e30
LOG2E = 1.4426950408889634
ATTN_EXTRA_ROWS = 16
ATTN_HEADS_PER_STEP = 8
ATTN_Q_COLS = 256
ATTN_PIPE_LAG = 3


def _cparams(sem):
    return pltpu.CompilerParams(dimension_semantics=sem, vmem_limit_bytes=VMEM_LIMIT)


def _ln(v):
    mu = jnp.mean(v, axis=-1, keepdims=True)
    vc = v - mu
    var = jnp.mean(vc * vc, axis=-1, keepdims=True)
    return vc * lax.rsqrt(var + LN_EPS)


def _split3(v):
    hi = v.astype(BF16)
    r1 = v - hi.astype(F32)
    mid = r1.astype(BF16)
    lo = (r1 - mid.astype(F32)).astype(BF16)
    return hi, mid, lo


def _dot(a, b):
    return jnp.dot(a, b, preferred_element_type=F32)


def _dot_nt(a, b):
    return lax.dot_general(a, b, (((1,), (1,)), ((), ())), preferred_element_type=F32)


def _dot_tn(a, b):
    return lax.dot_general(a, b, (((0,), (0,)), ((), ())), preferred_element_type=F32)


def _store_packed_rows(ref, v):
    n, d = v.shape
    half = d // 2
    vb = v.astype(BF16).astype(F32)
    lo_bits = lax.bitcast_convert_type(vb[:, :half], jnp.uint32)
    hi_bits = lax.bitcast_convert_type(vb[:, half:], jnp.uint32)
    words = (lo_bits >> 16) | (hi_bits & jnp.uint32(0xFFFF0000))
    for c in range(ROW_SLABS):
        ref[pl.ds(c, n, stride=ROW_SLABS), :] = words[:, c * V7X_LANES:(c + 1) * V7X_LANES]


def _load_packed_rows(ref, n, lead=()):
    lo, hi = [], []
    for c in range(ROW_SLABS):
        w = ref[lead + (pl.ds(c, n, stride=ROW_SLABS), slice(None))]
        lo.append(lax.bitcast_convert_type(w << 16, F32))
        hi.append(lax.bitcast_convert_type(w & jnp.uint32(0xFFFF0000), F32))
    return lo + hi


def _const_spec(shape):
    nd = len(shape)
    return pl.BlockSpec(shape, lambda *_: (0,) * nd)


def _ada_kernel(c_ref, w_ref, b_ref, o_ref):
    c = c_ref[...]
    cond = c * jax.nn.sigmoid(c)
    ch, cm, _ = _split3(cond)
    w = w_ref[...]
    wh, wm, _ = _split3(w)
    o_ref[...] = _dot(ch, wh) + _dot(ch, wm) + _dot(cm, wh) + b_ref[...]


def _ada(c_pad, w, b):
    rows, d = c_pad.shape
    n = w.shape[1]
    tn = 1024
    return pl.pallas_call(
        _ada_kernel,
        grid=(n // tn,),
        in_specs=[_const_spec((rows, d)),
                  pl.BlockSpec((d, tn), lambda j: (0, j)),
                  pl.BlockSpec((1, tn), lambda j: (0, j))],
        out_specs=pl.BlockSpec((rows, tn), lambda j: (0, j)),
        out_shape=jax.ShapeDtypeStruct((rows, n), F32),
        compiler_params=_cparams(("arbitrary",)),
        name="ada",
    )(c_pad, w, b)


def _inproj_kernel(x_ref, sc_ref, sh_ref, wglu_ref, wq_ref, wk_ref, wvt_ref, wf_ref, wga_ref, wgb_ref,
                   bf_ref, pq_ref, pk_ref,
                   u_ref, q_ref, k_ref, vt_ref, sga_ref, sgb_ref, carry_ref, *, tiles_per_seq, conv_ch):
    i = pl.program_id(0)
    tm = x_ref.shape[0]
    h = _ln(x_ref[...]) * (1.0 + sc_ref[0]) + sh_ref[0]
    hb = h.astype(BF16)

    glu = _dot(hb, wglu_ref[...])
    u_ref[...] = glu[:, :conv_ch] * jax.nn.sigmoid(glu[:, conv_ch:])

    f = _dot(hb, wf_ref[...]) + bf_ref[...]
    logf = jnp.minimum(f, 0.0) - jnp.log(1.0 + jnp.exp(-jnp.abs(f)))
    lh, lm, ll = _split3(logf)
    row = lax.broadcasted_iota(jnp.int32, (tm, tm), 0)
    col = lax.broadcasted_iota(jnp.int32, (tm, tm), 1)
    tri = jnp.where(row >= col, 1.0, 0.0).astype(BF16)
    cs = _dot(tri, lh) + _dot(tri, lm) + _dot(tri, ll)

    @pl.when(i % tiles_per_seq == 0)
    def _():
        carry_ref[...] = jnp.zeros_like(carry_ref)

    cum = cs + carry_ref[...]
    carry_ref[...] = cum[tm - 1:tm, :]

    ch, cm, cl = _split3(cum * LOG2E)
    lane = lax.broadcasted_iota(jnp.int32, cum.shape, 1)
    tail = jnp.where(lane == 24, 1.0, 0.0)
    pieces = jnp.where(lane < 8, ch.astype(F32), jnp.where(lane < 16, cm.astype(F32),
                       jnp.where(lane < 24, cl.astype(F32), tail))).astype(BF16)
    scale = FOX_HEAD_DIM ** -0.5 * LOG2E
    q_ref[...] = (_dot(hb, wq_ref[...]) * scale + _dot(pieces, pq_ref[...])).astype(BF16)
    k_ref[...] = (_dot(hb, wk_ref[...]) + _dot(pieces, pk_ref[...])).astype(BF16)
    vt_ref[0, 0] = _dot_nt(wvt_ref[...], hb).astype(BF16)
    sga_ref[...] = jax.nn.sigmoid(_dot(hb, wga_ref[...])).astype(BF16)
    sgb_ref[...] = jax.nn.sigmoid(_dot(hb, wgb_ref[...])).astype(BF16)


def _inproj(x2, scale1, shift1, wts, tm, seq):
    t, d = x2.shape
    tps = seq // tm
    bsz = t // seq
    wglu, wq, wk, wvt, wf, wga, wgb, bf, pq, pk = wts
    conv_ch = wglu.shape[1] // 2
    fw = wvt.shape[0]
    qw = wq.shape[1]
    mod_spec = pl.BlockSpec((1, 1, d), lambda i: (i // tps, 0, 0))
    row_spec = lambda n: pl.BlockSpec((tm, n), lambda i: (i, 0))
    return pl.pallas_call(
        functools.partial(_inproj_kernel, tiles_per_seq=tps, conv_ch=conv_ch),
        grid=(t // tm,),
        in_specs=[row_spec(d), mod_spec, mod_spec] + [_const_spec(w.shape) for w in wts],
        out_specs=[row_spec(conv_ch), row_spec(qw), row_spec(qw),
                   pl.BlockSpec((1, 1, fw, tm), lambda i: (i // tps, i % tps, 0, 0)),
                   row_spec(d), row_spec(d)],
        out_shape=[jax.ShapeDtypeStruct((t, conv_ch), F32),
                   jax.ShapeDtypeStruct((t, qw), BF16),
                   jax.ShapeDtypeStruct((t, qw), BF16),
                   jax.ShapeDtypeStruct((bsz, tps, fw, tm), BF16),
                   jax.ShapeDtypeStruct((t, d), BF16),
                   jax.ShapeDtypeStruct((t, d), BF16)],
        scratch_shapes=[pltpu.VMEM((1, V7X_LANES), F32)],
        compiler_params=_cparams(("arbitrary",)),
        name="inproj",
    )(x2, scale1, shift1, *wts)


def _conv_kernel(cur_ref, prev_ref, w_ref, b_ref, g_ref, be_ref, wout_ref, sga_ref, o_ref, ext_ref, shift_ref,
                 *, tiles_per_seq, chunk):
    i = pl.program_id(0)
    tm = cur_ref.shape[0]
    first = (i % tiles_per_seq) == 0
    ext_ref[0:CONV_HALO, :] = jnp.where(first, 0.0, prev_ref[...])
    ext_ref[CONV_HALO:, :] = cur_ref[...]
    w = w_ref[...]
    off = CONV_HALO - (CONV_WIDTH - 1)
    outs = []
    span = tm + CONV_HALO - 8
    for res in range(1, 8):
        shift_ref[res - 1] = ext_ref[res:res + span, :]
    for c0 in range(0, tm, chunk):
        acc = jnp.zeros((chunk, cur_ref.shape[1]), F32)
        for j in range(CONV_WIDTH):
            res, lo = (off + j) % 8, c0 + (off + j) // 8 * 8
            rows = ext_ref[lo:lo + chunk, :] if res == 0 else shift_ref[res - 1, lo:lo + chunk, :]
            acc = acc + w[j:j + 1, :] * rows
        outs.append(acc)
    v = jnp.concatenate(outs, axis=0) + b_ref[...]
    v = _ln(v) * g_ref[...] + be_ref[...]
    v = v * jax.nn.sigmoid(v)
    ya = _dot(v.astype(BF16), wout_ref[...])
    o_ref[...] = (sga_ref[...].astype(F32) * ya).astype(BF16)


def _conv(u, conv_w, conv_b, g, be, wout, sga, tm, seq):
    t, ch = u.shape
    d = wout.shape[1]
    tps = seq // tm
    halo_per_tile = tm // CONV_HALO
    return pl.pallas_call(
        functools.partial(_conv_kernel, tiles_per_seq=tps, chunk=64),
        grid=(t // tm,),
        in_specs=[pl.BlockSpec((tm, ch), lambda i: (i, 0)),
                  pl.BlockSpec((CONV_HALO, ch), lambda i: (jnp.maximum(i * halo_per_tile - 1, 0), 0)),
                  _const_spec(conv_w.shape), _const_spec(conv_b.shape), _const_spec(g.shape),
                  _const_spec(be.shape), _const_spec(wout.shape),
                  pl.BlockSpec((tm, d), lambda i: (i, 0))],
        out_specs=pl.BlockSpec((tm, d), lambda i: (i, 0)),
        out_shape=jax.ShapeDtypeStruct((t, d), BF16),
        scratch_shapes=[pltpu.VMEM((tm + CONV_HALO, ch), F32), pltpu.VMEM((7, tm + CONV_HALO - 8, ch), F32)],
        compiler_params=_cparams(("arbitrary",)),
        name="conv",
    )(u, u, conv_w, conv_b, g, be, wout, sga)


def _attn_kernel(q_ref, k_ref, vt_ref, o_ref, *, blk, heads):
    qi = pl.program_id(2)
    row = lax.broadcasted_iota(jnp.int32, (ATTN_EXTRA_ROWS, blk), 0)
    ones_rows = jnp.where(row == 0, 1.0, 0.0).astype(BF16)

    ncol = blk // ATTN_Q_COLS
    chains = [(j, c) for j in range(heads) for c in range(ncol)]

    def scores(kj, chain, masked):
        j, c = chain
        k = k_ref[pl.ds(pl.multiple_of(kj * blk, blk), blk), j * HEAD_PAD:(j + 1) * HEAD_PAD]
        q = q_ref[c * ATTN_Q_COLS:(c + 1) * ATTN_Q_COLS, j * HEAD_PAD:(j + 1) * HEAD_PAD]
        s = _dot_nt(k, q)
        if masked:
            kpos = lax.broadcasted_iota(jnp.int32, s.shape, 0)
            qpos = lax.broadcasted_iota(jnp.int32, s.shape, 1) + c * ATTN_Q_COLS
            s = jnp.where(kpos <= qpos, s, NEG_BIG)
        return s

    def probs(s, m):
        m_new = jnp.maximum(m, jnp.max(s, axis=0, keepdims=True))
        return jnp.exp2(s - m_new).astype(BF16), m_new

    def update(kj, chain, p, m, m_new, acc):
        j, _ = chain
        vt = vt_ref[kj, j * FOX_HEAD_DIM:(j + 1) * FOX_HEAD_DIM, :]
        lhs = jnp.concatenate([vt, ones_rows], axis=0)
        return jnp.exp2(m - m_new) * acc + _dot(lhs, p)

    def step(kj, carry, masked):
        n, lag = len(chains), ATTN_PIPE_LAG
        s, pm, out = {}, {}, [None] * n
        for i in range(n + lag):
            if i < n:
                s[i] = scores(kj, chains[i], masked)
            if lag - 1 <= i < n + lag - 1:
                pm[i - lag + 1] = probs(s.pop(i - lag + 1), carry[i - lag + 1][0])
            if i >= lag:
                p, m_new = pm.pop(i - lag)
                m, acc = carry[i - lag]
                out[i - lag] = (m_new, update(kj, chains[i - lag], p, m, m_new, acc))
        return tuple(out)

    init = tuple((jnp.full((1, ATTN_Q_COLS), NEG_BIG, F32),
                  jnp.zeros((FOX_HEAD_DIM + ATTN_EXTRA_ROWS, ATTN_Q_COLS), F32)) for _ in chains)
    carry = lax.fori_loop(0, qi, lambda kj, cr: step(kj, cr, False), init)
    carry = step(qi, carry, True)
    for (j, c), (_, acc) in zip(chains, carry):
        o_ref[j * FOX_HEAD_DIM:(j + 1) * FOX_HEAD_DIM, c * ATTN_Q_COLS:(c + 1) * ATTN_Q_COLS] = (
            acc[:FOX_HEAD_DIM] / acc[FOX_HEAD_DIM:FOX_HEAD_DIM + 1]).astype(BF16)


def _attn(q, k, vt, blk, heads):
    bsz, seq, _ = q.shape
    nkb = seq // blk
    return pl.pallas_call(
        functools.partial(_attn_kernel, blk=blk, heads=heads),
        grid=(bsz, FOX_HEADS // heads, seq // blk),
        in_specs=[pl.BlockSpec((None, blk, heads * HEAD_PAD), lambda b, h, i: (b, i, h)),
                  pl.BlockSpec((None, seq, heads * HEAD_PAD), lambda b, h, i: (b, 0, h),
                               pipeline_mode=pl.Buffered(1)),
                  pl.BlockSpec((None, nkb, heads * FOX_HEAD_DIM, blk), lambda b, h, i: (b, 0, h, 0),
                               pipeline_mode=pl.Buffered(1))],
        out_specs=pl.BlockSpec((None, heads * FOX_HEAD_DIM, blk), lambda b, h, i: (b, h, i)),
        out_shape=jax.ShapeDtypeStruct((bsz, FOX_HEADS * FOX_HEAD_DIM, seq), BF16),
        compiler_params=_cparams(("arbitrary", "arbitrary", "arbitrary")),
        name="attn",
    )(q, k, vt)


def _mix_kernel(ot_ref, gya_ref, sgb_ref, x_ref, g1_ref, sc2_ref, sh2_ref, lg_ref, lb_ref,
                wfox_ref, wmix_ref, wrh_ref, wrl_ref, wsg_ref, wsu_ref, wsd_ref,
                x1_ref, hp_ref, st_ref, shr_ref, *, alpha):
    yb = _dot_tn(ot_ref[...], wfox_ref[...])
    merged = gya_ref[...].astype(F32) + sgb_ref[...].astype(F32) * yb
    y = _dot(merged.astype(BF16), wmix_ref[...])
    x1 = _ln(alpha * x_ref[...] + (1.0 + g1_ref[0]) * y) * lg_ref[...] + lb_ref[...]
    x1_ref[...] = x1
    h2 = _ln(x1) * (1.0 + sc2_ref[0]) + sh2_ref[0]
    hb = h2.astype(BF16)
    hl = (h2 - hb.astype(F32)).astype(BF16)

    _store_packed_rows(hp_ref, h2)

    logits_t = _dot_nt(wrh_ref[...], hb) + _dot_nt(wrl_ref[...], hb) + _dot_nt(wrh_ref[...], hl)
    st_ref[...] = jax.nn.sigmoid(logits_t)

    g = _dot(hb, wsg_ref[...])
    u = _dot(hb, wsu_ref[...])
    a = (g * jax.nn.sigmoid(g) * u).astype(BF16)
    shr_ref[...] = _dot(a, wsd_ref[...])


def _mix(ot, gya, sgb, x2, gate1, scale2, shift2, lg, lb, wts, tm, seq, alpha):
    t, d = x2.shape
    tps = seq // tm
    fw = ot.shape[1]
    n_exp = wts[2].shape[0]
    mod_spec = pl.BlockSpec((1, 1, d), lambda i: (i // tps, 0, 0))
    row_spec = lambda n: pl.BlockSpec((tm, n), lambda i: (i, 0))
    return pl.pallas_call(
        functools.partial(_mix_kernel, alpha=alpha),
        grid=(t // tm,),
        in_specs=[pl.BlockSpec((None, fw, tm), lambda i: (i // tps, 0, i % tps)),
                  row_spec(d), row_spec(d), row_spec(d), mod_spec, mod_spec, mod_spec,
                  _const_spec(lg.shape), _const_spec(lb.shape)] + [_const_spec(w.shape) for w in wts],
        out_specs=[row_spec(d), pl.BlockSpec((tm * ROW_SLABS, V7X_LANES), lambda i: (i, 0)),
                   pl.BlockSpec((n_exp, tm), lambda i: (0, i)), row_spec(d)],
        out_shape=[jax.ShapeDtypeStruct((t, d), F32),
                   jax.ShapeDtypeStruct((t * ROW_SLABS, V7X_LANES), jnp.uint32),
                   jax.ShapeDtypeStruct((n_exp, t), F32),
                   jax.ShapeDtypeStruct((t, d), F32)],
        compiler_params=_cparams(("arbitrary",)),
        name="mix",
    )(ot, gya, sgb, x2, gate1, scale2, shift2, lg, lb, *wts)


def _route_kernel(st_ref, bias_ref, idx_ref, wts_ref, rank_ref, cnt_ref, carry_ref):
    i = pl.program_id(0)
    n_exp, tr = st_ref.shape
    gsz = n_exp // N_GROUPS
    neg_inf = -jnp.inf

    @pl.when(i == 0)
    def _():
        carry_ref[...] = jnp.zeros_like(carry_ref)

    shape3 = (N_GROUPS, gsz, tr)

    def max01(v):
        return jnp.max(jnp.max(v, axis=1, keepdims=True), axis=0, keepdims=True)

    def min01(v):
        return jnp.min(jnp.min(v, axis=1, keepdims=True), axis=0, keepdims=True)

    def sum01(v):
        return jnp.sum(jnp.sum(v, axis=1, keepdims=True), axis=0, keepdims=True)

    sc = st_ref[...].reshape(shape3)
    gsel = (st_ref[...] + bias_ref[...]).reshape(shape3)
    pos = lax.broadcasted_iota(jnp.int32, shape3, 1)
    m1 = jnp.max(gsel, axis=1, keepdims=True)
    i1 = jnp.min(jnp.where(gsel == m1, pos, gsz), axis=1, keepdims=True)
    m2 = jnp.max(jnp.where(pos == i1, neg_inf, gsel), axis=1, keepdims=True)
    gs = m1 + m2

    gid = lax.broadcasted_iota(jnp.int32, gs.shape, 0)
    gkeep = jnp.zeros(gs.shape, F32)
    for _ in range(TOPK_GROUPS):
        mx = jnp.max(gs, axis=0, keepdims=True)
        gi = jnp.min(jnp.where(gs == mx, gid, N_GROUPS), axis=0, keepdims=True)
        hit = gid == gi
        gkeep = gkeep + jnp.where(hit, 1.0, 0.0)
        gs = jnp.where(hit, neg_inf, gs)

    cur = jnp.where(jnp.broadcast_to(gkeep, shape3) > 0.5, gsel, neg_inf)
    eid = lax.broadcasted_iota(jnp.int32, shape3, 0) * gsz + pos
    onehot = jnp.zeros(shape3, F32)
    idxs, ws = [], []
    wsum = jnp.zeros((1, 1, tr), F32)
    for _ in range(TOP_K):
        mx = max01(cur)
        ik = min01(jnp.where(cur == mx, eid, n_exp))
        hit = eid == ik
        wk = sum01(jnp.where(hit, sc, 0.0))
        idxs.append(ik)
        ws.append(wk)
        wsum = wsum + wk
        onehot = onehot + jnp.where(hit, 1.0, 0.0)
        cur = jnp.where(hit, neg_inf, cur)

    ra = lax.broadcasted_iota(jnp.int32, (tr, tr), 0)
    rb = lax.broadcasted_iota(jnp.int32, (tr, tr), 1)
    upper = jnp.where(ra < rb, 1.0, 0.0).astype(BF16)
    onehot2 = onehot.reshape(n_exp, tr)
    prior = (_dot(onehot2.astype(BF16), upper) + carry_ref[...]).reshape(shape3)
    for slot in range(TOP_K):
        idx_ref[slot:slot + 1, :] = idxs[slot].reshape(1, tr)
        wts_ref[slot:slot + 1, :] = (ws[slot] / wsum * ROUTED_SCALE).reshape(1, tr)
        rk = sum01(jnp.where(eid == idxs[slot], prior, 0.0))
        rank_ref[slot:slot + 1, :] = rk.reshape(1, tr).astype(jnp.int32)
    total = carry_ref[...] + jnp.sum(onehot2, axis=1, keepdims=True)
    carry_ref[...] = total
    cnt_ref[...] = jnp.broadcast_to(total, cnt_ref.shape).astype(jnp.int32)


def _route(scores_t, bias_col, tr):
    n_exp, t = scores_t.shape
    slot_spec = pl.BlockSpec((TOP_K, tr), lambda i: (0, i))
    return pl.pallas_call(
        _route_kernel,
        grid=(t // tr,),
        in_specs=[pl.BlockSpec((n_exp, tr), lambda i: (0, i)), _const_spec(bias_col.shape)],
        out_specs=[slot_spec, slot_spec, slot_spec, _const_spec((n_exp, V7X_LANES))],
        out_shape=[jax.ShapeDtypeStruct((TOP_K, t), jnp.int32),
                   jax.ShapeDtypeStruct((TOP_K, t), F32),
                   jax.ShapeDtypeStruct((TOP_K, t), jnp.int32),
                   jax.ShapeDtypeStruct((n_exp, V7X_LANES), jnp.int32)],
        scratch_shapes=[pltpu.VMEM((n_exp, 1), F32)],
        compiler_params=_cparams(("arbitrary",)),
        name="route",
    )(scores_t, bias_col)


def _dest_kernel(pstart_ref, idx_ref, rank_ref, o_ref):
    n_exp = pstart_ref.shape[0]
    tr = idx_ref.shape[1]
    eid = lax.broadcasted_iota(jnp.int32, (n_exp, tr), 0)
    pstart = pstart_ref[...]
    for slot in range(TOP_K):
        hit = eid == idx_ref[slot:slot + 1, :]
        start = jnp.sum(jnp.where(hit, pstart, 0.0), axis=0, keepdims=True).astype(jnp.int32)
        o_ref[slot:slot + 1, :] = (start + rank_ref[slot:slot + 1, :]) * ROW_SLABS


def _dest(pstart, idx, rank, tr):
    k, t = idx.shape
    spec = pl.BlockSpec((k, tr), lambda i: (0, i))
    pstart_col = pstart.astype(F32)[:, None]
    return pl.pallas_call(
        _dest_kernel,
        grid=(t // tr,),
        in_specs=[_const_spec(pstart_col.shape), spec, spec],
        out_specs=spec,
        out_shape=jax.ShapeDtypeStruct((k, t), jnp.int32),
        compiler_params=_cparams(("arbitrary",)),
        name="dest",
    )(pstart_col, idx, rank)


def _dispatch_kernel(dest_ref, hp_ref, xs_ref, sem):
    td = hp_ref.shape[0] // ROW_SLABS

    def row_copy(r, slot):
        dst = pl.multiple_of(dest_ref[r * TOP_K + slot], ROW_SLABS)
        src = pl.multiple_of(r * ROW_SLABS, ROW_SLABS)
        return pltpu.make_async_copy(hp_ref.at[pl.ds(src, ROW_SLABS)], xs_ref.at[pl.ds(dst, ROW_SLABS)], sem)

    def start(r, c):
        for slot in range(TOP_K):
            row_copy(r, slot).start(priority=slot % 2)
        return c

    def wait(r, c):
        for slot in range(TOP_K):
            row_copy(r, slot).wait()
        return c

    lax.fori_loop(0, td, start, 0, unroll=4)
    lax.fori_loop(0, td, wait, 0, unroll=8)


def _dispatch(dest_flat, hp, n_pad, td):
    t = hp.shape[0] // ROW_SLABS
    return pl.pallas_call(
        _dispatch_kernel,
        grid=(t // td,),
        in_specs=[pl.BlockSpec((td * TOP_K,), lambda i: (i,), memory_space=pltpu.SMEM),
                  pl.BlockSpec((td * ROW_SLABS, V7X_LANES), lambda i: (i, 0))],
        out_specs=pl.BlockSpec(memory_space=pl.ANY),
        out_shape=jax.ShapeDtypeStruct((n_pad * ROW_SLABS, V7X_LANES), hp.dtype),
        scratch_shapes=[pltpu.SemaphoreType.DMA(())],
        compiler_params=_cparams(("arbitrary",)),
        name="dispatch",
    )(dest_flat, hp)


def _expert_kernel(be_ref, rows_ref, seq_ref, cnt_ref, xs_hbm, wg_hbm, wu_hbm, wd_hbm, ys_hbm,
                   xbuf, ybuf, wgf, wuf, wdf, wg_s, wu_s, wd_s, sem_x, sem_y, sem_w, *, layer):
    n = cnt_ref[0]
    n_seq = cnt_ref[1]
    blk_rows = EXPERT_BLOCK * ROW_SLABS
    nx, nw = EXPERT_X_AHEAD + 1, EXPERT_W_AHEAD + 1

    def block_rows(q):
        start = q * blk_rows
        return pl.ds(start if isinstance(q, int) else pl.multiple_of(start, blk_rows), blk_rows)

    def x_copy(q, slot):
        return pltpu.make_async_copy(xs_hbm.at[block_rows(q)], xbuf.at[slot], sem_x.at[slot])

    def y_copy(q, slot):
        return pltpu.make_async_copy(ybuf.at[slot], ys_hbm.at[block_rows(q)], sem_y.at[slot])

    def w_copies(k, ws):
        e = seq_ref[k]
        out = []
        for hbm, buf in ((wg_hbm, wgf), (wu_hbm, wuf), (wd_hbm, wdf)):
            half = buf.shape[1] // 2
            for part in range(2):
                rows = pl.ds(part * half, half)
                out.append((pltpu.make_async_copy(hbm.at[layer, e, rows], buf.at[ws, rows], sem_w.at[ws]), part))
        return out

    for j in range(EXPERT_X_AHEAD):
        @pl.when(j < n)
        def _():
            x_copy(j, j).start(priority=1)

    for j in range(EXPERT_W_AHEAD):
        @pl.when(j < n_seq)
        def _():
            for cp, prio in w_copies(j, j):
                cp.start(priority=prio)

    def body(q, k):
        slot = q % 2
        fresh = jnp.logical_or(q == 0, be_ref[q] != be_ref[jnp.maximum(q - 1, 0)])

        @pl.when(fresh)
        def _():
            ws = k % nw
            for cp, _ in w_copies(k, ws):
                cp.wait()
            wg_s[...] = wgf[ws].astype(BF16)
            wu_s[...] = wuf[ws].astype(BF16)
            wd_s[...] = wdf[ws].astype(BF16)

            @pl.when(k + EXPERT_W_AHEAD < n_seq)
            def _():
                for cp, prio in w_copies(k + EXPERT_W_AHEAD, (k + EXPERT_W_AHEAD) % nw):
                    cp.start(priority=prio)

        @pl.when(q + EXPERT_X_AHEAD < n)
        def _():
            x_copy(q + EXPERT_X_AHEAD, (q + EXPERT_X_AHEAD) % nx).start(priority=1)

        x_copy(q, q % nx).wait()

        @pl.when(q >= 2)
        def _():
            y_copy(q - 2, slot).wait()

        x = jnp.concatenate(_load_packed_rows(xbuf, EXPERT_BLOCK, lead=(q % nx,)), axis=1)
        valid = lax.broadcasted_iota(jnp.int32, x.shape, 0) < rows_ref[q]
        x = jnp.where(valid, x, 0.0).astype(BF16)
        g = _dot(x, wg_s[...])
        u = _dot(x, wu_s[...])
        a = (g * jax.nn.sigmoid(g) * u).astype(BF16)
        _store_packed_rows(ybuf.at[slot], _dot(a, wd_s[...]))
        y_copy(q, slot).start()
        return k + fresh.astype(jnp.int32)

    lax.fori_loop(0, n, body, jnp.int32(0))

    @pl.when(n >= 2)
    def _():
        y_copy(n - 2, n % 2).wait()

    y_copy(n - 1, (n - 1) % 2).wait()


def _experts(block_expert, block_rows, expert_seq, counts2, xs, wg, wu, wd, layer):
    _, n_exp, d, f = wg.shape
    blk = (EXPERT_BLOCK * ROW_SLABS, V7X_LANES)
    nx, nw = EXPERT_X_AHEAD + 1, EXPERT_W_AHEAD + 1
    any_spec = pl.BlockSpec(memory_space=pl.ANY)
    return pl.pallas_call(
        functools.partial(_expert_kernel, layer=layer),
        grid_spec=pltpu.PrefetchScalarGridSpec(
            num_scalar_prefetch=4, grid=(1,),
            in_specs=[any_spec, any_spec, any_spec, any_spec],
            out_specs=any_spec,
            scratch_shapes=[pltpu.VMEM((nx,) + blk, xs.dtype), pltpu.VMEM((2,) + blk, xs.dtype),
                            pltpu.VMEM((nw, d, f), F32), pltpu.VMEM((nw, d, f), F32), pltpu.VMEM((nw, f, d), F32),
                            pltpu.VMEM((d, f), BF16), pltpu.VMEM((d, f), BF16), pltpu.VMEM((f, d), BF16),
                            pltpu.SemaphoreType.DMA((nx,)), pltpu.SemaphoreType.DMA((2,)),
                            pltpu.SemaphoreType.DMA((nw,))]),
        out_shape=jax.ShapeDtypeStruct(xs.shape, xs.dtype),
        compiler_params=_cparams(("arbitrary",)),
        name="experts",
    )(block_expert, block_rows, expert_seq, counts2, xs, wg, wu, wd)


def _combine_kernel(d0_ref, d1_ref, d2_ref, ys_ref, w_ref, shr_ref, x1_ref, g2_ref, lg_ref, lb_ref, o_ref,
                    buf_a, buf_b, sem, *, alpha, tc):
    i = pl.program_id(0)
    last = pl.num_programs(0) - 1

    def row_copy(dref, r, slot, buf, sem_idx):
        src = pl.multiple_of(dref[r * TOP_K + slot], ROW_SLABS)
        dst = r * ROW_SLABS if isinstance(r, int) else pl.multiple_of(r * ROW_SLABS, ROW_SLABS)
        return pltpu.make_async_copy(ys_ref.at[pl.ds(src, ROW_SLABS)], buf.at[slot, pl.ds(dst, ROW_SLABS)],
                                     sem.at[sem_idx])

    def issue_unrolled(dref, buf, sem_idx):
        for r in range(tc):
            for slot in range(TOP_K):
                row_copy(dref, r, slot, buf, sem_idx).start(priority=slot % 2)

    def issue_loop(dref, buf, sem_idx):
        def start(r, c):
            for slot in range(TOP_K):
                row_copy(dref, r, slot, buf, sem_idx).start(priority=slot % 2)
            return c
        lax.fori_loop(0, tc, start, 0, unroll=4)

    def wait_all(dref, buf, sem_idx):
        def wait(r, c):
            for slot in range(TOP_K):
                row_copy(dref, r, slot, buf, sem_idx).wait()
            return c
        lax.fori_loop(0, tc, wait, 0, unroll=8)

    def reduce_tile(buf, rows):
        w = w_ref[rows, :]
        chunks = None
        for slot in range(TOP_K):
            wk = w[:, slot:slot + 1]
            part = [c * wk for c in _load_packed_rows(buf, tc, lead=(slot,))]
            chunks = part if chunks is None else [a + b for a, b in zip(chunks, part)]
        y = shr_ref[rows, :] + jnp.concatenate(chunks, axis=1)
        z = alpha * x1_ref[rows, :] + (1.0 + g2_ref[0]) * y
        o_ref[rows, :] = _ln(z) * lg_ref[...] + lb_ref[...]

    @pl.when(i == 0)
    def _():
        issue_loop(d0_ref, buf_a, 0)

    wait_all(d0_ref, buf_a, 0)
    issue_unrolled(d1_ref, buf_b, 1)
    reduce_tile(buf_a, pl.ds(0, tc))

    wait_all(d1_ref, buf_b, 1)
    issue_unrolled(d2_ref, buf_a, 0)
    reduce_tile(buf_b, pl.ds(tc, tc))

    @pl.when(i == last)
    def _():
        wait_all(d2_ref, buf_a, 0)


def _combine(dest_flat, ys, wts_tk, shared, x1, gate2, lg, lb, tc, seq, alpha):
    t, d = x1.shape
    tps = seq // (2 * tc)
    n_tiles = t // tc
    row_spec = lambda n: pl.BlockSpec((2 * tc, n), lambda i: (i, 0))
    tile_dest = lambda f: pl.BlockSpec((tc * TOP_K,), f, memory_space=pltpu.SMEM)
    buf = pltpu.VMEM((TOP_K, tc * ROW_SLABS, V7X_LANES), jnp.uint32)
    return pl.pallas_call(
        functools.partial(_combine_kernel, alpha=alpha, tc=tc),
        grid=(n_tiles // 2,),
        in_specs=[tile_dest(lambda i: (2 * i,)), tile_dest(lambda i: (2 * i + 1,)),
                  tile_dest(lambda i: (jnp.minimum(2 * i + 2, n_tiles - 1),)),
                  pl.BlockSpec(memory_space=pl.ANY),
                  row_spec(TOP_K), row_spec(d), row_spec(d),
                  pl.BlockSpec((1, 1, d), lambda i: (i // tps, 0, 0)),
                  _const_spec(lg.shape), _const_spec(lb.shape)],
        out_specs=row_spec(d),
        out_shape=jax.ShapeDtypeStruct((t, d), F32),
        scratch_shapes=[buf, buf, pltpu.SemaphoreType.DMA((2,))],
        compiler_params=_cparams(("arbitrary",)),
        name="combine",
    )(dest_flat, dest_flat, dest_flat, ys, wts_tk, shared, x1, gate2, lg, lb)


def _placement():
    pq = np.zeros((V7X_LANES, FOX_HEADS * HEAD_PAD), np.float32)
    pk = np.zeros((V7X_LANES, FOX_HEADS * HEAD_PAD), np.float32)
    for h in range(FOX_HEADS):
        base = h * HEAD_PAD + FOX_HEAD_DIM
        for piece in range(3):
            pq[piece * 8 + h, base + piece] = 1.0
            pk[24, base + piece] = 1.0
            pq[24, base + 3 + piece] = 1.0
            pk[piece * 8 + h, base + 3 + piece] = -1.0
    return jnp.asarray(pq, BF16), jnp.asarray(pk, BF16)


def _inproj_weights(w_in, b_forget, d):
    conv2 = d
    fw = FOX_HEADS * FOX_HEAD_DIM
    o1, o2, o3, o4 = conv2, conv2 + fw, conv2 + 2 * fw, conv2 + 3 * fw
    o5 = o4 + FOX_HEADS
    o6 = o5 + d

    def pad_heads(w):
        w = w.reshape(d, FOX_HEADS, FOX_HEAD_DIM)
        w = jnp.pad(w, ((0, 0), (0, 0), (0, HEAD_PAD - FOX_HEAD_DIM)))
        return w.reshape(d, FOX_HEADS * HEAD_PAD).astype(BF16)

    wglu = w_in[:, :o1].astype(BF16)
    wq = pad_heads(w_in[:, o1:o2])
    wk = pad_heads(w_in[:, o2:o3])
    wvt = w_in[:, o3:o4].T.astype(BF16)
    wf8 = w_in[:, o4:o5]
    wf = jnp.pad(jnp.concatenate([wf8, wf8, wf8], axis=1), ((0, 0), (0, V7X_LANES - 3 * FOX_HEADS))).astype(BF16)
    bf = jnp.pad(jnp.concatenate([b_forget, b_forget, b_forget]), (0, V7X_LANES - 3 * FOX_HEADS))[None, :].astype(F32)
    wga = w_in[:, o5:o6].astype(BF16)
    wgb = w_in[:, o6:].astype(BF16)
    pq, pk = _placement()
    return (wglu, wq, wk, wvt, wf, wga, wgb, bf, pq, pk)


def _layer(x2, ada, bsz, seq, w_in, b_forget, conv_w, conv_b, conv_ln_g, conv_ln_b, w_conv_out, w_fox_out,
           w_mix_out, ln1_g, ln1_b, w_router, router_bias, w_exp_gate, w_exp_up, w_exp_down,
           w_sh_gate, w_sh_up, w_sh_down, ln2_g, ln2_b, depth, layer):
    t, d = x2.shape
    n_exp = w_router.shape[1]
    alpha = (2.0 * depth) ** 0.25
    mods = [ada[:bsz, j * d:(j + 1) * d][:, None, :] for j in range(6)]
    shift1, scale1, gate1, shift2, scale2, gate2 = mods

    tm = min(512, seq)
    u, q, k, vt, sga, sgb = _inproj(x2, scale1, shift1, _inproj_weights(w_in, b_forget, d), tm, seq)

    tcv = min(256, seq)
    conv_w_pad = jnp.pad(conv_w, ((0, CONV_HALO - CONV_WIDTH), (0, 0)))
    gya = _conv(u, conv_w_pad, conv_b[None, :], conv_ln_g[None, :], conv_ln_b[None, :],
                w_conv_out.astype(BF16), sga, tcv, seq)

    ot = _attn(q.reshape(bsz, seq, -1), k.reshape(bsz, seq, -1), vt, tm, ATTN_HEADS_PER_STEP)

    tmx = min(256, seq)
    wr_t = w_router.T
    wr_h = wr_t.astype(BF16)
    wr_l = (wr_t - wr_h.astype(F32)).astype(BF16)
    mix_w = (w_fox_out.astype(BF16), w_mix_out.astype(BF16), wr_h, wr_l,
             w_sh_gate.astype(BF16), w_sh_up.astype(BF16), w_sh_down.astype(BF16))
    x1, hp, scores_t, shared = _mix(ot, gya, sgb, x2, gate1, scale2, shift2, ln1_g[None, :], ln1_b[None, :],
                                    mix_w, tmx, seq, alpha)

    tr = min(512, t)
    idx, wts, rank, cnt = _route(scores_t, router_bias[:, None], tr)

    counts = cnt[:, 0]
    padded = (counts + EXPERT_BLOCK - 1) // EXPERT_BLOCK * EXPERT_BLOCK
    pend = jnp.cumsum(padded)
    pstart = (pend - padded).astype(jnp.int32)
    n_assign = t * TOP_K
    n_pad = -(-(n_assign + n_exp * (EXPERT_BLOCK - 1)) // EXPERT_BLOCK) * EXPERT_BLOCK
    n_blocks = n_pad // EXPERT_BLOCK
    block_start = jnp.arange(n_blocks, dtype=jnp.int32) * EXPERT_BLOCK
    block_expert = jnp.minimum(jnp.sum(pend[None, :] <= block_start[:, None], axis=1), n_exp - 1).astype(jnp.int32)
    block_rows = jnp.clip((pstart + counts)[block_expert] - block_start, 0, EXPERT_BLOCK).astype(jnp.int32)
    n_used = (pend[-1:] // EXPERT_BLOCK).astype(jnp.int32)
    owns = counts > 0
    expert_seq = jnp.nonzero(owns, size=n_exp, fill_value=0)[0].astype(jnp.int32)
    counts2 = jnp.concatenate([n_used, jnp.sum(owns, dtype=jnp.int32)[None]])

    dest = _dest(pstart, idx, rank, tr)
    dest_flat = dest.T.reshape(-1)
    xs = _dispatch(dest_flat, hp, n_pad, min(256, t))
    ys = _experts(block_expert, block_rows, expert_seq, counts2, xs, w_exp_gate, w_exp_up, w_exp_down, layer)
    return _combine(dest_flat, ys, wts.T, shared, x1, gate2, ln2_g[None, :], ln2_b[None, :],
                    min(128, seq), seq, alpha)


def kernel(x, c, w_ada, b_ada, w_in, b_forget, conv_w, conv_b, conv_ln_g, conv_ln_b, w_conv_out, w_fox_out,
           w_mix_out, ln1_g, ln1_b, w_router, router_bias, w_exp_gate, w_exp_up, w_exp_down, w_sh_gate,
           w_sh_up, w_sh_down, ln2_g, ln2_b):
    bsz, seq, d = x.shape
    depth = w_ada.shape[0]
    c_pad = jnp.pad(c, ((0, 8 - bsz), (0, 0)))
    x2 = x.reshape(bsz * seq, d)
    for l in range(depth):
        ada = _ada(c_pad, w_ada[l], b_ada[l][None, :])
        x2 = _layer(x2, ada, bsz, seq, w_in[l], b_forget[l], conv_w[l], conv_b[l], conv_ln_g[l], conv_ln_b[l],
                    w_conv_out[l], w_fox_out[l], w_mix_out[l], ln1_g[l], ln1_b[l], w_router[l], router_bias[l],
                    w_exp_gate, w_exp_up, w_exp_down, w_sh_gate[l], w_sh_up[l], w_sh_down[l],
                    ln2_g[l], ln2_b[l], depth, l)
    return x2.reshape(bsz, seq, d)
```

```python
import functools

import jax
import jax.numpy as jnp
import numpy as np
from jax import lax
from jax.experimental import pallas as pl
from jax.experimental.pallas import tpu as pltpu

F32 = jnp.float32
BF16 = jnp.bfloat16

LN_EPS = 1e-5
CONV_WIDTH = 31
FOX_HEADS = 8
FOX_HEAD_DIM = 64
N_GROUPS = 8
TOPK_GROUPS = 4
TOP_K = 8
ROUTED_SCALE = 2.5
EXPERT_BLOCK = 256
EXPERT_X_AHEAD = 4
EXPERT_W_AHEAD = 4
ROW_SLABS = 4

V7X_LANES = 128
HEAD_PAD = 128
CONV_HALO = 32
VMEM_LIMIT = 56 * 1024 * 1024
NEG_BIG = -1e30
LOG2E = 1.4426950408889634
ATTN_EXTRA_ROWS = 16
ATTN_HEADS_PER_STEP = 8
ATTN_Q_COLS = 256
MIX_ROWS_PER_GROUP = 256
ATTN_PIPE_LAG = 3


def _cparams(sem):
    return pltpu.CompilerParams(dimension_semantics=sem, vmem_limit_bytes=VMEM_LIMIT)


def _ln(v):
    mu = jnp.mean(v, axis=-1, keepdims=True)
    vc = v - mu
    var = jnp.mean(vc * vc, axis=-1, keepdims=True)
    return vc * lax.rsqrt(var + LN_EPS)


def _split3(v):
    hi = v.astype(BF16)
    r1 = v - hi.astype(F32)
    mid = r1.astype(BF16)
    lo = (r1 - mid.astype(F32)).astype(BF16)
    return hi, mid, lo


def _dot(a, b):
    return jnp.dot(a, b, preferred_element_type=F32)


def _dot_nt(a, b):
    return lax.dot_general(a, b, (((1,), (1,)), ((), ())), preferred_element_type=F32)


def _dot_tn(a, b):
    return lax.dot_general(a, b, (((0,), (0,)), ((), ())), preferred_element_type=F32)


def _store_packed_rows(ref, v, row0=0):
    n, d = v.shape
    half = d // 2
    vb = v.astype(BF16).astype(F32)
    lo_bits = lax.bitcast_convert_type(vb[:, :half], jnp.uint32)
    hi_bits = lax.bitcast_convert_type(vb[:, half:], jnp.uint32)
    words = (lo_bits >> 16) | (hi_bits & jnp.uint32(0xFFFF0000))
    for c in range(ROW_SLABS):
        ref[pl.ds(row0 * ROW_SLABS + c, n, stride=ROW_SLABS), :] = words[:, c * V7X_LANES:(c + 1) * V7X_LANES]


def _load_packed_rows(ref, n, lead=(), row0=0):
    lo, hi = [], []
    for c in range(ROW_SLABS):
        w = ref[lead + (pl.ds(row0 * ROW_SLABS + c, n, stride=ROW_SLABS), slice(None))]
        lo.append(lax.bitcast_convert_type(w << 16, F32))
        hi.append(lax.bitcast_convert_type(w & jnp.uint32(0xFFFF0000), F32))
    return lo + hi


def _const_spec(shape):
    nd = len(shape)
    return pl.BlockSpec(shape, lambda *_: (0,) * nd)


def _ada_kernel(c_ref, w_ref, b_ref, o_ref):
    c = c_ref[...]
    cond = c * jax.nn.sigmoid(c)
    ch, cm, _ = _split3(cond)
    w = w_ref[...]
    wh, wm, _ = _split3(w)
    o_ref[...] = _dot(ch, wh) + _dot(ch, wm) + _dot(cm, wh) + b_ref[...]


def _ada(c_pad, w, b):
    rows, d = c_pad.shape
    n = w.shape[1]
    tn = 1024
    return pl.pallas_call(
        _ada_kernel,
        grid=(n // tn,),
        in_specs=[_const_spec((rows, d)),
                  pl.BlockSpec((d, tn), lambda j: (0, j)),
                  pl.BlockSpec((1, tn), lambda j: (0, j))],
        out_specs=pl.BlockSpec((rows, tn), lambda j: (0, j)),
        out_shape=jax.ShapeDtypeStruct((rows, n), F32),
        compiler_params=_cparams(("arbitrary",)),
        name="ada",
    )(c_pad, w, b)


def _inproj_kernel(x_ref, sc_ref, sh_ref, wglu_ref, wq_ref, wk_ref, wvt_ref, wf_ref, wga_ref, wgb_ref,
                   bf_ref, pq_ref, pk_ref,
                   u_ref, q_ref, k_ref, vt_ref, sga_ref, sgb_ref, carry_ref, *, tiles_per_seq, conv_ch):
    i = pl.program_id(0)
    tm = x_ref.shape[0]
    h = _ln(x_ref[...]) * (1.0 + sc_ref[0]) + sh_ref[0]
    hb = h.astype(BF16)

    glu = _dot(hb, wglu_ref[...])
    u_ref[...] = glu[:, :conv_ch] * jax.nn.sigmoid(glu[:, conv_ch:])

    f = _dot(hb, wf_ref[...]) + bf_ref[...]
    logf = jnp.minimum(f, 0.0) - jnp.log(1.0 + jnp.exp(-jnp.abs(f)))
    lh, lm, ll = _split3(logf)
    row = lax.broadcasted_iota(jnp.int32, (tm, tm), 0)
    col = lax.broadcasted_iota(jnp.int32, (tm, tm), 1)
    tri = jnp.where(row >= col, 1.0, 0.0).astype(BF16)
    cs = _dot(tri, lh) + _dot(tri, lm) + _dot(tri, ll)

    @pl.when(i % tiles_per_seq == 0)
    def _():
        carry_ref[...] = jnp.zeros_like(carry_ref)

    cum = cs + carry_ref[...]
    carry_ref[...] = cum[tm - 1:tm, :]

    ch, cm, cl = _split3(cum * LOG2E)
    lane = lax.broadcasted_iota(jnp.int32, cum.shape, 1)
    tail = jnp.where(lane == 24, 1.0, 0.0)
    pieces = jnp.where(lane < 8, ch.astype(F32), jnp.where(lane < 16, cm.astype(F32),
                       jnp.where(lane < 24, cl.astype(F32), tail))).astype(BF16)
    scale = FOX_HEAD_DIM ** -0.5 * LOG2E
    q_ref[...] = (_dot(hb, wq_ref[...]) * scale + _dot(pieces, pq_ref[...])).astype(BF16)
    k_ref[...] = (_dot(hb, wk_ref[...]) + _dot(pieces, pk_ref[...])).astype(BF16)
    vt_ref[0, 0] = _dot_nt(wvt_ref[...], hb).astype(BF16)
    sga_ref[...] = jax.nn.sigmoid(_dot(hb, wga_ref[...])).astype(BF16)
    sgb_ref[...] = jax.nn.sigmoid(_dot(hb, wgb_ref[...])).astype(BF16)


def _inproj(x2, scale1, shift1, wts, tm, seq):
    t, d = x2.shape
    tps = seq // tm
    bsz = t // seq
    wglu, wq, wk, wvt, wf, wga, wgb, bf, pq, pk = wts
    conv_ch = wglu.shape[1] // 2
    fw = wvt.shape[0]
    qw = wq.shape[1]
    mod_spec = pl.BlockSpec((1, 1, d), lambda i: (i // tps, 0, 0))
    row_spec = lambda n: pl.BlockSpec((tm, n), lambda i: (i, 0))
    return pl.pallas_call(
        functools.partial(_inproj_kernel, tiles_per_seq=tps, conv_ch=conv_ch),
        grid=(t // tm,),
        in_specs=[row_spec(d), mod_spec, mod_spec] + [_const_spec(w.shape) for w in wts],
        out_specs=[row_spec(conv_ch), row_spec(qw), row_spec(qw),
                   pl.BlockSpec((1, 1, fw, tm), lambda i: (i // tps, i % tps, 0, 0)),
                   row_spec(d), row_spec(d)],
        out_shape=[jax.ShapeDtypeStruct((t, conv_ch), F32),
                   jax.ShapeDtypeStruct((t, qw), BF16),
                   jax.ShapeDtypeStruct((t, qw), BF16),
                   jax.ShapeDtypeStruct((bsz, tps, fw, tm), BF16),
                   jax.ShapeDtypeStruct((t, d), BF16),
                   jax.ShapeDtypeStruct((t, d), BF16)],
        scratch_shapes=[pltpu.VMEM((1, V7X_LANES), F32)],
        compiler_params=_cparams(("arbitrary",)),
        name="inproj",
    )(x2, scale1, shift1, *wts)


def _conv_kernel(cur_ref, prev_ref, w_ref, b_ref, g_ref, be_ref, wout_ref, sga_ref, o_ref, ext_ref, shift_ref,
                 *, tiles_per_seq, chunk):
    i = pl.program_id(0)
    tm = cur_ref.shape[0]
    first = (i % tiles_per_seq) == 0
    ext_ref[0:CONV_HALO, :] = jnp.where(first, 0.0, prev_ref[...])
    ext_ref[CONV_HALO:, :] = cur_ref[...]
    w = w_ref[...]
    off = CONV_HALO - (CONV_WIDTH - 1)
    outs = []
    span = tm + CONV_HALO - 8
    for res in range(1, 8):
        shift_ref[res - 1] = ext_ref[res:res + span, :]
    for c0 in range(0, tm, chunk):
        acc = jnp.zeros((chunk, cur_ref.shape[1]), F32)
        for j in range(CONV_WIDTH):
            res, lo = (off + j) % 8, c0 + (off + j) // 8 * 8
            rows = ext_ref[lo:lo + chunk, :] if res == 0 else shift_ref[res - 1, lo:lo + chunk, :]
            acc = acc + w[j:j + 1, :] * rows
        outs.append(acc)
    v = jnp.concatenate(outs, axis=0) + b_ref[...]
    v = _ln(v) * g_ref[...] + be_ref[...]
    v = v * jax.nn.sigmoid(v)
    ya = _dot(v.astype(BF16), wout_ref[...])
    o_ref[...] = (sga_ref[...].astype(F32) * ya).astype(BF16)


def _conv(u, conv_w, conv_b, g, be, wout, sga, tm, seq):
    t, ch = u.shape
    d = wout.shape[1]
    tps = seq // tm
    halo_per_tile = tm // CONV_HALO
    return pl.pallas_call(
        functools.partial(_conv_kernel, tiles_per_seq=tps, chunk=64),
        grid=(t // tm,),
        in_specs=[pl.BlockSpec((tm, ch), lambda i: (i, 0)),
                  pl.BlockSpec((CONV_HALO, ch), lambda i: (jnp.maximum(i * halo_per_tile - 1, 0), 0)),
                  _const_spec(conv_w.shape), _const_spec(conv_b.shape), _const_spec(g.shape),
                  _const_spec(be.shape), _const_spec(wout.shape),
                  pl.BlockSpec((tm, d), lambda i: (i, 0))],
        out_specs=pl.BlockSpec((tm, d), lambda i: (i, 0)),
        out_shape=jax.ShapeDtypeStruct((t, d), BF16),
        scratch_shapes=[pltpu.VMEM((tm + CONV_HALO, ch), F32), pltpu.VMEM((7, tm + CONV_HALO - 8, ch), F32)],
        compiler_params=_cparams(("arbitrary",)),
        name="conv",
    )(u, u, conv_w, conv_b, g, be, wout, sga)


def _attn_kernel(q_ref, k_ref, vt_ref, o_ref, *, blk, heads):
    qi = pl.program_id(2)
    row = lax.broadcasted_iota(jnp.int32, (ATTN_EXTRA_ROWS, blk), 0)
    ones_rows = jnp.where(row == 0, 1.0, 0.0).astype(BF16)

    ncol = blk // ATTN_Q_COLS
    chains = [(j, c) for j in range(heads) for c in range(ncol)]

    def scores(kj, chain, masked):
        j, c = chain
        k = k_ref[pl.ds(pl.multiple_of(kj * blk, blk), blk), j * HEAD_PAD:(j + 1) * HEAD_PAD]
        q = q_ref[c * ATTN_Q_COLS:(c + 1) * ATTN_Q_COLS, j * HEAD_PAD:(j + 1) * HEAD_PAD]
        s = _dot_nt(k, q)
        if masked:
            kpos = lax.broadcasted_iota(jnp.int32, s.shape, 0)
            qpos = lax.broadcasted_iota(jnp.int32, s.shape, 1) + c * ATTN_Q_COLS
            s = jnp.where(kpos <= qpos, s, NEG_BIG)
        return s

    def probs(s, m):
        m_new = jnp.maximum(m, jnp.max(s, axis=0, keepdims=True))
        return jnp.exp2(s - m_new).astype(BF16), m_new

    def update(kj, chain, p, m, m_new, acc):
        j, _ = chain
        vt = vt_ref[kj, j * FOX_HEAD_DIM:(j + 1) * FOX_HEAD_DIM, :]
        lhs = jnp.concatenate([vt, ones_rows], axis=0)
        return jnp.exp2(m - m_new) * acc + _dot(lhs, p)

    def step(kj, carry, masked):
        n, lag = len(chains), ATTN_PIPE_LAG
        s, pm, out = {}, {}, [None] * n
        for i in range(n + lag):
            if i < n:
                s[i] = scores(kj, chains[i], masked)
            if lag - 1 <= i < n + lag - 1:
                pm[i - lag + 1] = probs(s.pop(i - lag + 1), carry[i - lag + 1][0])
            if i >= lag:
                p, m_new = pm.pop(i - lag)
                m, acc = carry[i - lag]
                out[i - lag] = (m_new, update(kj, chains[i - lag], p, m, m_new, acc))
        return tuple(out)

    init = tuple((jnp.full((1, ATTN_Q_COLS), NEG_BIG, F32),
                  jnp.zeros((FOX_HEAD_DIM + ATTN_EXTRA_ROWS, ATTN_Q_COLS), F32)) for _ in chains)
    carry = lax.fori_loop(0, qi, lambda kj, cr: step(kj, cr, False), init)
    carry = step(qi, carry, True)
    for (j, c), (_, acc) in zip(chains, carry):
        o_ref[j * FOX_HEAD_DIM:(j + 1) * FOX_HEAD_DIM, c * ATTN_Q_COLS:(c + 1) * ATTN_Q_COLS] = (
            acc[:FOX_HEAD_DIM] / acc[FOX_HEAD_DIM:FOX_HEAD_DIM + 1]).astype(BF16)


def _attn(q, k, vt, blk, heads):
    bsz, seq, _ = q.shape
    nkb = seq // blk
    return pl.pallas_call(
        functools.partial(_attn_kernel, blk=blk, heads=heads),
        grid=(bsz, FOX_HEADS // heads, seq // blk),
        in_specs=[pl.BlockSpec((None, blk, heads * HEAD_PAD), lambda b, h, i: (b, i, h)),
                  pl.BlockSpec((None, seq, heads * HEAD_PAD), lambda b, h, i: (b, 0, h),
                               pipeline_mode=pl.Buffered(1)),
                  pl.BlockSpec((None, nkb, heads * FOX_HEAD_DIM, blk), lambda b, h, i: (b, 0, h, 0),
                               pipeline_mode=pl.Buffered(1))],
        out_specs=pl.BlockSpec((None, heads * FOX_HEAD_DIM, blk), lambda b, h, i: (b, h, i)),
        out_shape=jax.ShapeDtypeStruct((bsz, FOX_HEADS * FOX_HEAD_DIM, seq), BF16),
        compiler_params=_cparams(("arbitrary", "arbitrary", "arbitrary")),
        name="attn",
    )(q, k, vt)


def _mix_kernel(ot_ref, gya_ref, sgb_ref, x_ref, g1_ref, sc2_ref, sh2_ref, lg_ref, lb_ref,
                wfox_ref, wmix_ref, wrh_ref, wrl_ref, wsg_ref, wsu_ref, wsd_ref,
                x1_ref, hp_ref, st_ref, shr_ref, *, alpha, sub):
    tm = x_ref.shape[0]
    groups = [pl.ds(r0, sub) for r0 in range(0, tm, sub)]
    ys = []
    for rows in groups:
        yb = _dot_tn(ot_ref[:, rows], wfox_ref[...])
        merged = gya_ref[rows, :].astype(F32) + sgb_ref[rows, :].astype(F32) * yb
        ys.append(_dot(merged.astype(BF16), wmix_ref[...]))
    for rows, y in zip(groups, ys):
        x1 = _ln(alpha * x_ref[rows, :] + (1.0 + g1_ref[0]) * y) * lg_ref[...] + lb_ref[...]
        x1_ref[rows, :] = x1
        h2 = _ln(x1) * (1.0 + sc2_ref[0]) + sh2_ref[0]
        hb = h2.astype(BF16)
        hl = (h2 - hb.astype(F32)).astype(BF16)
        _store_packed_rows(hp_ref, h2, row0=rows.start)

        logits_t = _dot_nt(wrh_ref[...], hb) + _dot_nt(wrl_ref[...], hb) + _dot_nt(wrh_ref[...], hl)
        st_ref[:, rows] = jax.nn.sigmoid(logits_t)

        g = _dot(hb, wsg_ref[...])
        u = _dot(hb, wsu_ref[...])
        a = (g * jax.nn.sigmoid(g) * u).astype(BF16)
        shr_ref[rows, :] = _dot(a, wsd_ref[...])


def _mix(ot, gya, sgb, x2, gate1, scale2, shift2, lg, lb, wts, tm, seq, alpha):
    t, d = x2.shape
    tps = seq // tm
    fw = ot.shape[1]
    n_exp = wts[2].shape[0]
    mod_spec = pl.BlockSpec((1, 1, d), lambda i: (i // tps, 0, 0))
    row_spec = lambda n: pl.BlockSpec((tm, n), lambda i: (i, 0))
    return pl.pallas_call(
        functools.partial(_mix_kernel, alpha=alpha, sub=min(MIX_ROWS_PER_GROUP, tm)),
        grid=(t // tm,),
        in_specs=[pl.BlockSpec((None, fw, tm), lambda i: (i // tps, 0, i % tps)),
                  row_spec(d), row_spec(d), row_spec(d), mod_spec, mod_spec, mod_spec,
                  _const_spec(lg.shape), _const_spec(lb.shape)] + [_const_spec(w.shape) for w in wts],
        out_specs=[row_spec(d), pl.BlockSpec((tm * ROW_SLABS, V7X_LANES), lambda i: (i, 0)),
                   pl.BlockSpec((n_exp, tm), lambda i: (0, i)), row_spec(d)],
        out_shape=[jax.ShapeDtypeStruct((t, d), F32),
                   jax.ShapeDtypeStruct((t * ROW_SLABS, V7X_LANES), jnp.uint32),
                   jax.ShapeDtypeStruct((n_exp, t), F32),
                   jax.ShapeDtypeStruct((t, d), F32)],
        compiler_params=_cparams(("arbitrary",)),
        name="mix",
    )(ot, gya, sgb, x2, gate1, scale2, shift2, lg, lb, *wts)


def _route_kernel(st_ref, bias_ref, idx_ref, wts_ref, rank_ref, cnt_ref, carry_ref):
    i = pl.program_id(0)
    n_exp, tr = st_ref.shape
    gsz = n_exp // N_GROUPS
    neg_inf = -jnp.inf

    @pl.when(i == 0)
    def _():
        carry_ref[...] = jnp.zeros_like(carry_ref)

    shape3 = (N_GROUPS, gsz, tr)

    def max01(v):
        return jnp.max(jnp.max(v, axis=1, keepdims=True), axis=0, keepdims=True)

    def min01(v):
        return jnp.min(jnp.min(v, axis=1, keepdims=True), axis=0, keepdims=True)

    def sum01(v):
        return jnp.sum(jnp.sum(v, axis=1, keepdims=True), axis=0, keepdims=True)

    sc = st_ref[...].reshape(shape3)
    gsel = (st_ref[...] + bias_ref[...]).reshape(shape3)
    pos = lax.broadcasted_iota(jnp.int32, shape3, 1)
    m1 = jnp.max(gsel, axis=1, keepdims=True)
    i1 = jnp.min(jnp.where(gsel == m1, pos, gsz), axis=1, keepdims=True)
    m2 = jnp.max(jnp.where(pos == i1, neg_inf, gsel), axis=1, keepdims=True)
    gs = m1 + m2

    gid = lax.broadcasted_iota(jnp.int32, gs.shape, 0)
    gkeep = jnp.zeros(gs.shape, F32)
    for _ in range(TOPK_GROUPS):
        mx = jnp.max(gs, axis=0, keepdims=True)
        gi = jnp.min(jnp.where(gs == mx, gid, N_GROUPS), axis=0, keepdims=True)
        hit = gid == gi
        gkeep = gkeep + jnp.where(hit, 1.0, 0.0)
        gs = jnp.where(hit, neg_inf, gs)

    cur = jnp.where(jnp.broadcast_to(gkeep, shape3) > 0.5, gsel, neg_inf)
    eid = lax.broadcasted_iota(jnp.int32, shape3, 0) * gsz + pos
    onehot = jnp.zeros(shape3, F32)
    idxs, ws = [], []
    wsum = jnp.zeros((1, 1, tr), F32)
    for _ in range(TOP_K):
        mx = max01(cur)
        ik = min01(jnp.where(cur == mx, eid, n_exp))
        hit = eid == ik
        wk = sum01(jnp.where(hit, sc, 0.0))
        idxs.append(ik)
        ws.append(wk)
        wsum = wsum + wk
        onehot = onehot + jnp.where(hit, 1.0, 0.0)
        cur = jnp.where(hit, neg_inf, cur)

    ra = lax.broadcasted_iota(jnp.int32, (tr, tr), 0)
    rb = lax.broadcasted_iota(jnp.int32, (tr, tr), 1)
    upper = jnp.where(ra < rb, 1.0, 0.0).astype(BF16)
    onehot2 = onehot.reshape(n_exp, tr)
    prior = (_dot(onehot2.astype(BF16), upper) + carry_ref[...]).reshape(shape3)
    for slot in range(TOP_K):
        idx_ref[slot:slot + 1, :] = idxs[slot].reshape(1, tr)
        wts_ref[slot:slot + 1, :] = (ws[slot] / wsum * ROUTED_SCALE).reshape(1, tr)
        rk = sum01(jnp.where(eid == idxs[slot], prior, 0.0))
        rank_ref[slot:slot + 1, :] = rk.reshape(1, tr).astype(jnp.int32)
    total = carry_ref[...] + jnp.sum(onehot2, axis=1, keepdims=True)
    carry_ref[...] = total
    cnt_ref[...] = jnp.broadcast_to(total, cnt_ref.shape).astype(jnp.int32)


def _route(scores_t, bias_col, tr):
    n_exp, t = scores_t.shape
    slot_spec = pl.BlockSpec((TOP_K, tr), lambda i: (0, i))
    return pl.pallas_call(
        _route_kernel,
        grid=(t // tr,),
        in_specs=[pl.BlockSpec((n_exp, tr), lambda i: (0, i)), _const_spec(bias_col.shape)],
        out_specs=[slot_spec, slot_spec, slot_spec, _const_spec((n_exp, V7X_LANES))],
        out_shape=[jax.ShapeDtypeStruct((TOP_K, t), jnp.int32),
                   jax.ShapeDtypeStruct((TOP_K, t), F32),
                   jax.ShapeDtypeStruct((TOP_K, t), jnp.int32),
                   jax.ShapeDtypeStruct((n_exp, V7X_LANES), jnp.int32)],
        scratch_shapes=[pltpu.VMEM((n_exp, 1), F32)],
        compiler_params=_cparams(("arbitrary",)),
        name="route",
    )(scores_t, bias_col)


def _dest_kernel(pstart_ref, idx_ref, rank_ref, o_ref):
    n_exp = pstart_ref.shape[0]
    tr = idx_ref.shape[1]
    eid = lax.broadcasted_iota(jnp.int32, (n_exp, tr), 0)
    pstart = pstart_ref[...]
    for slot in range(TOP_K):
        hit = eid == idx_ref[slot:slot + 1, :]
        start = jnp.sum(jnp.where(hit, pstart, 0.0), axis=0, keepdims=True).astype(jnp.int32)
        o_ref[slot:slot + 1, :] = (start + rank_ref[slot:slot + 1, :]) * ROW_SLABS


def _dest(pstart, idx, rank, tr):
    k, t = idx.shape
    spec = pl.BlockSpec((k, tr), lambda i: (0, i))
    pstart_col = pstart.astype(F32)[:, None]
    return pl.pallas_call(
        _dest_kernel,
        grid=(t // tr,),
        in_specs=[_const_spec(pstart_col.shape), spec, spec],
        out_specs=spec,
        out_shape=jax.ShapeDtypeStruct((k, t), jnp.int32),
        compiler_params=_cparams(("arbitrary",)),
        name="dest",
    )(pstart_col, idx, rank)


def _dispatch_kernel(dest_ref, hp_ref, xs_ref, sem):
    td = hp_ref.shape[0] // ROW_SLABS

    def row_copy(r, slot):
        dst = pl.multiple_of(dest_ref[r * TOP_K + slot], ROW_SLABS)
        src = pl.multiple_of(r * ROW_SLABS, ROW_SLABS)
        return pltpu.make_async_copy(hp_ref.at[pl.ds(src, ROW_SLABS)], xs_ref.at[pl.ds(dst, ROW_SLABS)], sem)

    def start(r, c):
        for slot in range(TOP_K):
            row_copy(r, slot).start(priority=slot % 2)
        return c

    def wait(r, c):
        for slot in range(TOP_K):
            row_copy(r, slot).wait()
        return c

    lax.fori_loop(0, td, start, 0, unroll=4)
    lax.fori_loop(0, td, wait, 0, unroll=8)


def _dispatch(dest_flat, hp, n_pad, td):
    t = hp.shape[0] // ROW_SLABS
    return pl.pallas_call(
        _dispatch_kernel,
        grid=(t // td,),
        in_specs=[pl.BlockSpec((td * TOP_K,), lambda i: (i,), memory_space=pltpu.SMEM),
                  pl.BlockSpec((td * ROW_SLABS, V7X_LANES), lambda i: (i, 0))],
        out_specs=pl.BlockSpec(memory_space=pl.ANY),
        out_shape=jax.ShapeDtypeStruct((n_pad * ROW_SLABS, V7X_LANES), hp.dtype),
        scratch_shapes=[pltpu.SemaphoreType.DMA(())],
        compiler_params=_cparams(("arbitrary",)),
        name="dispatch",
    )(dest_flat, hp)


def _expert_kernel(be_ref, rows_ref, seq_ref, cnt_ref, xs_hbm, wg_hbm, wu_hbm, wd_hbm, ys_hbm,
                   xbuf, ybuf, wgf, wuf, wdf, wgu_s, wd_s, sem_x, sem_y, sem_w, *, layer):
    n = cnt_ref[0]
    n_seq = cnt_ref[1]
    blk_rows = EXPERT_BLOCK * ROW_SLABS
    nx, nw = EXPERT_X_AHEAD + 1, EXPERT_W_AHEAD + 1

    def block_rows(q):
        start = q * blk_rows
        return pl.ds(start if isinstance(q, int) else pl.multiple_of(start, blk_rows), blk_rows)

    def x_copy(q, slot):
        return pltpu.make_async_copy(xs_hbm.at[block_rows(q)], xbuf.at[slot], sem_x.at[slot])

    def y_copy(q, slot):
        return pltpu.make_async_copy(ybuf.at[slot], ys_hbm.at[block_rows(q)], sem_y.at[slot])

    def w_copies(k, ws):
        e = seq_ref[k]
        out = []
        for hbm, buf in ((wg_hbm, wgf), (wu_hbm, wuf), (wd_hbm, wdf)):
            half = buf.shape[1] // 2
            for part in range(2):
                rows = pl.ds(part * half, half)
                out.append((pltpu.make_async_copy(hbm.at[layer, e, rows], buf.at[ws, rows], sem_w.at[ws]), part))
        return out

    for j in range(EXPERT_X_AHEAD):
        @pl.when(j < n)
        def _():
            x_copy(j, j).start(priority=1)

    for j in range(EXPERT_W_AHEAD):
        @pl.when(j < n_seq)
        def _():
            for cp, prio in w_copies(j, j):
                cp.start(priority=prio)

    def body(q, k):
        slot = q % 2
        fresh = jnp.logical_or(q == 0, be_ref[q] != be_ref[jnp.maximum(q - 1, 0)])

        @pl.when(fresh)
        def _():
            ws = k % nw
            for cp, _ in w_copies(k, ws):
                cp.wait()
            fw = wd_s.shape[0]
            wgu_s[:, :fw] = wgf[ws].astype(BF16)
            wgu_s[:, fw:] = wuf[ws].astype(BF16)
            wd_s[...] = wdf[ws].astype(BF16)

            @pl.when(k + EXPERT_W_AHEAD < n_seq)
            def _():
                for cp, prio in w_copies(k + EXPERT_W_AHEAD, (k + EXPERT_W_AHEAD) % nw):
                    cp.start(priority=prio)

        @pl.when(q + EXPERT_X_AHEAD < n)
        def _():
            x_copy(q + EXPERT_X_AHEAD, (q + EXPERT_X_AHEAD) % nx).start(priority=1)

        x_copy(q, q % nx).wait()

        @pl.when(q >= 2)
        def _():
            y_copy(q - 2, slot).wait()

        half = EXPERT_BLOCK // 2
        f = wd_s.shape[0]
        gu = []
        for h in range(2):
            x = jnp.concatenate(_load_packed_rows(xbuf, half, lead=(q % nx,), row0=h * half), axis=1)
            valid = lax.broadcasted_iota(jnp.int32, x.shape, 0) + h * half < rows_ref[q]
            x = jnp.where(valid, x, 0.0).astype(BF16)
            gu.append(_dot(x, wgu_s[...]))
        for h in range(2):
            g, u = gu[h][:, :f], gu[h][:, f:]
            a = (g * jax.nn.sigmoid(g) * u).astype(BF16)
            _store_packed_rows(ybuf.at[slot], _dot(a, wd_s[...]), row0=h * half)
        y_copy(q, slot).start()
        return k + fresh.astype(jnp.int32)

    lax.fori_loop(0, n, body, jnp.int32(0))

    @pl.when(n >= 2)
    def _():
        y_copy(n - 2, n % 2).wait()

    y_copy(n - 1, (n - 1) % 2).wait()


def _experts(block_expert, block_rows, expert_seq, counts2, xs, wg, wu, wd, layer):
    _, n_exp, d, f = wg.shape
    blk = (EXPERT_BLOCK * ROW_SLABS, V7X_LANES)
    nx, nw = EXPERT_X_AHEAD + 1, EXPERT_W_AHEAD + 1
    any_spec = pl.BlockSpec(memory_space=pl.ANY)
    return pl.pallas_call(
        functools.partial(_expert_kernel, layer=layer),
        grid_spec=pltpu.PrefetchScalarGridSpec(
            num_scalar_prefetch=4, grid=(1,),
            in_specs=[any_spec, any_spec, any_spec, any_spec],
            out_specs=any_spec,
            scratch_shapes=[pltpu.VMEM((nx,) + blk, xs.dtype), pltpu.VMEM((2,) + blk, xs.dtype),
                            pltpu.VMEM((nw, d, f), F32), pltpu.VMEM((nw, d, f), F32), pltpu.VMEM((nw, f, d), F32),
                            pltpu.VMEM((d, 2 * f), BF16), pltpu.VMEM((f, d), BF16),
                            pltpu.SemaphoreType.DMA((nx,)), pltpu.SemaphoreType.DMA((2,)),
                            pltpu.SemaphoreType.DMA((nw,))]),
        out_shape=jax.ShapeDtypeStruct(xs.shape, xs.dtype),
        compiler_params=_cparams(("arbitrary",)),
        name="experts",
    )(block_expert, block_rows, expert_seq, counts2, xs, wg, wu, wd)


def _combine_kernel(d0_ref, d1_ref, d2_ref, ys_ref, w_ref, shr_ref, x1_ref, g2_ref, lg_ref, lb_ref, o_ref,
                    buf_a, buf_b, sem, *, alpha, tc):
    i = pl.program_id(0)
    last = pl.num_programs(0) - 1

    def row_copy(dref, r, slot, buf, sem_idx):
        src = pl.multiple_of(dref[r * TOP_K + slot], ROW_SLABS)
        dst = r * ROW_SLABS if isinstance(r, int) else pl.multiple_of(r * ROW_SLABS, ROW_SLABS)
        return pltpu.make_async_copy(ys_ref.at[pl.ds(src, ROW_SLABS)], buf.at[slot, pl.ds(dst, ROW_SLABS)],
                                     sem.at[sem_idx])

    def issue_unrolled(dref, buf, sem_idx):
        for r in range(tc):
            for slot in range(TOP_K):
                row_copy(dref, r, slot, buf, sem_idx).start(priority=slot % 2)

    def issue_loop(dref, buf, sem_idx):
        def start(r, c):
            for slot in range(TOP_K):
                row_copy(dref, r, slot, buf, sem_idx).start(priority=slot % 2)
            return c
        lax.fori_loop(0, tc, start, 0, unroll=4)

    def wait_all(dref, buf, sem_idx):
        def wait(r, c):
            for slot in range(TOP_K):
                row_copy(dref, r, slot, buf, sem_idx).wait()
            return c
        lax.fori_loop(0, tc, wait, 0, unroll=8)

    def reduce_tile(buf, rows):
        w = w_ref[rows, :]
        chunks = None
        for slot in range(TOP_K):
            wk = w[:, slot:slot + 1]
            part = [c * wk for c in _load_packed_rows(buf, tc, lead=(slot,))]
            chunks = part if chunks is None else [a + b for a, b in zip(chunks, part)]
        y = shr_ref[rows, :] + jnp.concatenate(chunks, axis=1)
        z = alpha * x1_ref[rows, :] + (1.0 + g2_ref[0]) * y
        o_ref[rows, :] = _ln(z) * lg_ref[...] + lb_ref[...]

    @pl.when(i == 0)
    def _():
        issue_loop(d0_ref, buf_a, 0)

    wait_all(d0_ref, buf_a, 0)
    issue_unrolled(d1_ref, buf_b, 1)
    reduce_tile(buf_a, pl.ds(0, tc))

    wait_all(d1_ref, buf_b, 1)
    issue_unrolled(d2_ref, buf_a, 0)
    reduce_tile(buf_b, pl.ds(tc, tc))

    @pl.when(i == last)
    def _():
        wait_all(d2_ref, buf_a, 0)


def _combine(dest_flat, ys, wts_tk, shared, x1, gate2, lg, lb, tc, seq, alpha):
    t, d = x1.shape
    tps = seq // (2 * tc)
    n_tiles = t // tc
    row_spec = lambda n: pl.BlockSpec((2 * tc, n), lambda i: (i, 0))
    tile_dest = lambda f: pl.BlockSpec((tc * TOP_K,), f, memory_space=pltpu.SMEM)
    buf = pltpu.VMEM((TOP_K, tc * ROW_SLABS, V7X_LANES), jnp.uint32)
    return pl.pallas_call(
        functools.partial(_combine_kernel, alpha=alpha, tc=tc),
        grid=(n_tiles // 2,),
        in_specs=[tile_dest(lambda i: (2 * i,)), tile_dest(lambda i: (2 * i + 1,)),
                  tile_dest(lambda i: (jnp.minimum(2 * i + 2, n_tiles - 1),)),
                  pl.BlockSpec(memory_space=pl.ANY),
                  row_spec(TOP_K), row_spec(d), row_spec(d),
                  pl.BlockSpec((1, 1, d), lambda i: (i // tps, 0, 0)),
                  _const_spec(lg.shape), _const_spec(lb.shape)],
        out_specs=row_spec(d),
        out_shape=jax.ShapeDtypeStruct((t, d), F32),
        scratch_shapes=[buf, buf, pltpu.SemaphoreType.DMA((2,))],
        compiler_params=_cparams(("arbitrary",)),
        name="combine",
    )(dest_flat, dest_flat, dest_flat, ys, wts_tk, shared, x1, gate2, lg, lb)


def _placement():
    pq = np.zeros((V7X_LANES, FOX_HEADS * HEAD_PAD), np.float32)
    pk = np.zeros((V7X_LANES, FOX_HEADS * HEAD_PAD), np.float32)
    for h in range(FOX_HEADS):
        base = h * HEAD_PAD + FOX_HEAD_DIM
        for piece in range(3):
            pq[piece * 8 + h, base + piece] = 1.0
            pk[24, base + piece] = 1.0
            pq[24, base + 3 + piece] = 1.0
            pk[piece * 8 + h, base + 3 + piece] = -1.0
    return jnp.asarray(pq, BF16), jnp.asarray(pk, BF16)


def _inproj_weights(w_in, b_forget, d):
    conv2 = d
    fw = FOX_HEADS * FOX_HEAD_DIM
    o1, o2, o3, o4 = conv2, conv2 + fw, conv2 + 2 * fw, conv2 + 3 * fw
    o5 = o4 + FOX_HEADS
    o6 = o5 + d

    def pad_heads(w):
        w = w.reshape(d, FOX_HEADS, FOX_HEAD_DIM)
        w = jnp.pad(w, ((0, 0), (0, 0), (0, HEAD_PAD - FOX_HEAD_DIM)))
        return w.reshape(d, FOX_HEADS * HEAD_PAD).astype(BF16)

    wglu = w_in[:, :o1].astype(BF16)
    wq = pad_heads(w_in[:, o1:o2])
    wk = pad_heads(w_in[:, o2:o3])
    wvt = w_in[:, o3:o4].T.astype(BF16)
    wf8 = w_in[:, o4:o5]
    wf = jnp.pad(jnp.concatenate([wf8, wf8, wf8], axis=1), ((0, 0), (0, V7X_LANES - 3 * FOX_HEADS))).astype(BF16)
    bf = jnp.pad(jnp.concatenate([b_forget, b_forget, b_forget]), (0, V7X_LANES - 3 * FOX_HEADS))[None, :].astype(F32)
    wga = w_in[:, o5:o6].astype(BF16)
    wgb = w_in[:, o6:].astype(BF16)
    pq, pk = _placement()
    return (wglu, wq, wk, wvt, wf, wga, wgb, bf, pq, pk)


def _layer(x2, ada, bsz, seq, w_in, b_forget, conv_w, conv_b, conv_ln_g, conv_ln_b, w_conv_out, w_fox_out,
           w_mix_out, ln1_g, ln1_b, w_router, router_bias, w_exp_gate, w_exp_up, w_exp_down,
           w_sh_gate, w_sh_up, w_sh_down, ln2_g, ln2_b, depth, layer):
    t, d = x2.shape
    n_exp = w_router.shape[1]
    alpha = (2.0 * depth) ** 0.25
    mods = [ada[:bsz, j * d:(j + 1) * d][:, None, :] for j in range(6)]
    shift1, scale1, gate1, shift2, scale2, gate2 = mods

    tm = min(512, seq)
    u, q, k, vt, sga, sgb = _inproj(x2, scale1, shift1, _inproj_weights(w_in, b_forget, d), tm, seq)

    tcv = min(256, seq)
    conv_w_pad = jnp.pad(conv_w, ((0, CONV_HALO - CONV_WIDTH), (0, 0)))
    gya = _conv(u, conv_w_pad, conv_b[None, :], conv_ln_g[None, :], conv_ln_b[None, :],
                w_conv_out.astype(BF16), sga, tcv, seq)

    ot = _attn(q.reshape(bsz, seq, -1), k.reshape(bsz, seq, -1), vt, tm, ATTN_HEADS_PER_STEP)

    tmx = min(512, seq)
    wr_t = w_router.T
    wr_h = wr_t.astype(BF16)
    wr_l = (wr_t - wr_h.astype(F32)).astype(BF16)
    mix_w = (w_fox_out.astype(BF16), w_mix_out.astype(BF16), wr_h, wr_l,
             w_sh_gate.astype(BF16), w_sh_up.astype(BF16), w_sh_down.astype(BF16))
    x1, hp, scores_t, shared = _mix(ot, gya, sgb, x2, gate1, scale2, shift2, ln1_g[None, :], ln1_b[None, :],
                                    mix_w, tmx, seq, alpha)

    tr = min(512, t)
    idx, wts, rank, cnt = _route(scores_t, router_bias[:, None], tr)

    counts = cnt[:, 0]
    padded = (counts + EXPERT_BLOCK - 1) // EXPERT_BLOCK * EXPERT_BLOCK
    pend = jnp.cumsum(padded)
    pstart = (pend - padded).astype(jnp.int32)
    n_assign = t * TOP_K
    n_pad = -(-(n_assign + n_exp * (EXPERT_BLOCK - 1)) // EXPERT_BLOCK) * EXPERT_BLOCK
    n_blocks = n_pad // EXPERT_BLOCK
    block_start = jnp.arange(n_blocks, dtype=jnp.int32) * EXPERT_BLOCK
    block_expert = jnp.minimum(jnp.sum(pend[None, :] <= block_start[:, None], axis=1), n_exp - 1).astype(jnp.int32)
    block_rows = jnp.clip((pstart + counts)[block_expert] - block_start, 0, EXPERT_BLOCK).astype(jnp.int32)
    n_used = (pend[-1:] // EXPERT_BLOCK).astype(jnp.int32)
    owns = counts > 0
    expert_seq = jnp.nonzero(owns, size=n_exp, fill_value=0)[0].astype(jnp.int32)
    counts2 = jnp.concatenate([n_used, jnp.sum(owns, dtype=jnp.int32)[None]])

    dest = _dest(pstart, idx, rank, tr)
    dest_flat = dest.T.reshape(-1)
    xs = _dispatch(dest_flat, hp, n_pad, min(256, t))
    ys = _experts(block_expert, block_rows, expert_seq, counts2, xs, w_exp_gate, w_exp_up, w_exp_down, layer)
    return _combine(dest_flat, ys, wts.T, shared, x1, gate2, ln2_g[None, :], ln2_b[None, :],
                    min(128, seq), seq, alpha)


def kernel(x, c, w_ada, b_ada, w_in, b_forget, conv_w, conv_b, conv_ln_g, conv_ln_b, w_conv_out, w_fox_out,
           w_mix_out, ln1_g, ln1_b, w_router, router_bias, w_exp_gate, w_exp_up, w_exp_down, w_sh_gate,
           w_sh_up, w_sh_down, ln2_g, ln2_b):
    bsz, seq, d = x.shape
    depth = w_ada.shape[0]
    c_pad = jnp.pad(c, ((0, 8 - bsz), (0, 0)))
    x2 = x.reshape(bsz * seq, d)
    for l in range(depth):
        ada = _ada(c_pad, w_ada[l], b_ada[l][None, :])
        x2 = _layer(x2, ada, bsz, seq, w_in[l], b_forget[l], conv_w[l], conv_b[l], conv_ln_g[l], conv_ln_b[l],
                    w_conv_out[l], w_fox_out[l], w_mix_out[l], ln1_g[l], ln1_b[l], w_router[l], router_bias[l],
                    w_exp_gate, w_exp_up, w_exp_down, w_sh_gate[l], w_sh_up[l], w_sh_down[l],
                    ln2_g[l], ln2_b[l], depth, l)
    return x2.reshape(bsz, seq, d)
```

```python
import functools

import jax
import jax.numpy as jnp
import numpy as np
from jax import lax
from jax.experimental import pallas as pl
from jax.experimental.pallas import tpu as pltpu

F32 = jnp.float32
BF16 = jnp.bfloat16

LN_EPS = 1e-5
CONV_WIDTH = 31
FOX_HEADS = 8
FOX_HEAD_DIM = 64
N_GROUPS = 8
TOPK_GROUPS = 4
TOP_K = 8
ROUTED_SCALE = 2.5
EXPERT_BLOCK = 256
EXPERT_X_AHEAD = 4
EXPERT_W_AHEAD = 4
ROW_SLABS = 4

V7X_LANES = 128
HEAD_PAD = 128
CONV_HALO = 32
VMEM_LIMIT = 56 * 1024 * 1024
NEG_BIG = -1e30
LOG2E = 1.4426950408889634
ATTN_EXTRA_ROWS = 16
ATTN_HEADS_PER_STEP = 8
ATTN_Q_COLS = 256
MIX_ROWS_PER_GROUP = 256
ATTN_PIPE_LAG = 3


def _cparams(sem):
    return pltpu.CompilerParams(dimension_semantics=sem, vmem_limit_bytes=VMEM_LIMIT)


def _ln(v):
    mu = jnp.mean(v, axis=-1, keepdims=True)
    vc = v - mu
    var = jnp.mean(vc * vc, axis=-1, keepdims=True)
    return vc * lax.rsqrt(var + LN_EPS)


def _split3(v):
    hi = v.astype(BF16)
    r1 = v - hi.astype(F32)
    mid = r1.astype(BF16)
    lo = (r1 - mid.astype(F32)).astype(BF16)
    return hi, mid, lo


def _dot(a, b):
    return jnp.dot(a, b, preferred_element_type=F32)


def _dot_nt(a, b):
    return lax.dot_general(a, b, (((1,), (1,)), ((), ())), preferred_element_type=F32)


def _dot_tn(a, b):
    return lax.dot_general(a, b, (((0,), (0,)), ((), ())), preferred_element_type=F32)


def _store_packed_rows(ref, v, row0=0):
    n, d = v.shape
    half = d // 2
    vb = v.astype(BF16).astype(F32)
    lo_bits = lax.bitcast_convert_type(vb[:, :half], jnp.uint32)
    hi_bits = lax.bitcast_convert_type(vb[:, half:], jnp.uint32)
    words = (lo_bits >> 16) | (hi_bits & jnp.uint32(0xFFFF0000))
    for c in range(ROW_SLABS):
        ref[pl.ds(row0 * ROW_SLABS + c, n, stride=ROW_SLABS), :] = words[:, c * V7X_LANES:(c + 1) * V7X_LANES]


def _load_packed_rows(ref, n, lead=(), row0=0):
    lo, hi = [], []
    for c in range(ROW_SLABS):
        w = ref[lead + (pl.ds(row0 * ROW_SLABS + c, n, stride=ROW_SLABS), slice(None))]
        lo.append(lax.bitcast_convert_type(w << 16, F32))
        hi.append(lax.bitcast_convert_type(w & jnp.uint32(0xFFFF0000), F32))
    return lo + hi


def _const_spec(shape, single=False):
    nd = len(shape)
    if single:
        return pl.BlockSpec(shape, lambda *_: (0,) * nd, pipeline_mode=pl.Buffered(1))
    return pl.BlockSpec(shape, lambda *_: (0,) * nd)


def _ada_kernel(c_ref, w_ref, b_ref, o_ref):
    c = c_ref[...]
    cond = c * jax.nn.sigmoid(c)
    ch, cm, _ = _split3(cond)
    w = w_ref[...]
    wh, wm, _ = _split3(w)
    o_ref[...] = _dot(ch, wh) + _dot(ch, wm) + _dot(cm, wh) + b_ref[...]


def _ada(c_pad, w, b):
    rows, d = c_pad.shape
    n = w.shape[1]
    tn = 1024
    return pl.pallas_call(
        _ada_kernel,
        grid=(n // tn,),
        in_specs=[_const_spec((rows, d)),
                  pl.BlockSpec((d, tn), lambda j: (0, j)),
                  pl.BlockSpec((1, tn), lambda j: (0, j))],
        out_specs=pl.BlockSpec((rows, tn), lambda j: (0, j)),
        out_shape=jax.ShapeDtypeStruct((rows, n), F32),
        compiler_params=_cparams(("arbitrary",)),
        name="ada",
    )(c_pad, w, b)


def _inproj_kernel(x_ref, sc_ref, sh_ref, wglu_ref, wq_ref, wk_ref, wvt_ref, wf_ref, wga_ref, wgb_ref,
                   bf_ref, pq_ref, pk_ref, cw_ref, cb_ref, cg_ref, cbe_ref, wco_ref,
                   q_ref, k_ref, vt_ref, gya_ref, sgb_ref, carry_ref, halo_ref, ext_ref, shift_ref,
                   *, tiles_per_seq, conv_ch, chunk):
    i = pl.program_id(0)
    tm = x_ref.shape[0]

    @pl.when(i % tiles_per_seq == 0)
    def _():
        carry_ref[...] = jnp.zeros_like(carry_ref)
        halo_ref[...] = jnp.zeros_like(halo_ref)

    h = _ln(x_ref[...]) * (1.0 + sc_ref[0]) + sh_ref[0]
    hb = h.astype(BF16)

    glu = _dot(hb, wglu_ref[...])
    u = glu[:, :conv_ch] * jax.nn.sigmoid(glu[:, conv_ch:])

    ext_ref[0:CONV_HALO, :] = halo_ref[...]
    ext_ref[CONV_HALO:, :] = u
    halo_ref[...] = u[tm - CONV_HALO:, :]
    cw = cw_ref[...]
    off = CONV_HALO - (CONV_WIDTH - 1)
    span = tm + CONV_HALO - 8
    for res in range(1, 8):
        shift_ref[res - 1] = ext_ref[res:res + span, :]
    def conv_chunk(c0):
        acc = jnp.zeros((chunk, conv_ch), F32)
        for j in range(CONV_WIDTH):
            res, lo = (off + j) % 8, c0 + (off + j) // 8 * 8
            rows = ext_ref[lo:lo + chunk, :] if res == 0 else shift_ref[res - 1, lo:lo + chunk, :]
            acc = acc + cw[j:j + 1, :] * rows
        return acc

    n_chunks = tm // chunk
    per_gap = -(-n_chunks // 4)
    outs = []

    def conv_chunks():
        for _ in range(per_gap):
            if len(outs) < n_chunks:
                outs.append(conv_chunk(len(outs) * chunk))

    sgb_ref[...] = jax.nn.sigmoid(_dot(hb, wgb_ref[...])).astype(BF16)
    conv_chunks()

    f = _dot(hb, wf_ref[...]) + bf_ref[...]
    logf = jnp.minimum(f, 0.0) - jnp.log(1.0 + jnp.exp(-jnp.abs(f)))
    lh, lm, ll = _split3(logf)
    row = lax.broadcasted_iota(jnp.int32, (tm, tm), 0)
    col = lax.broadcasted_iota(jnp.int32, (tm, tm), 1)
    tri = jnp.where(row >= col, 1.0, 0.0).astype(BF16)
    cs = _dot(tri, lh) + _dot(tri, lm) + _dot(tri, ll)
    cum = cs + carry_ref[...]
    carry_ref[...] = cum[tm - 1:tm, :]
    conv_chunks()

    ch, cm, cl = _split3(cum * LOG2E)
    lane = lax.broadcasted_iota(jnp.int32, cum.shape, 1)
    tail = jnp.where(lane == 24, 1.0, 0.0)
    pieces = jnp.where(lane < 8, ch.astype(F32), jnp.where(lane < 16, cm.astype(F32),
                       jnp.where(lane < 24, cl.astype(F32), tail))).astype(BF16)
    scale = FOX_HEAD_DIM ** -0.5 * LOG2E
    q_ref[...] = (_dot(hb, wq_ref[...]) * scale + _dot(pieces, pq_ref[...])).astype(BF16)
    conv_chunks()
    k_ref[...] = (_dot(hb, wk_ref[...]) + _dot(pieces, pk_ref[...])).astype(BF16)
    conv_chunks()
    vt_ref[0, 0] = _dot_nt(wvt_ref[...], hb).astype(BF16)

    v = jnp.concatenate(outs, axis=0) + cb_ref[...]
    v = _ln(v) * cg_ref[...] + cbe_ref[...]
    v = v * jax.nn.sigmoid(v)
    sga = jax.nn.sigmoid(_dot(hb, wga_ref[...]))
    gya_ref[...] = (sga * _dot(v.astype(BF16), wco_ref[...])).astype(BF16)


def _inproj(x2, scale1, shift1, wts, conv_wts, tm, seq):
    t, d = x2.shape
    tps = seq // tm
    bsz = t // seq
    wglu, wq, wk, wvt, wf, wga, wgb, bf, pq, pk = wts
    conv_ch = wglu.shape[1] // 2
    fw = wvt.shape[0]
    qw = wq.shape[1]
    mod_spec = pl.BlockSpec((1, 1, d), lambda i: (i // tps, 0, 0))
    row_spec = lambda n: pl.BlockSpec((tm, n), lambda i: (i, 0))
    consts = list(wts) + list(conv_wts)
    return pl.pallas_call(
        functools.partial(_inproj_kernel, tiles_per_seq=tps, conv_ch=conv_ch, chunk=64),
        grid=(t // tm,),
        in_specs=[row_spec(d), mod_spec, mod_spec] + [_const_spec(w.shape, single=True) for w in consts],
        out_specs=[row_spec(qw), row_spec(qw),
                   pl.BlockSpec((1, 1, fw, tm), lambda i: (i // tps, i % tps, 0, 0)),
                   row_spec(d), row_spec(d)],
        out_shape=[jax.ShapeDtypeStruct((t, qw), BF16),
                   jax.ShapeDtypeStruct((t, qw), BF16),
                   jax.ShapeDtypeStruct((bsz, tps, fw, tm), BF16),
                   jax.ShapeDtypeStruct((t, d), BF16),
                   jax.ShapeDtypeStruct((t, d), BF16)],
        scratch_shapes=[pltpu.VMEM((1, V7X_LANES), F32), pltpu.VMEM((CONV_HALO, conv_ch), F32),
                        pltpu.VMEM((tm + CONV_HALO, conv_ch), F32),
                        pltpu.VMEM((7, tm + CONV_HALO - 8, conv_ch), F32)],
        compiler_params=_cparams(("arbitrary",)),
        name="inproj",
    )(x2, scale1, shift1, *consts)


def _attn_kernel(q_ref, k_ref, vt_ref, o_ref, *, blk, heads):
    qi = pl.program_id(2)
    row = lax.broadcasted_iota(jnp.int32, (ATTN_EXTRA_ROWS, blk), 0)
    ones_rows = jnp.where(row == 0, 1.0, 0.0).astype(BF16)

    ncol = blk // ATTN_Q_COLS
    chains = [(j, c) for j in range(heads) for c in range(ncol)]

    def scores(kj, chain, masked):
        j, c = chain
        k = k_ref[pl.ds(pl.multiple_of(kj * blk, blk), blk), j * HEAD_PAD:(j + 1) * HEAD_PAD]
        q = q_ref[c * ATTN_Q_COLS:(c + 1) * ATTN_Q_COLS, j * HEAD_PAD:(j + 1) * HEAD_PAD]
        s = _dot_nt(k, q)
        if masked:
            kpos = lax.broadcasted_iota(jnp.int32, s.shape, 0)
            qpos = lax.broadcasted_iota(jnp.int32, s.shape, 1) + c * ATTN_Q_COLS
            s = jnp.where(kpos <= qpos, s, NEG_BIG)
        return s

    def probs(s, m):
        m_new = jnp.maximum(m, jnp.max(s, axis=0, keepdims=True))
        return jnp.exp2(s - m_new).astype(BF16), m_new

    def update(kj, chain, p, m, m_new, acc):
        j, _ = chain
        vt = vt_ref[kj, j * FOX_HEAD_DIM:(j + 1) * FOX_HEAD_DIM, :]
        lhs = jnp.concatenate([vt, ones_rows], axis=0)
        return jnp.exp2(m - m_new) * acc + _dot(lhs, p)

    def step(kj, carry, masked):
        n, lag = len(chains), ATTN_PIPE_LAG
        s, pm, out = {}, {}, [None] * n
        for i in range(n + lag):
            if i < n:
                s[i] = scores(kj, chains[i], masked)
            if lag - 1 <= i < n + lag - 1:
                pm[i - lag + 1] = probs(s.pop(i - lag + 1), carry[i - lag + 1][0])
            if i >= lag:
                p, m_new = pm.pop(i - lag)
                m, acc = carry[i - lag]
                out[i - lag] = (m_new, update(kj, chains[i - lag], p, m, m_new, acc))
        return tuple(out)

    init = tuple((jnp.full((1, ATTN_Q_COLS), NEG_BIG, F32),
                  jnp.zeros((FOX_HEAD_DIM + ATTN_EXTRA_ROWS, ATTN_Q_COLS), F32)) for _ in chains)
    carry = lax.fori_loop(0, qi, lambda kj, cr: step(kj, cr, False), init)
    carry = step(qi, carry, True)
    for (j, c), (_, acc) in zip(chains, carry):
        o_ref[j * FOX_HEAD_DIM:(j + 1) * FOX_HEAD_DIM, c * ATTN_Q_COLS:(c + 1) * ATTN_Q_COLS] = (
            acc[:FOX_HEAD_DIM] / acc[FOX_HEAD_DIM:FOX_HEAD_DIM + 1]).astype(BF16)


def _attn(q, k, vt, blk, heads):
    bsz, seq, _ = q.shape
    nkb = seq // blk
    return pl.pallas_call(
        functools.partial(_attn_kernel, blk=blk, heads=heads),
        grid=(bsz, FOX_HEADS // heads, seq // blk),
        in_specs=[pl.BlockSpec((None, blk, heads * HEAD_PAD), lambda b, h, i: (b, i, h)),
                  pl.BlockSpec((None, seq, heads * HEAD_PAD), lambda b, h, i: (b, 0, h),
                               pipeline_mode=pl.Buffered(1)),
                  pl.BlockSpec((None, nkb, heads * FOX_HEAD_DIM, blk), lambda b, h, i: (b, 0, h, 0),
                               pipeline_mode=pl.Buffered(1))],
        out_specs=pl.BlockSpec((None, heads * FOX_HEAD_DIM, blk), lambda b, h, i: (b, h, i)),
        out_shape=jax.ShapeDtypeStruct((bsz, FOX_HEADS * FOX_HEAD_DIM, seq), BF16),
        compiler_params=_cparams(("arbitrary", "arbitrary", "arbitrary")),
        name="attn",
    )(q, k, vt)


def _mix_kernel(ot_ref, gya_ref, sgb_ref, x_ref, g1_ref, sc2_ref, sh2_ref, lg_ref, lb_ref,
                wfox_ref, wmix_ref, wrh_ref, wrl_ref, wsg_ref, wsu_ref, wsd_ref,
                x1_ref, hp_ref, st_ref, shr_ref, *, alpha, sub):
    tm = x_ref.shape[0]
    groups = [pl.ds(r0, sub) for r0 in range(0, tm, sub)]
    ys = []
    for rows in groups:
        yb = _dot_tn(ot_ref[:, rows], wfox_ref[...])
        merged = gya_ref[rows, :].astype(F32) + sgb_ref[rows, :].astype(F32) * yb
        ys.append(_dot(merged.astype(BF16), wmix_ref[...]))
    for rows, y in zip(groups, ys):
        x1 = _ln(alpha * x_ref[rows, :] + (1.0 + g1_ref[0]) * y) * lg_ref[...] + lb_ref[...]
        x1_ref[rows, :] = x1
        h2 = _ln(x1) * (1.0 + sc2_ref[0]) + sh2_ref[0]
        hb = h2.astype(BF16)
        hl = (h2 - hb.astype(F32)).astype(BF16)
        _store_packed_rows(hp_ref, h2, row0=rows.start)

        logits_t = _dot_nt(wrh_ref[...], hb) + _dot_nt(wrl_ref[...], hb) + _dot_nt(wrh_ref[...], hl)
        st_ref[:, rows] = jax.nn.sigmoid(logits_t)

        g = _dot(hb, wsg_ref[...])
        u = _dot(hb, wsu_ref[...])
        a = (g * jax.nn.sigmoid(g) * u).astype(BF16)
        shr_ref[rows, :] = _dot(a, wsd_ref[...])


def _mix(ot, gya, sgb, x2, gate1, scale2, shift2, lg, lb, wts, tm, seq, alpha):
    t, d = x2.shape
    tps = seq // tm
    fw = ot.shape[1]
    n_exp = wts[2].shape[0]
    mod_spec = pl.BlockSpec((1, 1, d), lambda i: (i // tps, 0, 0))
    row_spec = lambda n: pl.BlockSpec((tm, n), lambda i: (i, 0))
    return pl.pallas_call(
        functools.partial(_mix_kernel, alpha=alpha, sub=min(MIX_ROWS_PER_GROUP, tm)),
        grid=(t // tm,),
        in_specs=[pl.BlockSpec((None, fw, tm), lambda i: (i // tps, 0, i % tps)),
                  row_spec(d), row_spec(d), row_spec(d), mod_spec, mod_spec, mod_spec,
                  _const_spec(lg.shape), _const_spec(lb.shape)] + [_const_spec(w.shape) for w in wts],
        out_specs=[row_spec(d), pl.BlockSpec((tm * ROW_SLABS, V7X_LANES), lambda i: (i, 0)),
                   pl.BlockSpec((n_exp, tm), lambda i: (0, i)), row_spec(d)],
        out_shape=[jax.ShapeDtypeStruct((t, d), F32),
                   jax.ShapeDtypeStruct((t * ROW_SLABS, V7X_LANES), jnp.uint32),
                   jax.ShapeDtypeStruct((n_exp, t), F32),
                   jax.ShapeDtypeStruct((t, d), F32)],
        compiler_params=_cparams(("arbitrary",)),
        name="mix",
    )(ot, gya, sgb, x2, gate1, scale2, shift2, lg, lb, *wts)


def _route_kernel(st_ref, bias_ref, idx_ref, wts_ref, rank_ref, cnt_ref, carry_ref):
    i = pl.program_id(0)
    n_exp, tr = st_ref.shape
    gsz = n_exp // N_GROUPS
    neg_inf = -jnp.inf

    @pl.when(i == 0)
    def _():
        carry_ref[...] = jnp.zeros_like(carry_ref)

    shape3 = (N_GROUPS, gsz, tr)

    def max01(v):
        return jnp.max(jnp.max(v, axis=1, keepdims=True), axis=0, keepdims=True)

    def min01(v):
        return jnp.min(jnp.min(v, axis=1, keepdims=True), axis=0, keepdims=True)

    def sum01(v):
        return jnp.sum(jnp.sum(v, axis=1, keepdims=True), axis=0, keepdims=True)

    sc = st_ref[...].reshape(shape3)
    gsel = (st_ref[...] + bias_ref[...]).reshape(shape3)
    pos = lax.broadcasted_iota(jnp.int32, shape3, 1)
    m1 = jnp.max(gsel, axis=1, keepdims=True)
    i1 = jnp.min(jnp.where(gsel == m1, pos, gsz), axis=1, keepdims=True)
    m2 = jnp.max(jnp.where(pos == i1, neg_inf, gsel), axis=1, keepdims=True)
    gs = m1 + m2

    gid = lax.broadcasted_iota(jnp.int32, gs.shape, 0)
    gkeep = jnp.zeros(gs.shape, F32)
    for _ in range(TOPK_GROUPS):
        mx = jnp.max(gs, axis=0, keepdims=True)
        gi = jnp.min(jnp.where(gs == mx, gid, N_GROUPS), axis=0, keepdims=True)
        hit = gid == gi
        gkeep = gkeep + jnp.where(hit, 1.0, 0.0)
        gs = jnp.where(hit, neg_inf, gs)

    cur = jnp.where(jnp.broadcast_to(gkeep, shape3) > 0.5, gsel, neg_inf)
    eid = lax.broadcasted_iota(jnp.int32, shape3, 0) * gsz + pos
    onehot = jnp.zeros(shape3, F32)
    idxs, ws = [], []
    wsum = jnp.zeros((1, 1, tr), F32)
    for _ in range(TOP_K):
        mx = max01(cur)
        ik = min01(jnp.where(cur == mx, eid, n_exp))
        hit = eid == ik
        wk = sum01(jnp.where(hit, sc, 0.0))
        idxs.append(ik)
        ws.append(wk)
        wsum = wsum + wk
        onehot = onehot + jnp.where(hit, 1.0, 0.0)
        cur = jnp.where(hit, neg_inf, cur)

    ra = lax.broadcasted_iota(jnp.int32, (tr, tr), 0)
    rb = lax.broadcasted_iota(jnp.int32, (tr, tr), 1)
    upper = jnp.where(ra < rb, 1.0, 0.0).astype(BF16)
    onehot2 = onehot.reshape(n_exp, tr)
    prior = (_dot(onehot2.astype(BF16), upper) + carry_ref[...]).reshape(shape3)
    for slot in range(TOP_K):
        idx_ref[slot:slot + 1, :] = idxs[slot].reshape(1, tr)
        wts_ref[slot:slot + 1, :] = (ws[slot] / wsum * ROUTED_SCALE).reshape(1, tr)
        rk = sum01(jnp.where(eid == idxs[slot], prior, 0.0))
        rank_ref[slot:slot + 1, :] = rk.reshape(1, tr).astype(jnp.int32)
    total = carry_ref[...] + jnp.sum(onehot2, axis=1, keepdims=True)
    carry_ref[...] = total
    cnt_ref[...] = jnp.broadcast_to(total, cnt_ref.shape).astype(jnp.int32)


def _route(scores_t, bias_col, tr):
    n_exp, t = scores_t.shape
    slot_spec = pl.BlockSpec((TOP_K, tr), lambda i: (0, i))
    return pl.pallas_call(
        _route_kernel,
        grid=(t // tr,),
        in_specs=[pl.BlockSpec((n_exp, tr), lambda i: (0, i)), _const_spec(bias_col.shape)],
        out_specs=[slot_spec, slot_spec, slot_spec, _const_spec((n_exp, V7X_LANES))],
        out_shape=[jax.ShapeDtypeStruct((TOP_K, t), jnp.int32),
                   jax.ShapeDtypeStruct((TOP_K, t), F32),
                   jax.ShapeDtypeStruct((TOP_K, t), jnp.int32),
                   jax.ShapeDtypeStruct((n_exp, V7X_LANES), jnp.int32)],
        scratch_shapes=[pltpu.VMEM((n_exp, 1), F32)],
        compiler_params=_cparams(("arbitrary",)),
        name="route",
    )(scores_t, bias_col)


def _dest_kernel(pstart_ref, idx_ref, rank_ref, o_ref):
    n_exp = pstart_ref.shape[0]
    tr = idx_ref.shape[1]
    eid = lax.broadcasted_iota(jnp.int32, (n_exp, tr), 0)
    pstart = pstart_ref[...]
    for slot in range(TOP_K):
        hit = eid == idx_ref[slot:slot + 1, :]
        start = jnp.sum(jnp.where(hit, pstart, 0.0), axis=0, keepdims=True).astype(jnp.int32)
        o_ref[slot:slot + 1, :] = (start + rank_ref[slot:slot + 1, :]) * ROW_SLABS


def _dest(pstart, idx, rank, tr):
    k, t = idx.shape
    spec = pl.BlockSpec((k, tr), lambda i: (0, i))
    pstart_col = pstart.astype(F32)[:, None]
    return pl.pallas_call(
        _dest_kernel,
        grid=(t // tr,),
        in_specs=[_const_spec(pstart_col.shape), spec, spec],
        out_specs=spec,
        out_shape=jax.ShapeDtypeStruct((k, t), jnp.int32),
        compiler_params=_cparams(("arbitrary",)),
        name="dest",
    )(pstart_col, idx, rank)


def _dispatch_kernel(dest_ref, hp_ref, xs_ref, sem):
    td = hp_ref.shape[0] // ROW_SLABS

    def row_copy(r, slot):
        dst = pl.multiple_of(dest_ref[r * TOP_K + slot], ROW_SLABS)
        src = pl.multiple_of(r * ROW_SLABS, ROW_SLABS)
        return pltpu.make_async_copy(hp_ref.at[pl.ds(src, ROW_SLABS)], xs_ref.at[pl.ds(dst, ROW_SLABS)], sem)

    def start(r, c):
        for slot in range(TOP_K):
            row_copy(r, slot).start(priority=slot % 2)
        return c

    def wait(r, c):
        for slot in range(TOP_K):
            row_copy(r, slot).wait()
        return c

    lax.fori_loop(0, td, start, 0, unroll=4)
    lax.fori_loop(0, td, wait, 0, unroll=8)


def _dispatch(dest_flat, hp, n_pad, td):
    t = hp.shape[0] // ROW_SLABS
    return pl.pallas_call(
        _dispatch_kernel,
        grid=(t // td,),
        in_specs=[pl.BlockSpec((td * TOP_K,), lambda i: (i,), memory_space=pltpu.SMEM),
                  pl.BlockSpec((td * ROW_SLABS, V7X_LANES), lambda i: (i, 0))],
        out_specs=pl.BlockSpec(memory_space=pl.ANY),
        out_shape=jax.ShapeDtypeStruct((n_pad * ROW_SLABS, V7X_LANES), hp.dtype),
        scratch_shapes=[pltpu.SemaphoreType.DMA(())],
        compiler_params=_cparams(("arbitrary",)),
        name="dispatch",
    )(dest_flat, hp)


def _expert_kernel(be_ref, rows_ref, seq_ref, cnt_ref, xs_hbm, wg_hbm, wu_hbm, wd_hbm, ys_hbm,
                   xbuf, ybuf, wgf, wuf, wdf, wgu_s, wd_s, sem_x, sem_y, sem_w, *, layer):
    n = cnt_ref[0]
    n_seq = cnt_ref[1]
    blk_rows = EXPERT_BLOCK * ROW_SLABS
    nx, nw = EXPERT_X_AHEAD + 1, EXPERT_W_AHEAD + 1

    def block_rows(q):
        start = q * blk_rows
        return pl.ds(start if isinstance(q, int) else pl.multiple_of(start, blk_rows), blk_rows)

    def x_copy(q, slot):
        return pltpu.make_async_copy(xs_hbm.at[block_rows(q)], xbuf.at[slot], sem_x.at[slot])

    def y_copy(q, slot):
        return pltpu.make_async_copy(ybuf.at[slot], ys_hbm.at[block_rows(q)], sem_y.at[slot])

    def w_copies(k, ws):
        e = seq_ref[k]
        out = []
        for hbm, buf in ((wg_hbm, wgf), (wu_hbm, wuf), (wd_hbm, wdf)):
            half = buf.shape[1] // 2
            for part in range(2):
                rows = pl.ds(part * half, half)
                out.append((pltpu.make_async_copy(hbm.at[layer, e, rows], buf.at[ws, rows], sem_w.at[ws]), part))
        return out

    for j in range(EXPERT_X_AHEAD):
        @pl.when(j < n)
        def _():
            x_copy(j, j).start(priority=1)

    for j in range(EXPERT_W_AHEAD):
        @pl.when(j < n_seq)
        def _():
            for cp, prio in w_copies(j, j):
                cp.start(priority=prio)

    def body(q, k):
        slot = q % 2
        fresh = jnp.logical_or(q == 0, be_ref[q] != be_ref[jnp.maximum(q - 1, 0)])

        @pl.when(fresh)
        def _():
            ws = k % nw
            for cp, _ in w_copies(k, ws):
                cp.wait()
            fw = wd_s.shape[0]
            wgu_s[:, :fw] = wgf[ws].astype(BF16)
            wgu_s[:, fw:] = wuf[ws].astype(BF16)
            wd_s[...] = wdf[ws].astype(BF16)

            @pl.when(k + EXPERT_W_AHEAD < n_seq)
            def _():
                for cp, prio in w_copies(k + EXPERT_W_AHEAD, (k + EXPERT_W_AHEAD) % nw):
                    cp.start(priority=prio)

        @pl.when(q + EXPERT_X_AHEAD < n)
        def _():
            x_copy(q + EXPERT_X_AHEAD, (q + EXPERT_X_AHEAD) % nx).start(priority=1)

        x_copy(q, q % nx).wait()

        @pl.when(q >= 2)
        def _():
            y_copy(q - 2, slot).wait()

        half = EXPERT_BLOCK // 2
        f = wd_s.shape[0]
        gu = []
        for h in range(2):
            x = jnp.concatenate(_load_packed_rows(xbuf, half, lead=(q % nx,), row0=h * half), axis=1)
            valid = lax.broadcasted_iota(jnp.int32, x.shape, 0) + h * half < rows_ref[q]
            x = jnp.where(valid, x, 0.0).astype(BF16)
            gu.append(_dot(x, wgu_s[...]))
        for h in range(2):
            g, u = gu[h][:, :f], gu[h][:, f:]
            a = (g * jax.nn.sigmoid(g) * u).astype(BF16)
            _store_packed_rows(ybuf.at[slot], _dot(a, wd_s[...]), row0=h * half)
        y_copy(q, slot).start()
        return k + fresh.astype(jnp.int32)

    lax.fori_loop(0, n, body, jnp.int32(0))

    @pl.when(n >= 2)
    def _():
        y_copy(n - 2, n % 2).wait()

    y_copy(n - 1, (n - 1) % 2).wait()


def _experts(block_expert, block_rows, expert_seq, counts2, xs, wg, wu, wd, layer):
    _, n_exp, d, f = wg.shape
    blk = (EXPERT_BLOCK * ROW_SLABS, V7X_LANES)
    nx, nw = EXPERT_X_AHEAD + 1, EXPERT_W_AHEAD + 1
    any_spec = pl.BlockSpec(memory_space=pl.ANY)
    return pl.pallas_call(
        functools.partial(_expert_kernel, layer=layer),
        grid_spec=pltpu.PrefetchScalarGridSpec(
            num_scalar_prefetch=4, grid=(1,),
            in_specs=[any_spec, any_spec, any_spec, any_spec],
            out_specs=any_spec,
            scratch_shapes=[pltpu.VMEM((nx,) + blk, xs.dtype), pltpu.VMEM((2,) + blk, xs.dtype),
                            pltpu.VMEM((nw, d, f), F32), pltpu.VMEM((nw, d, f), F32), pltpu.VMEM((nw, f, d), F32),
                            pltpu.VMEM((d, 2 * f), BF16), pltpu.VMEM((f, d), BF16),
                            pltpu.SemaphoreType.DMA((nx,)), pltpu.SemaphoreType.DMA((2,)),
                            pltpu.SemaphoreType.DMA((nw,))]),
        out_shape=jax.ShapeDtypeStruct(xs.shape, xs.dtype),
        compiler_params=_cparams(("arbitrary",)),
        name="experts",
    )(block_expert, block_rows, expert_seq, counts2, xs, wg, wu, wd)


def _combine_kernel(d0_ref, d1_ref, d2_ref, ys_ref, w_ref, shr_ref, x1_ref, g2_ref, lg_ref, lb_ref, o_ref,
                    buf_a, buf_b, sem, *, alpha, tc):
    i = pl.program_id(0)
    last = pl.num_programs(0) - 1

    def row_copy(dref, r, slot, buf, sem_idx):
        src = pl.multiple_of(dref[r * TOP_K + slot], ROW_SLABS)
        dst = r * ROW_SLABS if isinstance(r, int) else pl.multiple_of(r * ROW_SLABS, ROW_SLABS)
        return pltpu.make_async_copy(ys_ref.at[pl.ds(src, ROW_SLABS)], buf.at[slot, pl.ds(dst, ROW_SLABS)],
                                     sem.at[sem_idx])

    def issue_unrolled(dref, buf, sem_idx):
        for r in range(tc):
            for slot in range(TOP_K):
                row_copy(dref, r, slot, buf, sem_idx).start(priority=slot % 2)

    def issue_loop(dref, buf, sem_idx):
        def start(r, c):
            for slot in range(TOP_K):
                row_copy(dref, r, slot, buf, sem_idx).start(priority=slot % 2)
            return c
        lax.fori_loop(0, tc, start, 0, unroll=4)

    def wait_all(dref, buf, sem_idx):
        def wait(r, c):
            for slot in range(TOP_K):
                row_copy(dref, r, slot, buf, sem_idx).wait()
            return c
        lax.fori_loop(0, tc, wait, 0, unroll=8)

    def reduce_tile(buf, rows):
        w = w_ref[rows, :]
        chunks = None
        for slot in range(TOP_K):
            wk = w[:, slot:slot + 1]
            part = [c * wk for c in _load_packed_rows(buf, tc, lead=(slot,))]
            chunks = part if chunks is None else [a + b for a, b in zip(chunks, part)]
        y = shr_ref[rows, :] + jnp.concatenate(chunks, axis=1)
        z = alpha * x1_ref[rows, :] + (1.0 + g2_ref[0]) * y
        o_ref[rows, :] = _ln(z) * lg_ref[...] + lb_ref[...]

    @pl.when(i == 0)
    def _():
        issue_loop(d0_ref, buf_a, 0)

    wait_all(d0_ref, buf_a, 0)
    issue_unrolled(d1_ref, buf_b, 1)
    reduce_tile(buf_a, pl.ds(0, tc))

    wait_all(d1_ref, buf_b, 1)
    issue_unrolled(d2_ref, buf_a, 0)
    reduce_tile(buf_b, pl.ds(tc, tc))

    @pl.when(i == last)
    def _():
        wait_all(d2_ref, buf_a, 0)


def _combine(dest_flat, ys, wts_tk, shared, x1, gate2, lg, lb, tc, seq, alpha):
    t, d = x1.shape
    tps = seq // (2 * tc)
    n_tiles = t // tc
    row_spec = lambda n: pl.BlockSpec((2 * tc, n), lambda i: (i, 0))
    tile_dest = lambda f: pl.BlockSpec((tc * TOP_K,), f, memory_space=pltpu.SMEM)
    buf = pltpu.VMEM((TOP_K, tc * ROW_SLABS, V7X_LANES), jnp.uint32)
    return pl.pallas_call(
        functools.partial(_combine_kernel, alpha=alpha, tc=tc),
        grid=(n_tiles // 2,),
        in_specs=[tile_dest(lambda i: (2 * i,)), tile_dest(lambda i: (2 * i + 1,)),
                  tile_dest(lambda i: (jnp.minimum(2 * i + 2, n_tiles - 1),)),
                  pl.BlockSpec(memory_space=pl.ANY),
                  row_spec(TOP_K), row_spec(d), row_spec(d),
                  pl.BlockSpec((1, 1, d), lambda i: (i // tps, 0, 0)),
                  _const_spec(lg.shape), _const_spec(lb.shape)],
        out_specs=row_spec(d),
        out_shape=jax.ShapeDtypeStruct((t, d), F32),
        scratch_shapes=[buf, buf, pltpu.SemaphoreType.DMA((2,))],
        compiler_params=_cparams(("arbitrary",)),
        name="combine",
    )(dest_flat, dest_flat, dest_flat, ys, wts_tk, shared, x1, gate2, lg, lb)


def _placement():
    pq = np.zeros((V7X_LANES, FOX_HEADS * HEAD_PAD), np.float32)
    pk = np.zeros((V7X_LANES, FOX_HEADS * HEAD_PAD), np.float32)
    for h in range(FOX_HEADS):
        base = h * HEAD_PAD + FOX_HEAD_DIM
        for piece in range(3):
            pq[piece * 8 + h, base + piece] = 1.0
            pk[24, base + piece] = 1.0
            pq[24, base + 3 + piece] = 1.0
            pk[piece * 8 + h, base + 3 + piece] = -1.0
    return jnp.asarray(pq, BF16), jnp.asarray(pk, BF16)


def _inproj_weights(w_in, b_forget, d):
    conv2 = d
    fw = FOX_HEADS * FOX_HEAD_DIM
    o1, o2, o3, o4 = conv2, conv2 + fw, conv2 + 2 * fw, conv2 + 3 * fw
    o5 = o4 + FOX_HEADS
    o6 = o5 + d

    def pad_heads(w):
        w = w.reshape(d, FOX_HEADS, FOX_HEAD_DIM)
        w = jnp.pad(w, ((0, 0), (0, 0), (0, HEAD_PAD - FOX_HEAD_DIM)))
        return w.reshape(d, FOX_HEADS * HEAD_PAD).astype(BF16)

    wglu = w_in[:, :o1].astype(BF16)
    wq = pad_heads(w_in[:, o1:o2])
    wk = pad_heads(w_in[:, o2:o3])
    wvt = w_in[:, o3:o4].T.astype(BF16)
    wf8 = w_in[:, o4:o5]
    wf = jnp.pad(jnp.concatenate([wf8, wf8, wf8], axis=1), ((0, 0), (0, V7X_LANES - 3 * FOX_HEADS))).astype(BF16)
    bf = jnp.pad(jnp.concatenate([b_forget, b_forget, b_forget]), (0, V7X_LANES - 3 * FOX_HEADS))[None, :].astype(F32)
    wga = w_in[:, o5:o6].astype(BF16)
    wgb = w_in[:, o6:].astype(BF16)
    pq, pk = _placement()
    return (wglu, wq, wk, wvt, wf, wga, wgb, bf, pq, pk)


def _layer(x2, ada, bsz, seq, w_in, b_forget, conv_w, conv_b, conv_ln_g, conv_ln_b, w_conv_out, w_fox_out,
           w_mix_out, ln1_g, ln1_b, w_router, router_bias, w_exp_gate, w_exp_up, w_exp_down,
           w_sh_gate, w_sh_up, w_sh_down, ln2_g, ln2_b, depth, layer):
    t, d = x2.shape
    n_exp = w_router.shape[1]
    alpha = (2.0 * depth) ** 0.25
    mods = [ada[:bsz, j * d:(j + 1) * d][:, None, :] for j in range(6)]
    shift1, scale1, gate1, shift2, scale2, gate2 = mods

    tm = min(512, seq)
    conv_w_pad = jnp.pad(conv_w, ((0, CONV_HALO - CONV_WIDTH), (0, 0)))
    conv_wts = (conv_w_pad, conv_b[None, :], conv_ln_g[None, :], conv_ln_b[None, :], w_conv_out.astype(BF16))
    q, k, vt, gya, sgb = _inproj(x2, scale1, shift1, _inproj_weights(w_in, b_forget, d), conv_wts, tm, seq)

    ot = _attn(q.reshape(bsz, seq, -1), k.reshape(bsz, seq, -1), vt, tm, ATTN_HEADS_PER_STEP)

    tmx = min(512, seq)
    wr_t = w_router.T
    wr_h = wr_t.astype(BF16)
    wr_l = (wr_t - wr_h.astype(F32)).astype(BF16)
    mix_w = (w_fox_out.astype(BF16), w_mix_out.astype(BF16), wr_h, wr_l,
             w_sh_gate.astype(BF16), w_sh_up.astype(BF16), w_sh_down.astype(BF16))
    x1, hp, scores_t, shared = _mix(ot, gya, sgb, x2, gate1, scale2, shift2, ln1_g[None, :], ln1_b[None, :],
                                    mix_w, tmx, seq, alpha)

    tr = min(512, t)
    idx, wts, rank, cnt = _route(scores_t, router_bias[:, None], tr)

    counts = cnt[:, 0]
    padded = (counts + EXPERT_BLOCK - 1) // EXPERT_BLOCK * EXPERT_BLOCK
    pend = jnp.cumsum(padded)
    pstart = (pend - padded).astype(jnp.int32)
    n_assign = t * TOP_K
    n_pad = -(-(n_assign + n_exp * (EXPERT_BLOCK - 1)) // EXPERT_BLOCK) * EXPERT_BLOCK
    n_blocks = n_pad // EXPERT_BLOCK
    block_start = jnp.arange(n_blocks, dtype=jnp.int32) * EXPERT_BLOCK
    block_expert = jnp.minimum(jnp.sum(pend[None, :] <= block_start[:, None], axis=1), n_exp - 1).astype(jnp.int32)
    block_rows = jnp.clip((pstart + counts)[block_expert] - block_start, 0, EXPERT_BLOCK).astype(jnp.int32)
    n_used = (pend[-1:] // EXPERT_BLOCK).astype(jnp.int32)
    owns = counts > 0
    expert_seq = jnp.nonzero(owns, size=n_exp, fill_value=0)[0].astype(jnp.int32)
    counts2 = jnp.concatenate([n_used, jnp.sum(owns, dtype=jnp.int32)[None]])

    dest = _dest(pstart, idx, rank, tr)
    dest_flat = dest.T.reshape(-1)
    xs = _dispatch(dest_flat, hp, n_pad, min(256, t))
    ys = _experts(block_expert, block_rows, expert_seq, counts2, xs, w_exp_gate, w_exp_up, w_exp_down, layer)
    return _combine(dest_flat, ys, wts.T, shared, x1, gate2, ln2_g[None, :], ln2_b[None, :],
                    min(128, seq), seq, alpha)


def kernel(x, c, w_ada, b_ada, w_in, b_forget, conv_w, conv_b, conv_ln_g, conv_ln_b, w_conv_out, w_fox_out,
           w_mix_out, ln1_g, ln1_b, w_router, router_bias, w_exp_gate, w_exp_up, w_exp_down, w_sh_gate,
           w_sh_up, w_sh_down, ln2_g, ln2_b):
    bsz, seq, d = x.shape
    depth = w_ada.shape[0]
    c_pad = jnp.pad(c, ((0, 8 - bsz), (0, 0)))
    x2 = x.reshape(bsz * seq, d)
    for l in range(depth):
        ada = _ada(c_pad, w_ada[l], b_ada[l][None, :])
        x2 = _layer(x2, ada, bsz, seq, w_in[l], b_forget[l], conv_w[l], conv_b[l], conv_ln_g[l], conv_ln_b[l],
                    w_conv_out[l], w_fox_out[l], w_mix_out[l], ln1_g[l], ln1_b[l], w_router[l], router_bias[l],
                    w_exp_gate, w_exp_up, w_exp_down, w_sh_gate[l], w_sh_up[l], w_sh_down[l],
                    ln2_g[l], ln2_b[l], depth, l)
    return x2.reshape(bsz, seq, d)
```

```python
import functools

import jax
import jax.numpy as jnp
import numpy as np
from jax import lax
from jax.experimental import pallas as pl
from jax.experimental.pallas import tpu as pltpu

F32 = jnp.float32
BF16 = jnp.bfloat16

LN_EPS = 1e-5
CONV_WIDTH = 31
FOX_HEADS = 8
FOX_HEAD_DIM = 64
N_GROUPS = 8
TOPK_GROUPS = 4
TOP_K = 8
ROUTED_SCALE = 2.5
EXPERT_BLOCK = 256
EXPERT_X_AHEAD = 4
EXPERT_W_AHEAD = 4
ROW_SLABS = 4

V7X_LANES = 128
HEAD_PAD = 128
FORGET_ONES_LANE = 3 * FOX_HEADS
V7X_SUBLANES = 8
ADA_COLS = 1024
CONV_CHUNK = 64
CONV_HALO = 32
VMEM_LIMIT = 56 * 1024 * 1024
NEG_BIG = -1e30
LOG2E = 1.4426950408889634
ATTN_EXTRA_ROWS = 16
ATTN_HEADS_PER_STEP = 8
ATTN_Q_COLS = 256
MIX_ROWS_PER_GROUP = 256
ATTN_PIPE_LAG = 3


def _cparams(sem):
    return pltpu.CompilerParams(dimension_semantics=sem, vmem_limit_bytes=VMEM_LIMIT)


def _ln(v):
    mu = jnp.mean(v, axis=-1, keepdims=True)
    vc = v - mu
    var = jnp.mean(vc * vc, axis=-1, keepdims=True)
    return vc * lax.rsqrt(var + LN_EPS)


def _split3(v):
    hi = v.astype(BF16)
    r1 = v - hi.astype(F32)
    mid = r1.astype(BF16)
    lo = (r1 - mid.astype(F32)).astype(BF16)
    return hi, mid, lo


def _dot(a, b):
    return jnp.dot(a, b, preferred_element_type=F32)


def _dot_nt(a, b):
    return lax.dot_general(a, b, (((1,), (1,)), ((), ())), preferred_element_type=F32)


def _dot_tn(a, b):
    return lax.dot_general(a, b, (((0,), (0,)), ((), ())), preferred_element_type=F32)


def _store_packed_rows(ref, v, row0=0):
    n, d = v.shape
    half = d // 2
    vb = v.astype(BF16).astype(F32)
    lo_bits = lax.bitcast_convert_type(vb[:, :half], jnp.uint32)
    hi_bits = lax.bitcast_convert_type(vb[:, half:], jnp.uint32)
    words = (lo_bits >> 16) | (hi_bits & jnp.uint32(0xFFFF0000))
    for c in range(ROW_SLABS):
        ref[pl.ds(row0 * ROW_SLABS + c, n, stride=ROW_SLABS), :] = words[:, c * V7X_LANES:(c + 1) * V7X_LANES]


def _load_packed_rows(ref, n, lead=(), row0=0):
    lo, hi = [], []
    for c in range(ROW_SLABS):
        w = ref[lead + (pl.ds(row0 * ROW_SLABS + c, n, stride=ROW_SLABS), slice(None))]
        lo.append(lax.bitcast_convert_type(w << 16, F32))
        hi.append(lax.bitcast_convert_type(w & jnp.uint32(0xFFFF0000), F32))
    return lo + hi


def _const_spec(shape, single=False):
    nd = len(shape)
    if single:
        return pl.BlockSpec(shape, lambda *_: (0,) * nd, pipeline_mode=pl.Buffered(1))
    return pl.BlockSpec(shape, lambda *_: (0,) * nd)


def _ada_kernel(c_ref, w_ref, b_ref, o_ref):
    c = c_ref[...]
    cond = c * jax.nn.sigmoid(c)
    ch, cm, _ = _split3(cond)
    w = w_ref[...]
    wh, wm, _ = _split3(w)
    o_ref[...] = _dot(ch, wh) + _dot(ch, wm) + _dot(cm, wh) + b_ref[...]


def _ada(c_pad, w, b):
    rows, d = c_pad.shape
    n = w.shape[1]
    tn = ADA_COLS
    return pl.pallas_call(
        _ada_kernel,
        grid=(n // tn,),
        in_specs=[_const_spec((rows, d)),
                  pl.BlockSpec((d, tn), lambda j: (0, j)),
                  pl.BlockSpec((1, tn), lambda j: (0, j))],
        out_specs=pl.BlockSpec((rows, tn), lambda j: (0, j)),
        out_shape=jax.ShapeDtypeStruct((rows, n), F32),
        compiler_params=_cparams(("arbitrary",)),
        name="ada",
    )(c_pad, w, b)


def _inproj_kernel(x_ref, sc_ref, sh_ref, wglu_ref, wq_ref, wk_ref, wvt_ref, wf_ref, wga_ref, wgb_ref,
                   bf_ref, pq_ref, pk_ref, cw_ref, cb_ref, cg_ref, cbe_ref, wco_ref,
                   q_ref, k_ref, vt_ref, gya_ref, sgb_ref, carry_ref, halo_ref, ext_ref, shift_ref,
                   *, tiles_per_seq, conv_ch, chunk):
    i = pl.program_id(0)
    tm = x_ref.shape[0]

    @pl.when(i % tiles_per_seq == 0)
    def _():
        carry_ref[...] = jnp.zeros_like(carry_ref)
        halo_ref[...] = jnp.zeros_like(halo_ref)

    h = _ln(x_ref[...]) * (1.0 + sc_ref[0]) + sh_ref[0]
    hb = h.astype(BF16)

    glu = _dot(hb, wglu_ref[...])
    u = glu[:, :conv_ch] * jax.nn.sigmoid(glu[:, conv_ch:])

    ext_ref[0:CONV_HALO, :] = halo_ref[...]
    ext_ref[CONV_HALO:, :] = u
    halo_ref[...] = u[tm - CONV_HALO:, :]
    cw = cw_ref[...]
    off = CONV_HALO - (CONV_WIDTH - 1)
    span = tm + CONV_HALO - 8
    for res in range(1, 8):
        shift_ref[res - 1] = ext_ref[res:res + span, :]
    def conv_chunk(c0):
        acc = jnp.zeros((chunk, conv_ch), F32)
        for j in range(CONV_WIDTH):
            res, lo = (off + j) % 8, c0 + (off + j) // 8 * 8
            rows = ext_ref[lo:lo + chunk, :] if res == 0 else shift_ref[res - 1, lo:lo + chunk, :]
            acc = acc + cw[j:j + 1, :] * rows
        return acc

    n_chunks = tm // chunk
    per_gap = -(-n_chunks // 4)
    outs = []

    def conv_chunks():
        for _ in range(per_gap):
            if len(outs) < n_chunks:
                outs.append(conv_chunk(len(outs) * chunk))

    sgb_ref[...] = jax.nn.sigmoid(_dot(hb, wgb_ref[...])).astype(BF16)
    conv_chunks()

    f = _dot(hb, wf_ref[...]) + bf_ref[...]
    logf = jnp.minimum(f, 0.0) - jnp.log(1.0 + jnp.exp(-jnp.abs(f)))
    lh, lm, ll = _split3(logf)
    row = lax.broadcasted_iota(jnp.int32, (tm, tm), 0)
    col = lax.broadcasted_iota(jnp.int32, (tm, tm), 1)
    tri = jnp.where(row >= col, 1.0, 0.0).astype(BF16)
    cs = _dot(tri, lh) + _dot(tri, lm) + _dot(tri, ll)
    cum = cs + carry_ref[...]
    carry_ref[...] = cum[tm - 1:tm, :]
    conv_chunks()

    ch, cm, cl = _split3(cum * LOG2E)
    lane = lax.broadcasted_iota(jnp.int32, cum.shape, 1)
    nh = FOX_HEADS
    tail = jnp.where(lane == FORGET_ONES_LANE, 1.0, 0.0)
    pieces = jnp.where(lane < nh, ch.astype(F32), jnp.where(lane < 2 * nh, cm.astype(F32),
                       jnp.where(lane < 3 * nh, cl.astype(F32), tail))).astype(BF16)
    scale = FOX_HEAD_DIM ** -0.5 * LOG2E
    q_ref[...] = (_dot(hb, wq_ref[...]) * scale + _dot(pieces, pq_ref[...])).astype(BF16)
    conv_chunks()
    k_ref[...] = (_dot(hb, wk_ref[...]) + _dot(pieces, pk_ref[...])).astype(BF16)
    conv_chunks()
    vt_ref[0, 0] = _dot_nt(wvt_ref[...], hb).astype(BF16)

    v = jnp.concatenate(outs, axis=0) + cb_ref[...]
    v = _ln(v) * cg_ref[...] + cbe_ref[...]
    v = v * jax.nn.sigmoid(v)
    sga = jax.nn.sigmoid(_dot(hb, wga_ref[...]))
    gya_ref[...] = (sga * _dot(v.astype(BF16), wco_ref[...])).astype(BF16)


def _inproj(x2, scale1, shift1, wts, conv_wts, tm, seq):
    t, d = x2.shape
    tps = seq // tm
    bsz = t // seq
    wglu, wq, wk, wvt, wf, wga, wgb, bf, pq, pk = wts
    conv_ch = wglu.shape[1] // 2
    fw = wvt.shape[0]
    qw = wq.shape[1]
    mod_spec = pl.BlockSpec((1, 1, d), lambda i: (i // tps, 0, 0))
    row_spec = lambda n: pl.BlockSpec((tm, n), lambda i: (i, 0))
    consts = list(wts) + list(conv_wts)
    return pl.pallas_call(
        functools.partial(_inproj_kernel, tiles_per_seq=tps, conv_ch=conv_ch, chunk=CONV_CHUNK),
        grid=(t // tm,),
        in_specs=[row_spec(d), mod_spec, mod_spec] + [_const_spec(w.shape, single=True) for w in consts],
        out_specs=[row_spec(qw), row_spec(qw),
                   pl.BlockSpec((1, 1, fw, tm), lambda i: (i // tps, i % tps, 0, 0)),
                   row_spec(d), row_spec(d)],
        out_shape=[jax.ShapeDtypeStruct((t, qw), BF16),
                   jax.ShapeDtypeStruct((t, qw), BF16),
                   jax.ShapeDtypeStruct((bsz, tps, fw, tm), BF16),
                   jax.ShapeDtypeStruct((t, d), BF16),
                   jax.ShapeDtypeStruct((t, d), BF16)],
        scratch_shapes=[pltpu.VMEM((1, V7X_LANES), F32), pltpu.VMEM((CONV_HALO, conv_ch), F32),
                        pltpu.VMEM((tm + CONV_HALO, conv_ch), F32),
                        pltpu.VMEM((7, tm + CONV_HALO - 8, conv_ch), F32)],
        compiler_params=_cparams(("arbitrary",)),
        name="inproj",
    )(x2, scale1, shift1, *consts)


def _attn_kernel(q_ref, k_ref, vt_ref, o_ref, *, blk, heads):
    qi = pl.program_id(2)
    row = lax.broadcasted_iota(jnp.int32, (ATTN_EXTRA_ROWS, blk), 0)
    ones_rows = jnp.where(row == 0, 1.0, 0.0).astype(BF16)

    ncol = blk // ATTN_Q_COLS
    chains = [(j, c) for j in range(heads) for c in range(ncol)]

    def scores(kj, chain, masked):
        j, c = chain
        k = k_ref[pl.ds(pl.multiple_of(kj * blk, blk), blk), j * HEAD_PAD:(j + 1) * HEAD_PAD]
        q = q_ref[c * ATTN_Q_COLS:(c + 1) * ATTN_Q_COLS, j * HEAD_PAD:(j + 1) * HEAD_PAD]
        s = _dot_nt(k, q)
        if masked:
            kpos = lax.broadcasted_iota(jnp.int32, s.shape, 0)
            qpos = lax.broadcasted_iota(jnp.int32, s.shape, 1) + c * ATTN_Q_COLS
            s = jnp.where(kpos <= qpos, s, NEG_BIG)
        return s

    def probs(s, m):
        m_new = jnp.maximum(m, jnp.max(s, axis=0, keepdims=True))
        return jnp.exp2(s - m_new).astype(BF16), m_new

    def update(kj, chain, p, m, m_new, acc):
        j, _ = chain
        vt = vt_ref[kj, j * FOX_HEAD_DIM:(j + 1) * FOX_HEAD_DIM, :]
        lhs = jnp.concatenate([vt, ones_rows], axis=0)
        return jnp.exp2(m - m_new) * acc + _dot(lhs, p)

    def step(kj, carry, masked):
        n, lag = len(chains), ATTN_PIPE_LAG
        s, pm, out = {}, {}, [None] * n
        for i in range(n + lag):
            if i < n:
                s[i] = scores(kj, chains[i], masked)
            if lag - 1 <= i < n + lag - 1:
                pm[i - lag + 1] = probs(s.pop(i - lag + 1), carry[i - lag + 1][0])
            if i >= lag:
                p, m_new = pm.pop(i - lag)
                m, acc = carry[i - lag]
                out[i - lag] = (m_new, update(kj, chains[i - lag], p, m, m_new, acc))
        return tuple(out)

    init = tuple((jnp.full((1, ATTN_Q_COLS), NEG_BIG, F32),
                  jnp.zeros((FOX_HEAD_DIM + ATTN_EXTRA_ROWS, ATTN_Q_COLS), F32)) for _ in chains)
    carry = lax.fori_loop(0, qi, lambda kj, cr: step(kj, cr, False), init)
    carry = step(qi, carry, True)
    for (j, c), (_, acc) in zip(chains, carry):
        o_ref[j * FOX_HEAD_DIM:(j + 1) * FOX_HEAD_DIM, c * ATTN_Q_COLS:(c + 1) * ATTN_Q_COLS] = (
            acc[:FOX_HEAD_DIM] / acc[FOX_HEAD_DIM:FOX_HEAD_DIM + 1]).astype(BF16)


def _attn(q, k, vt, blk, heads):
    bsz, seq, _ = q.shape
    nkb = seq // blk
    return pl.pallas_call(
        functools.partial(_attn_kernel, blk=blk, heads=heads),
        grid=(bsz, FOX_HEADS // heads, seq // blk),
        in_specs=[pl.BlockSpec((None, blk, heads * HEAD_PAD), lambda b, h, i: (b, i, h)),
                  pl.BlockSpec((None, seq, heads * HEAD_PAD), lambda b, h, i: (b, 0, h),
                               pipeline_mode=pl.Buffered(1)),
                  pl.BlockSpec((None, nkb, heads * FOX_HEAD_DIM, blk), lambda b, h, i: (b, 0, h, 0),
                               pipeline_mode=pl.Buffered(1))],
        out_specs=pl.BlockSpec((None, heads * FOX_HEAD_DIM, blk), lambda b, h, i: (b, h, i)),
        out_shape=jax.ShapeDtypeStruct((bsz, FOX_HEADS * FOX_HEAD_DIM, seq), BF16),
        compiler_params=_cparams(("arbitrary", "arbitrary", "arbitrary")),
        name="attn",
    )(q, k, vt)


def _mix_kernel(ot_ref, gya_ref, sgb_ref, x_ref, g1_ref, sc2_ref, sh2_ref, lg_ref, lb_ref,
                wfox_ref, wmix_ref, wrh_ref, wrl_ref, wsg_ref, wsu_ref, wsd_ref,
                x1_ref, hp_ref, st_ref, shr_ref, *, alpha, sub):
    tm = x_ref.shape[0]
    groups = [pl.ds(r0, sub) for r0 in range(0, tm, sub)]
    ys = []
    for rows in groups:
        yb = _dot_tn(ot_ref[:, rows], wfox_ref[...])
        merged = gya_ref[rows, :].astype(F32) + sgb_ref[rows, :].astype(F32) * yb
        ys.append(_dot(merged.astype(BF16), wmix_ref[...]))
    for rows, y in zip(groups, ys):
        x1 = _ln(alpha * x_ref[rows, :] + (1.0 + g1_ref[0]) * y) * lg_ref[...] + lb_ref[...]
        x1_ref[rows, :] = x1
        h2 = _ln(x1) * (1.0 + sc2_ref[0]) + sh2_ref[0]
        hb = h2.astype(BF16)
        hl = (h2 - hb.astype(F32)).astype(BF16)
        _store_packed_rows(hp_ref, h2, row0=rows.start)

        logits_t = _dot_nt(wrh_ref[...], hb) + _dot_nt(wrl_ref[...], hb) + _dot_nt(wrh_ref[...], hl)
        st_ref[:, rows] = jax.nn.sigmoid(logits_t)

        g = _dot(hb, wsg_ref[...])
        u = _dot(hb, wsu_ref[...])
        a = (g * jax.nn.sigmoid(g) * u).astype(BF16)
        shr_ref[rows, :] = _dot(a, wsd_ref[...])


def _mix(ot, gya, sgb, x2, gate1, scale2, shift2, lg, lb, wts, tm, seq, alpha):
    t, d = x2.shape
    tps = seq // tm
    fw = ot.shape[1]
    n_exp = wts[2].shape[0]
    mod_spec = pl.BlockSpec((1, 1, d), lambda i: (i // tps, 0, 0))
    row_spec = lambda n: pl.BlockSpec((tm, n), lambda i: (i, 0))
    return pl.pallas_call(
        functools.partial(_mix_kernel, alpha=alpha, sub=min(MIX_ROWS_PER_GROUP, tm)),
        grid=(t // tm,),
        in_specs=[pl.BlockSpec((None, fw, tm), lambda i: (i // tps, 0, i % tps)),
                  row_spec(d), row_spec(d), row_spec(d), mod_spec, mod_spec, mod_spec,
                  _const_spec(lg.shape), _const_spec(lb.shape)] + [_const_spec(w.shape) for w in wts],
        out_specs=[row_spec(d), pl.BlockSpec((tm * ROW_SLABS, V7X_LANES), lambda i: (i, 0)),
                   pl.BlockSpec((n_exp, tm), lambda i: (0, i)), row_spec(d)],
        out_shape=[jax.ShapeDtypeStruct((t, d), F32),
                   jax.ShapeDtypeStruct((t * ROW_SLABS, V7X_LANES), jnp.uint32),
                   jax.ShapeDtypeStruct((n_exp, t), F32),
                   jax.ShapeDtypeStruct((t, d), F32)],
        compiler_params=_cparams(("arbitrary",)),
        name="mix",
    )(ot, gya, sgb, x2, gate1, scale2, shift2, lg, lb, *wts)


def _route_kernel(st_ref, bias_ref, idx_ref, wts_ref, rank_ref, cnt_ref, carry_ref):
    i = pl.program_id(0)
    n_exp, tr = st_ref.shape
    gsz = n_exp // N_GROUPS
    neg_inf = -jnp.inf

    @pl.when(i == 0)
    def _():
        carry_ref[...] = jnp.zeros_like(carry_ref)

    shape3 = (N_GROUPS, gsz, tr)

    def max01(v):
        return jnp.max(jnp.max(v, axis=1, keepdims=True), axis=0, keepdims=True)

    def min01(v):
        return jnp.min(jnp.min(v, axis=1, keepdims=True), axis=0, keepdims=True)

    def sum01(v):
        return jnp.sum(jnp.sum(v, axis=1, keepdims=True), axis=0, keepdims=True)

    sc = st_ref[...].reshape(shape3)
    gsel = (st_ref[...] + bias_ref[...]).reshape(shape3)
    pos = lax.broadcasted_iota(jnp.int32, shape3, 1)
    m1 = jnp.max(gsel, axis=1, keepdims=True)
    i1 = jnp.min(jnp.where(gsel == m1, pos, gsz), axis=1, keepdims=True)
    m2 = jnp.max(jnp.where(pos == i1, neg_inf, gsel), axis=1, keepdims=True)
    gs = m1 + m2

    gid = lax.broadcasted_iota(jnp.int32, gs.shape, 0)
    gkeep = jnp.zeros(gs.shape, F32)
    for _ in range(TOPK_GROUPS):
        mx = jnp.max(gs, axis=0, keepdims=True)
        gi = jnp.min(jnp.where(gs == mx, gid, N_GROUPS), axis=0, keepdims=True)
        hit = gid == gi
        gkeep = gkeep + jnp.where(hit, 1.0, 0.0)
        gs = jnp.where(hit, neg_inf, gs)

    cur = jnp.where(jnp.broadcast_to(gkeep, shape3) > 0.5, gsel, neg_inf)
    eid = lax.broadcasted_iota(jnp.int32, shape3, 0) * gsz + pos
    onehot = jnp.zeros(shape3, F32)
    idxs, ws = [], []
    wsum = jnp.zeros((1, 1, tr), F32)
    for _ in range(TOP_K):
        mx = max01(cur)
        ik = min01(jnp.where(cur == mx, eid, n_exp))
        hit = eid == ik
        wk = sum01(jnp.where(hit, sc, 0.0))
        idxs.append(ik)
        ws.append(wk)
        wsum = wsum + wk
        onehot = onehot + jnp.where(hit, 1.0, 0.0)
        cur = jnp.where(hit, neg_inf, cur)

    ra = lax.broadcasted_iota(jnp.int32, (tr, tr), 0)
    rb = lax.broadcasted_iota(jnp.int32, (tr, tr), 1)
    upper = jnp.where(ra < rb, 1.0, 0.0).astype(BF16)
    onehot2 = onehot.reshape(n_exp, tr)
    prior = (_dot(onehot2.astype(BF16), upper) + carry_ref[...]).reshape(shape3)
    for slot in range(TOP_K):
        idx_ref[slot:slot + 1, :] = idxs[slot].reshape(1, tr)
        wts_ref[slot:slot + 1, :] = (ws[slot] / wsum * ROUTED_SCALE).reshape(1, tr)
        rk = sum01(jnp.where(eid == idxs[slot], prior, 0.0))
        rank_ref[slot:slot + 1, :] = rk.reshape(1, tr).astype(jnp.int32)
    total = carry_ref[...] + jnp.sum(onehot2, axis=1, keepdims=True)
    carry_ref[...] = total
    cnt_ref[...] = jnp.broadcast_to(total, cnt_ref.shape).astype(jnp.int32)


def _route(scores_t, bias_col, tr):
    n_exp, t = scores_t.shape
    slot_spec = pl.BlockSpec((TOP_K, tr), lambda i: (0, i))
    return pl.pallas_call(
        _route_kernel,
        grid=(t // tr,),
        in_specs=[pl.BlockSpec((n_exp, tr), lambda i: (0, i)), _const_spec(bias_col.shape)],
        out_specs=[slot_spec, slot_spec, slot_spec, _const_spec((n_exp, V7X_LANES))],
        out_shape=[jax.ShapeDtypeStruct((TOP_K, t), jnp.int32),
                   jax.ShapeDtypeStruct((TOP_K, t), F32),
                   jax.ShapeDtypeStruct((TOP_K, t), jnp.int32),
                   jax.ShapeDtypeStruct((n_exp, V7X_LANES), jnp.int32)],
        scratch_shapes=[pltpu.VMEM((n_exp, 1), F32)],
        compiler_params=_cparams(("arbitrary",)),
        name="route",
    )(scores_t, bias_col)


def _dest_kernel(pstart_ref, idx_ref, rank_ref, o_ref):
    n_exp = pstart_ref.shape[0]
    tr = idx_ref.shape[1]
    eid = lax.broadcasted_iota(jnp.int32, (n_exp, tr), 0)
    pstart = pstart_ref[...]
    for slot in range(TOP_K):
        hit = eid == idx_ref[slot:slot + 1, :]
        start = jnp.sum(jnp.where(hit, pstart, 0.0), axis=0, keepdims=True).astype(jnp.int32)
        o_ref[slot:slot + 1, :] = (start + rank_ref[slot:slot + 1, :]) * ROW_SLABS


def _dest(pstart, idx, rank, tr):
    k, t = idx.shape
    spec = pl.BlockSpec((k, tr), lambda i: (0, i))
    pstart_col = pstart.astype(F32)[:, None]
    return pl.pallas_call(
        _dest_kernel,
        grid=(t // tr,),
        in_specs=[_const_spec(pstart_col.shape), spec, spec],
        out_specs=spec,
        out_shape=jax.ShapeDtypeStruct((k, t), jnp.int32),
        compiler_params=_cparams(("arbitrary",)),
        name="dest",
    )(pstart_col, idx, rank)


def _dispatch_kernel(dest_ref, hp_ref, xs_ref, sem):
    td = hp_ref.shape[0] // ROW_SLABS

    def row_copy(r, slot):
        dst = pl.multiple_of(dest_ref[slot, r], ROW_SLABS)
        src = pl.multiple_of(r * ROW_SLABS, ROW_SLABS)
        return pltpu.make_async_copy(hp_ref.at[pl.ds(src, ROW_SLABS)], xs_ref.at[pl.ds(dst, ROW_SLABS)], sem)

    def start(r, c):
        for slot in range(TOP_K):
            row_copy(r, slot).start(priority=slot % 2)
        return c

    def wait(r, c):
        for slot in range(TOP_K):
            row_copy(r, slot).wait()
        return c

    lax.fori_loop(0, td, start, 0, unroll=4)
    lax.fori_loop(0, td, wait, 0, unroll=8)


def _dispatch(dest, hp, n_pad, td):
    t = hp.shape[0] // ROW_SLABS
    return pl.pallas_call(
        _dispatch_kernel,
        grid=(t // td,),
        in_specs=[pl.BlockSpec((TOP_K, td), lambda i: (0, i), memory_space=pltpu.SMEM),
                  pl.BlockSpec((td * ROW_SLABS, V7X_LANES), lambda i: (i, 0))],
        out_specs=pl.BlockSpec(memory_space=pl.ANY),
        out_shape=jax.ShapeDtypeStruct((n_pad * ROW_SLABS, V7X_LANES), hp.dtype),
        scratch_shapes=[pltpu.SemaphoreType.DMA(())],
        compiler_params=_cparams(("arbitrary",)),
        name="dispatch",
    )(dest, hp)


def _expert_kernel(be_ref, rows_ref, seq_ref, cnt_ref, xs_hbm, wg_hbm, wu_hbm, wd_hbm, ys_hbm,
                   xbuf, ybuf, wgf, wuf, wdf, wgu_s, wd_s, sem_x, sem_y, sem_w, *, layer):
    n = cnt_ref[0]
    n_seq = cnt_ref[1]
    blk_rows = EXPERT_BLOCK * ROW_SLABS
    nx, nw = EXPERT_X_AHEAD + 1, EXPERT_W_AHEAD + 1

    def block_rows(q):
        start = q * blk_rows
        return pl.ds(start if isinstance(q, int) else pl.multiple_of(start, blk_rows), blk_rows)

    def x_copy(q, slot):
        return pltpu.make_async_copy(xs_hbm.at[block_rows(q)], xbuf.at[slot], sem_x.at[slot])

    def y_copy(q, slot):
        return pltpu.make_async_copy(ybuf.at[slot], ys_hbm.at[block_rows(q)], sem_y.at[slot])

    def w_copies(k, ws):
        e = seq_ref[k]
        out = []
        for hbm, buf in ((wg_hbm, wgf), (wu_hbm, wuf), (wd_hbm, wdf)):
            half = buf.shape[1] // 2
            for part in range(2):
                rows = pl.ds(part * half, half)
                out.append((pltpu.make_async_copy(hbm.at[layer, e, rows], buf.at[ws, rows], sem_w.at[ws]), part))
        return out

    for j in range(EXPERT_X_AHEAD):
        @pl.when(j < n)
        def _():
            x_copy(j, j).start(priority=1)

    for j in range(EXPERT_W_AHEAD):
        @pl.when(j < n_seq)
        def _():
            for cp, prio in w_copies(j, j):
                cp.start(priority=prio)

    def body(q, k):
        slot = q % 2
        fresh = jnp.logical_or(q == 0, be_ref[q] != be_ref[jnp.maximum(q - 1, 0)])

        @pl.when(fresh)
        def _():
            ws = k % nw
            for cp, _ in w_copies(k, ws):
                cp.wait()
            fw = wd_s.shape[0]
            wgu_s[:, :fw] = wgf[ws].astype(BF16)
            wgu_s[:, fw:] = wuf[ws].astype(BF16)
            wd_s[...] = wdf[ws].astype(BF16)

            @pl.when(k + EXPERT_W_AHEAD < n_seq)
            def _():
                for cp, prio in w_copies(k + EXPERT_W_AHEAD, (k + EXPERT_W_AHEAD) % nw):
                    cp.start(priority=prio)

        @pl.when(q + EXPERT_X_AHEAD < n)
        def _():
            x_copy(q + EXPERT_X_AHEAD, (q + EXPERT_X_AHEAD) % nx).start(priority=1)

        x_copy(q, q % nx).wait()

        @pl.when(q >= 2)
        def _():
            y_copy(q - 2, slot).wait()

        half = EXPERT_BLOCK // 2
        f = wd_s.shape[0]
        gu = []
        for h in range(2):
            x = jnp.concatenate(_load_packed_rows(xbuf, half, lead=(q % nx,), row0=h * half), axis=1)
            valid = lax.broadcasted_iota(jnp.int32, x.shape, 0) + h * half < rows_ref[q]
            x = jnp.where(valid, x, 0.0).astype(BF16)
            gu.append(_dot(x, wgu_s[...]))
        for h in range(2):
            g, u = gu[h][:, :f], gu[h][:, f:]
            a = (g * jax.nn.sigmoid(g) * u).astype(BF16)
            _store_packed_rows(ybuf.at[slot], _dot(a, wd_s[...]), row0=h * half)
        y_copy(q, slot).start()
        return k + fresh.astype(jnp.int32)

    lax.fori_loop(0, n, body, jnp.int32(0))

    @pl.when(n >= 2)
    def _():
        y_copy(n - 2, n % 2).wait()

    y_copy(n - 1, (n - 1) % 2).wait()


def _experts(block_expert, block_rows, expert_seq, counts2, xs, wg, wu, wd, layer):
    _, n_exp, d, f = wg.shape
    blk = (EXPERT_BLOCK * ROW_SLABS, V7X_LANES)
    nx, nw = EXPERT_X_AHEAD + 1, EXPERT_W_AHEAD + 1
    any_spec = pl.BlockSpec(memory_space=pl.ANY)
    return pl.pallas_call(
        functools.partial(_expert_kernel, layer=layer),
        grid_spec=pltpu.PrefetchScalarGridSpec(
            num_scalar_prefetch=4, grid=(1,),
            in_specs=[any_spec, any_spec, any_spec, any_spec],
            out_specs=any_spec,
            scratch_shapes=[pltpu.VMEM((nx,) + blk, xs.dtype), pltpu.VMEM((2,) + blk, xs.dtype),
                            pltpu.VMEM((nw, d, f), F32), pltpu.VMEM((nw, d, f), F32), pltpu.VMEM((nw, f, d), F32),
                            pltpu.VMEM((d, 2 * f), BF16), pltpu.VMEM((f, d), BF16),
                            pltpu.SemaphoreType.DMA((nx,)), pltpu.SemaphoreType.DMA((2,)),
                            pltpu.SemaphoreType.DMA((nw,))]),
        out_shape=jax.ShapeDtypeStruct(xs.shape, xs.dtype),
        compiler_params=_cparams(("arbitrary",)),
        name="experts",
    )(block_expert, block_rows, expert_seq, counts2, xs, wg, wu, wd)


def _combine_kernel(d0_ref, d1_ref, d2_ref, ys_ref, w_ref, shr_ref, x1_ref, g2_ref, lg_ref, lb_ref, o_ref,
                    buf_a, buf_b, sem, *, alpha, tc):
    i = pl.program_id(0)
    last = pl.num_programs(0) - 1

    def row_copy(dref, r, slot, buf, sem_idx):
        src = pl.multiple_of(dref[slot, r], ROW_SLABS)
        dst = r * ROW_SLABS if isinstance(r, int) else pl.multiple_of(r * ROW_SLABS, ROW_SLABS)
        return pltpu.make_async_copy(ys_ref.at[pl.ds(src, ROW_SLABS)], buf.at[slot, pl.ds(dst, ROW_SLABS)],
                                     sem.at[sem_idx])

    def issue_unrolled(dref, buf, sem_idx):
        for r in range(tc):
            for slot in range(TOP_K):
                row_copy(dref, r, slot, buf, sem_idx).start(priority=slot % 2)

    def issue_loop(dref, buf, sem_idx):
        def start(r, c):
            for slot in range(TOP_K):
                row_copy(dref, r, slot, buf, sem_idx).start(priority=slot % 2)
            return c
        lax.fori_loop(0, tc, start, 0, unroll=4)

    def wait_all(dref, buf, sem_idx):
        def wait(r, c):
            for slot in range(TOP_K):
                row_copy(dref, r, slot, buf, sem_idx).wait()
            return c
        lax.fori_loop(0, tc, wait, 0, unroll=8)

    def reduce_tile(buf, rows):
        w = w_ref[rows, :]
        chunks = None
        for slot in range(TOP_K):
            wk = w[:, slot:slot + 1]
            part = [c * wk for c in _load_packed_rows(buf, tc, lead=(slot,))]
            chunks = part if chunks is None else [a + b for a, b in zip(chunks, part)]
        y = shr_ref[rows, :] + jnp.concatenate(chunks, axis=1)
        z = alpha * x1_ref[rows, :] + (1.0 + g2_ref[0]) * y
        o_ref[rows, :] = _ln(z) * lg_ref[...] + lb_ref[...]

    @pl.when(i == 0)
    def _():
        issue_loop(d0_ref, buf_a, 0)

    wait_all(d0_ref, buf_a, 0)
    issue_unrolled(d1_ref, buf_b, 1)
    reduce_tile(buf_a, pl.ds(0, tc))

    wait_all(d1_ref, buf_b, 1)
    issue_unrolled(d2_ref, buf_a, 0)
    reduce_tile(buf_b, pl.ds(tc, tc))

    @pl.when(i == last)
    def _():
        wait_all(d2_ref, buf_a, 0)


def _combine(dest, ys, wts_tk, shared, x1, gate2, lg, lb, tc, seq, alpha):
    t, d = x1.shape
    tps = seq // (2 * tc)
    n_tiles = t // tc
    row_spec = lambda n: pl.BlockSpec((2 * tc, n), lambda i: (i, 0))
    tile_dest = lambda f: pl.BlockSpec((TOP_K, tc), f, memory_space=pltpu.SMEM)
    buf = pltpu.VMEM((TOP_K, tc * ROW_SLABS, V7X_LANES), jnp.uint32)
    return pl.pallas_call(
        functools.partial(_combine_kernel, alpha=alpha, tc=tc),
        grid=(n_tiles // 2,),
        in_specs=[tile_dest(lambda i: (0, 2 * i)), tile_dest(lambda i: (0, 2 * i + 1)),
                  tile_dest(lambda i: (0, jnp.minimum(2 * i + 2, n_tiles - 1))),
                  pl.BlockSpec(memory_space=pl.ANY),
                  row_spec(TOP_K), row_spec(d), row_spec(d),
                  pl.BlockSpec((1, 1, d), lambda i: (i // tps, 0, 0)),
                  _const_spec(lg.shape), _const_spec(lb.shape)],
        out_specs=row_spec(d),
        out_shape=jax.ShapeDtypeStruct((t, d), F32),
        scratch_shapes=[buf, buf, pltpu.SemaphoreType.DMA((2,))],
        compiler_params=_cparams(("arbitrary",)),
        name="combine",
    )(dest, dest, dest, ys, wts_tk, shared, x1, gate2, lg, lb)


def _placement():
    pq = np.zeros((V7X_LANES, FOX_HEADS * HEAD_PAD), np.float32)
    pk = np.zeros((V7X_LANES, FOX_HEADS * HEAD_PAD), np.float32)
    for h in range(FOX_HEADS):
        base = h * HEAD_PAD + FOX_HEAD_DIM
        for piece in range(3):
            pq[piece * FOX_HEADS + h, base + piece] = 1.0
            pk[FORGET_ONES_LANE, base + piece] = 1.0
            pq[FORGET_ONES_LANE, base + 3 + piece] = 1.0
            pk[piece * FOX_HEADS + h, base + 3 + piece] = -1.0
    return jnp.asarray(pq, BF16), jnp.asarray(pk, BF16)


def _inproj_weights(w_in, b_forget, d):
    conv2 = d
    fw = FOX_HEADS * FOX_HEAD_DIM
    o1, o2, o3, o4 = conv2, conv2 + fw, conv2 + 2 * fw, conv2 + 3 * fw
    o5 = o4 + FOX_HEADS
    o6 = o5 + d

    def pad_heads(w):
        w = w.reshape(d, FOX_HEADS, FOX_HEAD_DIM)
        w = jnp.pad(w, ((0, 0), (0, 0), (0, HEAD_PAD - FOX_HEAD_DIM)))
        return w.reshape(d, FOX_HEADS * HEAD_PAD).astype(BF16)

    wglu = w_in[:, :o1].astype(BF16)
    wq = pad_heads(w_in[:, o1:o2])
    wk = pad_heads(w_in[:, o2:o3])
    wvt = w_in[:, o3:o4].T.astype(BF16)
    wf8 = w_in[:, o4:o5]
    wf = jnp.pad(jnp.concatenate([wf8, wf8, wf8], axis=1), ((0, 0), (0, V7X_LANES - 3 * FOX_HEADS))).astype(BF16)
    bf = jnp.pad(jnp.concatenate([b_forget, b_forget, b_forget]), (0, V7X_LANES - 3 * FOX_HEADS))[None, :].astype(F32)
    wga = w_in[:, o5:o6].astype(BF16)
    wgb = w_in[:, o6:].astype(BF16)
    pq, pk = _placement()
    return (wglu, wq, wk, wvt, wf, wga, wgb, bf, pq, pk)


def _tile_sizes(seq, t):
    return {"inproj": min(512, seq),
            "mix": min(2 * MIX_ROWS_PER_GROUP, seq),
            "route": min(512, t),
            "dispatch": min(256, t),
            "combine": min(128, seq)}


def _layer(x2, ada, bsz, seq, w_in, b_forget, conv_w, conv_b, conv_ln_g, conv_ln_b, w_conv_out, w_fox_out,
           w_mix_out, ln1_g, ln1_b, w_router, router_bias, w_exp_gate, w_exp_up, w_exp_down,
           w_sh_gate, w_sh_up, w_sh_down, ln2_g, ln2_b, depth, layer):
    t, d = x2.shape
    n_exp = w_router.shape[1]
    alpha = (2.0 * depth) ** 0.25
    mods = [ada[:bsz, j * d:(j + 1) * d][:, None, :] for j in range(6)]
    shift1, scale1, gate1, shift2, scale2, gate2 = mods

    tiles = _tile_sizes(seq, t)
    tm = tiles["inproj"]
    conv_w_pad = jnp.pad(conv_w, ((0, CONV_HALO - CONV_WIDTH), (0, 0)))
    conv_wts = (conv_w_pad, conv_b[None, :], conv_ln_g[None, :], conv_ln_b[None, :], w_conv_out.astype(BF16))
    q, k, vt, gya, sgb = _inproj(x2, scale1, shift1, _inproj_weights(w_in, b_forget, d), conv_wts, tm, seq)

    ot = _attn(q.reshape(bsz, seq, -1), k.reshape(bsz, seq, -1), vt, tm, ATTN_HEADS_PER_STEP)

    tmx = tiles["mix"]
    wr_t = w_router.T
    wr_h = wr_t.astype(BF16)
    wr_l = (wr_t - wr_h.astype(F32)).astype(BF16)
    mix_w = (w_fox_out.astype(BF16), w_mix_out.astype(BF16), wr_h, wr_l,
             w_sh_gate.astype(BF16), w_sh_up.astype(BF16), w_sh_down.astype(BF16))
    x1, hp, scores_t, shared = _mix(ot, gya, sgb, x2, gate1, scale2, shift2, ln1_g[None, :], ln1_b[None, :],
                                    mix_w, tmx, seq, alpha)

    tr = tiles["route"]
    idx, wts, rank, cnt = _route(scores_t, router_bias[:, None], tr)

    counts = cnt[:, 0]
    padded = (counts + EXPERT_BLOCK - 1) // EXPERT_BLOCK * EXPERT_BLOCK
    pend = jnp.cumsum(padded)
    pstart = (pend - padded).astype(jnp.int32)
    n_assign = t * TOP_K
    n_pad = -(-(n_assign + n_exp * (EXPERT_BLOCK - 1)) // EXPERT_BLOCK) * EXPERT_BLOCK
    n_blocks = n_pad // EXPERT_BLOCK
    block_start = jnp.arange(n_blocks, dtype=jnp.int32) * EXPERT_BLOCK
    block_expert = jnp.minimum(jnp.sum(pend[None, :] <= block_start[:, None], axis=1), n_exp - 1).astype(jnp.int32)
    block_rows = jnp.clip((pstart + counts)[block_expert] - block_start, 0, EXPERT_BLOCK).astype(jnp.int32)
    n_used = (pend[-1:] // EXPERT_BLOCK).astype(jnp.int32)
    owns = counts > 0
    expert_seq = jnp.nonzero(owns, size=n_exp, fill_value=0)[0].astype(jnp.int32)
    counts2 = jnp.concatenate([n_used, jnp.sum(owns, dtype=jnp.int32)[None]])

    dest = _dest(pstart, idx, rank, tr)
    xs = _dispatch(dest, hp, n_pad, tiles["dispatch"])
    ys = _experts(block_expert, block_rows, expert_seq, counts2, xs, w_exp_gate, w_exp_up, w_exp_down, layer)
    return _combine(dest, ys, wts.T, shared, x1, gate2, ln2_g[None, :], ln2_b[None, :],
                    tiles["combine"], seq, alpha)


def kernel(x, c, w_ada, b_ada, w_in, b_forget, conv_w, conv_b, conv_ln_g, conv_ln_b, w_conv_out, w_fox_out,
           w_mix_out, ln1_g, ln1_b, w_router, router_bias, w_exp_gate, w_exp_up, w_exp_down, w_sh_gate,
           w_sh_up, w_sh_down, ln2_g, ln2_b):
    bsz, seq, d = x.shape
    depth = w_ada.shape[0]
    c_pad = jnp.pad(c, ((0, -bsz % V7X_SUBLANES), (0, 0)))
    x2 = x.reshape(bsz * seq, d)
    for l in range(depth):
        ada = _ada(c_pad, w_ada[l], b_ada[l][None, :])
        x2 = _layer(x2, ada, bsz, seq, w_in[l], b_forget[l], conv_w[l], conv_b[l], conv_ln_g[l], conv_ln_b[l],
                    w_conv_out[l], w_fox_out[l], w_mix_out[l], ln1_g[l], ln1_b[l], w_router[l], router_bias[l],
                    w_exp_gate, w_exp_up, w_exp_down, w_sh_gate[l], w_sh_up[l], w_sh_down[l],
                    ln2_g[l], ln2_b[l], depth, l)
    return x2.reshape(bsz, seq, d)
```

```python
import functools

import jax
import jax.numpy as jnp
import numpy as np
from jax import lax
from jax.experimental import pallas as pl
from jax.experimental.pallas import tpu as pltpu

F32 = jnp.float32
BF16 = jnp.bfloat16

LN_EPS = 1e-5
CONV_WIDTH = 31
FOX_HEADS = 8
FOX_HEAD_DIM = 64
N_GROUPS = 8
TOPK_GROUPS = 4
TOP_K = 8
ROUTED_SCALE = 2.5
EXPERT_BLOCK = 256
EXPERT_X_AHEAD = 4
EXPERT_W_AHEAD = 4
ROW_SLABS = 4

V7X_LANES = 128
HEAD_PAD = 128
FORGET_ONES_LANE = 3 * FOX_HEADS
V7X_SUBLANES = 8
ADA_COLS = 1024
CONV_CHUNK = 64
CONV_HALO = 32
VMEM_LIMIT = 56 * 1024 * 1024
NEG_BIG = -1e30
LOG2E = 1.4426950408889634
ATTN_EXTRA_ROWS = 16
ATTN_HEADS_PER_STEP = 8
ATTN_Q_COLS = 256
MIX_ROWS_PER_GROUP = 256
ATTN_PIPE_LAG = 3


def _cparams(sem):
    return pltpu.CompilerParams(dimension_semantics=sem, vmem_limit_bytes=VMEM_LIMIT)


def _ln(v):
    mu = jnp.mean(v, axis=-1, keepdims=True)
    vc = v - mu
    var = jnp.mean(vc * vc, axis=-1, keepdims=True)
    return vc * lax.rsqrt(var + LN_EPS)


def _split3(v):
    hi = v.astype(BF16)
    r1 = v - hi.astype(F32)
    mid = r1.astype(BF16)
    lo = (r1 - mid.astype(F32)).astype(BF16)
    return hi, mid, lo


def _dot(a, b):
    return jnp.dot(a, b, preferred_element_type=F32)


def _dot_nt(a, b):
    return lax.dot_general(a, b, (((1,), (1,)), ((), ())), preferred_element_type=F32)


def _dot_tn(a, b):
    return lax.dot_general(a, b, (((0,), (0,)), ((), ())), preferred_element_type=F32)


def _store_packed_rows(ref, v, row0=0):
    n, d = v.shape
    half = d // 2
    vb = v.astype(BF16).astype(F32)
    lo_bits = lax.bitcast_convert_type(vb[:, :half], jnp.uint32)
    hi_bits = lax.bitcast_convert_type(vb[:, half:], jnp.uint32)
    words = (lo_bits >> 16) | (hi_bits & jnp.uint32(0xFFFF0000))
    for c in range(ROW_SLABS):
        ref[pl.ds(row0 * ROW_SLABS + c, n, stride=ROW_SLABS), :] = words[:, c * V7X_LANES:(c + 1) * V7X_LANES]


def _load_packed_rows(ref, n, lead=(), row0=0):
    lo, hi = [], []
    for c in range(ROW_SLABS):
        w = ref[lead + (pl.ds(row0 * ROW_SLABS + c, n, stride=ROW_SLABS), slice(None))]
        lo.append(lax.bitcast_convert_type(w << 16, F32))
        hi.append(lax.bitcast_convert_type(w & jnp.uint32(0xFFFF0000), F32))
    return lo + hi


def _const_spec(shape, single=False):
    nd = len(shape)
    if single:
        return pl.BlockSpec(shape, lambda *_: (0,) * nd, pipeline_mode=pl.Buffered(1))
    return pl.BlockSpec(shape, lambda *_: (0,) * nd)


def _ada_kernel(c_ref, w_ref, b_ref, o_ref):
    c = c_ref[...]
    cond = c * jax.nn.sigmoid(c)
    ch, cm, _ = _split3(cond)
    w = w_ref[...]
    wh, wm, _ = _split3(w)
    o_ref[...] = _dot(ch, wh) + _dot(ch, wm) + _dot(cm, wh) + b_ref[...]


def _ada(c_pad, w, b):
    rows, d = c_pad.shape
    n = w.shape[1]
    tn = ADA_COLS
    return pl.pallas_call(
        _ada_kernel,
        grid=(n // tn,),
        in_specs=[_const_spec((rows, d)),
                  pl.BlockSpec((d, tn), lambda j: (0, j)),
                  pl.BlockSpec((1, tn), lambda j: (0, j))],
        out_specs=pl.BlockSpec((rows, tn), lambda j: (0, j)),
        out_shape=jax.ShapeDtypeStruct((rows, n), F32),
        compiler_params=_cparams(("arbitrary",)),
        name="ada",
    )(c_pad, w, b)


def _inproj_kernel(x_ref, sc_ref, sh_ref, wglu_ref, wq_ref, wk_ref, wvt_ref, wf_ref, wga_ref, wgb_ref,
                   bf_ref, pq_ref, pk_ref, cw_ref, cb_ref, cg_ref, cbe_ref, wco_ref,
                   q_ref, k_ref, vt_ref, gya_ref, sgb_ref, carry_ref, halo_ref, ext_ref, shift_ref,
                   *, tiles_per_seq, conv_ch, chunk):
    i = pl.program_id(0)
    tm = x_ref.shape[0]

    @pl.when(i % tiles_per_seq == 0)
    def _():
        carry_ref[...] = jnp.zeros_like(carry_ref)
        halo_ref[...] = jnp.zeros_like(halo_ref)

    h = _ln(x_ref[...]) * (1.0 + sc_ref[0]) + sh_ref[0]
    hb = h.astype(BF16)

    glu = _dot(hb, wglu_ref[...])
    u = glu[:, :conv_ch] * jax.nn.sigmoid(glu[:, conv_ch:])

    ext_ref[0:CONV_HALO, :] = halo_ref[...]
    ext_ref[CONV_HALO:, :] = u
    halo_ref[...] = u[tm - CONV_HALO:, :]
    cw = cw_ref[...]
    off = CONV_HALO - (CONV_WIDTH - 1)
    span = tm + CONV_HALO - 8
    for res in range(1, 8):
        shift_ref[res - 1] = ext_ref[res:res + span, :]
    def conv_chunk(c0):
        acc = jnp.zeros((chunk, conv_ch), F32)
        for j in range(CONV_WIDTH):
            res, lo = (off + j) % 8, c0 + (off + j) // 8 * 8
            rows = ext_ref[lo:lo + chunk, :] if res == 0 else shift_ref[res - 1, lo:lo + chunk, :]
            acc = acc + cw[j:j + 1, :] * rows
        return acc

    n_chunks = tm // chunk
    per_gap = -(-n_chunks // 4)
    outs = []

    def conv_chunks():
        for _ in range(per_gap):
            if len(outs) < n_chunks:
                outs.append(conv_chunk(len(outs) * chunk))

    sgb_ref[...] = jax.nn.sigmoid(_dot(hb, wgb_ref[...])).astype(BF16)
    conv_chunks()

    f = _dot(hb, wf_ref[...]) + bf_ref[...]
    logf = jnp.minimum(f, 0.0) - jnp.log(1.0 + jnp.exp(-jnp.abs(f)))
    lh, lm, ll = _split3(logf)
    row = lax.broadcasted_iota(jnp.int32, (tm, tm), 0)
    col = lax.broadcasted_iota(jnp.int32, (tm, tm), 1)
    tri = jnp.where(row >= col, 1.0, 0.0).astype(BF16)
    cs = _dot(tri, lh) + _dot(tri, lm) + _dot(tri, ll)
    cum = cs + carry_ref[...]
    carry_ref[...] = cum[tm - 1:tm, :]
    conv_chunks()

    ch, cm, cl = _split3(cum * LOG2E)
    lane = lax.broadcasted_iota(jnp.int32, cum.shape, 1)
    nh = FOX_HEADS
    tail = jnp.where(lane == FORGET_ONES_LANE, 1.0, 0.0)
    pieces = jnp.where(lane < nh, ch.astype(F32), jnp.where(lane < 2 * nh, cm.astype(F32),
                       jnp.where(lane < 3 * nh, cl.astype(F32), tail))).astype(BF16)
    scale = FOX_HEAD_DIM ** -0.5 * LOG2E
    q_ref[...] = (_dot(hb, wq_ref[...]) * scale + _dot(pieces, pq_ref[...])).astype(BF16)
    conv_chunks()
    k_ref[...] = (_dot(hb, wk_ref[...]) + _dot(pieces, pk_ref[...])).astype(BF16)
    conv_chunks()
    vt_ref[0, 0] = _dot_nt(wvt_ref[...], hb).astype(BF16)

    v = jnp.concatenate(outs, axis=0) + cb_ref[...]
    v = _ln(v) * cg_ref[...] + cbe_ref[...]
    v = v * jax.nn.sigmoid(v)
    sga = jax.nn.sigmoid(_dot(hb, wga_ref[...]))
    gya_ref[...] = (sga * _dot(v.astype(BF16), wco_ref[...])).astype(BF16)


def _inproj(x2, scale1, shift1, wts, conv_wts, tm, seq):
    t, d = x2.shape
    tps = seq // tm
    bsz = t // seq
    wglu, wq, wk, wvt, wf, wga, wgb, bf, pq, pk = wts
    conv_ch = wglu.shape[1] // 2
    fw = wvt.shape[0]
    qw = wq.shape[1]
    mod_spec = pl.BlockSpec((1, 1, d), lambda i: (i // tps, 0, 0))
    row_spec = lambda n: pl.BlockSpec((tm, n), lambda i: (i, 0))
    consts = list(wts) + list(conv_wts)
    return pl.pallas_call(
        functools.partial(_inproj_kernel, tiles_per_seq=tps, conv_ch=conv_ch, chunk=CONV_CHUNK),
        grid=(t // tm,),
        in_specs=[row_spec(d), mod_spec, mod_spec] + [_const_spec(w.shape, single=True) for w in consts],
        out_specs=[row_spec(qw), row_spec(qw),
                   pl.BlockSpec((1, 1, fw, tm), lambda i: (i // tps, i % tps, 0, 0)),
                   row_spec(d), row_spec(d)],
        out_shape=[jax.ShapeDtypeStruct((t, qw), BF16),
                   jax.ShapeDtypeStruct((t, qw), BF16),
                   jax.ShapeDtypeStruct((bsz, tps, fw, tm), BF16),
                   jax.ShapeDtypeStruct((t, d), BF16),
                   jax.ShapeDtypeStruct((t, d), BF16)],
        scratch_shapes=[pltpu.VMEM((1, V7X_LANES), F32), pltpu.VMEM((CONV_HALO, conv_ch), F32),
                        pltpu.VMEM((tm + CONV_HALO, conv_ch), F32),
                        pltpu.VMEM((7, tm + CONV_HALO - 8, conv_ch), F32)],
        compiler_params=_cparams(("arbitrary",)),
        name="inproj",
    )(x2, scale1, shift1, *consts)


def _attn_kernel(q_ref, k_ref, vt_ref, o_ref, *, blk, heads):
    qi = pl.program_id(2)
    row = lax.broadcasted_iota(jnp.int32, (ATTN_EXTRA_ROWS, blk), 0)
    ones_rows = jnp.where(row == 0, 1.0, 0.0).astype(BF16)

    ncol = blk // ATTN_Q_COLS
    chains = [(j, c) for j in range(heads) for c in range(ncol)]

    def scores(kj, chain, masked):
        j, c = chain
        k = k_ref[pl.ds(pl.multiple_of(kj * blk, blk), blk), j * HEAD_PAD:(j + 1) * HEAD_PAD]
        q = q_ref[c * ATTN_Q_COLS:(c + 1) * ATTN_Q_COLS, j * HEAD_PAD:(j + 1) * HEAD_PAD]
        s = _dot_nt(k, q)
        if masked:
            kpos = lax.broadcasted_iota(jnp.int32, s.shape, 0)
            qpos = lax.broadcasted_iota(jnp.int32, s.shape, 1) + c * ATTN_Q_COLS
            s = jnp.where(kpos <= qpos, s, NEG_BIG)
        return s

    def probs(s, m):
        m_new = jnp.maximum(m, jnp.max(s, axis=0, keepdims=True))
        return jnp.exp2(s - m_new).astype(BF16), m_new

    def update(kj, chain, p, m, m_new, acc):
        j, _ = chain
        vt = vt_ref[kj, j * FOX_HEAD_DIM:(j + 1) * FOX_HEAD_DIM, :]
        lhs = jnp.concatenate([vt, ones_rows], axis=0)
        return jnp.exp2(m - m_new) * acc + _dot(lhs, p)

    def step(kj, carry, masked):
        n, lag = len(chains), ATTN_PIPE_LAG
        s, pm, out = {}, {}, [None] * n
        for i in range(n + lag):
            if i < n:
                s[i] = scores(kj, chains[i], masked)
            if lag - 1 <= i < n + lag - 1:
                pm[i - lag + 1] = probs(s.pop(i - lag + 1), carry[i - lag + 1][0])
            if i >= lag:
                p, m_new = pm.pop(i - lag)
                m, acc = carry[i - lag]
                out[i - lag] = (m_new, update(kj, chains[i - lag], p, m, m_new, acc))
        return tuple(out)

    init = tuple((jnp.full((1, ATTN_Q_COLS), NEG_BIG, F32),
                  jnp.zeros((FOX_HEAD_DIM + ATTN_EXTRA_ROWS, ATTN_Q_COLS), F32)) for _ in chains)
    carry = lax.fori_loop(0, qi, lambda kj, cr: step(kj, cr, False), init)
    carry = step(qi, carry, True)
    for (j, c), (_, acc) in zip(chains, carry):
        o_ref[j * FOX_HEAD_DIM:(j + 1) * FOX_HEAD_DIM, c * ATTN_Q_COLS:(c + 1) * ATTN_Q_COLS] = (
            acc[:FOX_HEAD_DIM] / acc[FOX_HEAD_DIM:FOX_HEAD_DIM + 1]).astype(BF16)


def _attn(q, k, vt, blk, heads):
    bsz, seq, _ = q.shape
    nkb = seq // blk
    return pl.pallas_call(
        functools.partial(_attn_kernel, blk=blk, heads=heads),
        grid=(bsz, FOX_HEADS // heads, seq // blk),
        in_specs=[pl.BlockSpec((None, blk, heads * HEAD_PAD), lambda b, h, i: (b, i, h)),
                  pl.BlockSpec((None, seq, heads * HEAD_PAD), lambda b, h, i: (b, 0, h),
                               pipeline_mode=pl.Buffered(1)),
                  pl.BlockSpec((None, nkb, heads * FOX_HEAD_DIM, blk), lambda b, h, i: (b, 0, h, 0),
                               pipeline_mode=pl.Buffered(1))],
        out_specs=pl.BlockSpec((None, heads * FOX_HEAD_DIM, blk), lambda b, h, i: (b, h, i)),
        out_shape=jax.ShapeDtypeStruct((bsz, FOX_HEADS * FOX_HEAD_DIM, seq), BF16),
        compiler_params=_cparams(("arbitrary", "arbitrary", "arbitrary")),
        name="attn",
    )(q, k, vt)


def _mix_kernel(ot_ref, gya_ref, sgb_ref, x_ref, g1_ref, sc2_ref, sh2_ref, lg_ref, lb_ref,
                wfox_ref, wmix_ref, wrh_ref, wrl_ref, wsg_ref, wsu_ref, wsd_ref,
                x1_ref, hp_ref, st_ref, shr_ref, *, alpha, sub):
    tm = x_ref.shape[0]
    groups = [pl.ds(r0, sub) for r0 in range(0, tm, sub)]
    ys = []
    for rows in groups:
        yb = _dot_tn(ot_ref[:, rows], wfox_ref[...])
        merged = gya_ref[rows, :].astype(F32) + sgb_ref[rows, :].astype(F32) * yb
        ys.append(_dot(merged.astype(BF16), wmix_ref[...]))
    for rows, y in zip(groups, ys):
        x1 = _ln(alpha * x_ref[rows, :] + (1.0 + g1_ref[0]) * y) * lg_ref[...] + lb_ref[...]
        x1_ref[rows, :] = x1
        h2 = _ln(x1) * (1.0 + sc2_ref[0]) + sh2_ref[0]
        hb = h2.astype(BF16)
        hl = (h2 - hb.astype(F32)).astype(BF16)
        _store_packed_rows(hp_ref, h2, row0=rows.start)

        logits_t = _dot_nt(wrh_ref[...], hb) + _dot_nt(wrl_ref[...], hb) + _dot_nt(wrh_ref[...], hl)
        st_ref[:, rows] = jax.nn.sigmoid(logits_t)

        g = _dot(hb, wsg_ref[...])
        u = _dot(hb, wsu_ref[...])
        a = (g * jax.nn.sigmoid(g) * u).astype(BF16)
        shr_ref[rows, :] = _dot(a, wsd_ref[...])


def _mix(ot, gya, sgb, x2, gate1, scale2, shift2, lg, lb, wts, tm, seq, alpha):
    t, d = x2.shape
    tps = seq // tm
    fw = ot.shape[1]
    n_exp = wts[2].shape[0]
    mod_spec = pl.BlockSpec((1, 1, d), lambda i: (i // tps, 0, 0))
    row_spec = lambda n: pl.BlockSpec((tm, n), lambda i: (i, 0))
    return pl.pallas_call(
        functools.partial(_mix_kernel, alpha=alpha, sub=min(MIX_ROWS_PER_GROUP, tm)),
        grid=(t // tm,),
        in_specs=[pl.BlockSpec((None, fw, tm), lambda i: (i // tps, 0, i % tps)),
                  row_spec(d), row_spec(d), row_spec(d), mod_spec, mod_spec, mod_spec,
                  _const_spec(lg.shape), _const_spec(lb.shape)] + [_const_spec(w.shape) for w in wts],
        out_specs=[row_spec(d), pl.BlockSpec((tm * ROW_SLABS, V7X_LANES), lambda i: (i, 0)),
                   pl.BlockSpec((n_exp, tm), lambda i: (0, i)), row_spec(d)],
        out_shape=[jax.ShapeDtypeStruct((t, d), F32),
                   jax.ShapeDtypeStruct((t * ROW_SLABS, V7X_LANES), jnp.uint32),
                   jax.ShapeDtypeStruct((n_exp, t), F32),
                   jax.ShapeDtypeStruct((t, d), F32)],
        compiler_params=_cparams(("arbitrary",)),
        name="mix",
    )(ot, gya, sgb, x2, gate1, scale2, shift2, lg, lb, *wts)


def _route_kernel(st_ref, bias_ref, idx_ref, wts_ref, rank_ref, cnt_ref, carry_ref):
    i = pl.program_id(0)
    n_exp, tr = st_ref.shape
    gsz = n_exp // N_GROUPS
    neg_inf = -jnp.inf

    @pl.when(i == 0)
    def _():
        carry_ref[...] = jnp.zeros_like(carry_ref)

    shape3 = (N_GROUPS, gsz, tr)

    def max01(v):
        return jnp.max(jnp.max(v, axis=0, keepdims=True), axis=1, keepdims=True)

    def min01(v):
        return jnp.min(jnp.min(v, axis=0, keepdims=True), axis=1, keepdims=True)

    def sum01(v):
        return jnp.sum(jnp.sum(v, axis=0, keepdims=True), axis=1, keepdims=True)

    sc = st_ref[...].reshape(shape3)
    gsel = (st_ref[...] + bias_ref[...]).reshape(shape3)
    pos = lax.broadcasted_iota(jnp.int32, shape3, 1)
    m1 = jnp.max(gsel, axis=1, keepdims=True)
    i1 = jnp.min(jnp.where(gsel == m1, pos, gsz), axis=1, keepdims=True)
    m2 = jnp.max(jnp.where(pos == i1, neg_inf, gsel), axis=1, keepdims=True)
    gs = m1 + m2

    gid = lax.broadcasted_iota(jnp.int32, gs.shape, 0)
    gkeep = jnp.zeros(gs.shape, F32)
    for _ in range(TOPK_GROUPS):
        mx = jnp.max(gs, axis=0, keepdims=True)
        gi = jnp.min(jnp.where(gs == mx, gid, N_GROUPS), axis=0, keepdims=True)
        hit = gid == gi
        gkeep = gkeep + jnp.where(hit, 1.0, 0.0)
        gs = jnp.where(hit, neg_inf, gs)

    cur0 = jnp.where(jnp.broadcast_to(gkeep, shape3) > 0.5, gsel, neg_inf)
    cur = cur0
    eid = lax.broadcasted_iota(jnp.int32, shape3, 0) * gsz + pos
    idxs, ws = [], []
    wsum = jnp.zeros((1, 1, tr), F32)
    for _ in range(TOP_K):
        mx = max01(cur)
        ik = min01(jnp.where(cur == mx, eid, n_exp))
        hit = eid == ik
        wk = sum01(jnp.where(hit, sc, 0.0))
        idxs.append(ik)
        ws.append(wk)
        wsum = wsum + wk
        cur = jnp.where(hit, neg_inf, cur)
    onehot = jnp.where(cur == cur0, 0.0, 1.0)

    ra = lax.broadcasted_iota(jnp.int32, (tr, tr), 0)
    rb = lax.broadcasted_iota(jnp.int32, (tr, tr), 1)
    upper = jnp.where(ra < rb, 1.0, 0.0).astype(BF16)
    onehot2 = onehot.reshape(n_exp, tr)
    prior = (_dot(onehot2.astype(BF16), upper) + carry_ref[...]).reshape(shape3)
    for slot in range(TOP_K):
        idx_ref[slot:slot + 1, :] = idxs[slot].reshape(1, tr)
        wts_ref[slot:slot + 1, :] = (ws[slot] / wsum * ROUTED_SCALE).reshape(1, tr)
        rk = sum01(jnp.where(eid == idxs[slot], prior, 0.0))
        rank_ref[slot:slot + 1, :] = rk.reshape(1, tr).astype(jnp.int32)
    total = carry_ref[...] + jnp.sum(onehot2, axis=1, keepdims=True)
    carry_ref[...] = total
    cnt_ref[...] = jnp.broadcast_to(total, cnt_ref.shape).astype(jnp.int32)


def _route(scores_t, bias_col, tr):
    n_exp, t = scores_t.shape
    slot_spec = pl.BlockSpec((TOP_K, tr), lambda i: (0, i))
    return pl.pallas_call(
        _route_kernel,
        grid=(t // tr,),
        in_specs=[pl.BlockSpec((n_exp, tr), lambda i: (0, i)), _const_spec(bias_col.shape)],
        out_specs=[slot_spec, slot_spec, slot_spec, _const_spec((n_exp, V7X_LANES))],
        out_shape=[jax.ShapeDtypeStruct((TOP_K, t), jnp.int32),
                   jax.ShapeDtypeStruct((TOP_K, t), F32),
                   jax.ShapeDtypeStruct((TOP_K, t), jnp.int32),
                   jax.ShapeDtypeStruct((n_exp, V7X_LANES), jnp.int32)],
        scratch_shapes=[pltpu.VMEM((n_exp, 1), F32)],
        compiler_params=_cparams(("arbitrary",)),
        name="route",
    )(scores_t, bias_col)


def _dest_kernel(pstart_ref, idx_ref, rank_ref, o_ref):
    n_exp = pstart_ref.shape[0]
    tr = idx_ref.shape[1]
    eid = lax.broadcasted_iota(jnp.int32, (n_exp, tr), 0)
    pstart = pstart_ref[...]
    for slot in range(TOP_K):
        hit = eid == idx_ref[slot:slot + 1, :]
        start = jnp.sum(jnp.where(hit, pstart, 0.0), axis=0, keepdims=True).astype(jnp.int32)
        o_ref[slot:slot + 1, :] = (start + rank_ref[slot:slot + 1, :]) * ROW_SLABS


def _dest(pstart, idx, rank, tr):
    k, t = idx.shape
    spec = pl.BlockSpec((k, tr), lambda i: (0, i))
    pstart_col = pstart.astype(F32)[:, None]
    return pl.pallas_call(
        _dest_kernel,
        grid=(t // tr,),
        in_specs=[_const_spec(pstart_col.shape), spec, spec],
        out_specs=spec,
        out_shape=jax.ShapeDtypeStruct((k, t), jnp.int32),
        compiler_params=_cparams(("arbitrary",)),
        name="dest",
    )(pstart_col, idx, rank)


def _dispatch_kernel(dest_ref, hp_ref, xs_ref, sem):
    td = hp_ref.shape[0] // ROW_SLABS

    def row_copy(r, slot):
        dst = pl.multiple_of(dest_ref[slot, r], ROW_SLABS)
        src = pl.multiple_of(r * ROW_SLABS, ROW_SLABS)
        return pltpu.make_async_copy(hp_ref.at[pl.ds(src, ROW_SLABS)], xs_ref.at[pl.ds(dst, ROW_SLABS)], sem)

    def start(r, c):
        for slot in range(TOP_K):
            row_copy(r, slot).start(priority=slot % 2)
        return c

    def wait(r, c):
        for slot in range(TOP_K):
            row_copy(r, slot).wait()
        return c

    lax.fori_loop(0, td, start, 0, unroll=4)
    lax.fori_loop(0, td, wait, 0, unroll=8)


def _dispatch(dest, hp, n_pad, td):
    t = hp.shape[0] // ROW_SLABS
    return pl.pallas_call(
        _dispatch_kernel,
        grid=(t // td,),
        in_specs=[pl.BlockSpec((TOP_K, td), lambda i: (0, i), memory_space=pltpu.SMEM),
                  pl.BlockSpec((td * ROW_SLABS, V7X_LANES), lambda i: (i, 0))],
        out_specs=pl.BlockSpec(memory_space=pl.ANY),
        out_shape=jax.ShapeDtypeStruct((n_pad * ROW_SLABS, V7X_LANES), hp.dtype),
        scratch_shapes=[pltpu.SemaphoreType.DMA(())],
        compiler_params=_cparams(("arbitrary",)),
        name="dispatch",
    )(dest, hp)


def _expert_kernel(be_ref, rows_ref, seq_ref, cnt_ref, xs_hbm, wg_hbm, wu_hbm, wd_hbm, ys_hbm,
                   xbuf, ybuf, wgf, wuf, wdf, wgu_s, wd_s, sem_x, sem_y, sem_w, *, layer):
    n = cnt_ref[0]
    n_seq = cnt_ref[1]
    blk_rows = EXPERT_BLOCK * ROW_SLABS
    nx, nw = EXPERT_X_AHEAD + 1, EXPERT_W_AHEAD + 1

    def block_rows(q):
        start = q * blk_rows
        return pl.ds(start if isinstance(q, int) else pl.multiple_of(start, blk_rows), blk_rows)

    def x_copy(q, slot):
        return pltpu.make_async_copy(xs_hbm.at[block_rows(q)], xbuf.at[slot], sem_x.at[slot])

    def y_copy(q, slot):
        return pltpu.make_async_copy(ybuf.at[slot], ys_hbm.at[block_rows(q)], sem_y.at[slot])

    def w_copies(k, ws):
        e = seq_ref[k]
        out = []
        for hbm, buf in ((wg_hbm, wgf), (wu_hbm, wuf), (wd_hbm, wdf)):
            half = buf.shape[1] // 2
            for part in range(2):
                rows = pl.ds(part * half, half)
                out.append((pltpu.make_async_copy(hbm.at[layer, e, rows], buf.at[ws, rows], sem_w.at[ws]), part))
        return out

    for j in range(EXPERT_X_AHEAD):
        @pl.when(j < n)
        def _():
            x_copy(j, j).start(priority=1)

    for j in range(EXPERT_W_AHEAD):
        @pl.when(j < n_seq)
        def _():
            for cp, prio in w_copies(j, j):
                cp.start(priority=prio)

    def body(q, k):
        slot = q % 2
        fresh = jnp.logical_or(q == 0, be_ref[q] != be_ref[jnp.maximum(q - 1, 0)])

        @pl.when(fresh)
        def _():
            ws = k % nw
            for cp, _ in w_copies(k, ws):
                cp.wait()
            fw = wd_s.shape[0]
            wgu_s[:, :fw] = wgf[ws].astype(BF16)
            wgu_s[:, fw:] = wuf[ws].astype(BF16)
            wd_s[...] = wdf[ws].astype(BF16)

            @pl.when(k + EXPERT_W_AHEAD < n_seq)
            def _():
                for cp, prio in w_copies(k + EXPERT_W_AHEAD, (k + EXPERT_W_AHEAD) % nw):
                    cp.start(priority=prio)

        @pl.when(q + EXPERT_X_AHEAD < n)
        def _():
            x_copy(q + EXPERT_X_AHEAD, (q + EXPERT_X_AHEAD) % nx).start(priority=1)

        x_copy(q, q % nx).wait()

        @pl.when(q >= 2)
        def _():
            y_copy(q - 2, slot).wait()

        half = EXPERT_BLOCK // 2
        f = wd_s.shape[0]
        gu = []
        for h in range(2):
            x = jnp.concatenate(_load_packed_rows(xbuf, half, lead=(q % nx,), row0=h * half), axis=1)
            valid = lax.broadcasted_iota(jnp.int32, x.shape, 0) + h * half < rows_ref[q]
            x = jnp.where(valid, x, 0.0).astype(BF16)
            gu.append(_dot(x, wgu_s[...]))
        for h in range(2):
            g, u = gu[h][:, :f], gu[h][:, f:]
            a = (g * jax.nn.sigmoid(g) * u).astype(BF16)
            _store_packed_rows(ybuf.at[slot], _dot(a, wd_s[...]), row0=h * half)
        y_copy(q, slot).start()
        return k + fresh.astype(jnp.int32)

    lax.fori_loop(0, n, body, jnp.int32(0))

    @pl.when(n >= 2)
    def _():
        y_copy(n - 2, n % 2).wait()

    y_copy(n - 1, (n - 1) % 2).wait()


def _experts(block_expert, block_rows, expert_seq, counts2, xs, wg, wu, wd, layer):
    _, n_exp, d, f = wg.shape
    blk = (EXPERT_BLOCK * ROW_SLABS, V7X_LANES)
    nx, nw = EXPERT_X_AHEAD + 1, EXPERT_W_AHEAD + 1
    any_spec = pl.BlockSpec(memory_space=pl.ANY)
    return pl.pallas_call(
        functools.partial(_expert_kernel, layer=layer),
        grid_spec=pltpu.PrefetchScalarGridSpec(
            num_scalar_prefetch=4, grid=(1,),
            in_specs=[any_spec, any_spec, any_spec, any_spec],
            out_specs=any_spec,
            scratch_shapes=[pltpu.VMEM((nx,) + blk, xs.dtype), pltpu.VMEM((2,) + blk, xs.dtype),
                            pltpu.VMEM((nw, d, f), F32), pltpu.VMEM((nw, d, f), F32), pltpu.VMEM((nw, f, d), F32),
                            pltpu.VMEM((d, 2 * f), BF16), pltpu.VMEM((f, d), BF16),
                            pltpu.SemaphoreType.DMA((nx,)), pltpu.SemaphoreType.DMA((2,)),
                            pltpu.SemaphoreType.DMA((nw,))]),
        out_shape=jax.ShapeDtypeStruct(xs.shape, xs.dtype),
        compiler_params=_cparams(("arbitrary",)),
        name="experts",
    )(block_expert, block_rows, expert_seq, counts2, xs, wg, wu, wd)


def _combine_kernel(d0_ref, d1_ref, d2_ref, ys_ref, w_ref, shr_ref, x1_ref, g2_ref, lg_ref, lb_ref, o_ref,
                    buf_a, buf_b, sem, *, alpha, tc):
    i = pl.program_id(0)
    last = pl.num_programs(0) - 1

    def row_copy(dref, r, slot, buf, sem_idx):
        src = pl.multiple_of(dref[slot, r], ROW_SLABS)
        dst = r * ROW_SLABS if isinstance(r, int) else pl.multiple_of(r * ROW_SLABS, ROW_SLABS)
        return pltpu.make_async_copy(ys_ref.at[pl.ds(src, ROW_SLABS)], buf.at[slot, pl.ds(dst, ROW_SLABS)],
                                     sem.at[sem_idx])

    def issue_unrolled(dref, buf, sem_idx):
        for r in range(tc):
            for slot in range(TOP_K):
                row_copy(dref, r, slot, buf, sem_idx).start(priority=slot % 2)

    def issue_loop(dref, buf, sem_idx):
        def start(r, c):
            for slot in range(TOP_K):
                row_copy(dref, r, slot, buf, sem_idx).start(priority=slot % 2)
            return c
        lax.fori_loop(0, tc, start, 0, unroll=4)

    def wait_all(dref, buf, sem_idx):
        def wait(r, c):
            for slot in range(TOP_K):
                row_copy(dref, r, slot, buf, sem_idx).wait()
            return c
        lax.fori_loop(0, tc, wait, 0, unroll=8)

    def reduce_tile(buf, rows):
        w = w_ref[rows, :]
        chunks = None
        for slot in range(TOP_K):
            wk = w[:, slot:slot + 1]
            part = [c * wk for c in _load_packed_rows(buf, tc, lead=(slot,))]
            chunks = part if chunks is None else [a + b for a, b in zip(chunks, part)]
        y = shr_ref[rows, :] + jnp.concatenate(chunks, axis=1)
        z = alpha * x1_ref[rows, :] + (1.0 + g2_ref[0]) * y
        o_ref[rows, :] = _ln(z) * lg_ref[...] + lb_ref[...]

    @pl.when(i == 0)
    def _():
        issue_loop(d0_ref, buf_a, 0)

    wait_all(d0_ref, buf_a, 0)
    issue_unrolled(d1_ref, buf_b, 1)
    reduce_tile(buf_a, pl.ds(0, tc))

    wait_all(d1_ref, buf_b, 1)
    issue_unrolled(d2_ref, buf_a, 0)
    reduce_tile(buf_b, pl.ds(tc, tc))

    @pl.when(i == last)
    def _():
        wait_all(d2_ref, buf_a, 0)


def _combine(dest, ys, wts_tk, shared, x1, gate2, lg, lb, tc, seq, alpha):
    t, d = x1.shape
    tps = seq // (2 * tc)
    n_tiles = t // tc
    row_spec = lambda n: pl.BlockSpec((2 * tc, n), lambda i: (i, 0))
    tile_dest = lambda f: pl.BlockSpec((TOP_K, tc), f, memory_space=pltpu.SMEM)
    buf = pltpu.VMEM((TOP_K, tc * ROW_SLABS, V7X_LANES), jnp.uint32)
    return pl.pallas_call(
        functools.partial(_combine_kernel, alpha=alpha, tc=tc),
        grid=(n_tiles // 2,),
        in_specs=[tile_dest(lambda i: (0, 2 * i)), tile_dest(lambda i: (0, 2 * i + 1)),
                  tile_dest(lambda i: (0, jnp.minimum(2 * i + 2, n_tiles - 1))),
                  pl.BlockSpec(memory_space=pl.ANY),
                  row_spec(TOP_K), row_spec(d), row_spec(d),
                  pl.BlockSpec((1, 1, d), lambda i: (i // tps, 0, 0)),
                  _const_spec(lg.shape), _const_spec(lb.shape)],
        out_specs=row_spec(d),
        out_shape=jax.ShapeDtypeStruct((t, d), F32),
        scratch_shapes=[buf, buf, pltpu.SemaphoreType.DMA((2,))],
        compiler_params=_cparams(("arbitrary",)),
        name="combine",
    )(dest, dest, dest, ys, wts_tk, shared, x1, gate2, lg, lb)


def _placement():
    pq = np.zeros((V7X_LANES, FOX_HEADS * HEAD_PAD), np.float32)
    pk = np.zeros((V7X_LANES, FOX_HEADS * HEAD_PAD), np.float32)
    for h in range(FOX_HEADS):
        base = h * HEAD_PAD + FOX_HEAD_DIM
        for piece in range(3):
            pq[piece * FOX_HEADS + h, base + piece] = 1.0
            pk[FORGET_ONES_LANE, base + piece] = 1.0
            pq[FORGET_ONES_LANE, base + 3 + piece] = 1.0
            pk[piece * FOX_HEADS + h, base + 3 + piece] = -1.0
    return jnp.asarray(pq, BF16), jnp.asarray(pk, BF16)


def _inproj_weights(w_in, b_forget, d):
    conv2 = d
    fw = FOX_HEADS * FOX_HEAD_DIM
    o1, o2, o3, o4 = conv2, conv2 + fw, conv2 + 2 * fw, conv2 + 3 * fw
    o5 = o4 + FOX_HEADS
    o6 = o5 + d

    def pad_heads(w):
        w = w.reshape(d, FOX_HEADS, FOX_HEAD_DIM)
        w = jnp.pad(w, ((0, 0), (0, 0), (0, HEAD_PAD - FOX_HEAD_DIM)))
        return w.reshape(d, FOX_HEADS * HEAD_PAD).astype(BF16)

    wglu = w_in[:, :o1].astype(BF16)
    wq = pad_heads(w_in[:, o1:o2])
    wk = pad_heads(w_in[:, o2:o3])
    wvt = w_in[:, o3:o4].T.astype(BF16)
    wf8 = w_in[:, o4:o5]
    wf = jnp.pad(jnp.concatenate([wf8, wf8, wf8], axis=1), ((0, 0), (0, V7X_LANES - 3 * FOX_HEADS))).astype(BF16)
    bf = jnp.pad(jnp.concatenate([b_forget, b_forget, b_forget]), (0, V7X_LANES - 3 * FOX_HEADS))[None, :].astype(F32)
    wga = w_in[:, o5:o6].astype(BF16)
    wgb = w_in[:, o6:].astype(BF16)
    pq, pk = _placement()
    return (wglu, wq, wk, wvt, wf, wga, wgb, bf, pq, pk)


def _tile_sizes(seq, t):
    return {"inproj": min(512, seq),
            "mix": min(2 * MIX_ROWS_PER_GROUP, seq),
            "route": min(512, t),
            "dispatch": min(256, t),
            "combine": min(128, seq)}


def _layer(x2, ada, bsz, seq, w_in, b_forget, conv_w, conv_b, conv_ln_g, conv_ln_b, w_conv_out, w_fox_out,
           w_mix_out, ln1_g, ln1_b, w_router, router_bias, w_exp_gate, w_exp_up, w_exp_down,
           w_sh_gate, w_sh_up, w_sh_down, ln2_g, ln2_b, depth, layer):
    t, d = x2.shape
    n_exp = w_router.shape[1]
    alpha = (2.0 * depth) ** 0.25
    mods = [ada[:bsz, j * d:(j + 1) * d][:, None, :] for j in range(6)]
    shift1, scale1, gate1, shift2, scale2, gate2 = mods

    tiles = _tile_sizes(seq, t)
    tm = tiles["inproj"]
    conv_w_pad = jnp.pad(conv_w, ((0, CONV_HALO - CONV_WIDTH), (0, 0)))
    conv_wts = (conv_w_pad, conv_b[None, :], conv_ln_g[None, :], conv_ln_b[None, :], w_conv_out.astype(BF16))
    q, k, vt, gya, sgb = _inproj(x2, scale1, shift1, _inproj_weights(w_in, b_forget, d), conv_wts, tm, seq)

    ot = _attn(q.reshape(bsz, seq, -1), k.reshape(bsz, seq, -1), vt, tm, ATTN_HEADS_PER_STEP)

    tmx = tiles["mix"]
    wr_t = w_router.T
    wr_h = wr_t.astype(BF16)
    wr_l = (wr_t - wr_h.astype(F32)).astype(BF16)
    mix_w = (w_fox_out.astype(BF16), w_mix_out.astype(BF16), wr_h, wr_l,
             w_sh_gate.astype(BF16), w_sh_up.astype(BF16), w_sh_down.astype(BF16))
    x1, hp, scores_t, shared = _mix(ot, gya, sgb, x2, gate1, scale2, shift2, ln1_g[None, :], ln1_b[None, :],
                                    mix_w, tmx, seq, alpha)

    tr = tiles["route"]
    idx, wts, rank, cnt = _route(scores_t, router_bias[:, None], tr)

    counts = cnt[:, 0]
    padded = (counts + EXPERT_BLOCK - 1) // EXPERT_BLOCK * EXPERT_BLOCK
    pend = jnp.cumsum(padded)
    pstart = (pend - padded).astype(jnp.int32)
    n_assign = t * TOP_K
    n_pad = -(-(n_assign + n_exp * (EXPERT_BLOCK - 1)) // EXPERT_BLOCK) * EXPERT_BLOCK
    n_blocks = n_pad // EXPERT_BLOCK
    block_start = jnp.arange(n_blocks, dtype=jnp.int32) * EXPERT_BLOCK
    block_expert = jnp.minimum(jnp.sum(pend[None, :] <= block_start[:, None], axis=1), n_exp - 1).astype(jnp.int32)
    block_rows = jnp.clip((pstart + counts)[block_expert] - block_start, 0, EXPERT_BLOCK).astype(jnp.int32)
    n_used = (pend[-1:] // EXPERT_BLOCK).astype(jnp.int32)
    owns = counts > 0
    expert_seq = jnp.nonzero(owns, size=n_exp, fill_value=0)[0].astype(jnp.int32)
    counts2 = jnp.concatenate([n_used, jnp.sum(owns, dtype=jnp.int32)[None]])

    dest = _dest(pstart, idx, rank, tr)
    xs = _dispatch(dest, hp, n_pad, tiles["dispatch"])
    ys = _experts(block_expert, block_rows, expert_seq, counts2, xs, w_exp_gate, w_exp_up, w_exp_down, layer)
    return _combine(dest, ys, wts.T, shared, x1, gate2, ln2_g[None, :], ln2_b[None, :],
                    tiles["combine"], seq, alpha)


def kernel(x, c, w_ada, b_ada, w_in, b_forget, conv_w, conv_b, conv_ln_g, conv_ln_b, w_conv_out, w_fox_out,
           w_mix_out, ln1_g, ln1_b, w_router, router_bias, w_exp_gate, w_exp_up, w_exp_down, w_sh_gate,
           w_sh_up, w_sh_down, ln2_g, ln2_b):
    bsz, seq, d = x.shape
    depth = w_ada.shape[0]
    c_pad = jnp.pad(c, ((0, -bsz % V7X_SUBLANES), (0, 0)))
    x2 = x.reshape(bsz * seq, d)
    for l in range(depth):
        ada = _ada(c_pad, w_ada[l], b_ada[l][None, :])
        x2 = _layer(x2, ada, bsz, seq, w_in[l], b_forget[l], conv_w[l], conv_b[l], conv_ln_g[l], conv_ln_b[l],
                    w_conv_out[l], w_fox_out[l], w_mix_out[l], ln1_g[l], ln1_b[l], w_router[l], router_bias[l],
                    w_exp_gate, w_exp_up, w_exp_down, w_sh_gate[l], w_sh_up[l], w_sh_down[l],
                    ln2_g[l], ln2_b[l], depth, l)
    return x2.reshape(bsz, seq, d)
```

```python
import functools

import jax
import jax.numpy as jnp
import numpy as np
from jax import lax
from jax.experimental import pallas as pl
from jax.experimental.pallas import tpu as pltpu

F32 = jnp.float32
BF16 = jnp.bfloat16

LN_EPS = 1e-5
CONV_WIDTH = 31
FOX_HEADS = 8
FOX_HEAD_DIM = 64
N_GROUPS = 8
TOPK_GROUPS = 4
TOP_K = 8
ROUTED_SCALE = 2.5
EXPERT_BLOCK = 256
EXPERT_X_AHEAD = 4
EXPERT_W_AHEAD = 4
EXPERT_W_PARTS = 4
ROW_SLABS = 4

V7X_LANES = 128
HEAD_PAD = 128
FORGET_ONES_LANE = 3 * FOX_HEADS
V7X_SUBLANES = 8
ADA_COLS = 1024
CONV_CHUNK = 64
CONV_HALO = 32
VMEM_LIMIT = 56 * 1024 * 1024
NEG_BIG = -1e30
LOG2E = 1.4426950408889634
ATTN_EXTRA_ROWS = 16
ATTN_HEADS_PER_STEP = 8
ATTN_Q_COLS = 256
MIX_ROWS_PER_GROUP = 256
ATTN_PIPE_LAG = 3


def _cparams(sem):
    return pltpu.CompilerParams(dimension_semantics=sem, vmem_limit_bytes=VMEM_LIMIT)


def _ln(v):
    mu = jnp.mean(v, axis=-1, keepdims=True)
    vc = v - mu
    var = jnp.mean(vc * vc, axis=-1, keepdims=True)
    return vc * lax.rsqrt(var + LN_EPS)


def _split3(v):
    hi = v.astype(BF16)
    r1 = v - hi.astype(F32)
    mid = r1.astype(BF16)
    lo = (r1 - mid.astype(F32)).astype(BF16)
    return hi, mid, lo


def _dot(a, b):
    return jnp.dot(a, b, preferred_element_type=F32)


def _dot_nt(a, b):
    return lax.dot_general(a, b, (((1,), (1,)), ((), ())), preferred_element_type=F32)


def _dot_tn(a, b):
    return lax.dot_general(a, b, (((0,), (0,)), ((), ())), preferred_element_type=F32)


def _store_packed_rows(ref, v, row0=0):
    n, d = v.shape
    half = d // 2
    vb = v.astype(BF16).astype(F32)
    lo_bits = lax.bitcast_convert_type(vb[:, :half], jnp.uint32)
    hi_bits = lax.bitcast_convert_type(vb[:, half:], jnp.uint32)
    words = (lo_bits >> 16) | (hi_bits & jnp.uint32(0xFFFF0000))
    for c in range(ROW_SLABS):
        ref[pl.ds(row0 * ROW_SLABS + c, n, stride=ROW_SLABS), :] = words[:, c * V7X_LANES:(c + 1) * V7X_LANES]


def _load_packed_rows(ref, n, lead=(), row0=0):
    lo, hi = [], []
    for c in range(ROW_SLABS):
        w = ref[lead + (pl.ds(row0 * ROW_SLABS + c, n, stride=ROW_SLABS), slice(None))]
        lo.append(lax.bitcast_convert_type(w << 16, F32))
        hi.append(lax.bitcast_convert_type(w & jnp.uint32(0xFFFF0000), F32))
    return lo + hi


def _const_spec(shape, single=False):
    nd = len(shape)
    if single:
        return pl.BlockSpec(shape, lambda *_: (0,) * nd, pipeline_mode=pl.Buffered(1))
    return pl.BlockSpec(shape, lambda *_: (0,) * nd)


def _ada_kernel(c_ref, w_ref, b_ref, o_ref):
    c = c_ref[...]
    cond = c * jax.nn.sigmoid(c)
    ch, cm, _ = _split3(cond)
    w = w_ref[...]
    wh, wm, _ = _split3(w)
    o_ref[...] = _dot(ch, wh) + _dot(ch, wm) + _dot(cm, wh) + b_ref[...]


def _ada(c_pad, w, b):
    rows, d = c_pad.shape
    n = w.shape[1]
    tn = ADA_COLS
    return pl.pallas_call(
        _ada_kernel,
        grid=(n // tn,),
        in_specs=[_const_spec((rows, d)),
                  pl.BlockSpec((d, tn), lambda j: (0, j)),
                  pl.BlockSpec((1, tn), lambda j: (0, j))],
        out_specs=pl.BlockSpec((rows, tn), lambda j: (0, j)),
        out_shape=jax.ShapeDtypeStruct((rows, n), F32),
        compiler_params=_cparams(("arbitrary",)),
        name="ada",
    )(c_pad, w, b)


def _inproj_kernel(x_ref, sc_ref, sh_ref, wglu_ref, wq_ref, wk_ref, wvt_ref, wf_ref, wga_ref, wgb_ref,
                   bf_ref, pq_ref, pk_ref, cw_ref, cb_ref, cg_ref, cbe_ref, wco_ref,
                   q_ref, k_ref, vt_ref, gya_ref, sgb_ref, carry_ref, halo_ref, ext_ref, shift_ref,
                   *, tiles_per_seq, conv_ch, chunk):
    i = pl.program_id(0)
    tm = x_ref.shape[0]

    @pl.when(i % tiles_per_seq == 0)
    def _():
        carry_ref[...] = jnp.zeros_like(carry_ref)
        halo_ref[...] = jnp.zeros_like(halo_ref)

    h = _ln(x_ref[...]) * (1.0 + sc_ref[0]) + sh_ref[0]
    hb = h.astype(BF16)

    glu = _dot(hb, wglu_ref[...])
    u = glu[:, :conv_ch] * jax.nn.sigmoid(glu[:, conv_ch:])

    ext_ref[0:CONV_HALO, :] = halo_ref[...]
    ext_ref[CONV_HALO:, :] = u
    halo_ref[...] = u[tm - CONV_HALO:, :]
    cw = cw_ref[...]
    off = CONV_HALO - (CONV_WIDTH - 1)
    span = tm + CONV_HALO - 8
    for res in range(1, 8):
        shift_ref[res - 1] = ext_ref[res:res + span, :]
    def conv_chunk(c0):
        acc = jnp.zeros((chunk, conv_ch), F32)
        for j in range(CONV_WIDTH):
            res, lo = (off + j) % 8, c0 + (off + j) // 8 * 8
            rows = ext_ref[lo:lo + chunk, :] if res == 0 else shift_ref[res - 1, lo:lo + chunk, :]
            acc = acc + cw[j:j + 1, :] * rows
        return acc

    n_chunks = tm // chunk
    per_gap = -(-n_chunks // 4)
    outs = []

    def conv_chunks():
        for _ in range(per_gap):
            if len(outs) < n_chunks:
                outs.append(conv_chunk(len(outs) * chunk))

    sgb_ref[...] = jax.nn.sigmoid(_dot(hb, wgb_ref[...])).astype(BF16)
    conv_chunks()

    f = _dot(hb, wf_ref[...]) + bf_ref[...]
    logf = jnp.minimum(f, 0.0) - jnp.log(1.0 + jnp.exp(-jnp.abs(f)))
    lh, lm, ll = _split3(logf)
    row = lax.broadcasted_iota(jnp.int32, (tm, tm), 0)
    col = lax.broadcasted_iota(jnp.int32, (tm, tm), 1)
    tri = jnp.where(row >= col, 1.0, 0.0).astype(BF16)
    cs = _dot(tri, lh) + _dot(tri, lm) + _dot(tri, ll)
    cum = cs + carry_ref[...]
    carry_ref[...] = cum[tm - 1:tm, :]
    conv_chunks()

    ch, cm, cl = _split3(cum * LOG2E)
    lane = lax.broadcasted_iota(jnp.int32, cum.shape, 1)
    nh = FOX_HEADS
    tail = jnp.where(lane == FORGET_ONES_LANE, 1.0, 0.0)
    pieces = jnp.where(lane < nh, ch.astype(F32), jnp.where(lane < 2 * nh, cm.astype(F32),
                       jnp.where(lane < 3 * nh, cl.astype(F32), tail))).astype(BF16)
    scale = FOX_HEAD_DIM ** -0.5 * LOG2E
    q_ref[...] = (_dot(hb, wq_ref[...]) * scale + _dot(pieces, pq_ref[...])).astype(BF16)
    conv_chunks()
    k_ref[...] = (_dot(hb, wk_ref[...]) + _dot(pieces, pk_ref[...])).astype(BF16)
    conv_chunks()
    vt_ref[0, 0] = _dot_nt(wvt_ref[...], hb).astype(BF16)

    v = jnp.concatenate(outs, axis=0) + cb_ref[...]
    v = _ln(v) * cg_ref[...] + cbe_ref[...]
    v = v * jax.nn.sigmoid(v)
    sga = jax.nn.sigmoid(_dot(hb, wga_ref[...]))
    gya_ref[...] = (sga * _dot(v.astype(BF16), wco_ref[...])).astype(BF16)


def _inproj(x2, scale1, shift1, wts, conv_wts, tm, seq):
    t, d = x2.shape
    tps = seq // tm
    bsz = t // seq
    wglu, wq, wk, wvt, wf, wga, wgb, bf, pq, pk = wts
    conv_ch = wglu.shape[1] // 2
    fw = wvt.shape[0]
    qw = wq.shape[1]
    mod_spec = pl.BlockSpec((1, 1, d), lambda i: (i // tps, 0, 0))
    row_spec = lambda n: pl.BlockSpec((tm, n), lambda i: (i, 0))
    consts = list(wts) + list(conv_wts)
    return pl.pallas_call(
        functools.partial(_inproj_kernel, tiles_per_seq=tps, conv_ch=conv_ch, chunk=CONV_CHUNK),
        grid=(t // tm,),
        in_specs=[row_spec(d), mod_spec, mod_spec] + [_const_spec(w.shape, single=True) for w in consts],
        out_specs=[row_spec(qw), row_spec(qw),
                   pl.BlockSpec((1, 1, fw, tm), lambda i: (i // tps, i % tps, 0, 0)),
                   row_spec(d), row_spec(d)],
        out_shape=[jax.ShapeDtypeStruct((t, qw), BF16),
                   jax.ShapeDtypeStruct((t, qw), BF16),
                   jax.ShapeDtypeStruct((bsz, tps, fw, tm), BF16),
                   jax.ShapeDtypeStruct((t, d), BF16),
                   jax.ShapeDtypeStruct((t, d), BF16)],
        scratch_shapes=[pltpu.VMEM((1, V7X_LANES), F32), pltpu.VMEM((CONV_HALO, conv_ch), F32),
                        pltpu.VMEM((tm + CONV_HALO, conv_ch), F32),
                        pltpu.VMEM((7, tm + CONV_HALO - 8, conv_ch), F32)],
        compiler_params=_cparams(("arbitrary",)),
        name="inproj",
    )(x2, scale1, shift1, *consts)


def _attn_kernel(q_ref, k_ref, vt_ref, o_ref, *, blk, heads):
    qi = pl.program_id(2)
    row = lax.broadcasted_iota(jnp.int32, (ATTN_EXTRA_ROWS, blk), 0)
    ones_rows = jnp.where(row == 0, 1.0, 0.0).astype(BF16)

    ncol = blk // ATTN_Q_COLS
    chains = [(j, c) for j in range(heads) for c in range(ncol)]

    def scores(kj, chain, masked):
        j, c = chain
        k = k_ref[pl.ds(pl.multiple_of(kj * blk, blk), blk), j * HEAD_PAD:(j + 1) * HEAD_PAD]
        q = q_ref[c * ATTN_Q_COLS:(c + 1) * ATTN_Q_COLS, j * HEAD_PAD:(j + 1) * HEAD_PAD]
        s = _dot_nt(k, q)
        if masked:
            kpos = lax.broadcasted_iota(jnp.int32, s.shape, 0)
            qpos = lax.broadcasted_iota(jnp.int32, s.shape, 1) + c * ATTN_Q_COLS
            s = jnp.where(kpos <= qpos, s, NEG_BIG)
        return s

    def probs(s, m):
        m_new = jnp.maximum(m, jnp.max(s, axis=0, keepdims=True))
        return jnp.exp2(s - m_new).astype(BF16), m_new

    def update(kj, chain, p, m, m_new, acc):
        j, _ = chain
        vt = vt_ref[kj, j * FOX_HEAD_DIM:(j + 1) * FOX_HEAD_DIM, :]
        lhs = jnp.concatenate([vt, ones_rows], axis=0)
        return jnp.exp2(m - m_new) * acc + _dot(lhs, p)

    def step(kj, carry, masked):
        n, lag = len(chains), ATTN_PIPE_LAG
        s, pm, out = {}, {}, [None] * n
        for i in range(n + lag):
            if i < n:
                s[i] = scores(kj, chains[i], masked)
            if lag - 1 <= i < n + lag - 1:
                pm[i - lag + 1] = probs(s.pop(i - lag + 1), carry[i - lag + 1][0])
            if i >= lag:
                p, m_new = pm.pop(i - lag)
                m, acc = carry[i - lag]
                out[i - lag] = (m_new, update(kj, chains[i - lag], p, m, m_new, acc))
        return tuple(out)

    init = tuple((jnp.full((1, ATTN_Q_COLS), NEG_BIG, F32),
                  jnp.zeros((FOX_HEAD_DIM + ATTN_EXTRA_ROWS, ATTN_Q_COLS), F32)) for _ in chains)
    carry = lax.fori_loop(0, qi, lambda kj, cr: step(kj, cr, False), init)
    carry = step(qi, carry, True)
    for (j, c), (_, acc) in zip(chains, carry):
        o_ref[j * FOX_HEAD_DIM:(j + 1) * FOX_HEAD_DIM, c * ATTN_Q_COLS:(c + 1) * ATTN_Q_COLS] = (
            acc[:FOX_HEAD_DIM] / acc[FOX_HEAD_DIM:FOX_HEAD_DIM + 1]).astype(BF16)


def _attn(q, k, vt, blk, heads):
    bsz, seq, _ = q.shape
    nkb = seq // blk
    return pl.pallas_call(
        functools.partial(_attn_kernel, blk=blk, heads=heads),
        grid=(bsz, FOX_HEADS // heads, seq // blk),
        in_specs=[pl.BlockSpec((None, blk, heads * HEAD_PAD), lambda b, h, i: (b, i, h)),
                  pl.BlockSpec((None, seq, heads * HEAD_PAD), lambda b, h, i: (b, 0, h),
                               pipeline_mode=pl.Buffered(1)),
                  pl.BlockSpec((None, nkb, heads * FOX_HEAD_DIM, blk), lambda b, h, i: (b, 0, h, 0),
                               pipeline_mode=pl.Buffered(1))],
        out_specs=pl.BlockSpec((None, heads * FOX_HEAD_DIM, blk), lambda b, h, i: (b, h, i)),
        out_shape=jax.ShapeDtypeStruct((bsz, FOX_HEADS * FOX_HEAD_DIM, seq), BF16),
        compiler_params=_cparams(("arbitrary", "arbitrary", "arbitrary")),
        name="attn",
    )(q, k, vt)


def _mix_kernel(ot_ref, gya_ref, sgb_ref, x_ref, g1_ref, sc2_ref, sh2_ref, lg_ref, lb_ref,
                wfox_ref, wmix_ref, wrh_ref, wrl_ref, wsg_ref, wsu_ref, wsd_ref,
                x1_ref, hp_ref, st_ref, shr_ref, *, alpha, sub):
    tm = x_ref.shape[0]
    groups = [pl.ds(r0, sub) for r0 in range(0, tm, sub)]
    ys = []
    for rows in groups:
        yb = _dot_tn(ot_ref[:, rows], wfox_ref[...])
        merged = gya_ref[rows, :].astype(F32) + sgb_ref[rows, :].astype(F32) * yb
        ys.append(_dot(merged.astype(BF16), wmix_ref[...]))
    for rows, y in zip(groups, ys):
        x1 = _ln(alpha * x_ref[rows, :] + (1.0 + g1_ref[0]) * y) * lg_ref[...] + lb_ref[...]
        x1_ref[rows, :] = x1
        h2 = _ln(x1) * (1.0 + sc2_ref[0]) + sh2_ref[0]
        hb = h2.astype(BF16)
        hl = (h2 - hb.astype(F32)).astype(BF16)
        _store_packed_rows(hp_ref, h2, row0=rows.start)

        logits_t = _dot_nt(wrh_ref[...], hb) + _dot_nt(wrl_ref[...], hb) + _dot_nt(wrh_ref[...], hl)
        st_ref[:, rows] = jax.nn.sigmoid(logits_t)

        g = _dot(hb, wsg_ref[...])
        u = _dot(hb, wsu_ref[...])
        a = (g * jax.nn.sigmoid(g) * u).astype(BF16)
        shr_ref[rows, :] = _dot(a, wsd_ref[...])


def _mix(ot, gya, sgb, x2, gate1, scale2, shift2, lg, lb, wts, tm, seq, alpha):
    t, d = x2.shape
    tps = seq // tm
    fw = ot.shape[1]
    n_exp = wts[2].shape[0]
    mod_spec = pl.BlockSpec((1, 1, d), lambda i: (i // tps, 0, 0))
    row_spec = lambda n: pl.BlockSpec((tm, n), lambda i: (i, 0))
    return pl.pallas_call(
        functools.partial(_mix_kernel, alpha=alpha, sub=min(MIX_ROWS_PER_GROUP, tm)),
        grid=(t // tm,),
        in_specs=[pl.BlockSpec((None, fw, tm), lambda i: (i // tps, 0, i % tps)),
                  row_spec(d), row_spec(d), row_spec(d), mod_spec, mod_spec, mod_spec,
                  _const_spec(lg.shape), _const_spec(lb.shape)] + [_const_spec(w.shape) for w in wts],
        out_specs=[row_spec(d), pl.BlockSpec((tm * ROW_SLABS, V7X_LANES), lambda i: (i, 0)),
                   pl.BlockSpec((n_exp, tm), lambda i: (0, i)), row_spec(d)],
        out_shape=[jax.ShapeDtypeStruct((t, d), F32),
                   jax.ShapeDtypeStruct((t * ROW_SLABS, V7X_LANES), jnp.uint32),
                   jax.ShapeDtypeStruct((n_exp, t), F32),
                   jax.ShapeDtypeStruct((t, d), F32)],
        compiler_params=_cparams(("arbitrary",)),
        name="mix",
    )(ot, gya, sgb, x2, gate1, scale2, shift2, lg, lb, *wts)


def _route_kernel(st_ref, bias_ref, idx_ref, wts_ref, rank_ref, cnt_ref, carry_ref):
    i = pl.program_id(0)
    n_exp, tr = st_ref.shape
    gsz = n_exp // N_GROUPS
    neg_inf = -jnp.inf

    @pl.when(i == 0)
    def _():
        carry_ref[...] = jnp.zeros_like(carry_ref)

    shape3 = (N_GROUPS, gsz, tr)

    def max01(v):
        return jnp.max(jnp.max(v, axis=0, keepdims=True), axis=1, keepdims=True)

    def min01(v):
        return jnp.min(jnp.min(v, axis=0, keepdims=True), axis=1, keepdims=True)

    def sum01(v):
        return jnp.sum(jnp.sum(v, axis=0, keepdims=True), axis=1, keepdims=True)

    sc = st_ref[...].reshape(shape3)
    gsel = (st_ref[...] + bias_ref[...]).reshape(shape3)
    pos = lax.broadcasted_iota(jnp.int32, shape3, 1)
    m1 = jnp.max(gsel, axis=1, keepdims=True)
    i1 = jnp.min(jnp.where(gsel == m1, pos, gsz), axis=1, keepdims=True)
    m2 = jnp.max(jnp.where(pos == i1, neg_inf, gsel), axis=1, keepdims=True)
    gs = m1 + m2

    gid = lax.broadcasted_iota(jnp.int32, gs.shape, 0)
    gkeep = jnp.zeros(gs.shape, F32)
    for _ in range(TOPK_GROUPS):
        mx = jnp.max(gs, axis=0, keepdims=True)
        gi = jnp.min(jnp.where(gs == mx, gid, N_GROUPS), axis=0, keepdims=True)
        hit = gid == gi
        gkeep = gkeep + jnp.where(hit, 1.0, 0.0)
        gs = jnp.where(hit, neg_inf, gs)

    cur0 = jnp.where(jnp.broadcast_to(gkeep, shape3) > 0.5, gsel, neg_inf)
    cur = cur0
    eid = lax.broadcasted_iota(jnp.int32, shape3, 0) * gsz + pos
    idxs, ws = [], []
    wsum = jnp.zeros((1, 1, tr), F32)
    for _ in range(TOP_K):
        mx = max01(cur)
        ik = min01(jnp.where(cur == mx, eid, n_exp))
        hit = eid == ik
        wk = sum01(jnp.where(hit, sc, 0.0))
        idxs.append(ik)
        ws.append(wk)
        wsum = wsum + wk
        cur = jnp.where(hit, neg_inf, cur)
    onehot = jnp.where(cur == cur0, 0.0, 1.0)

    ra = lax.broadcasted_iota(jnp.int32, (tr, tr), 0)
    rb = lax.broadcasted_iota(jnp.int32, (tr, tr), 1)
    upper = jnp.where(ra < rb, 1.0, 0.0).astype(BF16)
    onehot2 = onehot.reshape(n_exp, tr)
    prior = (_dot(onehot2.astype(BF16), upper) + carry_ref[...]).reshape(shape3)
    for slot in range(TOP_K):
        idx_ref[slot:slot + 1, :] = idxs[slot].reshape(1, tr)
        wts_ref[slot:slot + 1, :] = (ws[slot] / wsum * ROUTED_SCALE).reshape(1, tr)
        rk = sum01(jnp.where(eid == idxs[slot], prior, 0.0))
        rank_ref[slot:slot + 1, :] = rk.reshape(1, tr).astype(jnp.int32)
    total = carry_ref[...] + jnp.sum(onehot2, axis=1, keepdims=True)
    carry_ref[...] = total
    cnt_ref[...] = jnp.broadcast_to(total, cnt_ref.shape).astype(jnp.int32)


def _route(scores_t, bias_col, tr):
    n_exp, t = scores_t.shape
    slot_spec = pl.BlockSpec((TOP_K, tr), lambda i: (0, i))
    return pl.pallas_call(
        _route_kernel,
        grid=(t // tr,),
        in_specs=[pl.BlockSpec((n_exp, tr), lambda i: (0, i)), _const_spec(bias_col.shape)],
        out_specs=[slot_spec, slot_spec, slot_spec, _const_spec((n_exp, V7X_LANES))],
        out_shape=[jax.ShapeDtypeStruct((TOP_K, t), jnp.int32),
                   jax.ShapeDtypeStruct((TOP_K, t), F32),
                   jax.ShapeDtypeStruct((TOP_K, t), jnp.int32),
                   jax.ShapeDtypeStruct((n_exp, V7X_LANES), jnp.int32)],
        scratch_shapes=[pltpu.VMEM((n_exp, 1), F32)],
        compiler_params=_cparams(("arbitrary",)),
        name="route",
    )(scores_t, bias_col)


def _dest_kernel(pstart_ref, idx_ref, rank_ref, o_ref):
    n_exp = pstart_ref.shape[0]
    tr = idx_ref.shape[1]
    eid = lax.broadcasted_iota(jnp.int32, (n_exp, tr), 0)
    pstart = pstart_ref[...]
    for slot in range(TOP_K):
        hit = eid == idx_ref[slot:slot + 1, :]
        start = jnp.sum(jnp.where(hit, pstart, 0.0), axis=0, keepdims=True).astype(jnp.int32)
        o_ref[slot:slot + 1, :] = (start + rank_ref[slot:slot + 1, :]) * ROW_SLABS


def _dest(pstart, idx, rank, tr):
    k, t = idx.shape
    spec = pl.BlockSpec((k, tr), lambda i: (0, i))
    pstart_col = pstart.astype(F32)[:, None]
    return pl.pallas_call(
        _dest_kernel,
        grid=(t // tr,),
        in_specs=[_const_spec(pstart_col.shape), spec, spec],
        out_specs=spec,
        out_shape=jax.ShapeDtypeStruct((k, t), jnp.int32),
        compiler_params=_cparams(("arbitrary",)),
        name="dest",
    )(pstart_col, idx, rank)


def _dispatch_kernel(dest_ref, hp_ref, xs_ref, sem):
    td = hp_ref.shape[0] // ROW_SLABS

    def row_copy(r, slot):
        dst = pl.multiple_of(dest_ref[slot, r], ROW_SLABS)
        src = pl.multiple_of(r * ROW_SLABS, ROW_SLABS)
        return pltpu.make_async_copy(hp_ref.at[pl.ds(src, ROW_SLABS)], xs_ref.at[pl.ds(dst, ROW_SLABS)], sem)

    def start(r, c):
        for slot in range(TOP_K):
            row_copy(r, slot).start(priority=slot % 2)
        return c

    def wait(r, c):
        for slot in range(TOP_K):
            row_copy(r, slot).wait()
        return c

    lax.fori_loop(0, td, start, 0, unroll=4)
    lax.fori_loop(0, td, wait, 0, unroll=8)


def _dispatch(dest, hp, n_pad, td):
    t = hp.shape[0] // ROW_SLABS
    return pl.pallas_call(
        _dispatch_kernel,
        grid=(t // td,),
        in_specs=[pl.BlockSpec((TOP_K, td), lambda i: (0, i), memory_space=pltpu.SMEM),
                  pl.BlockSpec((td * ROW_SLABS, V7X_LANES), lambda i: (i, 0))],
        out_specs=pl.BlockSpec(memory_space=pl.ANY),
        out_shape=jax.ShapeDtypeStruct((n_pad * ROW_SLABS, V7X_LANES), hp.dtype),
        scratch_shapes=[pltpu.SemaphoreType.DMA(())],
        compiler_params=_cparams(("arbitrary",)),
        name="dispatch",
    )(dest, hp)


def _expert_kernel(be_ref, rows_ref, seq_ref, cnt_ref, xs_hbm, wg_hbm, wu_hbm, wd_hbm, ys_hbm,
                   xbuf, ybuf, wgf, wuf, wdf, wgu_s, wd_s, sem_x, sem_y, sem_w, *, layer):
    n = cnt_ref[0]
    n_seq = cnt_ref[1]
    blk_rows = EXPERT_BLOCK * ROW_SLABS
    nx, nw = EXPERT_X_AHEAD + 1, EXPERT_W_AHEAD + 1

    def block_rows(q):
        start = q * blk_rows
        return pl.ds(start if isinstance(q, int) else pl.multiple_of(start, blk_rows), blk_rows)

    def x_copy(q, slot):
        return pltpu.make_async_copy(xs_hbm.at[block_rows(q)], xbuf.at[slot], sem_x.at[slot])

    def y_copy(q, slot):
        return pltpu.make_async_copy(ybuf.at[slot], ys_hbm.at[block_rows(q)], sem_y.at[slot])

    def w_copies(k, ws):
        e = seq_ref[k]
        out = []
        for hbm, buf in ((wg_hbm, wgf), (wu_hbm, wuf), (wd_hbm, wdf)):
            band = buf.shape[1] // EXPERT_W_PARTS
            for part in range(EXPERT_W_PARTS):
                rows = pl.ds(part * band, band)
                out.append((pltpu.make_async_copy(hbm.at[layer, e, rows], buf.at[ws, rows], sem_w.at[ws]),
                            part % 2))
        return out

    for j in range(EXPERT_X_AHEAD):
        @pl.when(j < n)
        def _():
            x_copy(j, j).start(priority=1)

    for j in range(EXPERT_W_AHEAD):
        @pl.when(j < n_seq)
        def _():
            for cp, prio in w_copies(j, j):
                cp.start(priority=prio)

    def body(q, k):
        slot = q % 2
        fresh = jnp.logical_or(q == 0, be_ref[q] != be_ref[jnp.maximum(q - 1, 0)])

        @pl.when(fresh)
        def _():
            ws = k % nw
            for cp, _ in w_copies(k, ws):
                cp.wait()
            fw = wd_s.shape[0]
            wgu_s[:, :fw] = wgf[ws].astype(BF16)
            wgu_s[:, fw:] = wuf[ws].astype(BF16)
            wd_s[...] = wdf[ws].astype(BF16)

            @pl.when(k + EXPERT_W_AHEAD < n_seq)
            def _():
                for cp, prio in w_copies(k + EXPERT_W_AHEAD, (k + EXPERT_W_AHEAD) % nw):
                    cp.start(priority=prio)

        @pl.when(q + EXPERT_X_AHEAD < n)
        def _():
            x_copy(q + EXPERT_X_AHEAD, (q + EXPERT_X_AHEAD) % nx).start(priority=1)

        x_copy(q, q % nx).wait()

        @pl.when(q >= 2)
        def _():
            y_copy(q - 2, slot).wait()

        half = EXPERT_BLOCK // 2
        f = wd_s.shape[0]
        gu = []
        for h in range(2):
            x = jnp.concatenate(_load_packed_rows(xbuf, half, lead=(q % nx,), row0=h * half), axis=1)
            valid = lax.broadcasted_iota(jnp.int32, x.shape, 0) + h * half < rows_ref[q]
            x = jnp.where(valid, x, 0.0).astype(BF16)
            gu.append(_dot(x, wgu_s[...]))
        for h in range(2):
            g, u = gu[h][:, :f], gu[h][:, f:]
            a = (g * jax.nn.sigmoid(g) * u).astype(BF16)
            _store_packed_rows(ybuf.at[slot], _dot(a, wd_s[...]), row0=h * half)
        y_copy(q, slot).start()
        return k + fresh.astype(jnp.int32)

    lax.fori_loop(0, n, body, jnp.int32(0))

    @pl.when(n >= 2)
    def _():
        y_copy(n - 2, n % 2).wait()

    y_copy(n - 1, (n - 1) % 2).wait()


def _experts(block_expert, block_rows, expert_seq, counts2, xs, wg, wu, wd, layer):
    _, n_exp, d, f = wg.shape
    blk = (EXPERT_BLOCK * ROW_SLABS, V7X_LANES)
    nx, nw = EXPERT_X_AHEAD + 1, EXPERT_W_AHEAD + 1
    any_spec = pl.BlockSpec(memory_space=pl.ANY)
    return pl.pallas_call(
        functools.partial(_expert_kernel, layer=layer),
        grid_spec=pltpu.PrefetchScalarGridSpec(
            num_scalar_prefetch=4, grid=(1,),
            in_specs=[any_spec, any_spec, any_spec, any_spec],
            out_specs=any_spec,
            scratch_shapes=[pltpu.VMEM((nx,) + blk, xs.dtype), pltpu.VMEM((2,) + blk, xs.dtype),
                            pltpu.VMEM((nw, d, f), F32), pltpu.VMEM((nw, d, f), F32), pltpu.VMEM((nw, f, d), F32),
                            pltpu.VMEM((d, 2 * f), BF16), pltpu.VMEM((f, d), BF16),
                            pltpu.SemaphoreType.DMA((nx,)), pltpu.SemaphoreType.DMA((2,)),
                            pltpu.SemaphoreType.DMA((nw,))]),
        out_shape=jax.ShapeDtypeStruct(xs.shape, xs.dtype),
        compiler_params=_cparams(("arbitrary",)),
        name="experts",
    )(block_expert, block_rows, expert_seq, counts2, xs, wg, wu, wd)


def _combine_kernel(d0_ref, d1_ref, d2_ref, ys_ref, w_ref, shr_ref, x1_ref, g2_ref, lg_ref, lb_ref, o_ref,
                    buf_a, buf_b, sem, *, alpha, tc):
    i = pl.program_id(0)
    last = pl.num_programs(0) - 1

    def row_copy(dref, r, slot, buf, sem_idx):
        src = pl.multiple_of(dref[slot, r], ROW_SLABS)
        dst = r * ROW_SLABS if isinstance(r, int) else pl.multiple_of(r * ROW_SLABS, ROW_SLABS)
        return pltpu.make_async_copy(ys_ref.at[pl.ds(src, ROW_SLABS)], buf.at[slot, pl.ds(dst, ROW_SLABS)],
                                     sem.at[sem_idx])

    def issue_unrolled(dref, buf, sem_idx):
        for r in range(tc):
            for slot in range(TOP_K):
                row_copy(dref, r, slot, buf, sem_idx).start(priority=slot % 2)

    def issue_loop(dref, buf, sem_idx):
        def start(r, c):
            for slot in range(TOP_K):
                row_copy(dref, r, slot, buf, sem_idx).start(priority=slot % 2)
            return c
        lax.fori_loop(0, tc, start, 0, unroll=4)

    def wait_all(dref, buf, sem_idx):
        def wait(r, c):
            for slot in range(TOP_K):
                row_copy(dref, r, slot, buf, sem_idx).wait()
            return c
        lax.fori_loop(0, tc, wait, 0, unroll=8)

    def reduce_tile(buf, rows):
        w = w_ref[rows, :]
        chunks = None
        for slot in range(TOP_K):
            wk = w[:, slot:slot + 1]
            part = [c * wk for c in _load_packed_rows(buf, tc, lead=(slot,))]
            chunks = part if chunks is None else [a + b for a, b in zip(chunks, part)]
        y = shr_ref[rows, :] + jnp.concatenate(chunks, axis=1)
        z = alpha * x1_ref[rows, :] + (1.0 + g2_ref[0]) * y
        o_ref[rows, :] = _ln(z) * lg_ref[...] + lb_ref[...]

    @pl.when(i == 0)
    def _():
        issue_loop(d0_ref, buf_a, 0)

    wait_all(d0_ref, buf_a, 0)
    issue_unrolled(d1_ref, buf_b, 1)
    reduce_tile(buf_a, pl.ds(0, tc))

    wait_all(d1_ref, buf_b, 1)
    issue_unrolled(d2_ref, buf_a, 0)
    reduce_tile(buf_b, pl.ds(tc, tc))

    @pl.when(i == last)
    def _():
        wait_all(d2_ref, buf_a, 0)


def _combine(dest, ys, wts_tk, shared, x1, gate2, lg, lb, tc, seq, alpha):
    t, d = x1.shape
    tps = seq // (2 * tc)
    n_tiles = t // tc
    row_spec = lambda n: pl.BlockSpec((2 * tc, n), lambda i: (i, 0))
    tile_dest = lambda f: pl.BlockSpec((TOP_K, tc), f, memory_space=pltpu.SMEM)
    buf = pltpu.VMEM((TOP_K, tc * ROW_SLABS, V7X_LANES), jnp.uint32)
    return pl.pallas_call(
        functools.partial(_combine_kernel, alpha=alpha, tc=tc),
        grid=(n_tiles // 2,),
        in_specs=[tile_dest(lambda i: (0, 2 * i)), tile_dest(lambda i: (0, 2 * i + 1)),
                  tile_dest(lambda i: (0, jnp.minimum(2 * i + 2, n_tiles - 1))),
                  pl.BlockSpec(memory_space=pl.ANY),
                  row_spec(TOP_K), row_spec(d), row_spec(d),
                  pl.BlockSpec((1, 1, d), lambda i: (i // tps, 0, 0)),
                  _const_spec(lg.shape), _const_spec(lb.shape)],
        out_specs=row_spec(d),
        out_shape=jax.ShapeDtypeStruct((t, d), F32),
        scratch_shapes=[buf, buf, pltpu.SemaphoreType.DMA((2,))],
        compiler_params=_cparams(("arbitrary",)),
        name="combine",
    )(dest, dest, dest, ys, wts_tk, shared, x1, gate2, lg, lb)


def _placement():
    pq = np.zeros((V7X_LANES, FOX_HEADS * HEAD_PAD), np.float32)
    pk = np.zeros((V7X_LANES, FOX_HEADS * HEAD_PAD), np.float32)
    for h in range(FOX_HEADS):
        base = h * HEAD_PAD + FOX_HEAD_DIM
        for piece in range(3):
            pq[piece * FOX_HEADS + h, base + piece] = 1.0
            pk[FORGET_ONES_LANE, base + piece] = 1.0
            pq[FORGET_ONES_LANE, base + 3 + piece] = 1.0
            pk[piece * FOX_HEADS + h, base + 3 + piece] = -1.0
    return jnp.asarray(pq, BF16), jnp.asarray(pk, BF16)


def _inproj_weights(w_in, b_forget, d):
    conv2 = d
    fw = FOX_HEADS * FOX_HEAD_DIM
    o1, o2, o3, o4 = conv2, conv2 + fw, conv2 + 2 * fw, conv2 + 3 * fw
    o5 = o4 + FOX_HEADS
    o6 = o5 + d

    def pad_heads(w):
        w = w.reshape(d, FOX_HEADS, FOX_HEAD_DIM)
        w = jnp.pad(w, ((0, 0), (0, 0), (0, HEAD_PAD - FOX_HEAD_DIM)))
        return w.reshape(d, FOX_HEADS * HEAD_PAD).astype(BF16)

    wglu = w_in[:, :o1].astype(BF16)
    wq = pad_heads(w_in[:, o1:o2])
    wk = pad_heads(w_in[:, o2:o3])
    wvt = w_in[:, o3:o4].T.astype(BF16)
    wf8 = w_in[:, o4:o5]
    wf = jnp.pad(jnp.concatenate([wf8, wf8, wf8], axis=1), ((0, 0), (0, V7X_LANES - 3 * FOX_HEADS))).astype(BF16)
    bf = jnp.pad(jnp.concatenate([b_forget, b_forget, b_forget]), (0, V7X_LANES - 3 * FOX_HEADS))[None, :].astype(F32)
    wga = w_in[:, o5:o6].astype(BF16)
    wgb = w_in[:, o6:].astype(BF16)
    pq, pk = _placement()
    return (wglu, wq, wk, wvt, wf, wga, wgb, bf, pq, pk)


def _tile_sizes(seq, t):
    return {"inproj": min(512, seq),
            "mix": min(2 * MIX_ROWS_PER_GROUP, seq),
            "route": min(512, t),
            "dispatch": min(256, t),
            "combine": min(128, seq)}


def _layer(x2, ada, bsz, seq, w_in, b_forget, conv_w, conv_b, conv_ln_g, conv_ln_b, w_conv_out, w_fox_out,
           w_mix_out, ln1_g, ln1_b, w_router, router_bias, w_exp_gate, w_exp_up, w_exp_down,
           w_sh_gate, w_sh_up, w_sh_down, ln2_g, ln2_b, depth, layer):
    t, d = x2.shape
    n_exp = w_router.shape[1]
    alpha = (2.0 * depth) ** 0.25
    mods = [ada[:bsz, j * d:(j + 1) * d][:, None, :] for j in range(6)]
    shift1, scale1, gate1, shift2, scale2, gate2 = mods

    tiles = _tile_sizes(seq, t)
    tm = tiles["inproj"]
    conv_w_pad = jnp.pad(conv_w, ((0, CONV_HALO - CONV_WIDTH), (0, 0)))
    conv_wts = (conv_w_pad, conv_b[None, :], conv_ln_g[None, :], conv_ln_b[None, :], w_conv_out.astype(BF16))
    q, k, vt, gya, sgb = _inproj(x2, scale1, shift1, _inproj_weights(w_in, b_forget, d), conv_wts, tm, seq)

    ot = _attn(q.reshape(bsz, seq, -1), k.reshape(bsz, seq, -1), vt, tm, ATTN_HEADS_PER_STEP)

    tmx = tiles["mix"]
    wr_t = w_router.T
    wr_h = wr_t.astype(BF16)
    wr_l = (wr_t - wr_h.astype(F32)).astype(BF16)
    mix_w = (w_fox_out.astype(BF16), w_mix_out.astype(BF16), wr_h, wr_l,
             w_sh_gate.astype(BF16), w_sh_up.astype(BF16), w_sh_down.astype(BF16))
    x1, hp, scores_t, shared = _mix(ot, gya, sgb, x2, gate1, scale2, shift2, ln1_g[None, :], ln1_b[None, :],
                                    mix_w, tmx, seq, alpha)

    tr = tiles["route"]
    idx, wts, rank, cnt = _route(scores_t, router_bias[:, None], tr)

    counts = cnt[:, 0]
    padded = (counts + EXPERT_BLOCK - 1) // EXPERT_BLOCK * EXPERT_BLOCK
    pend = jnp.cumsum(padded)
    pstart = (pend - padded).astype(jnp.int32)
    n_assign = t * TOP_K
    n_pad = -(-(n_assign + n_exp * (EXPERT_BLOCK - 1)) // EXPERT_BLOCK) * EXPERT_BLOCK
    n_blocks = n_pad // EXPERT_BLOCK
    block_start = jnp.arange(n_blocks, dtype=jnp.int32) * EXPERT_BLOCK
    block_expert = jnp.minimum(jnp.sum(pend[None, :] <= block_start[:, None], axis=1), n_exp - 1).astype(jnp.int32)
    block_rows = jnp.clip((pstart + counts)[block_expert] - block_start, 0, EXPERT_BLOCK).astype(jnp.int32)
    n_used = (pend[-1:] // EXPERT_BLOCK).astype(jnp.int32)
    owns = counts > 0
    expert_seq = jnp.nonzero(owns, size=n_exp, fill_value=0)[0].astype(jnp.int32)
    counts2 = jnp.concatenate([n_used, jnp.sum(owns, dtype=jnp.int32)[None]])

    dest = _dest(pstart, idx, rank, tr)
    xs = _dispatch(dest, hp, n_pad, tiles["dispatch"])
    ys = _experts(block_expert, block_rows, expert_seq, counts2, xs, w_exp_gate, w_exp_up, w_exp_down, layer)
    return _combine(dest, ys, wts.T, shared, x1, gate2, ln2_g[None, :], ln2_b[None, :],
                    tiles["combine"], seq, alpha)


def kernel(x, c, w_ada, b_ada, w_in, b_forget, conv_w, conv_b, conv_ln_g, conv_ln_b, w_conv_out, w_fox_out,
           w_mix_out, ln1_g, ln1_b, w_router, router_bias, w_exp_gate, w_exp_up, w_exp_down, w_sh_gate,
           w_sh_up, w_sh_down, ln2_g, ln2_b):
    bsz, seq, d = x.shape
    depth = w_ada.shape[0]
    c_pad = jnp.pad(c, ((0, -bsz % V7X_SUBLANES), (0, 0)))
    x2 = x.reshape(bsz * seq, d)
    for l in range(depth):
        ada = _ada(c_pad, w_ada[l], b_ada[l][None, :])
        x2 = _layer(x2, ada, bsz, seq, w_in[l], b_forget[l], conv_w[l], conv_b[l], conv_ln_g[l], conv_ln_b[l],
                    w_conv_out[l], w_fox_out[l], w_mix_out[l], ln1_g[l], ln1_b[l], w_router[l], router_bias[l],
                    w_exp_gate, w_exp_up, w_exp_down, w_sh_gate[l], w_sh_up[l], w_sh_down[l],
                    ln2_g[l], ln2_b[l], depth, l)
    return x2.reshape(bsz, seq, d)
```

```python
import functools

import jax
import jax.numpy as jnp
import numpy as np
from jax import lax
from jax.experimental import pallas as pl
from jax.experimental.pallas import tpu as pltpu

F32 = jnp.float32
BF16 = jnp.bfloat16

LN_EPS = 1e-5
CONV_WIDTH = 31
FOX_HEADS = 8
FOX_HEAD_DIM = 64
N_GROUPS = 8
TOPK_GROUPS = 4
TOP_K = 8
ROUTED_SCALE = 2.5
EXPERT_BLOCK = 256
EXPERT_X_AHEAD = 4
EXPERT_W_AHEAD = 4
EXPERT_W_PARTS = 4
ROW_SLABS = 4

V7X_LANES = 128
HEAD_PAD = 128
FORGET_ONES_LANE = 3 * FOX_HEADS
V7X_SUBLANES = 8
ADA_COLS = 1024
CONV_CHUNK = 64
CONV_HALO = 32
VMEM_LIMIT = 56 * 1024 * 1024
NEG_BIG = -1e30
LOG2E = 1.4426950408889634
ATTN_EXTRA_ROWS = 16
ATTN_HEADS_PER_STEP = 8
ATTN_Q_COLS = 256
MIX_ROWS_PER_GROUP = 256
ATTN_PIPE_LAG = 4


def _cparams(sem):
    return pltpu.CompilerParams(dimension_semantics=sem, vmem_limit_bytes=VMEM_LIMIT)


def _ln(v):
    mu = jnp.mean(v, axis=-1, keepdims=True)
    vc = v - mu
    var = jnp.mean(vc * vc, axis=-1, keepdims=True)
    return vc * lax.rsqrt(var + LN_EPS)


def _split3(v):
    hi = v.astype(BF16)
    r1 = v - hi.astype(F32)
    mid = r1.astype(BF16)
    lo = (r1 - mid.astype(F32)).astype(BF16)
    return hi, mid, lo


def _dot(a, b):
    return jnp.dot(a, b, preferred_element_type=F32)


def _dot_nt(a, b):
    return lax.dot_general(a, b, (((1,), (1,)), ((), ())), preferred_element_type=F32)


def _dot_tn(a, b):
    return lax.dot_general(a, b, (((0,), (0,)), ((), ())), preferred_element_type=F32)


def _store_packed_rows(ref, v, row0=0):
    n, d = v.shape
    half = d // 2
    vb = v.astype(BF16).astype(F32)
    lo_bits = lax.bitcast_convert_type(vb[:, :half], jnp.uint32)
    hi_bits = lax.bitcast_convert_type(vb[:, half:], jnp.uint32)
    words = (lo_bits >> 16) | (hi_bits & jnp.uint32(0xFFFF0000))
    for c in range(ROW_SLABS):
        ref[pl.ds(row0 * ROW_SLABS + c, n, stride=ROW_SLABS), :] = words[:, c * V7X_LANES:(c + 1) * V7X_LANES]


def _load_packed_rows(ref, n, lead=(), row0=0):
    lo, hi = [], []
    for c in range(ROW_SLABS):
        w = ref[lead + (pl.ds(row0 * ROW_SLABS + c, n, stride=ROW_SLABS), slice(None))]
        lo.append(lax.bitcast_convert_type(w << 16, F32))
        hi.append(lax.bitcast_convert_type(w & jnp.uint32(0xFFFF0000), F32))
    return lo + hi


def _const_spec(shape, single=False):
    nd = len(shape)
    if single:
        return pl.BlockSpec(shape, lambda *_: (0,) * nd, pipeline_mode=pl.Buffered(1))
    return pl.BlockSpec(shape, lambda *_: (0,) * nd)


def _ada_kernel(c_ref, w_ref, b_ref, o_ref):
    c = c_ref[...]
    cond = c * jax.nn.sigmoid(c)
    ch, cm, _ = _split3(cond)
    w = w_ref[...]
    wh, wm, _ = _split3(w)
    o_ref[...] = _dot(ch, wh) + _dot(ch, wm) + _dot(cm, wh) + b_ref[...]


def _ada(c_pad, w, b):
    rows, d = c_pad.shape
    n = w.shape[1]
    tn = ADA_COLS
    return pl.pallas_call(
        _ada_kernel,
        grid=(n // tn,),
        in_specs=[_const_spec((rows, d)),
                  pl.BlockSpec((d, tn), lambda j: (0, j)),
                  pl.BlockSpec((1, tn), lambda j: (0, j))],
        out_specs=pl.BlockSpec((rows, tn), lambda j: (0, j)),
        out_shape=jax.ShapeDtypeStruct((rows, n), F32),
        compiler_params=_cparams(("arbitrary",)),
        name="ada",
    )(c_pad, w, b)


def _inproj_kernel(x_ref, sc_ref, sh_ref, wglu_ref, wq_ref, wk_ref, wvt_ref, wf_ref, wga_ref, wgb_ref,
                   bf_ref, pq_ref, pk_ref, cw_ref, cb_ref, cg_ref, cbe_ref, wco_ref,
                   q_ref, k_ref, vt_ref, gya_ref, sgb_ref, carry_ref, halo_ref, ext_ref, shift_ref,
                   *, tiles_per_seq, conv_ch, chunk):
    i = pl.program_id(0)
    tm = x_ref.shape[0]

    @pl.when(i % tiles_per_seq == 0)
    def _():
        carry_ref[...] = jnp.zeros_like(carry_ref)
        halo_ref[...] = jnp.zeros_like(halo_ref)

    h = _ln(x_ref[...]) * (1.0 + sc_ref[0]) + sh_ref[0]
    hb = h.astype(BF16)

    glu = _dot(hb, wglu_ref[...])
    u = glu[:, :conv_ch] * jax.nn.sigmoid(glu[:, conv_ch:])

    ext_ref[0:CONV_HALO, :] = halo_ref[...]
    ext_ref[CONV_HALO:, :] = u
    halo_ref[...] = u[tm - CONV_HALO:, :]
    cw = cw_ref[...]
    off = CONV_HALO - (CONV_WIDTH - 1)
    span = tm + CONV_HALO - 8
    for res in range(1, 8):
        shift_ref[res - 1] = ext_ref[res:res + span, :]
    def conv_chunk(c0):
        acc = jnp.zeros((chunk, conv_ch), F32)
        for j in range(CONV_WIDTH):
            res, lo = (off + j) % 8, c0 + (off + j) // 8 * 8
            rows = ext_ref[lo:lo + chunk, :] if res == 0 else shift_ref[res - 1, lo:lo + chunk, :]
            acc = acc + cw[j:j + 1, :] * rows
        return acc

    n_chunks = tm // chunk
    per_gap = -(-n_chunks // 4)
    outs = []

    def conv_chunks():
        for _ in range(per_gap):
            if len(outs) < n_chunks:
                outs.append(conv_chunk(len(outs) * chunk))

    sgb_ref[...] = jax.nn.sigmoid(_dot(hb, wgb_ref[...])).astype(BF16)
    conv_chunks()

    f = _dot(hb, wf_ref[...]) + bf_ref[...]
    logf = jnp.minimum(f, 0.0) - jnp.log(1.0 + jnp.exp(-jnp.abs(f)))
    lh, lm, ll = _split3(logf)
    row = lax.broadcasted_iota(jnp.int32, (tm, tm), 0)
    col = lax.broadcasted_iota(jnp.int32, (tm, tm), 1)
    tri = jnp.where(row >= col, 1.0, 0.0).astype(BF16)
    cs = _dot(tri, lh) + _dot(tri, lm) + _dot(tri, ll)
    cum = cs + carry_ref[...]
    carry_ref[...] = cum[tm - 1:tm, :]
    conv_chunks()

    ch, cm, cl = _split3(cum * LOG2E)
    lane = lax.broadcasted_iota(jnp.int32, cum.shape, 1)
    nh = FOX_HEADS
    tail = jnp.where(lane == FORGET_ONES_LANE, 1.0, 0.0)
    pieces = jnp.where(lane < nh, ch.astype(F32), jnp.where(lane < 2 * nh, cm.astype(F32),
                       jnp.where(lane < 3 * nh, cl.astype(F32), tail))).astype(BF16)
    scale = FOX_HEAD_DIM ** -0.5 * LOG2E
    q_ref[...] = (_dot(hb, wq_ref[...]) * scale + _dot(pieces, pq_ref[...])).astype(BF16)
    conv_chunks()
    k_ref[...] = (_dot(hb, wk_ref[...]) + _dot(pieces, pk_ref[...])).astype(BF16)
    conv_chunks()
    vt_ref[0, 0] = _dot_nt(wvt_ref[...], hb).astype(BF16)

    v = jnp.concatenate(outs, axis=0) + cb_ref[...]
    v = _ln(v) * cg_ref[...] + cbe_ref[...]
    v = v * jax.nn.sigmoid(v)
    sga = jax.nn.sigmoid(_dot(hb, wga_ref[...]))
    gya_ref[...] = (sga * _dot(v.astype(BF16), wco_ref[...])).astype(BF16)


def _inproj(x2, scale1, shift1, wts, conv_wts, tm, seq):
    t, d = x2.shape
    tps = seq // tm
    bsz = t // seq
    wglu, wq, wk, wvt, wf, wga, wgb, bf, pq, pk = wts
    conv_ch = wglu.shape[1] // 2
    fw = wvt.shape[0]
    qw = wq.shape[1]
    mod_spec = pl.BlockSpec((1, 1, d), lambda i: (i // tps, 0, 0))
    row_spec = lambda n: pl.BlockSpec((tm, n), lambda i: (i, 0))
    consts = list(wts) + list(conv_wts)
    return pl.pallas_call(
        functools.partial(_inproj_kernel, tiles_per_seq=tps, conv_ch=conv_ch, chunk=CONV_CHUNK),
        grid=(t // tm,),
        in_specs=[row_spec(d), mod_spec, mod_spec] + [_const_spec(w.shape, single=True) for w in consts],
        out_specs=[row_spec(qw), row_spec(qw),
                   pl.BlockSpec((1, 1, fw, tm), lambda i: (i // tps, i % tps, 0, 0)),
                   row_spec(d), row_spec(d)],
        out_shape=[jax.ShapeDtypeStruct((t, qw), BF16),
                   jax.ShapeDtypeStruct((t, qw), BF16),
                   jax.ShapeDtypeStruct((bsz, tps, fw, tm), BF16),
                   jax.ShapeDtypeStruct((t, d), BF16),
                   jax.ShapeDtypeStruct((t, d), BF16)],
        scratch_shapes=[pltpu.VMEM((1, V7X_LANES), F32), pltpu.VMEM((CONV_HALO, conv_ch), F32),
                        pltpu.VMEM((tm + CONV_HALO, conv_ch), F32),
                        pltpu.VMEM((7, tm + CONV_HALO - 8, conv_ch), F32)],
        compiler_params=_cparams(("arbitrary",)),
        name="inproj",
    )(x2, scale1, shift1, *consts)


def _attn_kernel(q_ref, k_ref, vt_ref, o_ref, *, blk, heads):
    qi = pl.program_id(2)
    row = lax.broadcasted_iota(jnp.int32, (ATTN_EXTRA_ROWS, blk), 0)
    ones_rows = jnp.where(row == 0, 1.0, 0.0).astype(BF16)

    ncol = blk // ATTN_Q_COLS
    chains = [(j, c) for j in range(heads) for c in range(ncol)]

    def scores(kj, chain, masked):
        j, c = chain
        k = k_ref[pl.ds(pl.multiple_of(kj * blk, blk), blk), j * HEAD_PAD:(j + 1) * HEAD_PAD]
        q = q_ref[c * ATTN_Q_COLS:(c + 1) * ATTN_Q_COLS, j * HEAD_PAD:(j + 1) * HEAD_PAD]
        s = _dot_nt(k, q)
        if masked:
            kpos = lax.broadcasted_iota(jnp.int32, s.shape, 0)
            qpos = lax.broadcasted_iota(jnp.int32, s.shape, 1) + c * ATTN_Q_COLS
            s = jnp.where(kpos <= qpos, s, NEG_BIG)
        return s

    def probs(s, m):
        m_new = jnp.maximum(m, jnp.max(s, axis=0, keepdims=True))
        return jnp.exp2(s - m_new).astype(BF16), m_new

    def update(kj, chain, p, m, m_new, acc):
        j, _ = chain
        vt = vt_ref[kj, j * FOX_HEAD_DIM:(j + 1) * FOX_HEAD_DIM, :]
        lhs = jnp.concatenate([vt, ones_rows], axis=0)
        return jnp.exp2(m - m_new) * acc + _dot(lhs, p)

    def step(kj, carry, masked):
        n, lag = len(chains), ATTN_PIPE_LAG
        s, pm, out = {}, {}, [None] * n
        for i in range(n + lag):
            if i < n:
                s[i] = scores(kj, chains[i], masked)
            if lag - 1 <= i < n + lag - 1:
                pm[i - lag + 1] = probs(s.pop(i - lag + 1), carry[i - lag + 1][0])
            if i >= lag:
                p, m_new = pm.pop(i - lag)
                m, acc = carry[i - lag]
                out[i - lag] = (m_new, update(kj, chains[i - lag], p, m, m_new, acc))
        return tuple(out)

    init = tuple((jnp.full((1, ATTN_Q_COLS), NEG_BIG, F32),
                  jnp.zeros((FOX_HEAD_DIM + ATTN_EXTRA_ROWS, ATTN_Q_COLS), F32)) for _ in chains)
    carry = lax.fori_loop(0, qi, lambda kj, cr: step(kj, cr, False), init)
    carry = step(qi, carry, True)
    for (j, c), (_, acc) in zip(chains, carry):
        o_ref[j * FOX_HEAD_DIM:(j + 1) * FOX_HEAD_DIM, c * ATTN_Q_COLS:(c + 1) * ATTN_Q_COLS] = (
            acc[:FOX_HEAD_DIM] / acc[FOX_HEAD_DIM:FOX_HEAD_DIM + 1]).astype(BF16)


def _attn(q, k, vt, blk, heads):
    bsz, seq, _ = q.shape
    nkb = seq // blk
    return pl.pallas_call(
        functools.partial(_attn_kernel, blk=blk, heads=heads),
        grid=(bsz, FOX_HEADS // heads, seq // blk),
        in_specs=[pl.BlockSpec((None, blk, heads * HEAD_PAD), lambda b, h, i: (b, i, h)),
                  pl.BlockSpec((None, seq, heads * HEAD_PAD), lambda b, h, i: (b, 0, h),
                               pipeline_mode=pl.Buffered(1)),
                  pl.BlockSpec((None, nkb, heads * FOX_HEAD_DIM, blk), lambda b, h, i: (b, 0, h, 0),
                               pipeline_mode=pl.Buffered(1))],
        out_specs=pl.BlockSpec((None, heads * FOX_HEAD_DIM, blk), lambda b, h, i: (b, h, i)),
        out_shape=jax.ShapeDtypeStruct((bsz, FOX_HEADS * FOX_HEAD_DIM, seq), BF16),
        compiler_params=_cparams(("arbitrary", "arbitrary", "arbitrary")),
        name="attn",
    )(q, k, vt)


def _mix_kernel(ot_ref, gya_ref, sgb_ref, x_ref, g1_ref, sc2_ref, sh2_ref, lg_ref, lb_ref,
                wfox_ref, wmix_ref, wrh_ref, wrl_ref, wsg_ref, wsu_ref, wsd_ref,
                x1_ref, hp_ref, st_ref, shr_ref, *, alpha, sub):
    tm = x_ref.shape[0]
    groups = [pl.ds(r0, sub) for r0 in range(0, tm, sub)]
    ys = []
    for rows in groups:
        yb = _dot_tn(ot_ref[:, rows], wfox_ref[...])
        merged = gya_ref[rows, :].astype(F32) + sgb_ref[rows, :].astype(F32) * yb
        ys.append(_dot(merged.astype(BF16), wmix_ref[...]))
    for rows, y in zip(groups, ys):
        x1 = _ln(alpha * x_ref[rows, :] + (1.0 + g1_ref[0]) * y) * lg_ref[...] + lb_ref[...]
        x1_ref[rows, :] = x1
        h2 = _ln(x1) * (1.0 + sc2_ref[0]) + sh2_ref[0]
        hb = h2.astype(BF16)
        hl = (h2 - hb.astype(F32)).astype(BF16)
        _store_packed_rows(hp_ref, h2, row0=rows.start)

        logits_t = _dot_nt(wrh_ref[...], hb) + _dot_nt(wrl_ref[...], hb) + _dot_nt(wrh_ref[...], hl)
        st_ref[:, rows] = jax.nn.sigmoid(logits_t)

        g = _dot(hb, wsg_ref[...])
        u = _dot(hb, wsu_ref[...])
        a = (g * jax.nn.sigmoid(g) * u).astype(BF16)
        shr_ref[rows, :] = _dot(a, wsd_ref[...])


def _mix(ot, gya, sgb, x2, gate1, scale2, shift2, lg, lb, wts, tm, seq, alpha):
    t, d = x2.shape
    tps = seq // tm
    fw = ot.shape[1]
    n_exp = wts[2].shape[0]
    mod_spec = pl.BlockSpec((1, 1, d), lambda i: (i // tps, 0, 0))
    row_spec = lambda n: pl.BlockSpec((tm, n), lambda i: (i, 0))
    return pl.pallas_call(
        functools.partial(_mix_kernel, alpha=alpha, sub=min(MIX_ROWS_PER_GROUP, tm)),
        grid=(t // tm,),
        in_specs=[pl.BlockSpec((None, fw, tm), lambda i: (i // tps, 0, i % tps)),
                  row_spec(d), row_spec(d), row_spec(d), mod_spec, mod_spec, mod_spec,
                  _const_spec(lg.shape), _const_spec(lb.shape)] + [_const_spec(w.shape) for w in wts],
        out_specs=[row_spec(d), pl.BlockSpec((tm * ROW_SLABS, V7X_LANES), lambda i: (i, 0)),
                   pl.BlockSpec((n_exp, tm), lambda i: (0, i)), row_spec(d)],
        out_shape=[jax.ShapeDtypeStruct((t, d), F32),
                   jax.ShapeDtypeStruct((t * ROW_SLABS, V7X_LANES), jnp.uint32),
                   jax.ShapeDtypeStruct((n_exp, t), F32),
                   jax.ShapeDtypeStruct((t, d), F32)],
        compiler_params=_cparams(("arbitrary",)),
        name="mix",
    )(ot, gya, sgb, x2, gate1, scale2, shift2, lg, lb, *wts)


def _route_kernel(st_ref, bias_ref, idx_ref, wts_ref, rank_ref, cnt_ref, carry_ref):
    i = pl.program_id(0)
    n_exp, tr = st_ref.shape
    gsz = n_exp // N_GROUPS
    neg_inf = -jnp.inf

    @pl.when(i == 0)
    def _():
        carry_ref[...] = jnp.zeros_like(carry_ref)

    shape3 = (N_GROUPS, gsz, tr)

    def max01(v):
        return jnp.max(jnp.max(v, axis=0, keepdims=True), axis=1, keepdims=True)

    def min01(v):
        return jnp.min(jnp.min(v, axis=0, keepdims=True), axis=1, keepdims=True)

    def sum01(v):
        return jnp.sum(jnp.sum(v, axis=0, keepdims=True), axis=1, keepdims=True)

    sc = st_ref[...].reshape(shape3)
    gsel = (st_ref[...] + bias_ref[...]).reshape(shape3)
    pos = lax.broadcasted_iota(jnp.int32, shape3, 1)
    m1 = jnp.max(gsel, axis=1, keepdims=True)
    i1 = jnp.min(jnp.where(gsel == m1, pos, gsz), axis=1, keepdims=True)
    m2 = jnp.max(jnp.where(pos == i1, neg_inf, gsel), axis=1, keepdims=True)
    gs = m1 + m2

    gid = lax.broadcasted_iota(jnp.int32, gs.shape, 0)
    gkeep = jnp.zeros(gs.shape, F32)
    for _ in range(TOPK_GROUPS):
        mx = jnp.max(gs, axis=0, keepdims=True)
        gi = jnp.min(jnp.where(gs == mx, gid, N_GROUPS), axis=0, keepdims=True)
        hit = gid == gi
        gkeep = gkeep + jnp.where(hit, 1.0, 0.0)
        gs = jnp.where(hit, neg_inf, gs)

    cur0 = jnp.where(jnp.broadcast_to(gkeep, shape3) > 0.5, gsel, neg_inf)
    cur = cur0
    eid = lax.broadcasted_iota(jnp.int32, shape3, 0) * gsz + pos
    idxs, ws = [], []
    wsum = jnp.zeros((1, 1, tr), F32)
    for _ in range(TOP_K):
        mx = max01(cur)
        ik = min01(jnp.where(cur == mx, eid, n_exp))
        hit = eid == ik
        wk = sum01(jnp.where(hit, sc, 0.0))
        idxs.append(ik)
        ws.append(wk)
        wsum = wsum + wk
        cur = jnp.where(hit, neg_inf, cur)
    onehot = jnp.where(cur == cur0, 0.0, 1.0)

    ra = lax.broadcasted_iota(jnp.int32, (tr, tr), 0)
    rb = lax.broadcasted_iota(jnp.int32, (tr, tr), 1)
    upper = jnp.where(ra < rb, 1.0, 0.0).astype(BF16)
    onehot2 = onehot.reshape(n_exp, tr)
    prior = (_dot(onehot2.astype(BF16), upper) + carry_ref[...]).reshape(shape3)
    for slot in range(TOP_K):
        idx_ref[slot:slot + 1, :] = idxs[slot].reshape(1, tr)
        wts_ref[slot:slot + 1, :] = (ws[slot] / wsum * ROUTED_SCALE).reshape(1, tr)
        rk = sum01(jnp.where(eid == idxs[slot], prior, 0.0))
        rank_ref[slot:slot + 1, :] = rk.reshape(1, tr).astype(jnp.int32)
    total = carry_ref[...] + jnp.sum(onehot2, axis=1, keepdims=True)
    carry_ref[...] = total
    cnt_ref[...] = jnp.broadcast_to(total, cnt_ref.shape).astype(jnp.int32)


def _route(scores_t, bias_col, tr):
    n_exp, t = scores_t.shape
    slot_spec = pl.BlockSpec((TOP_K, tr), lambda i: (0, i))
    return pl.pallas_call(
        _route_kernel,
        grid=(t // tr,),
        in_specs=[pl.BlockSpec((n_exp, tr), lambda i: (0, i)), _const_spec(bias_col.shape)],
        out_specs=[slot_spec, slot_spec, slot_spec, _const_spec((n_exp, V7X_LANES))],
        out_shape=[jax.ShapeDtypeStruct((TOP_K, t), jnp.int32),
                   jax.ShapeDtypeStruct((TOP_K, t), F32),
                   jax.ShapeDtypeStruct((TOP_K, t), jnp.int32),
                   jax.ShapeDtypeStruct((n_exp, V7X_LANES), jnp.int32)],
        scratch_shapes=[pltpu.VMEM((n_exp, 1), F32)],
        compiler_params=_cparams(("arbitrary",)),
        name="route",
    )(scores_t, bias_col)


def _dest_kernel(pstart_ref, idx_ref, rank_ref, o_ref):
    n_exp = pstart_ref.shape[0]
    tr = idx_ref.shape[1]
    eid = lax.broadcasted_iota(jnp.int32, (n_exp, tr), 0)
    pstart = pstart_ref[...]
    for slot in range(TOP_K):
        hit = eid == idx_ref[slot:slot + 1, :]
        start = jnp.sum(jnp.where(hit, pstart, 0.0), axis=0, keepdims=True).astype(jnp.int32)
        o_ref[slot:slot + 1, :] = (start + rank_ref[slot:slot + 1, :]) * ROW_SLABS


def _dest(pstart, idx, rank, tr):
    k, t = idx.shape
    spec = pl.BlockSpec((k, tr), lambda i: (0, i))
    pstart_col = pstart.astype(F32)[:, None]
    return pl.pallas_call(
        _dest_kernel,
        grid=(t // tr,),
        in_specs=[_const_spec(pstart_col.shape), spec, spec],
        out_specs=spec,
        out_shape=jax.ShapeDtypeStruct((k, t), jnp.int32),
        compiler_params=_cparams(("arbitrary",)),
        name="dest",
    )(pstart_col, idx, rank)


def _dispatch_kernel(dest_ref, hp_ref, xs_ref, sem):
    td = hp_ref.shape[0] // ROW_SLABS

    def row_copy(r, slot):
        dst = pl.multiple_of(dest_ref[slot, r], ROW_SLABS)
        src = pl.multiple_of(r * ROW_SLABS, ROW_SLABS)
        return pltpu.make_async_copy(hp_ref.at[pl.ds(src, ROW_SLABS)], xs_ref.at[pl.ds(dst, ROW_SLABS)], sem)

    def start(r, c):
        for slot in range(TOP_K):
            row_copy(r, slot).start(priority=slot % 2)
        return c

    def wait(r, c):
        for slot in range(TOP_K):
            row_copy(r, slot).wait()
        return c

    lax.fori_loop(0, td, start, 0, unroll=4)
    lax.fori_loop(0, td, wait, 0, unroll=8)


def _dispatch(dest, hp, n_pad, td):
    t = hp.shape[0] // ROW_SLABS
    return pl.pallas_call(
        _dispatch_kernel,
        grid=(t // td,),
        in_specs=[pl.BlockSpec((TOP_K, td), lambda i: (0, i), memory_space=pltpu.SMEM),
                  pl.BlockSpec((td * ROW_SLABS, V7X_LANES), lambda i: (i, 0))],
        out_specs=pl.BlockSpec(memory_space=pl.ANY),
        out_shape=jax.ShapeDtypeStruct((n_pad * ROW_SLABS, V7X_LANES), hp.dtype),
        scratch_shapes=[pltpu.SemaphoreType.DMA(())],
        compiler_params=_cparams(("arbitrary",)),
        name="dispatch",
    )(dest, hp)


def _expert_kernel(be_ref, rows_ref, seq_ref, cnt_ref, xs_hbm, wg_hbm, wu_hbm, wd_hbm, ys_hbm,
                   xbuf, ybuf, wgf, wuf, wdf, wgu_s, wd_s, sem_x, sem_y, sem_w, *, layer):
    n = cnt_ref[0]
    n_seq = cnt_ref[1]
    blk_rows = EXPERT_BLOCK * ROW_SLABS
    nx, nw = EXPERT_X_AHEAD + 1, EXPERT_W_AHEAD + 1

    def block_rows(q):
        start = q * blk_rows
        return pl.ds(start if isinstance(q, int) else pl.multiple_of(start, blk_rows), blk_rows)

    def x_copy(q, slot):
        return pltpu.make_async_copy(xs_hbm.at[block_rows(q)], xbuf.at[slot], sem_x.at[slot])

    def y_copy(q, slot):
        return pltpu.make_async_copy(ybuf.at[slot], ys_hbm.at[block_rows(q)], sem_y.at[slot])

    def w_copies(k, ws):
        e = seq_ref[k]
        out = []
        for hbm, buf in ((wg_hbm, wgf), (wu_hbm, wuf), (wd_hbm, wdf)):
            band = buf.shape[1] // EXPERT_W_PARTS
            for part in range(EXPERT_W_PARTS):
                rows = pl.ds(part * band, band)
                out.append((pltpu.make_async_copy(hbm.at[layer, e, rows], buf.at[ws, rows], sem_w.at[ws]),
                            part % 2))
        return out

    for j in range(EXPERT_X_AHEAD):
        @pl.when(j < n)
        def _():
            x_copy(j, j).start(priority=1)

    for j in range(EXPERT_W_AHEAD):
        @pl.when(j < n_seq)
        def _():
            for cp, prio in w_copies(j, j):
                cp.start(priority=prio)

    def body(q, k):
        slot = q % 2
        fresh = jnp.logical_or(q == 0, be_ref[q] != be_ref[jnp.maximum(q - 1, 0)])

        @pl.when(fresh)
        def _():
            ws = k % nw
            for cp, _ in w_copies(k, ws):
                cp.wait()
            fw = wd_s.shape[0]
            wgu_s[:, :fw] = wgf[ws].astype(BF16)
            wgu_s[:, fw:] = wuf[ws].astype(BF16)
            wd_s[...] = wdf[ws].astype(BF16)

            @pl.when(k + EXPERT_W_AHEAD < n_seq)
            def _():
                for cp, prio in w_copies(k + EXPERT_W_AHEAD, (k + EXPERT_W_AHEAD) % nw):
                    cp.start(priority=prio)

        @pl.when(q + EXPERT_X_AHEAD < n)
        def _():
            x_copy(q + EXPERT_X_AHEAD, (q + EXPERT_X_AHEAD) % nx).start(priority=1)

        x_copy(q, q % nx).wait()

        @pl.when(q >= 2)
        def _():
            y_copy(q - 2, slot).wait()

        half = EXPERT_BLOCK // 2
        f = wd_s.shape[0]
        gu = []
        for h in range(2):
            x = jnp.concatenate(_load_packed_rows(xbuf, half, lead=(q % nx,), row0=h * half), axis=1)
            valid = lax.broadcasted_iota(jnp.int32, x.shape, 0) + h * half < rows_ref[q]
            x = jnp.where(valid, x, 0.0).astype(BF16)
            gu.append(_dot(x, wgu_s[...]))
        for h in range(2):
            g, u = gu[h][:, :f], gu[h][:, f:]
            a = (g * jax.nn.sigmoid(g) * u).astype(BF16)
            _store_packed_rows(ybuf.at[slot], _dot(a, wd_s[...]), row0=h * half)
        y_copy(q, slot).start()
        return k + fresh.astype(jnp.int32)

    lax.fori_loop(0, n, body, jnp.int32(0))

    @pl.when(n >= 2)
    def _():
        y_copy(n - 2, n % 2).wait()

    y_copy(n - 1, (n - 1) % 2).wait()


def _experts(block_expert, block_rows, expert_seq, counts2, xs, wg, wu, wd, layer):
    _, n_exp, d, f = wg.shape
    blk = (EXPERT_BLOCK * ROW_SLABS, V7X_LANES)
    nx, nw = EXPERT_X_AHEAD + 1, EXPERT_W_AHEAD + 1
    any_spec = pl.BlockSpec(memory_space=pl.ANY)
    return pl.pallas_call(
        functools.partial(_expert_kernel, layer=layer),
        grid_spec=pltpu.PrefetchScalarGridSpec(
            num_scalar_prefetch=4, grid=(1,),
            in_specs=[any_spec, any_spec, any_spec, any_spec],
            out_specs=any_spec,
            scratch_shapes=[pltpu.VMEM((nx,) + blk, xs.dtype), pltpu.VMEM((2,) + blk, xs.dtype),
                            pltpu.VMEM((nw, d, f), F32), pltpu.VMEM((nw, d, f), F32), pltpu.VMEM((nw, f, d), F32),
                            pltpu.VMEM((d, 2 * f), BF16), pltpu.VMEM((f, d), BF16),
                            pltpu.SemaphoreType.DMA((nx,)), pltpu.SemaphoreType.DMA((2,)),
                            pltpu.SemaphoreType.DMA((nw,))]),
        out_shape=jax.ShapeDtypeStruct(xs.shape, xs.dtype),
        compiler_params=_cparams(("arbitrary",)),
        name="experts",
    )(block_expert, block_rows, expert_seq, counts2, xs, wg, wu, wd)


def _combine_kernel(d0_ref, d1_ref, d2_ref, ys_ref, w_ref, shr_ref, x1_ref, g2_ref, lg_ref, lb_ref, o_ref,
                    buf_a, buf_b, sem, *, alpha, tc):
    i = pl.program_id(0)
    last = pl.num_programs(0) - 1

    def row_copy(dref, r, slot, buf, sem_idx):
        src = pl.multiple_of(dref[slot, r], ROW_SLABS)
        dst = r * ROW_SLABS if isinstance(r, int) else pl.multiple_of(r * ROW_SLABS, ROW_SLABS)
        return pltpu.make_async_copy(ys_ref.at[pl.ds(src, ROW_SLABS)], buf.at[slot, pl.ds(dst, ROW_SLABS)],
                                     sem.at[sem_idx])

    def issue_unrolled(dref, buf, sem_idx):
        for r in range(tc):
            for slot in range(TOP_K):
                row_copy(dref, r, slot, buf, sem_idx).start(priority=slot % 2)

    def issue_loop(dref, buf, sem_idx):
        def start(r, c):
            for slot in range(TOP_K):
                row_copy(dref, r, slot, buf, sem_idx).start(priority=slot % 2)
            return c
        lax.fori_loop(0, tc, start, 0, unroll=4)

    def wait_all(dref, buf, sem_idx):
        def wait(r, c):
            for slot in range(TOP_K):
                row_copy(dref, r, slot, buf, sem_idx).wait()
            return c
        lax.fori_loop(0, tc, wait, 0, unroll=8)

    def reduce_tile(buf, rows):
        w = w_ref[rows, :]
        chunks = None
        for slot in range(TOP_K):
            wk = w[:, slot:slot + 1]
            part = [c * wk for c in _load_packed_rows(buf, tc, lead=(slot,))]
            chunks = part if chunks is None else [a + b for a, b in zip(chunks, part)]
        y = shr_ref[rows, :] + jnp.concatenate(chunks, axis=1)
        z = alpha * x1_ref[rows, :] + (1.0 + g2_ref[0]) * y
        o_ref[rows, :] = _ln(z) * lg_ref[...] + lb_ref[...]

    @pl.when(i == 0)
    def _():
        issue_loop(d0_ref, buf_a, 0)

    wait_all(d0_ref, buf_a, 0)
    issue_unrolled(d1_ref, buf_b, 1)
    reduce_tile(buf_a, pl.ds(0, tc))

    wait_all(d1_ref, buf_b, 1)
    issue_unrolled(d2_ref, buf_a, 0)
    reduce_tile(buf_b, pl.ds(tc, tc))

    @pl.when(i == last)
    def _():
        wait_all(d2_ref, buf_a, 0)


def _combine(dest, ys, wts_tk, shared, x1, gate2, lg, lb, tc, seq, alpha):
    t, d = x1.shape
    tps = seq // (2 * tc)
    n_tiles = t // tc
    row_spec = lambda n: pl.BlockSpec((2 * tc, n), lambda i: (i, 0))
    tile_dest = lambda f: pl.BlockSpec((TOP_K, tc), f, memory_space=pltpu.SMEM)
    buf = pltpu.VMEM((TOP_K, tc * ROW_SLABS, V7X_LANES), jnp.uint32)
    return pl.pallas_call(
        functools.partial(_combine_kernel, alpha=alpha, tc=tc),
        grid=(n_tiles // 2,),
        in_specs=[tile_dest(lambda i: (0, 2 * i)), tile_dest(lambda i: (0, 2 * i + 1)),
                  tile_dest(lambda i: (0, jnp.minimum(2 * i + 2, n_tiles - 1))),
                  pl.BlockSpec(memory_space=pl.ANY),
                  row_spec(TOP_K), row_spec(d), row_spec(d),
                  pl.BlockSpec((1, 1, d), lambda i: (i // tps, 0, 0)),
                  _const_spec(lg.shape), _const_spec(lb.shape)],
        out_specs=row_spec(d),
        out_shape=jax.ShapeDtypeStruct((t, d), F32),
        scratch_shapes=[buf, buf, pltpu.SemaphoreType.DMA((2,))],
        compiler_params=_cparams(("arbitrary",)),
        name="combine",
    )(dest, dest, dest, ys, wts_tk, shared, x1, gate2, lg, lb)


def _placement():
    pq = np.zeros((V7X_LANES, FOX_HEADS * HEAD_PAD), np.float32)
    pk = np.zeros((V7X_LANES, FOX_HEADS * HEAD_PAD), np.float32)
    for h in range(FOX_HEADS):
        base = h * HEAD_PAD + FOX_HEAD_DIM
        for piece in range(3):
            pq[piece * FOX_HEADS + h, base + piece] = 1.0
            pk[FORGET_ONES_LANE, base + piece] = 1.0
            pq[FORGET_ONES_LANE, base + 3 + piece] = 1.0
            pk[piece * FOX_HEADS + h, base + 3 + piece] = -1.0
    return jnp.asarray(pq, BF16), jnp.asarray(pk, BF16)


def _inproj_weights(w_in, b_forget, d):
    conv2 = d
    fw = FOX_HEADS * FOX_HEAD_DIM
    o1, o2, o3, o4 = conv2, conv2 + fw, conv2 + 2 * fw, conv2 + 3 * fw
    o5 = o4 + FOX_HEADS
    o6 = o5 + d

    def pad_heads(w):
        w = w.reshape(d, FOX_HEADS, FOX_HEAD_DIM)
        w = jnp.pad(w, ((0, 0), (0, 0), (0, HEAD_PAD - FOX_HEAD_DIM)))
        return w.reshape(d, FOX_HEADS * HEAD_PAD).astype(BF16)

    wglu = w_in[:, :o1].astype(BF16)
    wq = pad_heads(w_in[:, o1:o2])
    wk = pad_heads(w_in[:, o2:o3])
    wvt = w_in[:, o3:o4].T.astype(BF16)
    wf8 = w_in[:, o4:o5]
    wf = jnp.pad(jnp.concatenate([wf8, wf8, wf8], axis=1), ((0, 0), (0, V7X_LANES - 3 * FOX_HEADS))).astype(BF16)
    bf = jnp.pad(jnp.concatenate([b_forget, b_forget, b_forget]), (0, V7X_LANES - 3 * FOX_HEADS))[None, :].astype(F32)
    wga = w_in[:, o5:o6].astype(BF16)
    wgb = w_in[:, o6:].astype(BF16)
    pq, pk = _placement()
    return (wglu, wq, wk, wvt, wf, wga, wgb, bf, pq, pk)


def _tile_sizes(seq, t):
    return {"inproj": min(512, seq),
            "mix": min(2 * MIX_ROWS_PER_GROUP, seq),
            "route": min(512, t),
            "dispatch": min(256, t),
            "combine": min(128, seq)}


def _layer(x2, ada, bsz, seq, w_in, b_forget, conv_w, conv_b, conv_ln_g, conv_ln_b, w_conv_out, w_fox_out,
           w_mix_out, ln1_g, ln1_b, w_router, router_bias, w_exp_gate, w_exp_up, w_exp_down,
           w_sh_gate, w_sh_up, w_sh_down, ln2_g, ln2_b, depth, layer):
    t, d = x2.shape
    n_exp = w_router.shape[1]
    alpha = (2.0 * depth) ** 0.25
    mods = [ada[:bsz, j * d:(j + 1) * d][:, None, :] for j in range(6)]
    shift1, scale1, gate1, shift2, scale2, gate2 = mods

    tiles = _tile_sizes(seq, t)
    tm = tiles["inproj"]
    conv_w_pad = jnp.pad(conv_w, ((0, CONV_HALO - CONV_WIDTH), (0, 0)))
    conv_wts = (conv_w_pad, conv_b[None, :], conv_ln_g[None, :], conv_ln_b[None, :], w_conv_out.astype(BF16))
    q, k, vt, gya, sgb = _inproj(x2, scale1, shift1, _inproj_weights(w_in, b_forget, d), conv_wts, tm, seq)

    ot = _attn(q.reshape(bsz, seq, -1), k.reshape(bsz, seq, -1), vt, tm, ATTN_HEADS_PER_STEP)

    tmx = tiles["mix"]
    wr_t = w_router.T
    wr_h = wr_t.astype(BF16)
    wr_l = (wr_t - wr_h.astype(F32)).astype(BF16)
    mix_w = (w_fox_out.astype(BF16), w_mix_out.astype(BF16), wr_h, wr_l,
             w_sh_gate.astype(BF16), w_sh_up.astype(BF16), w_sh_down.astype(BF16))
    x1, hp, scores_t, shared = _mix(ot, gya, sgb, x2, gate1, scale2, shift2, ln1_g[None, :], ln1_b[None, :],
                                    mix_w, tmx, seq, alpha)

    tr = tiles["route"]
    idx, wts, rank, cnt = _route(scores_t, router_bias[:, None], tr)

    counts = cnt[:, 0]
    padded = (counts + EXPERT_BLOCK - 1) // EXPERT_BLOCK * EXPERT_BLOCK
    pend = jnp.cumsum(padded)
    pstart = (pend - padded).astype(jnp.int32)
    n_assign = t * TOP_K
    n_pad = -(-(n_assign + n_exp * (EXPERT_BLOCK - 1)) // EXPERT_BLOCK) * EXPERT_BLOCK
    n_blocks = n_pad // EXPERT_BLOCK
    block_start = jnp.arange(n_blocks, dtype=jnp.int32) * EXPERT_BLOCK
    block_expert = jnp.minimum(jnp.sum(pend[None, :] <= block_start[:, None], axis=1), n_exp - 1).astype(jnp.int32)
    block_rows = jnp.clip((pstart + counts)[block_expert] - block_start, 0, EXPERT_BLOCK).astype(jnp.int32)
    n_used = (pend[-1:] // EXPERT_BLOCK).astype(jnp.int32)
    owns = counts > 0
    expert_seq = jnp.nonzero(owns, size=n_exp, fill_value=0)[0].astype(jnp.int32)
    counts2 = jnp.concatenate([n_used, jnp.sum(owns, dtype=jnp.int32)[None]])

    dest = _dest(pstart, idx, rank, tr)
    xs = _dispatch(dest, hp, n_pad, tiles["dispatch"])
    ys = _experts(block_expert, block_rows, expert_seq, counts2, xs, w_exp_gate, w_exp_up, w_exp_down, layer)
    return _combine(dest, ys, wts.T, shared, x1, gate2, ln2_g[None, :], ln2_b[None, :],
                    tiles["combine"], seq, alpha)


def kernel(x, c, w_ada, b_ada, w_in, b_forget, conv_w, conv_b, conv_ln_g, conv_ln_b, w_conv_out, w_fox_out,
           w_mix_out, ln1_g, ln1_b, w_router, router_bias, w_exp_gate, w_exp_up, w_exp_down, w_sh_gate,
           w_sh_up, w_sh_down, ln2_g, ln2_b):
    bsz, seq, d = x.shape
    depth = w_ada.shape[0]
    c_pad = jnp.pad(c, ((0, -bsz % V7X_SUBLANES), (0, 0)))
    x2 = x.reshape(bsz * seq, d)
    for l in range(depth):
        ada = _ada(c_pad, w_ada[l], b_ada[l][None, :])
        x2 = _layer(x2, ada, bsz, seq, w_in[l], b_forget[l], conv_w[l], conv_b[l], conv_ln_g[l], conv_ln_b[l],
                    w_conv_out[l], w_fox_out[l], w_mix_out[l], ln1_g[l], ln1_b[l], w_router[l], router_bias[l],
                    w_exp_gate, w_exp_up, w_exp_down, w_sh_gate[l], w_sh_up[l], w_sh_down[l],
                    ln2_g[l], ln2_b[l], depth, l)
    return x2.reshape(bsz, seq, d)
```

```python
import functools

import jax
import jax.numpy as jnp
import numpy as np
from jax import lax
from jax.experimental import pallas as pl
from jax.experimental.pallas import tpu as pltpu

F32 = jnp.float32
BF16 = jnp.bfloat16

LN_EPS = 1e-5
CONV_WIDTH = 31
FOX_HEADS = 8
FOX_HEAD_DIM = 64
N_GROUPS = 8
TOPK_GROUPS = 4
TOP_K = 8
ROUTED_SCALE = 2.5
EXPERT_BLOCK = 256
EXPERT_X_AHEAD = 4
EXPERT_W_AHEAD = 4
EXPERT_W_PARTS = 8
ROW_SLABS = 4

V7X_LANES = 128
HEAD_PAD = 128
FORGET_ONES_LANE = 3 * FOX_HEADS
V7X_SUBLANES = 8
ADA_COLS = 1024
CONV_CHUNK = 64
CONV_HALO = 32
VMEM_LIMIT = 56 * 1024 * 1024
NEG_BIG = -1e30
LOG2E = 1.4426950408889634
ATTN_EXTRA_ROWS = 16
ATTN_HEADS_PER_STEP = 8
ATTN_Q_COLS = 256
MIX_ROWS_PER_GROUP = 256
ATTN_PIPE_LAG = 3


def _cparams(sem):
    return pltpu.CompilerParams(dimension_semantics=sem, vmem_limit_bytes=VMEM_LIMIT)


def _ln(v):
    mu = jnp.mean(v, axis=-1, keepdims=True)
    vc = v - mu
    var = jnp.mean(vc * vc, axis=-1, keepdims=True)
    return vc * lax.rsqrt(var + LN_EPS)


def _split3(v):
    hi = v.astype(BF16)
    r1 = v - hi.astype(F32)
    mid = r1.astype(BF16)
    lo = (r1 - mid.astype(F32)).astype(BF16)
    return hi, mid, lo


def _dot(a, b):
    return jnp.dot(a, b, preferred_element_type=F32)


def _dot_nt(a, b):
    return lax.dot_general(a, b, (((1,), (1,)), ((), ())), preferred_element_type=F32)


def _dot_tn(a, b):
    return lax.dot_general(a, b, (((0,), (0,)), ((), ())), preferred_element_type=F32)


def _store_packed_rows(ref, v, row0=0):
    n, d = v.shape
    half = d // 2
    vb = v.astype(BF16).astype(F32)
    lo_bits = lax.bitcast_convert_type(vb[:, :half], jnp.uint32)
    hi_bits = lax.bitcast_convert_type(vb[:, half:], jnp.uint32)
    words = (lo_bits >> 16) | (hi_bits & jnp.uint32(0xFFFF0000))
    for c in range(ROW_SLABS):
        ref[pl.ds(row0 * ROW_SLABS + c, n, stride=ROW_SLABS), :] = words[:, c * V7X_LANES:(c + 1) * V7X_LANES]


def _load_packed_rows(ref, n, lead=(), row0=0):
    lo, hi = [], []
    for c in range(ROW_SLABS):
        w = ref[lead + (pl.ds(row0 * ROW_SLABS + c, n, stride=ROW_SLABS), slice(None))]
        lo.append(lax.bitcast_convert_type(w << 16, F32))
        hi.append(lax.bitcast_convert_type(w & jnp.uint32(0xFFFF0000), F32))
    return lo + hi


def _const_spec(shape, single=False):
    nd = len(shape)
    if single:
        return pl.BlockSpec(shape, lambda *_: (0,) * nd, pipeline_mode=pl.Buffered(1))
    return pl.BlockSpec(shape, lambda *_: (0,) * nd)


def _ada_kernel(c_ref, w_ref, b_ref, o_ref):
    c = c_ref[...]
    cond = c * jax.nn.sigmoid(c)
    ch, cm, _ = _split3(cond)
    w = w_ref[...]
    wh, wm, _ = _split3(w)
    o_ref[...] = _dot(ch, wh) + _dot(ch, wm) + _dot(cm, wh) + b_ref[...]


def _ada(c_pad, w, b):
    rows, d = c_pad.shape
    n = w.shape[1]
    tn = ADA_COLS
    return pl.pallas_call(
        _ada_kernel,
        grid=(n // tn,),
        in_specs=[_const_spec((rows, d)),
                  pl.BlockSpec((d, tn), lambda j: (0, j)),
                  pl.BlockSpec((1, tn), lambda j: (0, j))],
        out_specs=pl.BlockSpec((rows, tn), lambda j: (0, j)),
        out_shape=jax.ShapeDtypeStruct((rows, n), F32),
        compiler_params=_cparams(("arbitrary",)),
        name="ada",
    )(c_pad, w, b)


def _inproj_kernel(x_ref, sc_ref, sh_ref, wglu_ref, wq_ref, wk_ref, wvt_ref, wf_ref, wga_ref, wgb_ref,
                   bf_ref, pq_ref, pk_ref, cw_ref, cb_ref, cg_ref, cbe_ref, wco_ref,
                   q_ref, k_ref, vt_ref, gya_ref, sgb_ref, carry_ref, halo_ref, ext_ref, shift_ref,
                   *, tiles_per_seq, conv_ch, chunk):
    i = pl.program_id(0)
    tm = x_ref.shape[0]

    @pl.when(i % tiles_per_seq == 0)
    def _():
        carry_ref[...] = jnp.zeros_like(carry_ref)
        halo_ref[...] = jnp.zeros_like(halo_ref)

    h = _ln(x_ref[...]) * (1.0 + sc_ref[0]) + sh_ref[0]
    hb = h.astype(BF16)

    glu = _dot(hb, wglu_ref[...])
    u = glu[:, :conv_ch] * jax.nn.sigmoid(glu[:, conv_ch:])

    ext_ref[0:CONV_HALO, :] = halo_ref[...]
    ext_ref[CONV_HALO:, :] = u
    halo_ref[...] = u[tm - CONV_HALO:, :]
    cw = cw_ref[...]
    off = CONV_HALO - (CONV_WIDTH - 1)
    span = tm + CONV_HALO - 8
    for res in range(1, 8):
        shift_ref[res - 1] = ext_ref[res:res + span, :]
    def conv_chunk(c0):
        acc = jnp.zeros((chunk, conv_ch), F32)
        for j in range(CONV_WIDTH):
            res, lo = (off + j) % 8, c0 + (off + j) // 8 * 8
            rows = ext_ref[lo:lo + chunk, :] if res == 0 else shift_ref[res - 1, lo:lo + chunk, :]
            acc = acc + cw[j:j + 1, :] * rows
        return acc

    n_chunks = tm // chunk
    per_gap = -(-n_chunks // 4)
    outs = []

    def conv_chunks():
        for _ in range(per_gap):
            if len(outs) < n_chunks:
                outs.append(conv_chunk(len(outs) * chunk))

    sgb_ref[...] = jax.nn.sigmoid(_dot(hb, wgb_ref[...])).astype(BF16)
    conv_chunks()

    f = _dot(hb, wf_ref[...]) + bf_ref[...]
    logf = jnp.minimum(f, 0.0) - jnp.log(1.0 + jnp.exp(-jnp.abs(f)))
    lh, lm, ll = _split3(logf)
    row = lax.broadcasted_iota(jnp.int32, (tm, tm), 0)
    col = lax.broadcasted_iota(jnp.int32, (tm, tm), 1)
    tri = jnp.where(row >= col, 1.0, 0.0).astype(BF16)
    cs = _dot(tri, lh) + _dot(tri, lm) + _dot(tri, ll)
    cum = cs + carry_ref[...]
    carry_ref[...] = cum[tm - 1:tm, :]
    conv_chunks()

    ch, cm, cl = _split3(cum * LOG2E)
    lane = lax.broadcasted_iota(jnp.int32, cum.shape, 1)
    nh = FOX_HEADS
    tail = jnp.where(lane == FORGET_ONES_LANE, 1.0, 0.0)
    pieces = jnp.where(lane < nh, ch.astype(F32), jnp.where(lane < 2 * nh, cm.astype(F32),
                       jnp.where(lane < 3 * nh, cl.astype(F32), tail))).astype(BF16)
    scale = FOX_HEAD_DIM ** -0.5 * LOG2E
    q_ref[...] = (_dot(hb, wq_ref[...]) * scale + _dot(pieces, pq_ref[...])).astype(BF16)
    conv_chunks()
    k_ref[...] = (_dot(hb, wk_ref[...]) + _dot(pieces, pk_ref[...])).astype(BF16)
    conv_chunks()
    vt_ref[0, 0] = _dot_nt(wvt_ref[...], hb).astype(BF16)

    v = jnp.concatenate(outs, axis=0) + cb_ref[...]
    v = _ln(v) * cg_ref[...] + cbe_ref[...]
    v = v * jax.nn.sigmoid(v)
    sga = jax.nn.sigmoid(_dot(hb, wga_ref[...]))
    gya_ref[...] = (sga * _dot(v.astype(BF16), wco_ref[...])).astype(BF16)


def _inproj(x2, scale1, shift1, wts, conv_wts, tm, seq):
    t, d = x2.shape
    tps = seq // tm
    bsz = t // seq
    wglu, wq, wk, wvt, wf, wga, wgb, bf, pq, pk = wts
    conv_ch = wglu.shape[1] // 2
    fw = wvt.shape[0]
    qw = wq.shape[1]
    mod_spec = pl.BlockSpec((1, 1, d), lambda i: (i // tps, 0, 0))
    row_spec = lambda n: pl.BlockSpec((tm, n), lambda i: (i, 0))
    consts = list(wts) + list(conv_wts)
    return pl.pallas_call(
        functools.partial(_inproj_kernel, tiles_per_seq=tps, conv_ch=conv_ch, chunk=CONV_CHUNK),
        grid=(t // tm,),
        in_specs=[row_spec(d), mod_spec, mod_spec] + [_const_spec(w.shape, single=True) for w in consts],
        out_specs=[row_spec(qw), row_spec(qw),
                   pl.BlockSpec((1, 1, fw, tm), lambda i: (i // tps, i % tps, 0, 0)),
                   row_spec(d), row_spec(d)],
        out_shape=[jax.ShapeDtypeStruct((t, qw), BF16),
                   jax.ShapeDtypeStruct((t, qw), BF16),
                   jax.ShapeDtypeStruct((bsz, tps, fw, tm), BF16),
                   jax.ShapeDtypeStruct((t, d), BF16),
                   jax.ShapeDtypeStruct((t, d), BF16)],
        scratch_shapes=[pltpu.VMEM((1, V7X_LANES), F32), pltpu.VMEM((CONV_HALO, conv_ch), F32),
                        pltpu.VMEM((tm + CONV_HALO, conv_ch), F32),
                        pltpu.VMEM((7, tm + CONV_HALO - 8, conv_ch), F32)],
        compiler_params=_cparams(("arbitrary",)),
        name="inproj",
    )(x2, scale1, shift1, *consts)


def _attn_kernel(q_ref, k_ref, vt_ref, o_ref, *, blk, heads):
    qi = pl.program_id(2)
    row = lax.broadcasted_iota(jnp.int32, (ATTN_EXTRA_ROWS, blk), 0)
    ones_rows = jnp.where(row == 0, 1.0, 0.0).astype(BF16)

    ncol = blk // ATTN_Q_COLS
    chains = [(j, c) for j in range(heads) for c in range(ncol)]

    def scores(kj, chain, masked):
        j, c = chain
        k = k_ref[pl.ds(pl.multiple_of(kj * blk, blk), blk), j * HEAD_PAD:(j + 1) * HEAD_PAD]
        q = q_ref[c * ATTN_Q_COLS:(c + 1) * ATTN_Q_COLS, j * HEAD_PAD:(j + 1) * HEAD_PAD]
        s = _dot_nt(k, q)
        if masked:
            kpos = lax.broadcasted_iota(jnp.int32, s.shape, 0)
            qpos = lax.broadcasted_iota(jnp.int32, s.shape, 1) + c * ATTN_Q_COLS
            s = jnp.where(kpos <= qpos, s, NEG_BIG)
        return s

    def probs(s, m):
        m_new = jnp.maximum(m, jnp.max(s, axis=0, keepdims=True))
        return jnp.exp2(s - m_new).astype(BF16), m_new

    def update(kj, chain, p, m, m_new, acc):
        j, _ = chain
        vt = vt_ref[kj, j * FOX_HEAD_DIM:(j + 1) * FOX_HEAD_DIM, :]
        lhs = jnp.concatenate([vt, ones_rows], axis=0)
        return jnp.exp2(m - m_new) * acc + _dot(lhs, p)

    def step(kj, carry, masked):
        n, lag = len(chains), ATTN_PIPE_LAG
        s, pm, out = {}, {}, [None] * n
        for i in range(n + lag):
            if i < n:
                s[i] = scores(kj, chains[i], masked)
            if lag - 1 <= i < n + lag - 1:
                pm[i - lag + 1] = probs(s.pop(i - lag + 1), carry[i - lag + 1][0])
            if i >= lag:
                p, m_new = pm.pop(i - lag)
                m, acc = carry[i - lag]
                out[i - lag] = (m_new, update(kj, chains[i - lag], p, m, m_new, acc))
        return tuple(out)

    init = tuple((jnp.full((1, ATTN_Q_COLS), NEG_BIG, F32),
                  jnp.zeros((FOX_HEAD_DIM + ATTN_EXTRA_ROWS, ATTN_Q_COLS), F32)) for _ in chains)
    carry = lax.fori_loop(0, qi, lambda kj, cr: step(kj, cr, False), init)
    carry = step(qi, carry, True)
    for (j, c), (_, acc) in zip(chains, carry):
        o_ref[j * FOX_HEAD_DIM:(j + 1) * FOX_HEAD_DIM, c * ATTN_Q_COLS:(c + 1) * ATTN_Q_COLS] = (
            acc[:FOX_HEAD_DIM] / acc[FOX_HEAD_DIM:FOX_HEAD_DIM + 1]).astype(BF16)


def _attn(q, k, vt, blk, heads):
    bsz, seq, _ = q.shape
    nkb = seq // blk
    return pl.pallas_call(
        functools.partial(_attn_kernel, blk=blk, heads=heads),
        grid=(bsz, FOX_HEADS // heads, seq // blk),
        in_specs=[pl.BlockSpec((None, blk, heads * HEAD_PAD), lambda b, h, i: (b, i, h)),
                  pl.BlockSpec((None, seq, heads * HEAD_PAD), lambda b, h, i: (b, 0, h),
                               pipeline_mode=pl.Buffered(1)),
                  pl.BlockSpec((None, nkb, heads * FOX_HEAD_DIM, blk), lambda b, h, i: (b, 0, h, 0),
                               pipeline_mode=pl.Buffered(1))],
        out_specs=pl.BlockSpec((None, heads * FOX_HEAD_DIM, blk), lambda b, h, i: (b, h, i)),
        out_shape=jax.ShapeDtypeStruct((bsz, FOX_HEADS * FOX_HEAD_DIM, seq), BF16),
        compiler_params=_cparams(("arbitrary", "arbitrary", "arbitrary")),
        name="attn",
    )(q, k, vt)


def _mix_kernel(ot_ref, gya_ref, sgb_ref, x_ref, g1_ref, sc2_ref, sh2_ref, lg_ref, lb_ref,
                wfox_ref, wmix_ref, wrh_ref, wrl_ref, wsg_ref, wsu_ref, wsd_ref,
                x1_ref, hp_ref, st_ref, shr_ref, *, alpha, sub):
    tm = x_ref.shape[0]
    groups = [pl.ds(r0, sub) for r0 in range(0, tm, sub)]
    ys = []
    for rows in groups:
        yb = _dot_tn(ot_ref[:, rows], wfox_ref[...])
        merged = gya_ref[rows, :].astype(F32) + sgb_ref[rows, :].astype(F32) * yb
        ys.append(_dot(merged.astype(BF16), wmix_ref[...]))
    for rows, y in zip(groups, ys):
        x1 = _ln(alpha * x_ref[rows, :] + (1.0 + g1_ref[0]) * y) * lg_ref[...] + lb_ref[...]
        x1_ref[rows, :] = x1
        h2 = _ln(x1) * (1.0 + sc2_ref[0]) + sh2_ref[0]
        hb = h2.astype(BF16)
        hl = (h2 - hb.astype(F32)).astype(BF16)
        _store_packed_rows(hp_ref, h2, row0=rows.start)

        logits_t = _dot_nt(wrh_ref[...], hb) + _dot_nt(wrl_ref[...], hb) + _dot_nt(wrh_ref[...], hl)
        st_ref[:, rows] = jax.nn.sigmoid(logits_t)

        g = _dot(hb, wsg_ref[...])
        u = _dot(hb, wsu_ref[...])
        a = (g * jax.nn.sigmoid(g) * u).astype(BF16)
        shr_ref[rows, :] = _dot(a, wsd_ref[...])


def _mix(ot, gya, sgb, x2, gate1, scale2, shift2, lg, lb, wts, tm, seq, alpha):
    t, d = x2.shape
    tps = seq // tm
    fw = ot.shape[1]
    n_exp = wts[2].shape[0]
    mod_spec = pl.BlockSpec((1, 1, d), lambda i: (i // tps, 0, 0))
    row_spec = lambda n: pl.BlockSpec((tm, n), lambda i: (i, 0))
    return pl.pallas_call(
        functools.partial(_mix_kernel, alpha=alpha, sub=min(MIX_ROWS_PER_GROUP, tm)),
        grid=(t // tm,),
        in_specs=[pl.BlockSpec((None, fw, tm), lambda i: (i // tps, 0, i % tps)),
                  row_spec(d), row_spec(d), row_spec(d), mod_spec, mod_spec, mod_spec,
                  _const_spec(lg.shape), _const_spec(lb.shape)] + [_const_spec(w.shape) for w in wts],
        out_specs=[row_spec(d), pl.BlockSpec((tm * ROW_SLABS, V7X_LANES), lambda i: (i, 0)),
                   pl.BlockSpec((n_exp, tm), lambda i: (0, i)), row_spec(d)],
        out_shape=[jax.ShapeDtypeStruct((t, d), F32),
                   jax.ShapeDtypeStruct((t * ROW_SLABS, V7X_LANES), jnp.uint32),
                   jax.ShapeDtypeStruct((n_exp, t), F32),
                   jax.ShapeDtypeStruct((t, d), F32)],
        compiler_params=_cparams(("arbitrary",)),
        name="mix",
    )(ot, gya, sgb, x2, gate1, scale2, shift2, lg, lb, *wts)


def _route_kernel(st_ref, bias_ref, idx_ref, wts_ref, rank_ref, cnt_ref, carry_ref):
    i = pl.program_id(0)
    n_exp, tr = st_ref.shape
    gsz = n_exp // N_GROUPS
    neg_inf = -jnp.inf

    @pl.when(i == 0)
    def _():
        carry_ref[...] = jnp.zeros_like(carry_ref)

    shape3 = (N_GROUPS, gsz, tr)

    def max01(v):
        return jnp.max(jnp.max(v, axis=0, keepdims=True), axis=1, keepdims=True)

    def min01(v):
        return jnp.min(jnp.min(v, axis=0, keepdims=True), axis=1, keepdims=True)

    def sum01(v):
        return jnp.sum(jnp.sum(v, axis=0, keepdims=True), axis=1, keepdims=True)

    sc = st_ref[...].reshape(shape3)
    gsel = (st_ref[...] + bias_ref[...]).reshape(shape3)
    pos = lax.broadcasted_iota(jnp.int32, shape3, 1)
    m1 = jnp.max(gsel, axis=1, keepdims=True)
    i1 = jnp.min(jnp.where(gsel == m1, pos, gsz), axis=1, keepdims=True)
    m2 = jnp.max(jnp.where(pos == i1, neg_inf, gsel), axis=1, keepdims=True)
    gs = m1 + m2

    gid = lax.broadcasted_iota(jnp.int32, gs.shape, 0)
    gkeep = jnp.zeros(gs.shape, F32)
    for _ in range(TOPK_GROUPS):
        mx = jnp.max(gs, axis=0, keepdims=True)
        gi = jnp.min(jnp.where(gs == mx, gid, N_GROUPS), axis=0, keepdims=True)
        hit = gid == gi
        gkeep = gkeep + jnp.where(hit, 1.0, 0.0)
        gs = jnp.where(hit, neg_inf, gs)

    cur0 = jnp.where(jnp.broadcast_to(gkeep, shape3) > 0.5, gsel, neg_inf)
    cur = cur0
    eid = lax.broadcasted_iota(jnp.int32, shape3, 0) * gsz + pos
    idxs, ws = [], []
    wsum = jnp.zeros((1, 1, tr), F32)
    for _ in range(TOP_K):
        mx = max01(cur)
        ik = min01(jnp.where(cur == mx, eid, n_exp))
        hit = eid == ik
        wk = sum01(jnp.where(hit, sc, 0.0))
        idxs.append(ik)
        ws.append(wk)
        wsum = wsum + wk
        cur = jnp.where(hit, neg_inf, cur)
    onehot = jnp.where(cur == cur0, 0.0, 1.0)

    ra = lax.broadcasted_iota(jnp.int32, (tr, tr), 0)
    rb = lax.broadcasted_iota(jnp.int32, (tr, tr), 1)
    upper = jnp.where(ra < rb, 1.0, 0.0).astype(BF16)
    onehot2 = onehot.reshape(n_exp, tr)
    prior = (_dot(onehot2.astype(BF16), upper) + carry_ref[...]).reshape(shape3)
    for slot in range(TOP_K):
        idx_ref[slot:slot + 1, :] = idxs[slot].reshape(1, tr)
        wts_ref[slot:slot + 1, :] = (ws[slot] / wsum * ROUTED_SCALE).reshape(1, tr)
        rk = sum01(jnp.where(eid == idxs[slot], prior, 0.0))
        rank_ref[slot:slot + 1, :] = rk.reshape(1, tr).astype(jnp.int32)
    total = carry_ref[...] + jnp.sum(onehot2, axis=1, keepdims=True)
    carry_ref[...] = total
    cnt_ref[...] = jnp.broadcast_to(total, cnt_ref.shape).astype(jnp.int32)


def _route(scores_t, bias_col, tr):
    n_exp, t = scores_t.shape
    slot_spec = pl.BlockSpec((TOP_K, tr), lambda i: (0, i))
    return pl.pallas_call(
        _route_kernel,
        grid=(t // tr,),
        in_specs=[pl.BlockSpec((n_exp, tr), lambda i: (0, i)), _const_spec(bias_col.shape)],
        out_specs=[slot_spec, slot_spec, slot_spec, _const_spec((n_exp, V7X_LANES))],
        out_shape=[jax.ShapeDtypeStruct((TOP_K, t), jnp.int32),
                   jax.ShapeDtypeStruct((TOP_K, t), F32),
                   jax.ShapeDtypeStruct((TOP_K, t), jnp.int32),
                   jax.ShapeDtypeStruct((n_exp, V7X_LANES), jnp.int32)],
        scratch_shapes=[pltpu.VMEM((n_exp, 1), F32)],
        compiler_params=_cparams(("arbitrary",)),
        name="route",
    )(scores_t, bias_col)


def _dest_kernel(pstart_ref, idx_ref, rank_ref, o_ref):
    n_exp = pstart_ref.shape[0]
    tr = idx_ref.shape[1]
    eid = lax.broadcasted_iota(jnp.int32, (n_exp, tr), 0)
    pstart = pstart_ref[...]
    for slot in range(TOP_K):
        hit = eid == idx_ref[slot:slot + 1, :]
        start = jnp.sum(jnp.where(hit, pstart, 0.0), axis=0, keepdims=True).astype(jnp.int32)
        o_ref[slot:slot + 1, :] = (start + rank_ref[slot:slot + 1, :]) * ROW_SLABS


def _dest(pstart, idx, rank, tr):
    k, t = idx.shape
    spec = pl.BlockSpec((k, tr), lambda i: (0, i))
    pstart_col = pstart.astype(F32)[:, None]
    return pl.pallas_call(
        _dest_kernel,
        grid=(t // tr,),
        in_specs=[_const_spec(pstart_col.shape), spec, spec],
        out_specs=spec,
        out_shape=jax.ShapeDtypeStruct((k, t), jnp.int32),
        compiler_params=_cparams(("arbitrary",)),
        name="dest",
    )(pstart_col, idx, rank)


def _dispatch_kernel(dest_ref, hp_ref, xs_ref, sem):
    td = hp_ref.shape[0] // ROW_SLABS

    def row_copy(r, slot):
        dst = pl.multiple_of(dest_ref[slot, r], ROW_SLABS)
        src = pl.multiple_of(r * ROW_SLABS, ROW_SLABS)
        return pltpu.make_async_copy(hp_ref.at[pl.ds(src, ROW_SLABS)], xs_ref.at[pl.ds(dst, ROW_SLABS)], sem)

    def start(r, c):
        for slot in range(TOP_K):
            row_copy(r, slot).start(priority=slot % 2)
        return c

    def wait(r, c):
        for slot in range(TOP_K):
            row_copy(r, slot).wait()
        return c

    lax.fori_loop(0, td, start, 0, unroll=4)
    lax.fori_loop(0, td, wait, 0, unroll=8)


def _dispatch(dest, hp, n_pad, td):
    t = hp.shape[0] // ROW_SLABS
    return pl.pallas_call(
        _dispatch_kernel,
        grid=(t // td,),
        in_specs=[pl.BlockSpec((TOP_K, td), lambda i: (0, i), memory_space=pltpu.SMEM),
                  pl.BlockSpec((td * ROW_SLABS, V7X_LANES), lambda i: (i, 0))],
        out_specs=pl.BlockSpec(memory_space=pl.ANY),
        out_shape=jax.ShapeDtypeStruct((n_pad * ROW_SLABS, V7X_LANES), hp.dtype),
        scratch_shapes=[pltpu.SemaphoreType.DMA(())],
        compiler_params=_cparams(("arbitrary",)),
        name="dispatch",
    )(dest, hp)


def _expert_kernel(be_ref, rows_ref, seq_ref, cnt_ref, xs_hbm, wg_hbm, wu_hbm, wd_hbm, ys_hbm,
                   xbuf, ybuf, wgf, wuf, wdf, wgu_s, wd_s, sem_x, sem_y, sem_w, *, layer):
    n = cnt_ref[0]
    n_seq = cnt_ref[1]
    blk_rows = EXPERT_BLOCK * ROW_SLABS
    nx, nw = EXPERT_X_AHEAD + 1, EXPERT_W_AHEAD + 1

    def block_rows(q):
        start = q * blk_rows
        return pl.ds(start if isinstance(q, int) else pl.multiple_of(start, blk_rows), blk_rows)

    def x_copy(q, slot):
        return pltpu.make_async_copy(xs_hbm.at[block_rows(q)], xbuf.at[slot], sem_x.at[slot])

    def y_copy(q, slot):
        return pltpu.make_async_copy(ybuf.at[slot], ys_hbm.at[block_rows(q)], sem_y.at[slot])

    def w_copies(k, ws):
        e = seq_ref[k]
        out = []
        for hbm, buf in ((wg_hbm, wgf), (wu_hbm, wuf), (wd_hbm, wdf)):
            band = buf.shape[1] // EXPERT_W_PARTS
            for part in range(EXPERT_W_PARTS):
                rows = pl.ds(part * band, band)
                out.append((pltpu.make_async_copy(hbm.at[layer, e, rows], buf.at[ws, rows], sem_w.at[ws]),
                            part % 2))
        return out

    for j in range(EXPERT_X_AHEAD):
        @pl.when(j < n)
        def _():
            x_copy(j, j).start(priority=1)

    for j in range(EXPERT_W_AHEAD):
        @pl.when(j < n_seq)
        def _():
            for cp, prio in w_copies(j, j):
                cp.start(priority=prio)

    def body(q, k):
        slot = q % 2
        fresh = jnp.logical_or(q == 0, be_ref[q] != be_ref[jnp.maximum(q - 1, 0)])

        @pl.when(fresh)
        def _():
            ws = k % nw
            for cp, _ in w_copies(k, ws):
                cp.wait()
            fw = wd_s.shape[0]
            wgu_s[:, :fw] = wgf[ws].astype(BF16)
            wgu_s[:, fw:] = wuf[ws].astype(BF16)
            wd_s[...] = wdf[ws].astype(BF16)

            @pl.when(k + EXPERT_W_AHEAD < n_seq)
            def _():
                for cp, prio in w_copies(k + EXPERT_W_AHEAD, (k + EXPERT_W_AHEAD) % nw):
                    cp.start(priority=prio)

        @pl.when(q + EXPERT_X_AHEAD < n)
        def _():
            x_copy(q + EXPERT_X_AHEAD, (q + EXPERT_X_AHEAD) % nx).start(priority=1)

        x_copy(q, q % nx).wait()

        @pl.when(q >= 2)
        def _():
            y_copy(q - 2, slot).wait()

        half = EXPERT_BLOCK // 2
        f = wd_s.shape[0]
        gu = []
        for h in range(2):
            x = jnp.concatenate(_load_packed_rows(xbuf, half, lead=(q % nx,), row0=h * half), axis=1)
            valid = lax.broadcasted_iota(jnp.int32, x.shape, 0) + h * half < rows_ref[q]
            x = jnp.where(valid, x, 0.0).astype(BF16)
            gu.append(_dot(x, wgu_s[...]))
        for h in range(2):
            g, u = gu[h][:, :f], gu[h][:, f:]
            a = (g * jax.nn.sigmoid(g) * u).astype(BF16)
            _store_packed_rows(ybuf.at[slot], _dot(a, wd_s[...]), row0=h * half)
        y_copy(q, slot).start()
        return k + fresh.astype(jnp.int32)

    lax.fori_loop(0, n, body, jnp.int32(0))

    @pl.when(n >= 2)
    def _():
        y_copy(n - 2, n % 2).wait()

    y_copy(n - 1, (n - 1) % 2).wait()


def _experts(block_expert, block_rows, expert_seq, counts2, xs, wg, wu, wd, layer):
    _, n_exp, d, f = wg.shape
    blk = (EXPERT_BLOCK * ROW_SLABS, V7X_LANES)
    nx, nw = EXPERT_X_AHEAD + 1, EXPERT_W_AHEAD + 1
    any_spec = pl.BlockSpec(memory_space=pl.ANY)
    return pl.pallas_call(
        functools.partial(_expert_kernel, layer=layer),
        grid_spec=pltpu.PrefetchScalarGridSpec(
            num_scalar_prefetch=4, grid=(1,),
            in_specs=[any_spec, any_spec, any_spec, any_spec],
            out_specs=any_spec,
            scratch_shapes=[pltpu.VMEM((nx,) + blk, xs.dtype), pltpu.VMEM((2,) + blk, xs.dtype),
                            pltpu.VMEM((nw, d, f), F32), pltpu.VMEM((nw, d, f), F32), pltpu.VMEM((nw, f, d), F32),
                            pltpu.VMEM((d, 2 * f), BF16), pltpu.VMEM((f, d), BF16),
                            pltpu.SemaphoreType.DMA((nx,)), pltpu.SemaphoreType.DMA((2,)),
                            pltpu.SemaphoreType.DMA((nw,))]),
        out_shape=jax.ShapeDtypeStruct(xs.shape, xs.dtype),
        compiler_params=_cparams(("arbitrary",)),
        name="experts",
    )(block_expert, block_rows, expert_seq, counts2, xs, wg, wu, wd)


def _combine_kernel(d0_ref, d1_ref, d2_ref, ys_ref, w_ref, shr_ref, x1_ref, g2_ref, lg_ref, lb_ref, o_ref,
                    buf_a, buf_b, sem, *, alpha, tc):
    i = pl.program_id(0)
    last = pl.num_programs(0) - 1

    def row_copy(dref, r, slot, buf, sem_idx):
        src = pl.multiple_of(dref[slot, r], ROW_SLABS)
        dst = r * ROW_SLABS if isinstance(r, int) else pl.multiple_of(r * ROW_SLABS, ROW_SLABS)
        return pltpu.make_async_copy(ys_ref.at[pl.ds(src, ROW_SLABS)], buf.at[slot, pl.ds(dst, ROW_SLABS)],
                                     sem.at[sem_idx])

    def issue_unrolled(dref, buf, sem_idx):
        for r in range(tc):
            for slot in range(TOP_K):
                row_copy(dref, r, slot, buf, sem_idx).start(priority=slot % 2)

    def issue_loop(dref, buf, sem_idx):
        def start(r, c):
            for slot in range(TOP_K):
                row_copy(dref, r, slot, buf, sem_idx).start(priority=slot % 2)
            return c
        lax.fori_loop(0, tc, start, 0, unroll=4)

    def wait_all(dref, buf, sem_idx):
        def wait(r, c):
            for slot in range(TOP_K):
                row_copy(dref, r, slot, buf, sem_idx).wait()
            return c
        lax.fori_loop(0, tc, wait, 0, unroll=8)

    def reduce_tile(buf, rows):
        w = w_ref[rows, :]
        chunks = None
        for slot in range(TOP_K):
            wk = w[:, slot:slot + 1]
            part = [c * wk for c in _load_packed_rows(buf, tc, lead=(slot,))]
            chunks = part if chunks is None else [a + b for a, b in zip(chunks, part)]
        y = shr_ref[rows, :] + jnp.concatenate(chunks, axis=1)
        z = alpha * x1_ref[rows, :] + (1.0 + g2_ref[0]) * y
        o_ref[rows, :] = _ln(z) * lg_ref[...] + lb_ref[...]

    @pl.when(i == 0)
    def _():
        issue_loop(d0_ref, buf_a, 0)

    wait_all(d0_ref, buf_a, 0)
    issue_unrolled(d1_ref, buf_b, 1)
    reduce_tile(buf_a, pl.ds(0, tc))

    wait_all(d1_ref, buf_b, 1)
    issue_unrolled(d2_ref, buf_a, 0)
    reduce_tile(buf_b, pl.ds(tc, tc))

    @pl.when(i == last)
    def _():
        wait_all(d2_ref, buf_a, 0)


def _combine(dest, ys, wts_tk, shared, x1, gate2, lg, lb, tc, seq, alpha):
    t, d = x1.shape
    tps = seq // (2 * tc)
    n_tiles = t // tc
    row_spec = lambda n: pl.BlockSpec((2 * tc, n), lambda i: (i, 0))
    tile_dest = lambda f: pl.BlockSpec((TOP_K, tc), f, memory_space=pltpu.SMEM)
    buf = pltpu.VMEM((TOP_K, tc * ROW_SLABS, V7X_LANES), jnp.uint32)
    return pl.pallas_call(
        functools.partial(_combine_kernel, alpha=alpha, tc=tc),
        grid=(n_tiles // 2,),
        in_specs=[tile_dest(lambda i: (0, 2 * i)), tile_dest(lambda i: (0, 2 * i + 1)),
                  tile_dest(lambda i: (0, jnp.minimum(2 * i + 2, n_tiles - 1))),
                  pl.BlockSpec(memory_space=pl.ANY),
                  row_spec(TOP_K), row_spec(d), row_spec(d),
                  pl.BlockSpec((1, 1, d), lambda i: (i // tps, 0, 0)),
                  _const_spec(lg.shape), _const_spec(lb.shape)],
        out_specs=row_spec(d),
        out_shape=jax.ShapeDtypeStruct((t, d), F32),
        scratch_shapes=[buf, buf, pltpu.SemaphoreType.DMA((2,))],
        compiler_params=_cparams(("arbitrary",)),
        name="combine",
    )(dest, dest, dest, ys, wts_tk, shared, x1, gate2, lg, lb)


def _placement():
    pq = np.zeros((V7X_LANES, FOX_HEADS * HEAD_PAD), np.float32)
    pk = np.zeros((V7X_LANES, FOX_HEADS * HEAD_PAD), np.float32)
    for h in range(FOX_HEADS):
        base = h * HEAD_PAD + FOX_HEAD_DIM
        for piece in range(3):
            pq[piece * FOX_HEADS + h, base + piece] = 1.0
            pk[FORGET_ONES_LANE, base + piece] = 1.0
            pq[FORGET_ONES_LANE, base + 3 + piece] = 1.0
            pk[piece * FOX_HEADS + h, base + 3 + piece] = -1.0
    return jnp.asarray(pq, BF16), jnp.asarray(pk, BF16)


def _inproj_weights(w_in, b_forget, d):
    conv2 = d
    fw = FOX_HEADS * FOX_HEAD_DIM
    o1, o2, o3, o4 = conv2, conv2 + fw, conv2 + 2 * fw, conv2 + 3 * fw
    o5 = o4 + FOX_HEADS
    o6 = o5 + d

    def pad_heads(w):
        w = w.reshape(d, FOX_HEADS, FOX_HEAD_DIM)
        w = jnp.pad(w, ((0, 0), (0, 0), (0, HEAD_PAD - FOX_HEAD_DIM)))
        return w.reshape(d, FOX_HEADS * HEAD_PAD).astype(BF16)

    wglu = w_in[:, :o1].astype(BF16)
    wq = pad_heads(w_in[:, o1:o2])
    wk = pad_heads(w_in[:, o2:o3])
    wvt = w_in[:, o3:o4].T.astype(BF16)
    wf8 = w_in[:, o4:o5]
    wf = jnp.pad(jnp.concatenate([wf8, wf8, wf8], axis=1), ((0, 0), (0, V7X_LANES - 3 * FOX_HEADS))).astype(BF16)
    bf = jnp.pad(jnp.concatenate([b_forget, b_forget, b_forget]), (0, V7X_LANES - 3 * FOX_HEADS))[None, :].astype(F32)
    wga = w_in[:, o5:o6].astype(BF16)
    wgb = w_in[:, o6:].astype(BF16)
    pq, pk = _placement()
    return (wglu, wq, wk, wvt, wf, wga, wgb, bf, pq, pk)


def _tile_sizes(seq, t):
    return {"inproj": min(512, seq),
            "mix": min(2 * MIX_ROWS_PER_GROUP, seq),
            "route": min(512, t),
            "dispatch": min(256, t),
            "combine": min(128, seq)}


def _layer(x2, ada, bsz, seq, w_in, b_forget, conv_w, conv_b, conv_ln_g, conv_ln_b, w_conv_out, w_fox_out,
           w_mix_out, ln1_g, ln1_b, w_router, router_bias, w_exp_gate, w_exp_up, w_exp_down,
           w_sh_gate, w_sh_up, w_sh_down, ln2_g, ln2_b, depth, layer):
    t, d = x2.shape
    n_exp = w_router.shape[1]
    alpha = (2.0 * depth) ** 0.25
    mods = [ada[:bsz, j * d:(j + 1) * d][:, None, :] for j in range(6)]
    shift1, scale1, gate1, shift2, scale2, gate2 = mods

    tiles = _tile_sizes(seq, t)
    tm = tiles["inproj"]
    conv_w_pad = jnp.pad(conv_w, ((0, CONV_HALO - CONV_WIDTH), (0, 0)))
    conv_wts = (conv_w_pad, conv_b[None, :], conv_ln_g[None, :], conv_ln_b[None, :], w_conv_out.astype(BF16))
    q, k, vt, gya, sgb = _inproj(x2, scale1, shift1, _inproj_weights(w_in, b_forget, d), conv_wts, tm, seq)

    ot = _attn(q.reshape(bsz, seq, -1), k.reshape(bsz, seq, -1), vt, tm, ATTN_HEADS_PER_STEP)

    tmx = tiles["mix"]
    wr_t = w_router.T
    wr_h = wr_t.astype(BF16)
    wr_l = (wr_t - wr_h.astype(F32)).astype(BF16)
    mix_w = (w_fox_out.astype(BF16), w_mix_out.astype(BF16), wr_h, wr_l,
             w_sh_gate.astype(BF16), w_sh_up.astype(BF16), w_sh_down.astype(BF16))
    x1, hp, scores_t, shared = _mix(ot, gya, sgb, x2, gate1, scale2, shift2, ln1_g[None, :], ln1_b[None, :],
                                    mix_w, tmx, seq, alpha)

    tr = tiles["route"]
    idx, wts, rank, cnt = _route(scores_t, router_bias[:, None], tr)

    counts = cnt[:, 0]
    padded = (counts + EXPERT_BLOCK - 1) // EXPERT_BLOCK * EXPERT_BLOCK
    pend = jnp.cumsum(padded)
    pstart = (pend - padded).astype(jnp.int32)
    n_assign = t * TOP_K
    n_pad = -(-(n_assign + n_exp * (EXPERT_BLOCK - 1)) // EXPERT_BLOCK) * EXPERT_BLOCK
    n_blocks = n_pad // EXPERT_BLOCK
    block_start = jnp.arange(n_blocks, dtype=jnp.int32) * EXPERT_BLOCK
    block_expert = jnp.minimum(jnp.sum(pend[None, :] <= block_start[:, None], axis=1), n_exp - 1).astype(jnp.int32)
    block_rows = jnp.clip((pstart + counts)[block_expert] - block_start, 0, EXPERT_BLOCK).astype(jnp.int32)
    n_used = (pend[-1:] // EXPERT_BLOCK).astype(jnp.int32)
    owns = counts > 0
    expert_seq = jnp.nonzero(owns, size=n_exp, fill_value=0)[0].astype(jnp.int32)
    counts2 = jnp.concatenate([n_used, jnp.sum(owns, dtype=jnp.int32)[None]])

    dest = _dest(pstart, idx, rank, tr)
    xs = _dispatch(dest, hp, n_pad, tiles["dispatch"])
    ys = _experts(block_expert, block_rows, expert_seq, counts2, xs, w_exp_gate, w_exp_up, w_exp_down, layer)
    return _combine(dest, ys, wts.T, shared, x1, gate2, ln2_g[None, :], ln2_b[None, :],
                    tiles["combine"], seq, alpha)


def kernel(x, c, w_ada, b_ada, w_in, b_forget, conv_w, conv_b, conv_ln_g, conv_ln_b, w_conv_out, w_fox_out,
           w_mix_out, ln1_g, ln1_b, w_router, router_bias, w_exp_gate, w_exp_up, w_exp_down, w_sh_gate,
           w_sh_up, w_sh_down, ln2_g, ln2_b):
    bsz, seq, d = x.shape
    depth = w_ada.shape[0]
    c_pad = jnp.pad(c, ((0, -bsz % V7X_SUBLANES), (0, 0)))
    x2 = x.reshape(bsz * seq, d)
    for l in range(depth):
        ada = _ada(c_pad, w_ada[l], b_ada[l][None, :])
        x2 = _layer(x2, ada, bsz, seq, w_in[l], b_forget[l], conv_w[l], conv_b[l], conv_ln_g[l], conv_ln_b[l],
                    w_conv_out[l], w_fox_out[l], w_mix_out[l], ln1_g[l], ln1_b[l], w_router[l], router_bias[l],
                    w_exp_gate, w_exp_up, w_exp_down, w_sh_gate[l], w_sh_up[l], w_sh_down[l],
                    ln2_g[l], ln2_b[l], depth, l)
    return x2.reshape(bsz, seq, d)
```

```python
import functools

import jax
import jax.numpy as jnp
import numpy as np
from jax import lax
from jax.experimental import pallas as pl
from jax.experimental.pallas import tpu as pltpu

F32 = jnp.float32
BF16 = jnp.bfloat16

LN_EPS = 1e-5
CONV_WIDTH = 31
FOX_HEADS = 8
FOX_HEAD_DIM = 64
N_GROUPS = 8
TOPK_GROUPS = 4
TOP_K = 8
ROUTED_SCALE = 2.5
EXPERT_BLOCK = 256
EXPERT_X_AHEAD = 4
EXPERT_W_AHEAD = 4
EXPERT_W_PARTS = 16
EXPERT_XY_PARTS = 4
ROW_SLABS = 4

V7X_LANES = 128
HEAD_PAD = 128
FORGET_ONES_LANE = 3 * FOX_HEADS
V7X_SUBLANES = 8
ADA_COLS = 1024
CONV_CHUNK = 64
CONV_HALO = 32
VMEM_LIMIT = 56 * 1024 * 1024
NEG_BIG = -1e30
LOG2E = 1.4426950408889634
ATTN_EXTRA_ROWS = 16
ATTN_HEADS_PER_STEP = 8
ATTN_Q_COLS = 256
MIX_ROWS_PER_GROUP = 256
ATTN_PIPE_LAG = 3


def _cparams(sem):
    return pltpu.CompilerParams(dimension_semantics=sem, vmem_limit_bytes=VMEM_LIMIT)


def _ln(v):
    mu = jnp.mean(v, axis=-1, keepdims=True)
    vc = v - mu
    var = jnp.mean(vc * vc, axis=-1, keepdims=True)
    return vc * lax.rsqrt(var + LN_EPS)


def _split3(v):
    hi = v.astype(BF16)
    r1 = v - hi.astype(F32)
    mid = r1.astype(BF16)
    lo = (r1 - mid.astype(F32)).astype(BF16)
    return hi, mid, lo


def _dot(a, b):
    return jnp.dot(a, b, preferred_element_type=F32)


def _dot_nt(a, b):
    return lax.dot_general(a, b, (((1,), (1,)), ((), ())), preferred_element_type=F32)


def _dot_tn(a, b):
    return lax.dot_general(a, b, (((0,), (0,)), ((), ())), preferred_element_type=F32)


def _store_packed_rows(ref, v, row0=0):
    n, d = v.shape
    half = d // 2
    vb = v.astype(BF16).astype(F32)
    lo_bits = lax.bitcast_convert_type(vb[:, :half], jnp.uint32)
    hi_bits = lax.bitcast_convert_type(vb[:, half:], jnp.uint32)
    words = (lo_bits >> 16) | (hi_bits & jnp.uint32(0xFFFF0000))
    for c in range(ROW_SLABS):
        ref[pl.ds(row0 * ROW_SLABS + c, n, stride=ROW_SLABS), :] = words[:, c * V7X_LANES:(c + 1) * V7X_LANES]


def _load_packed_rows(ref, n, lead=(), row0=0):
    lo, hi = [], []
    for c in range(ROW_SLABS):
        w = ref[lead + (pl.ds(row0 * ROW_SLABS + c, n, stride=ROW_SLABS), slice(None))]
        lo.append(lax.bitcast_convert_type(w << 16, F32))
        hi.append(lax.bitcast_convert_type(w & jnp.uint32(0xFFFF0000), F32))
    return lo + hi


def _const_spec(shape, single=False):
    nd = len(shape)
    if single:
        return pl.BlockSpec(shape, lambda *_: (0,) * nd, pipeline_mode=pl.Buffered(1))
    return pl.BlockSpec(shape, lambda *_: (0,) * nd)


def _ada_kernel(c_ref, w_ref, b_ref, o_ref):
    c = c_ref[...]
    cond = c * jax.nn.sigmoid(c)
    ch, cm, _ = _split3(cond)
    w = w_ref[...]
    wh, wm, _ = _split3(w)
    o_ref[...] = _dot(ch, wh) + _dot(ch, wm) + _dot(cm, wh) + b_ref[...]


def _ada(c_pad, w, b):
    rows, d = c_pad.shape
    n = w.shape[1]
    tn = ADA_COLS
    return pl.pallas_call(
        _ada_kernel,
        grid=(n // tn,),
        in_specs=[_const_spec((rows, d)),
                  pl.BlockSpec((d, tn), lambda j: (0, j)),
                  pl.BlockSpec((1, tn), lambda j: (0, j))],
        out_specs=pl.BlockSpec((rows, tn), lambda j: (0, j)),
        out_shape=jax.ShapeDtypeStruct((rows, n), F32),
        compiler_params=_cparams(("arbitrary",)),
        name="ada",
    )(c_pad, w, b)


def _inproj_kernel(x_ref, sc_ref, sh_ref, wglu_ref, wq_ref, wk_ref, wvt_ref, wf_ref, wga_ref, wgb_ref,
                   bf_ref, pq_ref, pk_ref, cw_ref, cb_ref, cg_ref, cbe_ref, wco_ref,
                   q_ref, k_ref, vt_ref, gya_ref, sgb_ref, carry_ref, halo_ref, ext_ref, shift_ref,
                   *, tiles_per_seq, conv_ch, chunk):
    i = pl.program_id(0)
    tm = x_ref.shape[0]

    @pl.when(i % tiles_per_seq == 0)
    def _():
        carry_ref[...] = jnp.zeros_like(carry_ref)
        halo_ref[...] = jnp.zeros_like(halo_ref)

    h = _ln(x_ref[...]) * (1.0 + sc_ref[0]) + sh_ref[0]
    hb = h.astype(BF16)

    glu = _dot(hb, wglu_ref[...])
    u = glu[:, :conv_ch] * jax.nn.sigmoid(glu[:, conv_ch:])

    ext_ref[0:CONV_HALO, :] = halo_ref[...]
    ext_ref[CONV_HALO:, :] = u
    halo_ref[...] = u[tm - CONV_HALO:, :]
    cw = cw_ref[...]
    off = CONV_HALO - (CONV_WIDTH - 1)
    span = tm + CONV_HALO - 8
    for res in range(1, 8):
        shift_ref[res - 1] = ext_ref[res:res + span, :]
    def conv_chunk(c0):
        acc = jnp.zeros((chunk, conv_ch), F32)
        for j in range(CONV_WIDTH):
            res, lo = (off + j) % 8, c0 + (off + j) // 8 * 8
            rows = ext_ref[lo:lo + chunk, :] if res == 0 else shift_ref[res - 1, lo:lo + chunk, :]
            acc = acc + cw[j:j + 1, :] * rows
        return acc

    n_chunks = tm // chunk
    per_gap = -(-n_chunks // 4)
    outs = []

    def conv_chunks():
        for _ in range(per_gap):
            if len(outs) < n_chunks:
                outs.append(conv_chunk(len(outs) * chunk))

    sgb_ref[...] = jax.nn.sigmoid(_dot(hb, wgb_ref[...])).astype(BF16)
    conv_chunks()

    f = _dot(hb, wf_ref[...]) + bf_ref[...]
    logf = jnp.minimum(f, 0.0) - jnp.log(1.0 + jnp.exp(-jnp.abs(f)))
    lh, lm, ll = _split3(logf)
    row = lax.broadcasted_iota(jnp.int32, (tm, tm), 0)
    col = lax.broadcasted_iota(jnp.int32, (tm, tm), 1)
    tri = jnp.where(row >= col, 1.0, 0.0).astype(BF16)
    cs = _dot(tri, lh) + _dot(tri, lm) + _dot(tri, ll)
    cum = cs + carry_ref[...]
    carry_ref[...] = cum[tm - 1:tm, :]
    conv_chunks()

    ch, cm, cl = _split3(cum * LOG2E)
    lane = lax.broadcasted_iota(jnp.int32, cum.shape, 1)
    nh = FOX_HEADS
    tail = jnp.where(lane == FORGET_ONES_LANE, 1.0, 0.0)
    pieces = jnp.where(lane < nh, ch.astype(F32), jnp.where(lane < 2 * nh, cm.astype(F32),
                       jnp.where(lane < 3 * nh, cl.astype(F32), tail))).astype(BF16)
    scale = FOX_HEAD_DIM ** -0.5 * LOG2E
    q_ref[...] = (_dot(hb, wq_ref[...]) * scale + _dot(pieces, pq_ref[...])).astype(BF16)
    conv_chunks()
    k_ref[...] = (_dot(hb, wk_ref[...]) + _dot(pieces, pk_ref[...])).astype(BF16)
    conv_chunks()
    vt_ref[0, 0] = _dot_nt(wvt_ref[...], hb).astype(BF16)

    v = jnp.concatenate(outs, axis=0) + cb_ref[...]
    v = _ln(v) * cg_ref[...] + cbe_ref[...]
    v = v * jax.nn.sigmoid(v)
    sga = jax.nn.sigmoid(_dot(hb, wga_ref[...]))
    gya_ref[...] = (sga * _dot(v.astype(BF16), wco_ref[...])).astype(BF16)


def _inproj(x2, scale1, shift1, wts, conv_wts, tm, seq):
    t, d = x2.shape
    tps = seq // tm
    bsz = t // seq
    wglu, wq, wk, wvt, wf, wga, wgb, bf, pq, pk = wts
    conv_ch = wglu.shape[1] // 2
    fw = wvt.shape[0]
    qw = wq.shape[1]
    mod_spec = pl.BlockSpec((1, 1, d), lambda i: (i // tps, 0, 0))
    row_spec = lambda n: pl.BlockSpec((tm, n), lambda i: (i, 0))
    consts = list(wts) + list(conv_wts)
    return pl.pallas_call(
        functools.partial(_inproj_kernel, tiles_per_seq=tps, conv_ch=conv_ch, chunk=CONV_CHUNK),
        grid=(t // tm,),
        in_specs=[row_spec(d), mod_spec, mod_spec] + [_const_spec(w.shape, single=True) for w in consts],
        out_specs=[row_spec(qw), row_spec(qw),
                   pl.BlockSpec((1, 1, fw, tm), lambda i: (i // tps, i % tps, 0, 0)),
                   row_spec(d), row_spec(d)],
        out_shape=[jax.ShapeDtypeStruct((t, qw), BF16),
                   jax.ShapeDtypeStruct((t, qw), BF16),
                   jax.ShapeDtypeStruct((bsz, tps, fw, tm), BF16),
                   jax.ShapeDtypeStruct((t, d), BF16),
                   jax.ShapeDtypeStruct((t, d), BF16)],
        scratch_shapes=[pltpu.VMEM((1, V7X_LANES), F32), pltpu.VMEM((CONV_HALO, conv_ch), F32),
                        pltpu.VMEM((tm + CONV_HALO, conv_ch), F32),
                        pltpu.VMEM((7, tm + CONV_HALO - 8, conv_ch), F32)],
        compiler_params=_cparams(("arbitrary",)),
        name="inproj",
    )(x2, scale1, shift1, *consts)


def _attn_kernel(q_ref, k_ref, vt_ref, o_ref, *, blk, heads):
    qi = pl.program_id(2)
    row = lax.broadcasted_iota(jnp.int32, (ATTN_EXTRA_ROWS, blk), 0)
    ones_rows = jnp.where(row == 0, 1.0, 0.0).astype(BF16)

    ncol = blk // ATTN_Q_COLS
    chains = [(j, c) for j in range(heads) for c in range(ncol)]

    def scores(kj, chain, masked):
        j, c = chain
        k = k_ref[pl.ds(pl.multiple_of(kj * blk, blk), blk), j * HEAD_PAD:(j + 1) * HEAD_PAD]
        q = q_ref[c * ATTN_Q_COLS:(c + 1) * ATTN_Q_COLS, j * HEAD_PAD:(j + 1) * HEAD_PAD]
        s = _dot_nt(k, q)
        if masked:
            kpos = lax.broadcasted_iota(jnp.int32, s.shape, 0)
            qpos = lax.broadcasted_iota(jnp.int32, s.shape, 1) + c * ATTN_Q_COLS
            s = jnp.where(kpos <= qpos, s, NEG_BIG)
        return s

    def probs(s, m):
        m_new = jnp.maximum(m, jnp.max(s, axis=0, keepdims=True))
        return jnp.exp2(s - m_new).astype(BF16), m_new

    def update(kj, chain, p, m, m_new, acc):
        j, _ = chain
        vt = vt_ref[kj, j * FOX_HEAD_DIM:(j + 1) * FOX_HEAD_DIM, :]
        lhs = jnp.concatenate([vt, ones_rows], axis=0)
        return jnp.exp2(m - m_new) * acc + _dot(lhs, p)

    def step(kj, carry, masked):
        n, lag = len(chains), ATTN_PIPE_LAG
        s, pm, out = {}, {}, [None] * n
        for i in range(n + lag):
            if i < n:
                s[i] = scores(kj, chains[i], masked)
            if lag - 1 <= i < n + lag - 1:
                pm[i - lag + 1] = probs(s.pop(i - lag + 1), carry[i - lag + 1][0])
            if i >= lag:
                p, m_new = pm.pop(i - lag)
                m, acc = carry[i - lag]
                out[i - lag] = (m_new, update(kj, chains[i - lag], p, m, m_new, acc))
        return tuple(out)

    init = tuple((jnp.full((1, ATTN_Q_COLS), NEG_BIG, F32),
                  jnp.zeros((FOX_HEAD_DIM + ATTN_EXTRA_ROWS, ATTN_Q_COLS), F32)) for _ in chains)
    carry = lax.fori_loop(0, qi, lambda kj, cr: step(kj, cr, False), init)
    carry = step(qi, carry, True)
    for (j, c), (_, acc) in zip(chains, carry):
        o_ref[j * FOX_HEAD_DIM:(j + 1) * FOX_HEAD_DIM, c * ATTN_Q_COLS:(c + 1) * ATTN_Q_COLS] = (
            acc[:FOX_HEAD_DIM] / acc[FOX_HEAD_DIM:FOX_HEAD_DIM + 1]).astype(BF16)


def _attn(q, k, vt, blk, heads):
    bsz, seq, _ = q.shape
    nkb = seq // blk
    return pl.pallas_call(
        functools.partial(_attn_kernel, blk=blk, heads=heads),
        grid=(bsz, FOX_HEADS // heads, seq // blk),
        in_specs=[pl.BlockSpec((None, blk, heads * HEAD_PAD), lambda b, h, i: (b, i, h)),
                  pl.BlockSpec((None, seq, heads * HEAD_PAD), lambda b, h, i: (b, 0, h),
                               pipeline_mode=pl.Buffered(1)),
                  pl.BlockSpec((None, nkb, heads * FOX_HEAD_DIM, blk), lambda b, h, i: (b, 0, h, 0),
                               pipeline_mode=pl.Buffered(1))],
        out_specs=pl.BlockSpec((None, heads * FOX_HEAD_DIM, blk), lambda b, h, i: (b, h, i)),
        out_shape=jax.ShapeDtypeStruct((bsz, FOX_HEADS * FOX_HEAD_DIM, seq), BF16),
        compiler_params=_cparams(("arbitrary", "arbitrary", "arbitrary")),
        name="attn",
    )(q, k, vt)


def _mix_kernel(ot_ref, gya_ref, sgb_ref, x_ref, g1_ref, sc2_ref, sh2_ref, lg_ref, lb_ref,
                wfox_ref, wmix_ref, wrh_ref, wrl_ref, wsg_ref, wsu_ref, wsd_ref,
                x1_ref, hp_ref, st_ref, shr_ref, *, alpha, sub):
    tm = x_ref.shape[0]
    groups = [pl.ds(r0, sub) for r0 in range(0, tm, sub)]
    ys = []
    for rows in groups:
        yb = _dot_tn(ot_ref[:, rows], wfox_ref[...])
        merged = gya_ref[rows, :].astype(F32) + sgb_ref[rows, :].astype(F32) * yb
        ys.append(_dot(merged.astype(BF16), wmix_ref[...]))
    for rows, y in zip(groups, ys):
        x1 = _ln(alpha * x_ref[rows, :] + (1.0 + g1_ref[0]) * y) * lg_ref[...] + lb_ref[...]
        x1_ref[rows, :] = x1
        h2 = _ln(x1) * (1.0 + sc2_ref[0]) + sh2_ref[0]
        hb = h2.astype(BF16)
        hl = (h2 - hb.astype(F32)).astype(BF16)
        _store_packed_rows(hp_ref, h2, row0=rows.start)

        logits_t = _dot_nt(wrh_ref[...], hb) + _dot_nt(wrl_ref[...], hb) + _dot_nt(wrh_ref[...], hl)
        st_ref[:, rows] = jax.nn.sigmoid(logits_t)

        g = _dot(hb, wsg_ref[...])
        u = _dot(hb, wsu_ref[...])
        a = (g * jax.nn.sigmoid(g) * u).astype(BF16)
        shr_ref[rows, :] = _dot(a, wsd_ref[...])


def _mix(ot, gya, sgb, x2, gate1, scale2, shift2, lg, lb, wts, tm, seq, alpha):
    t, d = x2.shape
    tps = seq // tm
    fw = ot.shape[1]
    n_exp = wts[2].shape[0]
    mod_spec = pl.BlockSpec((1, 1, d), lambda i: (i // tps, 0, 0))
    row_spec = lambda n: pl.BlockSpec((tm, n), lambda i: (i, 0))
    return pl.pallas_call(
        functools.partial(_mix_kernel, alpha=alpha, sub=min(MIX_ROWS_PER_GROUP, tm)),
        grid=(t // tm,),
        in_specs=[pl.BlockSpec((None, fw, tm), lambda i: (i // tps, 0, i % tps)),
                  row_spec(d), row_spec(d), row_spec(d), mod_spec, mod_spec, mod_spec,
                  _const_spec(lg.shape), _const_spec(lb.shape)] + [_const_spec(w.shape) for w in wts],
        out_specs=[row_spec(d), pl.BlockSpec((tm * ROW_SLABS, V7X_LANES), lambda i: (i, 0)),
                   pl.BlockSpec((n_exp, tm), lambda i: (0, i)), row_spec(d)],
        out_shape=[jax.ShapeDtypeStruct((t, d), F32),
                   jax.ShapeDtypeStruct((t * ROW_SLABS, V7X_LANES), jnp.uint32),
                   jax.ShapeDtypeStruct((n_exp, t), F32),
                   jax.ShapeDtypeStruct((t, d), F32)],
        compiler_params=_cparams(("arbitrary",)),
        name="mix",
    )(ot, gya, sgb, x2, gate1, scale2, shift2, lg, lb, *wts)


def _route_kernel(st_ref, bias_ref, idx_ref, wts_ref, rank_ref, cnt_ref, carry_ref):
    i = pl.program_id(0)
    n_exp, tr = st_ref.shape
    gsz = n_exp // N_GROUPS
    neg_inf = -jnp.inf

    @pl.when(i == 0)
    def _():
        carry_ref[...] = jnp.zeros_like(carry_ref)

    shape3 = (N_GROUPS, gsz, tr)

    def max01(v):
        return jnp.max(jnp.max(v, axis=0, keepdims=True), axis=1, keepdims=True)

    def min01(v):
        return jnp.min(jnp.min(v, axis=0, keepdims=True), axis=1, keepdims=True)

    def sum01(v):
        return jnp.sum(jnp.sum(v, axis=0, keepdims=True), axis=1, keepdims=True)

    sc = st_ref[...].reshape(shape3)
    gsel = (st_ref[...] + bias_ref[...]).reshape(shape3)
    pos = lax.broadcasted_iota(jnp.int32, shape3, 1)
    m1 = jnp.max(gsel, axis=1, keepdims=True)
    i1 = jnp.min(jnp.where(gsel == m1, pos, gsz), axis=1, keepdims=True)
    m2 = jnp.max(jnp.where(pos == i1, neg_inf, gsel), axis=1, keepdims=True)
    gs = m1 + m2

    gid = lax.broadcasted_iota(jnp.int32, gs.shape, 0)
    gkeep = jnp.zeros(gs.shape, F32)
    for _ in range(TOPK_GROUPS):
        mx = jnp.max(gs, axis=0, keepdims=True)
        gi = jnp.min(jnp.where(gs == mx, gid, N_GROUPS), axis=0, keepdims=True)
        hit = gid == gi
        gkeep = gkeep + jnp.where(hit, 1.0, 0.0)
        gs = jnp.where(hit, neg_inf, gs)

    cur0 = jnp.where(jnp.broadcast_to(gkeep, shape3) > 0.5, gsel, neg_inf)
    cur = cur0
    eid = lax.broadcasted_iota(jnp.int32, shape3, 0) * gsz + pos
    idxs, ws = [], []
    wsum = jnp.zeros((1, 1, tr), F32)
    for _ in range(TOP_K):
        mx = max01(cur)
        ik = min01(jnp.where(cur == mx, eid, n_exp))
        hit = eid == ik
        wk = sum01(jnp.where(hit, sc, 0.0))
        idxs.append(ik)
        ws.append(wk)
        wsum = wsum + wk
        cur = jnp.where(hit, neg_inf, cur)
    onehot = jnp.where(cur == cur0, 0.0, 1.0)

    ra = lax.broadcasted_iota(jnp.int32, (tr, tr), 0)
    rb = lax.broadcasted_iota(jnp.int32, (tr, tr), 1)
    upper = jnp.where(ra < rb, 1.0, 0.0).astype(BF16)
    onehot2 = onehot.reshape(n_exp, tr)
    prior = (_dot(onehot2.astype(BF16), upper) + carry_ref[...]).reshape(shape3)
    for slot in range(TOP_K):
        idx_ref[slot:slot + 1, :] = idxs[slot].reshape(1, tr)
        wts_ref[slot:slot + 1, :] = (ws[slot] / wsum * ROUTED_SCALE).reshape(1, tr)
        rk = sum01(jnp.where(eid == idxs[slot], prior, 0.0))
        rank_ref[slot:slot + 1, :] = rk.reshape(1, tr).astype(jnp.int32)
    total = carry_ref[...] + jnp.sum(onehot2, axis=1, keepdims=True)
    carry_ref[...] = total
    cnt_ref[...] = jnp.broadcast_to(total, cnt_ref.shape).astype(jnp.int32)


def _route(scores_t, bias_col, tr):
    n_exp, t = scores_t.shape
    slot_spec = pl.BlockSpec((TOP_K, tr), lambda i: (0, i))
    return pl.pallas_call(
        _route_kernel,
        grid=(t // tr,),
        in_specs=[pl.BlockSpec((n_exp, tr), lambda i: (0, i)), _const_spec(bias_col.shape)],
        out_specs=[slot_spec, slot_spec, slot_spec, _const_spec((n_exp, V7X_LANES))],
        out_shape=[jax.ShapeDtypeStruct((TOP_K, t), jnp.int32),
                   jax.ShapeDtypeStruct((TOP_K, t), F32),
                   jax.ShapeDtypeStruct((TOP_K, t), jnp.int32),
                   jax.ShapeDtypeStruct((n_exp, V7X_LANES), jnp.int32)],
        scratch_shapes=[pltpu.VMEM((n_exp, 1), F32)],
        compiler_params=_cparams(("arbitrary",)),
        name="route",
    )(scores_t, bias_col)


def _dest_kernel(pstart_ref, idx_ref, rank_ref, o_ref):
    n_exp = pstart_ref.shape[0]
    tr = idx_ref.shape[1]
    eid = lax.broadcasted_iota(jnp.int32, (n_exp, tr), 0)
    pstart = pstart_ref[...]
    for slot in range(TOP_K):
        hit = eid == idx_ref[slot:slot + 1, :]
        start = jnp.sum(jnp.where(hit, pstart, 0.0), axis=0, keepdims=True).astype(jnp.int32)
        o_ref[slot:slot + 1, :] = (start + rank_ref[slot:slot + 1, :]) * ROW_SLABS


def _dest(pstart, idx, rank, tr):
    k, t = idx.shape
    spec = pl.BlockSpec((k, tr), lambda i: (0, i))
    pstart_col = pstart.astype(F32)[:, None]
    return pl.pallas_call(
        _dest_kernel,
        grid=(t // tr,),
        in_specs=[_const_spec(pstart_col.shape), spec, spec],
        out_specs=spec,
        out_shape=jax.ShapeDtypeStruct((k, t), jnp.int32),
        compiler_params=_cparams(("arbitrary",)),
        name="dest",
    )(pstart_col, idx, rank)


def _dispatch_kernel(dest_ref, hp_ref, xs_ref, sem):
    td = hp_ref.shape[0] // ROW_SLABS

    def row_copy(r, slot):
        dst = pl.multiple_of(dest_ref[slot, r], ROW_SLABS)
        src = pl.multiple_of(r * ROW_SLABS, ROW_SLABS)
        return pltpu.make_async_copy(hp_ref.at[pl.ds(src, ROW_SLABS)], xs_ref.at[pl.ds(dst, ROW_SLABS)], sem)

    def start(r, c):
        for slot in range(TOP_K):
            row_copy(r, slot).start(priority=slot % 2)
        return c

    def wait(r, c):
        for slot in range(TOP_K):
            row_copy(r, slot).wait()
        return c

    lax.fori_loop(0, td, start, 0, unroll=4)
    lax.fori_loop(0, td, wait, 0, unroll=8)


def _dispatch(dest, hp, n_pad, td):
    t = hp.shape[0] // ROW_SLABS
    return pl.pallas_call(
        _dispatch_kernel,
        grid=(t // td,),
        in_specs=[pl.BlockSpec((TOP_K, td), lambda i: (0, i), memory_space=pltpu.SMEM),
                  pl.BlockSpec((td * ROW_SLABS, V7X_LANES), lambda i: (i, 0))],
        out_specs=pl.BlockSpec(memory_space=pl.ANY),
        out_shape=jax.ShapeDtypeStruct((n_pad * ROW_SLABS, V7X_LANES), hp.dtype),
        scratch_shapes=[pltpu.SemaphoreType.DMA(())],
        compiler_params=_cparams(("arbitrary",)),
        name="dispatch",
    )(dest, hp)


class _CopyGroup:
    def __init__(self, copies):
        self.copies = copies

    def start(self, priority=0):
        for i, cp in enumerate(self.copies):
            cp.start(priority=(priority + i) % 2)

    def wait(self):
        for cp in self.copies:
            cp.wait()


def _expert_kernel(be_ref, rows_ref, seq_ref, cnt_ref, xs_hbm, wg_hbm, wu_hbm, wd_hbm, ys_hbm,
                   xbuf, ybuf, wgf, wuf, wdf, wgu_s, wd_s, sem_x, sem_y, sem_w, *, layer):
    n = cnt_ref[0]
    n_seq = cnt_ref[1]
    blk_rows = EXPERT_BLOCK * ROW_SLABS
    nx, nw = EXPERT_X_AHEAD + 1, EXPERT_W_AHEAD + 1

    def block_rows(q):
        start = q * blk_rows
        return pl.ds(start if isinstance(q, int) else pl.multiple_of(start, blk_rows), blk_rows)

    def band_rows(q, part):
        band = blk_rows // EXPERT_XY_PARTS
        start = q * blk_rows + part * band
        return (pl.ds(start if isinstance(q, int) else pl.multiple_of(start, band), band),
                pl.ds(part * band, band))

    def x_copy(q, slot):
        return _CopyGroup([pltpu.make_async_copy(xs_hbm.at[band_rows(q, p)[0]], xbuf.at[slot, band_rows(q, p)[1]],
                                                 sem_x.at[slot]) for p in range(EXPERT_XY_PARTS)])

    def y_copy(q, slot):
        return _CopyGroup([pltpu.make_async_copy(ybuf.at[slot, band_rows(q, p)[1]], ys_hbm.at[band_rows(q, p)[0]],
                                                 sem_y.at[slot]) for p in range(EXPERT_XY_PARTS)])

    def w_copies(k, ws):
        e = seq_ref[k]
        out = []
        for hbm, buf in ((wg_hbm, wgf), (wu_hbm, wuf), (wd_hbm, wdf)):
            band = buf.shape[1] // EXPERT_W_PARTS
            for part in range(EXPERT_W_PARTS):
                rows = pl.ds(part * band, band)
                out.append((pltpu.make_async_copy(hbm.at[layer, e, rows], buf.at[ws, rows], sem_w.at[ws]),
                            part % 2))
        return out

    for j in range(EXPERT_X_AHEAD):
        @pl.when(j < n)
        def _():
            x_copy(j, j).start(priority=1)

    for j in range(EXPERT_W_AHEAD):
        @pl.when(j < n_seq)
        def _():
            for cp, prio in w_copies(j, j):
                cp.start(priority=prio)

    def body(q, k):
        slot = q % 2
        fresh = jnp.logical_or(q == 0, be_ref[q] != be_ref[jnp.maximum(q - 1, 0)])

        @pl.when(fresh)
        def _():
            ws = k % nw
            for cp, _ in w_copies(k, ws):
                cp.wait()
            fw = wd_s.shape[0]
            wgu_s[:, :fw] = wgf[ws].astype(BF16)
            wgu_s[:, fw:] = wuf[ws].astype(BF16)
            wd_s[...] = wdf[ws].astype(BF16)

            @pl.when(k + EXPERT_W_AHEAD < n_seq)
            def _():
                for cp, prio in w_copies(k + EXPERT_W_AHEAD, (k + EXPERT_W_AHEAD) % nw):
                    cp.start(priority=prio)

        @pl.when(q + EXPERT_X_AHEAD < n)
        def _():
            x_copy(q + EXPERT_X_AHEAD, (q + EXPERT_X_AHEAD) % nx).start(priority=1)

        x_copy(q, q % nx).wait()

        @pl.when(q >= 2)
        def _():
            y_copy(q - 2, slot).wait()

        half = EXPERT_BLOCK // 2
        f = wd_s.shape[0]
        gu = []
        for h in range(2):
            x = jnp.concatenate(_load_packed_rows(xbuf, half, lead=(q % nx,), row0=h * half), axis=1)
            valid = lax.broadcasted_iota(jnp.int32, x.shape, 0) + h * half < rows_ref[q]
            x = jnp.where(valid, x, 0.0).astype(BF16)
            gu.append(_dot(x, wgu_s[...]))
        for h in range(2):
            g, u = gu[h][:, :f], gu[h][:, f:]
            a = (g * jax.nn.sigmoid(g) * u).astype(BF16)
            _store_packed_rows(ybuf.at[slot], _dot(a, wd_s[...]), row0=h * half)
        y_copy(q, slot).start()
        return k + fresh.astype(jnp.int32)

    lax.fori_loop(0, n, body, jnp.int32(0))

    @pl.when(n >= 2)
    def _():
        y_copy(n - 2, n % 2).wait()

    y_copy(n - 1, (n - 1) % 2).wait()


def _experts(block_expert, block_rows, expert_seq, counts2, xs, wg, wu, wd, layer):
    _, n_exp, d, f = wg.shape
    blk = (EXPERT_BLOCK * ROW_SLABS, V7X_LANES)
    nx, nw = EXPERT_X_AHEAD + 1, EXPERT_W_AHEAD + 1
    any_spec = pl.BlockSpec(memory_space=pl.ANY)
    return pl.pallas_call(
        functools.partial(_expert_kernel, layer=layer),
        grid_spec=pltpu.PrefetchScalarGridSpec(
            num_scalar_prefetch=4, grid=(1,),
            in_specs=[any_spec, any_spec, any_spec, any_spec],
            out_specs=any_spec,
            scratch_shapes=[pltpu.VMEM((nx,) + blk, xs.dtype), pltpu.VMEM((2,) + blk, xs.dtype),
                            pltpu.VMEM((nw, d, f), F32), pltpu.VMEM((nw, d, f), F32), pltpu.VMEM((nw, f, d), F32),
                            pltpu.VMEM((d, 2 * f), BF16), pltpu.VMEM((f, d), BF16),
                            pltpu.SemaphoreType.DMA((nx,)), pltpu.SemaphoreType.DMA((2,)),
                            pltpu.SemaphoreType.DMA((nw,))]),
        out_shape=jax.ShapeDtypeStruct(xs.shape, xs.dtype),
        compiler_params=_cparams(("arbitrary",)),
        name="experts",
    )(block_expert, block_rows, expert_seq, counts2, xs, wg, wu, wd)


def _combine_kernel(d0_ref, d1_ref, d2_ref, ys_ref, w_ref, shr_ref, x1_ref, g2_ref, lg_ref, lb_ref, o_ref,
                    buf_a, buf_b, sem, *, alpha, tc):
    i = pl.program_id(0)
    last = pl.num_programs(0) - 1

    def row_copy(dref, r, slot, buf, sem_idx):
        src = pl.multiple_of(dref[slot, r], ROW_SLABS)
        dst = r * ROW_SLABS if isinstance(r, int) else pl.multiple_of(r * ROW_SLABS, ROW_SLABS)
        return pltpu.make_async_copy(ys_ref.at[pl.ds(src, ROW_SLABS)], buf.at[slot, pl.ds(dst, ROW_SLABS)],
                                     sem.at[sem_idx])

    def issue_unrolled(dref, buf, sem_idx):
        for r in range(tc):
            for slot in range(TOP_K):
                row_copy(dref, r, slot, buf, sem_idx).start(priority=slot % 2)

    def issue_loop(dref, buf, sem_idx):
        def start(r, c):
            for slot in range(TOP_K):
                row_copy(dref, r, slot, buf, sem_idx).start(priority=slot % 2)
            return c
        lax.fori_loop(0, tc, start, 0, unroll=4)

    def wait_all(dref, buf, sem_idx):
        def wait(r, c):
            for slot in range(TOP_K):
                row_copy(dref, r, slot, buf, sem_idx).wait()
            return c
        lax.fori_loop(0, tc, wait, 0, unroll=8)

    def reduce_tile(buf, rows):
        w = w_ref[rows, :]
        chunks = None
        for slot in range(TOP_K):
            wk = w[:, slot:slot + 1]
            part = [c * wk for c in _load_packed_rows(buf, tc, lead=(slot,))]
            chunks = part if chunks is None else [a + b for a, b in zip(chunks, part)]
        y = shr_ref[rows, :] + jnp.concatenate(chunks, axis=1)
        z = alpha * x1_ref[rows, :] + (1.0 + g2_ref[0]) * y
        o_ref[rows, :] = _ln(z) * lg_ref[...] + lb_ref[...]

    @pl.when(i == 0)
    def _():
        issue_loop(d0_ref, buf_a, 0)

    wait_all(d0_ref, buf_a, 0)
    issue_unrolled(d1_ref, buf_b, 1)
    reduce_tile(buf_a, pl.ds(0, tc))

    wait_all(d1_ref, buf_b, 1)
    issue_unrolled(d2_ref, buf_a, 0)
    reduce_tile(buf_b, pl.ds(tc, tc))

    @pl.when(i == last)
    def _():
        wait_all(d2_ref, buf_a, 0)


def _combine(dest, ys, wts_tk, shared, x1, gate2, lg, lb, tc, seq, alpha):
    t, d = x1.shape
    tps = seq // (2 * tc)
    n_tiles = t // tc
    row_spec = lambda n: pl.BlockSpec((2 * tc, n), lambda i: (i, 0))
    tile_dest = lambda f: pl.BlockSpec((TOP_K, tc), f, memory_space=pltpu.SMEM)
    buf = pltpu.VMEM((TOP_K, tc * ROW_SLABS, V7X_LANES), jnp.uint32)
    return pl.pallas_call(
        functools.partial(_combine_kernel, alpha=alpha, tc=tc),
        grid=(n_tiles // 2,),
        in_specs=[tile_dest(lambda i: (0, 2 * i)), tile_dest(lambda i: (0, 2 * i + 1)),
                  tile_dest(lambda i: (0, jnp.minimum(2 * i + 2, n_tiles - 1))),
                  pl.BlockSpec(memory_space=pl.ANY),
                  row_spec(TOP_K), row_spec(d), row_spec(d),
                  pl.BlockSpec((1, 1, d), lambda i: (i // tps, 0, 0)),
                  _const_spec(lg.shape), _const_spec(lb.shape)],
        out_specs=row_spec(d),
        out_shape=jax.ShapeDtypeStruct((t, d), F32),
        scratch_shapes=[buf, buf, pltpu.SemaphoreType.DMA((2,))],
        compiler_params=_cparams(("arbitrary",)),
        name="combine",
    )(dest, dest, dest, ys, wts_tk, shared, x1, gate2, lg, lb)


def _placement():
    pq = np.zeros((V7X_LANES, FOX_HEADS * HEAD_PAD), np.float32)
    pk = np.zeros((V7X_LANES, FOX_HEADS * HEAD_PAD), np.float32)
    for h in range(FOX_HEADS):
        base = h * HEAD_PAD + FOX_HEAD_DIM
        for piece in range(3):
            pq[piece * FOX_HEADS + h, base + piece] = 1.0
            pk[FORGET_ONES_LANE, base + piece] = 1.0
            pq[FORGET_ONES_LANE, base + 3 + piece] = 1.0
            pk[piece * FOX_HEADS + h, base + 3 + piece] = -1.0
    return jnp.asarray(pq, BF16), jnp.asarray(pk, BF16)


def _inproj_weights(w_in, b_forget, d):
    conv2 = d
    fw = FOX_HEADS * FOX_HEAD_DIM
    o1, o2, o3, o4 = conv2, conv2 + fw, conv2 + 2 * fw, conv2 + 3 * fw
    o5 = o4 + FOX_HEADS
    o6 = o5 + d

    def pad_heads(w):
        w = w.reshape(d, FOX_HEADS, FOX_HEAD_DIM)
        w = jnp.pad(w, ((0, 0), (0, 0), (0, HEAD_PAD - FOX_HEAD_DIM)))
        return w.reshape(d, FOX_HEADS * HEAD_PAD).astype(BF16)

    wglu = w_in[:, :o1].astype(BF16)
    wq = pad_heads(w_in[:, o1:o2])
    wk = pad_heads(w_in[:, o2:o3])
    wvt = w_in[:, o3:o4].T.astype(BF16)
    wf8 = w_in[:, o4:o5]
    wf = jnp.pad(jnp.concatenate([wf8, wf8, wf8], axis=1), ((0, 0), (0, V7X_LANES - 3 * FOX_HEADS))).astype(BF16)
    bf = jnp.pad(jnp.concatenate([b_forget, b_forget, b_forget]), (0, V7X_LANES - 3 * FOX_HEADS))[None, :].astype(F32)
    wga = w_in[:, o5:o6].astype(BF16)
    wgb = w_in[:, o6:].astype(BF16)
    pq, pk = _placement()
    return (wglu, wq, wk, wvt, wf, wga, wgb, bf, pq, pk)


def _tile_sizes(seq, t):
    return {"inproj": min(512, seq),
            "mix": min(2 * MIX_ROWS_PER_GROUP, seq),
            "route": min(512, t),
            "dispatch": min(256, t),
            "combine": min(128, seq)}


def _layer(x2, ada, bsz, seq, w_in, b_forget, conv_w, conv_b, conv_ln_g, conv_ln_b, w_conv_out, w_fox_out,
           w_mix_out, ln1_g, ln1_b, w_router, router_bias, w_exp_gate, w_exp_up, w_exp_down,
           w_sh_gate, w_sh_up, w_sh_down, ln2_g, ln2_b, depth, layer):
    t, d = x2.shape
    n_exp = w_router.shape[1]
    alpha = (2.0 * depth) ** 0.25
    mods = [ada[:bsz, j * d:(j + 1) * d][:, None, :] for j in range(6)]
    shift1, scale1, gate1, shift2, scale2, gate2 = mods

    tiles = _tile_sizes(seq, t)
    tm = tiles["inproj"]
    conv_w_pad = jnp.pad(conv_w, ((0, CONV_HALO - CONV_WIDTH), (0, 0)))
    conv_wts = (conv_w_pad, conv_b[None, :], conv_ln_g[None, :], conv_ln_b[None, :], w_conv_out.astype(BF16))
    q, k, vt, gya, sgb = _inproj(x2, scale1, shift1, _inproj_weights(w_in, b_forget, d), conv_wts, tm, seq)

    ot = _attn(q.reshape(bsz, seq, -1), k.reshape(bsz, seq, -1), vt, tm, ATTN_HEADS_PER_STEP)

    tmx = tiles["mix"]
    wr_t = w_router.T
    wr_h = wr_t.astype(BF16)
    wr_l = (wr_t - wr_h.astype(F32)).astype(BF16)
    mix_w = (w_fox_out.astype(BF16), w_mix_out.astype(BF16), wr_h, wr_l,
             w_sh_gate.astype(BF16), w_sh_up.astype(BF16), w_sh_down.astype(BF16))
    x1, hp, scores_t, shared = _mix(ot, gya, sgb, x2, gate1, scale2, shift2, ln1_g[None, :], ln1_b[None, :],
                                    mix_w, tmx, seq, alpha)

    tr = tiles["route"]
    idx, wts, rank, cnt = _route(scores_t, router_bias[:, None], tr)

    counts = cnt[:, 0]
    padded = (counts + EXPERT_BLOCK - 1) // EXPERT_BLOCK * EXPERT_BLOCK
    pend = jnp.cumsum(padded)
    pstart = (pend - padded).astype(jnp.int32)
    n_assign = t * TOP_K
    n_pad = -(-(n_assign + n_exp * (EXPERT_BLOCK - 1)) // EXPERT_BLOCK) * EXPERT_BLOCK
    n_blocks = n_pad // EXPERT_BLOCK
    block_start = jnp.arange(n_blocks, dtype=jnp.int32) * EXPERT_BLOCK
    block_expert = jnp.minimum(jnp.sum(pend[None, :] <= block_start[:, None], axis=1), n_exp - 1).astype(jnp.int32)
    block_rows = jnp.clip((pstart + counts)[block_expert] - block_start, 0, EXPERT_BLOCK).astype(jnp.int32)
    n_used = (pend[-1:] // EXPERT_BLOCK).astype(jnp.int32)
    owns = counts > 0
    expert_seq = jnp.nonzero(owns, size=n_exp, fill_value=0)[0].astype(jnp.int32)
    counts2 = jnp.concatenate([n_used, jnp.sum(owns, dtype=jnp.int32)[None]])

    dest = _dest(pstart, idx, rank, tr)
    xs = _dispatch(dest, hp, n_pad, tiles["dispatch"])
    ys = _experts(block_expert, block_rows, expert_seq, counts2, xs, w_exp_gate, w_exp_up, w_exp_down, layer)
    return _combine(dest, ys, wts.T, shared, x1, gate2, ln2_g[None, :], ln2_b[None, :],
                    tiles["combine"], seq, alpha)


def kernel(x, c, w_ada, b_ada, w_in, b_forget, conv_w, conv_b, conv_ln_g, conv_ln_b, w_conv_out, w_fox_out,
           w_mix_out, ln1_g, ln1_b, w_router, router_bias, w_exp_gate, w_exp_up, w_exp_down, w_sh_gate,
           w_sh_up, w_sh_down, ln2_g, ln2_b):
    bsz, seq, d = x.shape
    depth = w_ada.shape[0]
    c_pad = jnp.pad(c, ((0, -bsz % V7X_SUBLANES), (0, 0)))
    x2 = x.reshape(bsz * seq, d)
    for l in range(depth):
        ada = _ada(c_pad, w_ada[l], b_ada[l][None, :])
        x2 = _layer(x2, ada, bsz, seq, w_in[l], b_forget[l], conv_w[l], conv_b[l], conv_ln_g[l], conv_ln_b[l],
                    w_conv_out[l], w_fox_out[l], w_mix_out[l], ln1_g[l], ln1_b[l], w_router[l], router_bias[l],
                    w_exp_gate, w_exp_up, w_exp_down, w_sh_gate[l], w_sh_up[l], w_sh_down[l],
                    ln2_g[l], ln2_b[l], depth, l)
    return x2.reshape(bsz, seq, d)
```

```python
import functools

import jax
import jax.numpy as jnp
import numpy as np
from jax import lax
from jax.experimental import pallas as pl
from jax.experimental.pallas import tpu as pltpu

F32 = jnp.float32
BF16 = jnp.bfloat16

LN_EPS = 1e-5
CONV_WIDTH = 31
FOX_HEADS = 8
FOX_HEAD_DIM = 64
N_GROUPS = 8
TOPK_GROUPS = 4
TOP_K = 8
ROUTED_SCALE = 2.5
EXPERT_BLOCK = 256
EXPERT_X_AHEAD = 4
EXPERT_W_AHEAD = 4
EXPERT_W_PARTS = 16
ROW_SLABS = 4

V7X_LANES = 128
HEAD_PAD = 128
FORGET_ONES_LANE = 3 * FOX_HEADS
V7X_SUBLANES = 8
ADA_COLS = 1024
CONV_CHUNK = 64
CONV_HALO = 32
VMEM_LIMIT = 56 * 1024 * 1024
NEG_BIG = -1e30
LOG2E = 1.4426950408889634
ATTN_EXTRA_ROWS = 16
ATTN_HEADS_PER_STEP = 8
ATTN_Q_COLS = 256
MIX_ROWS_PER_GROUP = 256
ATTN_PIPE_LAG = 3


def _cparams(sem):
    return pltpu.CompilerParams(dimension_semantics=sem, vmem_limit_bytes=VMEM_LIMIT)


def _ln(v):
    mu = jnp.mean(v, axis=-1, keepdims=True)
    vc = v - mu
    var = jnp.mean(vc * vc, axis=-1, keepdims=True)
    return vc * lax.rsqrt(var + LN_EPS)


def _split3(v):
    hi = v.astype(BF16)
    r1 = v - hi.astype(F32)
    mid = r1.astype(BF16)
    lo = (r1 - mid.astype(F32)).astype(BF16)
    return hi, mid, lo


def _dot(a, b):
    return jnp.dot(a, b, preferred_element_type=F32)


def _dot_nt(a, b):
    return lax.dot_general(a, b, (((1,), (1,)), ((), ())), preferred_element_type=F32)


def _dot_tn(a, b):
    return lax.dot_general(a, b, (((0,), (0,)), ((), ())), preferred_element_type=F32)


def _store_packed_rows(ref, v, row0=0):
    n, d = v.shape
    half = d // 2
    vb = v.astype(BF16).astype(F32)
    lo_bits = lax.bitcast_convert_type(vb[:, :half], jnp.uint32)
    hi_bits = lax.bitcast_convert_type(vb[:, half:], jnp.uint32)
    words = (lo_bits >> 16) | (hi_bits & jnp.uint32(0xFFFF0000))
    for c in range(ROW_SLABS):
        ref[pl.ds(row0 * ROW_SLABS + c, n, stride=ROW_SLABS), :] = words[:, c * V7X_LANES:(c + 1) * V7X_LANES]


def _load_packed_rows(ref, n, lead=(), row0=0):
    lo, hi = [], []
    for c in range(ROW_SLABS):
        w = ref[lead + (pl.ds(row0 * ROW_SLABS + c, n, stride=ROW_SLABS), slice(None))]
        lo.append(lax.bitcast_convert_type(w << 16, F32))
        hi.append(lax.bitcast_convert_type(w & jnp.uint32(0xFFFF0000), F32))
    return lo + hi


def _const_spec(shape, single=False):
    nd = len(shape)
    if single:
        return pl.BlockSpec(shape, lambda *_: (0,) * nd, pipeline_mode=pl.Buffered(1))
    return pl.BlockSpec(shape, lambda *_: (0,) * nd)


def _ada_kernel(c_ref, w_ref, b_ref, o_ref):
    c = c_ref[...]
    cond = c * jax.nn.sigmoid(c)
    ch, cm, _ = _split3(cond)
    w = w_ref[...]
    wh, wm, _ = _split3(w)
    o_ref[...] = _dot(ch, wh) + _dot(ch, wm) + _dot(cm, wh) + b_ref[...]


def _ada(c_pad, w, b):
    rows, d = c_pad.shape
    n = w.shape[1]
    tn = ADA_COLS
    return pl.pallas_call(
        _ada_kernel,
        grid=(n // tn,),
        in_specs=[_const_spec((rows, d)),
                  pl.BlockSpec((d, tn), lambda j: (0, j)),
                  pl.BlockSpec((1, tn), lambda j: (0, j))],
        out_specs=pl.BlockSpec((rows, tn), lambda j: (0, j)),
        out_shape=jax.ShapeDtypeStruct((rows, n), F32),
        compiler_params=_cparams(("arbitrary",)),
        name="ada",
    )(c_pad, w, b)


def _inproj_kernel(x_ref, sc_ref, sh_ref, wglu_ref, wq_ref, wk_ref, wvt_ref, wf_ref, wga_ref, wgb_ref,
                   bf_ref, pq_ref, pk_ref, cw_ref, cb_ref, cg_ref, cbe_ref, wco_ref,
                   q_ref, k_ref, vt_ref, gya_ref, sgb_ref, carry_ref, halo_ref, ext_ref, shift_ref,
                   *, tiles_per_seq, conv_ch, chunk):
    i = pl.program_id(0)
    tm = x_ref.shape[0]

    @pl.when(i % tiles_per_seq == 0)
    def _():
        carry_ref[...] = jnp.zeros_like(carry_ref)
        halo_ref[...] = jnp.zeros_like(halo_ref)

    h = _ln(x_ref[...]) * (1.0 + sc_ref[0]) + sh_ref[0]
    hb = h.astype(BF16)

    glu = _dot(hb, wglu_ref[...])
    u = glu[:, :conv_ch] * jax.nn.sigmoid(glu[:, conv_ch:])

    ext_ref[0:CONV_HALO, :] = halo_ref[...]
    ext_ref[CONV_HALO:, :] = u
    halo_ref[...] = u[tm - CONV_HALO:, :]
    cw = cw_ref[...]
    off = CONV_HALO - (CONV_WIDTH - 1)
    span = tm + CONV_HALO - 8
    for res in range(1, 8):
        shift_ref[res - 1] = ext_ref[res:res + span, :]
    def conv_chunk(c0):
        acc = jnp.zeros((chunk, conv_ch), F32)
        for j in range(CONV_WIDTH):
            res, lo = (off + j) % 8, c0 + (off + j) // 8 * 8
            rows = ext_ref[lo:lo + chunk, :] if res == 0 else shift_ref[res - 1, lo:lo + chunk, :]
            acc = acc + cw[j:j + 1, :] * rows
        return acc

    n_chunks = tm // chunk
    per_gap = -(-n_chunks // 4)
    outs = []

    def conv_chunks():
        for _ in range(per_gap):
            if len(outs) < n_chunks:
                outs.append(conv_chunk(len(outs) * chunk))

    sgb_ref[...] = jax.nn.sigmoid(_dot(hb, wgb_ref[...])).astype(BF16)
    conv_chunks()

    f = _dot(hb, wf_ref[...]) + bf_ref[...]
    logf = jnp.minimum(f, 0.0) - jnp.log(1.0 + jnp.exp(-jnp.abs(f)))
    lh, lm, ll = _split3(logf)
    row = lax.broadcasted_iota(jnp.int32, (tm, tm), 0)
    col = lax.broadcasted_iota(jnp.int32, (tm, tm), 1)
    tri = jnp.where(row >= col, 1.0, 0.0).astype(BF16)
    cs = _dot(tri, lh) + _dot(tri, lm) + _dot(tri, ll)
    cum = cs + carry_ref[...]
    carry_ref[...] = cum[tm - 1:tm, :]
    conv_chunks()

    ch, cm, cl = _split3(cum * LOG2E)
    lane = lax.broadcasted_iota(jnp.int32, cum.shape, 1)
    nh = FOX_HEADS
    tail = jnp.where(lane == FORGET_ONES_LANE, 1.0, 0.0)
    pieces = jnp.where(lane < nh, ch.astype(F32), jnp.where(lane < 2 * nh, cm.astype(F32),
                       jnp.where(lane < 3 * nh, cl.astype(F32), tail))).astype(BF16)
    scale = FOX_HEAD_DIM ** -0.5 * LOG2E
    q_ref[...] = (_dot(hb, wq_ref[...]) * scale + _dot(pieces, pq_ref[...])).astype(BF16)
    conv_chunks()
    k_ref[...] = (_dot(hb, wk_ref[...]) + _dot(pieces, pk_ref[...])).astype(BF16)
    conv_chunks()
    vt_ref[0, 0] = _dot_nt(wvt_ref[...], hb).astype(BF16)

    v = jnp.concatenate(outs, axis=0) + cb_ref[...]
    v = _ln(v) * cg_ref[...] + cbe_ref[...]
    v = v * jax.nn.sigmoid(v)
    sga = jax.nn.sigmoid(_dot(hb, wga_ref[...]))
    gya_ref[...] = (sga * _dot(v.astype(BF16), wco_ref[...])).astype(BF16)


def _inproj(x2, scale1, shift1, wts, conv_wts, tm, seq):
    t, d = x2.shape
    tps = seq // tm
    bsz = t // seq
    wglu, wq, wk, wvt, wf, wga, wgb, bf, pq, pk = wts
    conv_ch = wglu.shape[1] // 2
    fw = wvt.shape[0]
    qw = wq.shape[1]
    mod_spec = pl.BlockSpec((1, 1, d), lambda i: (i // tps, 0, 0))
    row_spec = lambda n: pl.BlockSpec((tm, n), lambda i: (i, 0))
    consts = list(wts) + list(conv_wts)
    return pl.pallas_call(
        functools.partial(_inproj_kernel, tiles_per_seq=tps, conv_ch=conv_ch, chunk=CONV_CHUNK),
        grid=(t // tm,),
        in_specs=[row_spec(d), mod_spec, mod_spec] + [_const_spec(w.shape, single=True) for w in consts],
        out_specs=[row_spec(qw), row_spec(qw),
                   pl.BlockSpec((1, 1, fw, tm), lambda i: (i // tps, i % tps, 0, 0)),
                   row_spec(d), row_spec(d)],
        out_shape=[jax.ShapeDtypeStruct((t, qw), BF16),
                   jax.ShapeDtypeStruct((t, qw), BF16),
                   jax.ShapeDtypeStruct((bsz, tps, fw, tm), BF16),
                   jax.ShapeDtypeStruct((t, d), BF16),
                   jax.ShapeDtypeStruct((t, d), BF16)],
        scratch_shapes=[pltpu.VMEM((1, V7X_LANES), F32), pltpu.VMEM((CONV_HALO, conv_ch), F32),
                        pltpu.VMEM((tm + CONV_HALO, conv_ch), F32),
                        pltpu.VMEM((7, tm + CONV_HALO - 8, conv_ch), F32)],
        compiler_params=_cparams(("arbitrary",)),
        name="inproj",
    )(x2, scale1, shift1, *consts)


def _attn_kernel(q_ref, k_ref, vt_ref, o_ref, *, blk, heads):
    qi = pl.program_id(2)
    row = lax.broadcasted_iota(jnp.int32, (ATTN_EXTRA_ROWS, blk), 0)
    ones_rows = jnp.where(row == 0, 1.0, 0.0).astype(BF16)

    ncol = blk // ATTN_Q_COLS
    chains = [(j, c) for j in range(heads) for c in range(ncol)]

    def scores(kj, chain, masked):
        j, c = chain
        k = k_ref[pl.ds(pl.multiple_of(kj * blk, blk), blk), j * HEAD_PAD:(j + 1) * HEAD_PAD]
        q = q_ref[c * ATTN_Q_COLS:(c + 1) * ATTN_Q_COLS, j * HEAD_PAD:(j + 1) * HEAD_PAD]
        s = _dot_nt(k, q)
        if masked:
            kpos = lax.broadcasted_iota(jnp.int32, s.shape, 0)
            qpos = lax.broadcasted_iota(jnp.int32, s.shape, 1) + c * ATTN_Q_COLS
            s = jnp.where(kpos <= qpos, s, NEG_BIG)
        return s

    def probs(s, m):
        m_new = jnp.maximum(m, jnp.max(s, axis=0, keepdims=True))
        return jnp.exp2(s - m_new).astype(BF16), m_new

    def update(kj, chain, p, m, m_new, acc):
        j, _ = chain
        vt = vt_ref[kj, j * FOX_HEAD_DIM:(j + 1) * FOX_HEAD_DIM, :]
        lhs = jnp.concatenate([vt, ones_rows], axis=0)
        return jnp.exp2(m - m_new) * acc + _dot(lhs, p)

    def step(kj, carry, masked):
        n, lag = len(chains), ATTN_PIPE_LAG
        s, pm, out = {}, {}, [None] * n
        for i in range(n + lag):
            if i < n:
                s[i] = scores(kj, chains[i], masked)
            if lag - 1 <= i < n + lag - 1:
                pm[i - lag + 1] = probs(s.pop(i - lag + 1), carry[i - lag + 1][0])
            if i >= lag:
                p, m_new = pm.pop(i - lag)
                m, acc = carry[i - lag]
                out[i - lag] = (m_new, update(kj, chains[i - lag], p, m, m_new, acc))
        return tuple(out)

    init = tuple((jnp.full((1, ATTN_Q_COLS), NEG_BIG, F32),
                  jnp.zeros((FOX_HEAD_DIM + ATTN_EXTRA_ROWS, ATTN_Q_COLS), F32)) for _ in chains)
    carry = lax.fori_loop(0, qi, lambda kj, cr: step(kj, cr, False), init)
    carry = step(qi, carry, True)
    for (j, c), (_, acc) in zip(chains, carry):
        o_ref[j * FOX_HEAD_DIM:(j + 1) * FOX_HEAD_DIM, c * ATTN_Q_COLS:(c + 1) * ATTN_Q_COLS] = (
            acc[:FOX_HEAD_DIM] / acc[FOX_HEAD_DIM:FOX_HEAD_DIM + 1]).astype(BF16)


def _attn(q, k, vt, blk, heads):
    bsz, seq, _ = q.shape
    nkb = seq // blk
    return pl.pallas_call(
        functools.partial(_attn_kernel, blk=blk, heads=heads),
        grid=(bsz, FOX_HEADS // heads, seq // blk),
        in_specs=[pl.BlockSpec((None, blk, heads * HEAD_PAD), lambda b, h, i: (b, i, h)),
                  pl.BlockSpec((None, seq, heads * HEAD_PAD), lambda b, h, i: (b, 0, h),
                               pipeline_mode=pl.Buffered(1)),
                  pl.BlockSpec((None, nkb, heads * FOX_HEAD_DIM, blk), lambda b, h, i: (b, 0, h, 0),
                               pipeline_mode=pl.Buffered(1))],
        out_specs=pl.BlockSpec((None, heads * FOX_HEAD_DIM, blk), lambda b, h, i: (b, h, i)),
        out_shape=jax.ShapeDtypeStruct((bsz, FOX_HEADS * FOX_HEAD_DIM, seq), BF16),
        compiler_params=_cparams(("arbitrary", "arbitrary", "arbitrary")),
        name="attn",
    )(q, k, vt)


def _mix_kernel(ot_ref, gya_ref, sgb_ref, x_ref, g1_ref, sc2_ref, sh2_ref, lg_ref, lb_ref,
                wfox_ref, wmix_ref, wrh_ref, wrl_ref, wsg_ref, wsu_ref, wsd_ref,
                x1_ref, hp_ref, st_ref, shr_ref, *, alpha, sub):
    tm = x_ref.shape[0]
    groups = [pl.ds(r0, sub) for r0 in range(0, tm, sub)]
    ys = []
    for rows in groups:
        yb = _dot_tn(ot_ref[:, rows], wfox_ref[...])
        merged = gya_ref[rows, :].astype(F32) + sgb_ref[rows, :].astype(F32) * yb
        ys.append(_dot(merged.astype(BF16), wmix_ref[...]))
    for rows, y in zip(groups, ys):
        x1 = _ln(alpha * x_ref[rows, :] + (1.0 + g1_ref[0]) * y) * lg_ref[...] + lb_ref[...]
        x1_ref[rows, :] = x1
        h2 = _ln(x1) * (1.0 + sc2_ref[0]) + sh2_ref[0]
        hb = h2.astype(BF16)
        hl = (h2 - hb.astype(F32)).astype(BF16)
        _store_packed_rows(hp_ref, h2, row0=rows.start)

        logits_t = _dot_nt(wrh_ref[...], hb) + _dot_nt(wrl_ref[...], hb) + _dot_nt(wrh_ref[...], hl)
        st_ref[:, rows] = jax.nn.sigmoid(logits_t)

        g = _dot(hb, wsg_ref[...])
        u = _dot(hb, wsu_ref[...])
        a = (g * jax.nn.sigmoid(g) * u).astype(BF16)
        shr_ref[rows, :] = _dot(a, wsd_ref[...])


def _mix(ot, gya, sgb, x2, gate1, scale2, shift2, lg, lb, wts, tm, seq, alpha):
    t, d = x2.shape
    tps = seq // tm
    fw = ot.shape[1]
    n_exp = wts[2].shape[0]
    mod_spec = pl.BlockSpec((1, 1, d), lambda i: (i // tps, 0, 0))
    row_spec = lambda n: pl.BlockSpec((tm, n), lambda i: (i, 0))
    return pl.pallas_call(
        functools.partial(_mix_kernel, alpha=alpha, sub=min(MIX_ROWS_PER_GROUP, tm)),
        grid=(t // tm,),
        in_specs=[pl.BlockSpec((None, fw, tm), lambda i: (i // tps, 0, i % tps)),
                  row_spec(d), row_spec(d), row_spec(d), mod_spec, mod_spec, mod_spec,
                  _const_spec(lg.shape), _const_spec(lb.shape)] + [_const_spec(w.shape) for w in wts],
        out_specs=[row_spec(d), pl.BlockSpec((tm * ROW_SLABS, V7X_LANES), lambda i: (i, 0)),
                   pl.BlockSpec((n_exp, tm), lambda i: (0, i)), row_spec(d)],
        out_shape=[jax.ShapeDtypeStruct((t, d), F32),
                   jax.ShapeDtypeStruct((t * ROW_SLABS, V7X_LANES), jnp.uint32),
                   jax.ShapeDtypeStruct((n_exp, t), F32),
                   jax.ShapeDtypeStruct((t, d), F32)],
        compiler_params=_cparams(("arbitrary",)),
        name="mix",
    )(ot, gya, sgb, x2, gate1, scale2, shift2, lg, lb, *wts)


def _route_kernel(st_ref, bias_ref, idx_ref, wts_ref, rank_ref, cnt_ref, carry_ref):
    i = pl.program_id(0)
    n_exp, tr = st_ref.shape
    gsz = n_exp // N_GROUPS
    neg_inf = -jnp.inf

    @pl.when(i == 0)
    def _():
        carry_ref[...] = jnp.zeros_like(carry_ref)

    shape3 = (N_GROUPS, gsz, tr)

    def max01(v):
        return jnp.max(jnp.max(v, axis=0, keepdims=True), axis=1, keepdims=True)

    def min01(v):
        return jnp.min(jnp.min(v, axis=0, keepdims=True), axis=1, keepdims=True)

    def sum01(v):
        return jnp.sum(jnp.sum(v, axis=0, keepdims=True), axis=1, keepdims=True)

    sc = st_ref[...].reshape(shape3)
    gsel = (st_ref[...] + bias_ref[...]).reshape(shape3)
    pos = lax.broadcasted_iota(jnp.int32, shape3, 1)
    m1 = jnp.max(gsel, axis=1, keepdims=True)
    i1 = jnp.min(jnp.where(gsel == m1, pos, gsz), axis=1, keepdims=True)
    m2 = jnp.max(jnp.where(pos == i1, neg_inf, gsel), axis=1, keepdims=True)
    gs = m1 + m2

    gid = lax.broadcasted_iota(jnp.int32, gs.shape, 0)
    gkeep = jnp.zeros(gs.shape, F32)
    for _ in range(TOPK_GROUPS):
        mx = jnp.max(gs, axis=0, keepdims=True)
        gi = jnp.min(jnp.where(gs == mx, gid, N_GROUPS), axis=0, keepdims=True)
        hit = gid == gi
        gkeep = gkeep + jnp.where(hit, 1.0, 0.0)
        gs = jnp.where(hit, neg_inf, gs)

    cur0 = jnp.where(jnp.broadcast_to(gkeep, shape3) > 0.5, gsel, neg_inf)
    cur = cur0
    eid = lax.broadcasted_iota(jnp.int32, shape3, 0) * gsz + pos
    idxs, ws = [], []
    wsum = jnp.zeros((1, 1, tr), F32)
    for _ in range(TOP_K):
        mx = max01(cur)
        ik = min01(jnp.where(cur == mx, eid, n_exp))
        hit = eid == ik
        wk = sum01(jnp.where(hit, sc, 0.0))
        idxs.append(ik)
        ws.append(wk)
        wsum = wsum + wk
        cur = jnp.where(hit, neg_inf, cur)
    onehot = jnp.where(cur == cur0, 0.0, 1.0)

    ra = lax.broadcasted_iota(jnp.int32, (tr, tr), 0)
    rb = lax.broadcasted_iota(jnp.int32, (tr, tr), 1)
    upper = jnp.where(ra < rb, 1.0, 0.0).astype(BF16)
    onehot2 = onehot.reshape(n_exp, tr)
    prior = (_dot(onehot2.astype(BF16), upper) + carry_ref[...]).reshape(shape3)
    for slot in range(TOP_K):
        idx_ref[slot:slot + 1, :] = idxs[slot].reshape(1, tr)
        wts_ref[slot:slot + 1, :] = (ws[slot] / wsum * ROUTED_SCALE).reshape(1, tr)
        rk = sum01(jnp.where(eid == idxs[slot], prior, 0.0))
        rank_ref[slot:slot + 1, :] = rk.reshape(1, tr).astype(jnp.int32)
    total = carry_ref[...] + jnp.sum(onehot2, axis=1, keepdims=True)
    carry_ref[...] = total
    cnt_ref[...] = jnp.broadcast_to(total, cnt_ref.shape).astype(jnp.int32)


def _route(scores_t, bias_col, tr):
    n_exp, t = scores_t.shape
    slot_spec = pl.BlockSpec((TOP_K, tr), lambda i: (0, i))
    return pl.pallas_call(
        _route_kernel,
        grid=(t // tr,),
        in_specs=[pl.BlockSpec((n_exp, tr), lambda i: (0, i)), _const_spec(bias_col.shape)],
        out_specs=[slot_spec, slot_spec, slot_spec, _const_spec((n_exp, V7X_LANES))],
        out_shape=[jax.ShapeDtypeStruct((TOP_K, t), jnp.int32),
                   jax.ShapeDtypeStruct((TOP_K, t), F32),
                   jax.ShapeDtypeStruct((TOP_K, t), jnp.int32),
                   jax.ShapeDtypeStruct((n_exp, V7X_LANES), jnp.int32)],
        scratch_shapes=[pltpu.VMEM((n_exp, 1), F32)],
        compiler_params=_cparams(("arbitrary",)),
        name="route",
    )(scores_t, bias_col)


def _dest_kernel(pstart_ref, idx_ref, rank_ref, o_ref):
    n_exp = pstart_ref.shape[0]
    tr = idx_ref.shape[1]
    eid = lax.broadcasted_iota(jnp.int32, (n_exp, tr), 0)
    pstart = pstart_ref[...]
    for slot in range(TOP_K):
        hit = eid == idx_ref[slot:slot + 1, :]
        start = jnp.sum(jnp.where(hit, pstart, 0.0), axis=0, keepdims=True).astype(jnp.int32)
        o_ref[slot:slot + 1, :] = (start + rank_ref[slot:slot + 1, :]) * ROW_SLABS


def _dest(pstart, idx, rank, tr):
    k, t = idx.shape
    spec = pl.BlockSpec((k, tr), lambda i: (0, i))
    pstart_col = pstart.astype(F32)[:, None]
    return pl.pallas_call(
        _dest_kernel,
        grid=(t // tr,),
        in_specs=[_const_spec(pstart_col.shape), spec, spec],
        out_specs=spec,
        out_shape=jax.ShapeDtypeStruct((k, t), jnp.int32),
        compiler_params=_cparams(("arbitrary",)),
        name="dest",
    )(pstart_col, idx, rank)


def _dispatch_kernel(dest_ref, hp_ref, xs_ref, sem):
    td = hp_ref.shape[0] // ROW_SLABS

    def row_copy(r, slot):
        dst = pl.multiple_of(dest_ref[slot, r], ROW_SLABS)
        src = pl.multiple_of(r * ROW_SLABS, ROW_SLABS)
        return pltpu.make_async_copy(hp_ref.at[pl.ds(src, ROW_SLABS)], xs_ref.at[pl.ds(dst, ROW_SLABS)], sem)

    def start(r, c):
        for slot in range(TOP_K):
            row_copy(r, slot).start(priority=slot % 2)
        return c

    def wait(r, c):
        for slot in range(TOP_K):
            row_copy(r, slot).wait()
        return c

    lax.fori_loop(0, td, start, 0, unroll=4)
    lax.fori_loop(0, td, wait, 0, unroll=8)


def _dispatch(dest, hp, n_pad, td):
    t = hp.shape[0] // ROW_SLABS
    return pl.pallas_call(
        _dispatch_kernel,
        grid=(t // td,),
        in_specs=[pl.BlockSpec((TOP_K, td), lambda i: (0, i), memory_space=pltpu.SMEM),
                  pl.BlockSpec((td * ROW_SLABS, V7X_LANES), lambda i: (i, 0))],
        out_specs=pl.BlockSpec(memory_space=pl.ANY),
        out_shape=jax.ShapeDtypeStruct((n_pad * ROW_SLABS, V7X_LANES), hp.dtype),
        scratch_shapes=[pltpu.SemaphoreType.DMA(())],
        compiler_params=_cparams(("arbitrary",)),
        name="dispatch",
    )(dest, hp)


def _expert_kernel(be_ref, rows_ref, seq_ref, cnt_ref, xs_hbm, wg_hbm, wu_hbm, wd_hbm, ys_hbm,
                   xbuf, ybuf, wgf, wuf, wdf, wgu_s, wd_s, sem_x, sem_y, sem_w, *, layer):
    n = cnt_ref[0]
    n_seq = cnt_ref[1]
    blk_rows = EXPERT_BLOCK * ROW_SLABS
    nx, nw = EXPERT_X_AHEAD + 1, EXPERT_W_AHEAD + 1

    def block_rows(q):
        start = q * blk_rows
        return pl.ds(start if isinstance(q, int) else pl.multiple_of(start, blk_rows), blk_rows)

    def x_copy(q, slot):
        return pltpu.make_async_copy(xs_hbm.at[block_rows(q)], xbuf.at[slot], sem_x.at[slot])

    def y_copy(q, slot):
        return pltpu.make_async_copy(ybuf.at[slot], ys_hbm.at[block_rows(q)], sem_y.at[slot])

    def w_copies(k, ws):
        e = seq_ref[k]
        out = []
        for hbm, buf in ((wg_hbm, wgf), (wu_hbm, wuf), (wd_hbm, wdf)):
            band = buf.shape[1] // EXPERT_W_PARTS
            for part in range(EXPERT_W_PARTS):
                rows = pl.ds(part * band, band)
                out.append((pltpu.make_async_copy(hbm.at[layer, e, rows], buf.at[ws, rows], sem_w.at[ws]),
                            part % 2))
        return out

    for j in range(EXPERT_X_AHEAD):
        @pl.when(j < n)
        def _():
            x_copy(j, j).start(priority=1)

    for j in range(EXPERT_W_AHEAD):
        @pl.when(j < n_seq)
        def _():
            for cp, prio in w_copies(j, j):
                cp.start(priority=prio)

    def body(q, k):
        slot = q % 2
        fresh = jnp.logical_or(q == 0, be_ref[q] != be_ref[jnp.maximum(q - 1, 0)])

        @pl.when(fresh)
        def _():
            ws = k % nw
            for cp, _ in w_copies(k, ws):
                cp.wait()
            fw = wd_s.shape[0]
            wgu_s[:, :fw] = wgf[ws].astype(BF16)
            wgu_s[:, fw:] = wuf[ws].astype(BF16)
            wd_s[...] = wdf[ws].astype(BF16)

            @pl.when(k + EXPERT_W_AHEAD < n_seq)
            def _():
                for cp, prio in w_copies(k + EXPERT_W_AHEAD, (k + EXPERT_W_AHEAD) % nw):
                    cp.start(priority=prio)

        @pl.when(q + EXPERT_X_AHEAD < n)
        def _():
            x_copy(q + EXPERT_X_AHEAD, (q + EXPERT_X_AHEAD) % nx).start(priority=1)

        x_copy(q, q % nx).wait()

        @pl.when(q >= 2)
        def _():
            y_copy(q - 2, slot).wait()

        half = EXPERT_BLOCK // 2
        f = wd_s.shape[0]
        gu = []
        for h in range(2):
            x = jnp.concatenate(_load_packed_rows(xbuf, half, lead=(q % nx,), row0=h * half), axis=1)
            valid = lax.broadcasted_iota(jnp.int32, x.shape, 0) + h * half < rows_ref[q]
            x = jnp.where(valid, x, 0.0).astype(BF16)
            gu.append(_dot(x, wgu_s[...]))
        for h in range(2):
            g, u = gu[h][:, :f], gu[h][:, f:]
            a = (g * jax.nn.sigmoid(g) * u).astype(BF16)
            _store_packed_rows(ybuf.at[slot], _dot(a, wd_s[...]), row0=h * half)
        y_copy(q, slot).start()
        return k + fresh.astype(jnp.int32)

    lax.fori_loop(0, n, body, jnp.int32(0))

    @pl.when(n >= 2)
    def _():
        y_copy(n - 2, n % 2).wait()

    y_copy(n - 1, (n - 1) % 2).wait()


def _experts(block_expert, block_rows, expert_seq, counts2, xs, wg, wu, wd, layer):
    _, n_exp, d, f = wg.shape
    blk = (EXPERT_BLOCK * ROW_SLABS, V7X_LANES)
    nx, nw = EXPERT_X_AHEAD + 1, EXPERT_W_AHEAD + 1
    any_spec = pl.BlockSpec(memory_space=pl.ANY)
    return pl.pallas_call(
        functools.partial(_expert_kernel, layer=layer),
        grid_spec=pltpu.PrefetchScalarGridSpec(
            num_scalar_prefetch=4, grid=(1,),
            in_specs=[any_spec, any_spec, any_spec, any_spec],
            out_specs=any_spec,
            scratch_shapes=[pltpu.VMEM((nx,) + blk, xs.dtype), pltpu.VMEM((2,) + blk, xs.dtype),
                            pltpu.VMEM((nw, d, f), F32), pltpu.VMEM((nw, d, f), F32), pltpu.VMEM((nw, f, d), F32),
                            pltpu.VMEM((d, 2 * f), BF16), pltpu.VMEM((f, d), BF16),
                            pltpu.SemaphoreType.DMA((nx,)), pltpu.SemaphoreType.DMA((2,)),
                            pltpu.SemaphoreType.DMA((nw,))]),
        out_shape=jax.ShapeDtypeStruct(xs.shape, xs.dtype),
        compiler_params=_cparams(("arbitrary",)),
        name="experts",
    )(block_expert, block_rows, expert_seq, counts2, xs, wg, wu, wd)


def _combine_kernel(d0_ref, d1_ref, d2_ref, ys_ref, w_ref, shr_ref, x1_ref, g2_ref, lg_ref, lb_ref, o_ref,
                    buf_a, buf_b, sem, *, alpha, tc):
    i = pl.program_id(0)
    last = pl.num_programs(0) - 1

    def row_copy(dref, r, slot, buf, sem_idx):
        src = pl.multiple_of(dref[slot, r], ROW_SLABS)
        dst = r * ROW_SLABS if isinstance(r, int) else pl.multiple_of(r * ROW_SLABS, ROW_SLABS)
        return pltpu.make_async_copy(ys_ref.at[pl.ds(src, ROW_SLABS)], buf.at[slot, pl.ds(dst, ROW_SLABS)],
                                     sem.at[sem_idx])

    def issue_unrolled(dref, buf, sem_idx):
        for r in range(tc):
            for slot in range(TOP_K):
                row_copy(dref, r, slot, buf, sem_idx).start(priority=slot % 2)

    def issue_loop(dref, buf, sem_idx):
        def start(r, c):
            for slot in range(TOP_K):
                row_copy(dref, r, slot, buf, sem_idx).start(priority=slot % 2)
            return c
        lax.fori_loop(0, tc, start, 0, unroll=4)

    def wait_all(dref, buf, sem_idx):
        def wait(r, c):
            for slot in range(TOP_K):
                row_copy(dref, r, slot, buf, sem_idx).wait()
            return c
        lax.fori_loop(0, tc, wait, 0, unroll=8)

    def reduce_tile(buf, rows):
        w = w_ref[rows, :]
        chunks = None
        for slot in range(TOP_K):
            wk = w[:, slot:slot + 1]
            part = [c * wk for c in _load_packed_rows(buf, tc, lead=(slot,))]
            chunks = part if chunks is None else [a + b for a, b in zip(chunks, part)]
        y = shr_ref[rows, :] + jnp.concatenate(chunks, axis=1)
        z = alpha * x1_ref[rows, :] + (1.0 + g2_ref[0]) * y
        o_ref[rows, :] = _ln(z) * lg_ref[...] + lb_ref[...]

    @pl.when(i == 0)
    def _():
        issue_loop(d0_ref, buf_a, 0)

    wait_all(d0_ref, buf_a, 0)
    issue_unrolled(d1_ref, buf_b, 1)
    reduce_tile(buf_a, pl.ds(0, tc))

    wait_all(d1_ref, buf_b, 1)
    issue_unrolled(d2_ref, buf_a, 0)
    reduce_tile(buf_b, pl.ds(tc, tc))

    @pl.when(i == last)
    def _():
        wait_all(d2_ref, buf_a, 0)


def _combine(dest, ys, wts_tk, shared, x1, gate2, lg, lb, tc, seq, alpha):
    t, d = x1.shape
    tps = seq // (2 * tc)
    n_tiles = t // tc
    row_spec = lambda n: pl.BlockSpec((2 * tc, n), lambda i: (i, 0))
    tile_dest = lambda f: pl.BlockSpec((TOP_K, tc), f, memory_space=pltpu.SMEM)
    buf = pltpu.VMEM((TOP_K, tc * ROW_SLABS, V7X_LANES), jnp.uint32)
    return pl.pallas_call(
        functools.partial(_combine_kernel, alpha=alpha, tc=tc),
        grid=(n_tiles // 2,),
        in_specs=[tile_dest(lambda i: (0, 2 * i)), tile_dest(lambda i: (0, 2 * i + 1)),
                  tile_dest(lambda i: (0, jnp.minimum(2 * i + 2, n_tiles - 1))),
                  pl.BlockSpec(memory_space=pl.ANY),
                  row_spec(TOP_K), row_spec(d), row_spec(d),
                  pl.BlockSpec((1, 1, d), lambda i: (i // tps, 0, 0)),
                  _const_spec(lg.shape), _const_spec(lb.shape)],
        out_specs=row_spec(d),
        out_shape=jax.ShapeDtypeStruct((t, d), F32),
        scratch_shapes=[buf, buf, pltpu.SemaphoreType.DMA((2,))],
        compiler_params=_cparams(("arbitrary",)),
        name="combine",
    )(dest, dest, dest, ys, wts_tk, shared, x1, gate2, lg, lb)


def _placement():
    pq = np.zeros((V7X_LANES, FOX_HEADS * HEAD_PAD), np.float32)
    pk = np.zeros((V7X_LANES, FOX_HEADS * HEAD_PAD), np.float32)
    for h in range(FOX_HEADS):
        base = h * HEAD_PAD + FOX_HEAD_DIM
        for piece in range(3):
            pq[piece * FOX_HEADS + h, base + piece] = 1.0
            pk[FORGET_ONES_LANE, base + piece] = 1.0
            pq[FORGET_ONES_LANE, base + 3 + piece] = 1.0
            pk[piece * FOX_HEADS + h, base + 3 + piece] = -1.0
    return jnp.asarray(pq, BF16), jnp.asarray(pk, BF16)


def _inproj_weights(w_in, b_forget, d):
    conv2 = d
    fw = FOX_HEADS * FOX_HEAD_DIM
    o1, o2, o3, o4 = conv2, conv2 + fw, conv2 + 2 * fw, conv2 + 3 * fw
    o5 = o4 + FOX_HEADS
    o6 = o5 + d

    def pad_heads(w):
        w = w.reshape(d, FOX_HEADS, FOX_HEAD_DIM)
        w = jnp.pad(w, ((0, 0), (0, 0), (0, HEAD_PAD - FOX_HEAD_DIM)))
        return w.reshape(d, FOX_HEADS * HEAD_PAD).astype(BF16)

    wglu = w_in[:, :o1].astype(BF16)
    wq = pad_heads(w_in[:, o1:o2])
    wk = pad_heads(w_in[:, o2:o3])
    wvt = w_in[:, o3:o4].T.astype(BF16)
    wf8 = w_in[:, o4:o5]
    wf = jnp.pad(jnp.concatenate([wf8, wf8, wf8], axis=1), ((0, 0), (0, V7X_LANES - 3 * FOX_HEADS))).astype(BF16)
    bf = jnp.pad(jnp.concatenate([b_forget, b_forget, b_forget]), (0, V7X_LANES - 3 * FOX_HEADS))[None, :].astype(F32)
    wga = w_in[:, o5:o6].astype(BF16)
    wgb = w_in[:, o6:].astype(BF16)
    pq, pk = _placement()
    return (wglu, wq, wk, wvt, wf, wga, wgb, bf, pq, pk)


def _tile_sizes(seq, t):
    return {"inproj": min(512, seq),
            "mix": min(2 * MIX_ROWS_PER_GROUP, seq),
            "route": min(512, t),
            "dispatch": min(256, t),
            "combine": min(128, seq)}


def _layer(x2, ada, bsz, seq, w_in, b_forget, conv_w, conv_b, conv_ln_g, conv_ln_b, w_conv_out, w_fox_out,
           w_mix_out, ln1_g, ln1_b, w_router, router_bias, w_exp_gate, w_exp_up, w_exp_down,
           w_sh_gate, w_sh_up, w_sh_down, ln2_g, ln2_b, depth, layer):
    t, d = x2.shape
    n_exp = w_router.shape[1]
    alpha = (2.0 * depth) ** 0.25
    mods = [ada[:bsz, j * d:(j + 1) * d][:, None, :] for j in range(6)]
    shift1, scale1, gate1, shift2, scale2, gate2 = mods

    tiles = _tile_sizes(seq, t)
    tm = tiles["inproj"]
    conv_w_pad = jnp.pad(conv_w, ((0, CONV_HALO - CONV_WIDTH), (0, 0)))
    conv_wts = (conv_w_pad, conv_b[None, :], conv_ln_g[None, :], conv_ln_b[None, :], w_conv_out.astype(BF16))
    q, k, vt, gya, sgb = _inproj(x2, scale1, shift1, _inproj_weights(w_in, b_forget, d), conv_wts, tm, seq)

    ot = _attn(q.reshape(bsz, seq, -1), k.reshape(bsz, seq, -1), vt, tm, ATTN_HEADS_PER_STEP)

    tmx = tiles["mix"]
    wr_t = w_router.T
    wr_h = wr_t.astype(BF16)
    wr_l = (wr_t - wr_h.astype(F32)).astype(BF16)
    mix_w = (w_fox_out.astype(BF16), w_mix_out.astype(BF16), wr_h, wr_l,
             w_sh_gate.astype(BF16), w_sh_up.astype(BF16), w_sh_down.astype(BF16))
    x1, hp, scores_t, shared = _mix(ot, gya, sgb, x2, gate1, scale2, shift2, ln1_g[None, :], ln1_b[None, :],
                                    mix_w, tmx, seq, alpha)

    tr = tiles["route"]
    idx, wts, rank, cnt = _route(scores_t, router_bias[:, None], tr)

    counts = cnt[:, 0]
    padded = (counts + EXPERT_BLOCK - 1) // EXPERT_BLOCK * EXPERT_BLOCK
    pend = jnp.cumsum(padded)
    pstart = (pend - padded).astype(jnp.int32)
    n_assign = t * TOP_K
    n_pad = -(-(n_assign + n_exp * (EXPERT_BLOCK - 1)) // EXPERT_BLOCK) * EXPERT_BLOCK
    n_blocks = n_pad // EXPERT_BLOCK
    block_start = jnp.arange(n_blocks, dtype=jnp.int32) * EXPERT_BLOCK
    block_expert = jnp.minimum(jnp.sum(pend[None, :] <= block_start[:, None], axis=1), n_exp - 1).astype(jnp.int32)
    block_rows = jnp.clip((pstart + counts)[block_expert] - block_start, 0, EXPERT_BLOCK).astype(jnp.int32)
    n_used = (pend[-1:] // EXPERT_BLOCK).astype(jnp.int32)
    owns = counts > 0
    expert_seq = jnp.nonzero(owns, size=n_exp, fill_value=0)[0].astype(jnp.int32)
    counts2 = jnp.concatenate([n_used, jnp.sum(owns, dtype=jnp.int32)[None]])

    dest = _dest(pstart, idx, rank, tr)
    xs = _dispatch(dest, hp, n_pad, tiles["dispatch"])
    ys = _experts(block_expert, block_rows, expert_seq, counts2, xs, w_exp_gate, w_exp_up, w_exp_down, layer)
    return _combine(dest, ys, wts.T, shared, x1, gate2, ln2_g[None, :], ln2_b[None, :],
                    tiles["combine"], seq, alpha)


def kernel(x, c, w_ada, b_ada, w_in, b_forget, conv_w, conv_b, conv_ln_g, conv_ln_b, w_conv_out, w_fox_out,
           w_mix_out, ln1_g, ln1_b, w_router, router_bias, w_exp_gate, w_exp_up, w_exp_down, w_sh_gate,
           w_sh_up, w_sh_down, ln2_g, ln2_b):
    bsz, seq, d = x.shape
    depth = w_ada.shape[0]
    c_pad = jnp.pad(c, ((0, -bsz % V7X_SUBLANES), (0, 0)))
    x2 = x.reshape(bsz * seq, d)
    for l in range(depth):
        ada = _ada(c_pad, w_ada[l], b_ada[l][None, :])
        x2 = _layer(x2, ada, bsz, seq, w_in[l], b_forget[l], conv_w[l], conv_b[l], conv_ln_g[l], conv_ln_b[l],
                    w_conv_out[l], w_fox_out[l], w_mix_out[l], ln1_g[l], ln1_b[l], w_router[l], router_bias[l],
                    w_exp_gate, w_exp_up, w_exp_down, w_sh_gate[l], w_sh_up[l], w_sh_down[l],
                    ln2_g[l], ln2_b[l], depth, l)
    return x2.reshape(bsz, seq, d)
```
